```python
import jax, jax.numpy as jnp
from jax import lax
import numpy as np

D_MODEL = 1024
BATCH = 4
SEQ = 4096
DEPTH = 1
DEC_BATCH = 128
DEC_SEQ = 8
PAST_LEN = 8192
PAGE_SIZE = 128

HEAD_DIM = 64
N_HEADS = 8
KV_HEADS = 2
GROUP = N_HEADS // KV_HEADS
WINDOW = 128
ROT_DIM = HEAD_DIM // 4
ROPE_THETA = 500000.0
ATT_WIDTH = N_HEADS * HEAD_DIM
KV_WIDTH = KV_HEADS * HEAD_DIM
GLA_HEADS = 4
GLA_KDIM = D_MODEL // 2
GLA_VDIM = D_MODEL
GLA_DK = GLA_KDIM // GLA_HEADS
GLA_DV = GLA_VDIM // GLA_HEADS
GK_RANK = 16
GK_NORMALIZER = 16.0
GLA_CHUNK = 64
N_EXPERTS = 32
TOP_K = 4
D_FF = D_MODEL
SWIGLU_LIMIT = 7.0
SWIGLU_ALPHA = 1.702
MOE_BLOCK = 128
EPS = 1e-5
NEG_INF = -1e30
SPLIT_SIZES = (ATT_WIDTH, KV_WIDTH, KV_WIDTH, GLA_KDIM, GLA_KDIM, GLA_VDIM, GK_RANK, GLA_VDIM, D_MODEL, D_MODEL)
IN_COLS = ATT_WIDTH + 2 * KV_WIDTH + 2 * GLA_KDIM + 2 * GLA_VDIM + GK_RANK + 2 * D_MODEL

kernel_name = 'hybrid_swa_sink_gla_moe_step'


def rmsnorm(x, g):
    xf = x.astype(jnp.float32)
    y = xf * lax.rsqrt(jnp.mean(xf * xf, axis=-1, keepdims=True) + EPS)
    return (y * g.astype(jnp.float32)).astype(x.dtype)


def rope_partial(x, pos):
    half = ROT_DIM // 2
    inv = ROPE_THETA ** (-jnp.arange(0, ROT_DIM, 2, dtype=jnp.float32) / ROT_DIM)
    ang = pos.astype(jnp.float32)[:, None] * inv[None, :]
    cos = jnp.cos(ang)[:, None, :]
    sin = jnp.sin(ang)[:, None, :]
    xf = x.astype(jnp.float32)
    x1, x2 = xf[..., :half], xf[..., half:ROT_DIM]
    out = jnp.concatenate([x1 * cos - x2 * sin, x2 * cos + x1 * sin, xf[..., ROT_DIM:]], axis=-1)
    return out.astype(x.dtype)


def project(h, pos, w_in, w_gk_up, b_gk):
    B, T = h.shape[0], h.shape[1]
    offs = np.cumsum(SPLIT_SIZES)[:-1].tolist()
    q_a, k_a, v_a, q_g, k_g, v_g, gk_low, r_g, gate_a, gate_b = jnp.split(h @ w_in, offs, axis=-1)
    q_a = rope_partial(q_a.reshape(B, T, N_HEADS, HEAD_DIM), pos).reshape(B, T, KV_HEADS, GROUP, HEAD_DIM)
    k_a = rope_partial(k_a.reshape(B, T, KV_HEADS, HEAD_DIM), pos)
    v_a = v_a.reshape(B, T, KV_HEADS, HEAD_DIM)
    q_g = q_g.reshape(B, T, GLA_HEADS, GLA_DK) * (GLA_DK ** -0.5)
    k_g = k_g.reshape(B, T, GLA_HEADS, GLA_DK)
    v_g = v_g.reshape(B, T, GLA_HEADS, GLA_DV)
    log_a = jax.nn.log_sigmoid((gk_low @ w_gk_up + b_gk).astype(jnp.float32)) / GK_NORMALIZER
    log_a = log_a.reshape(B, T, GLA_HEADS, GLA_DK)
    return q_a, k_a, v_a, q_g, k_g, v_g, log_a, r_g, gate_a, gate_b


def attend_sink(q, k, v, mask, sink):
    s = jnp.einsum('...qkgd,...skd->...kgqs', q, k).astype(jnp.float32) * (HEAD_DIM ** -0.5)
    s = jnp.where(mask, s, NEG_INF)
    sk = sink.astype(jnp.float32)[:, :, None, None]
    m = jnp.maximum(jnp.max(s, axis=-1, keepdims=True), sk)
    p = jnp.exp(s - m)
    denom = jnp.sum(p, axis=-1, keepdims=True) + jnp.exp(sk - m)
    p = (p / denom).astype(v.dtype)
    return jnp.einsum('...kgqs,...skd->...qkgd', p, v)


def swa_prompt(q, k, v, sink):
    B, T = q.shape[0], q.shape[1]
    nb = T // WINDOW
    qb = q.reshape(B, nb, WINDOW, KV_HEADS, GROUP, HEAD_DIM)
    kb = k.reshape(B, nb, WINDOW, KV_HEADS, HEAD_DIM)
    vb = v.reshape(B, nb, WINDOW, KV_HEADS, HEAD_DIM)
    prev = lambda t: jnp.concatenate([jnp.zeros_like(t[:, :1]), t[:, :-1]], axis=1)
    kk = jnp.concatenate([prev(kb), kb], axis=2)
    vv = jnp.concatenate([prev(vb), vb], axis=2)
    qi = jnp.arange(WINDOW)[:, None] + WINDOW
    si = jnp.arange(2 * WINDOW)[None, :]
    band = (si <= qi) & (qi - si < WINDOW)
    valid = band[None] & ((jnp.arange(nb)[:, None, None] > 0) | (si >= WINDOW)[None])
    o = attend_sink(qb, kk, vv, valid[:, None, None], sink)
    return o.reshape(B, T, ATT_WIDTH), k[:, -WINDOW:], v[:, -WINDOW:]


def swa_sample(q, k, v, buf_k, buf_v, sink):
    B, S = q.shape[0], q.shape[1]
    kk = jnp.concatenate([buf_k.astype(k.dtype), k], axis=1)
    vv = jnp.concatenate([buf_v.astype(v.dtype), v], axis=1)
    kpos = jnp.concatenate([PAST_LEN - WINDOW + jnp.arange(WINDOW), PAST_LEN + jnp.arange(S)])
    qpos = PAST_LEN + jnp.arange(S)
    mask = (kpos[None, :] <= qpos[:, None]) & (qpos[:, None] - kpos[None, :] < WINDOW)
    o = attend_sink(q, kk, vv, mask[None, None], sink)
    return o.reshape(B, S, ATT_WIDTH), kk[:, -WINDOW:], vv[:, -WINDOW:]


def gla_chunk(S0, q, k, v, la):
    q = q.astype(jnp.float32)
    k = k.astype(jnp.float32)
    v = v.astype(jnp.float32)
    b = jnp.cumsum(la, axis=1)
    C = q.shape[1]
    causal = jnp.tril(jnp.ones((C, C), dtype=bool))[None, :, :, None, None]
    diff = b[:, :, None] - b[:, None]
    decay = jnp.where(causal, jnp.exp(jnp.where(causal, diff, 0.0)), 0.0)
    att = jnp.einsum('bthk,bshk,btshk->bhts', q, k, decay)
    o = jnp.einsum('bthk,bhkv->bthv', q * jnp.exp(b), S0) + jnp.einsum('bhts,bshv->bthv', att, v)
    b_last = b[:, -1]
    S_new = jnp.exp(b_last)[..., None] * S0 + jnp.einsum('bshk,bshv->bhkv', k * jnp.exp(b_last[:, None] - b), v)
    return S_new, o


def gla_prompt(q, k, v, la):
    B, T = q.shape[0], q.shape[1]
    nc = T // GLA_CHUNK
    chunks = lambda t: jnp.moveaxis(t.reshape((B, nc, GLA_CHUNK) + t.shape[2:]), 1, 0)
    S0 = jnp.zeros((B, GLA_HEADS, GLA_DK, GLA_DV), jnp.float32)
    S_fin, o = lax.scan(lambda S, xs: gla_chunk(S, *xs), S0, (chunks(q), chunks(k), chunks(v), chunks(la)))
    return jnp.moveaxis(o, 0, 1).reshape(B, T, GLA_HEADS, GLA_DV), S_fin


def gla_output(o, r, gla_norm, dtype):
    B, T = o.shape[0], o.shape[1]
    o = rmsnorm(o.astype(dtype), gla_norm).reshape(B, T, GLA_VDIM)
    return o * jax.nn.silu(r)


def merge(o_attn, o_gla, gate_a, gate_b, w_branch_a, w_branch_b, w_out):
    m = jax.nn.sigmoid(gate_a) * (o_attn @ w_branch_a) + jax.nn.sigmoid(gate_b) * (o_gla @ w_branch_b)
    return m @ w_out


def moe(h, router_w, router_b, w_up, b_up, w_down, b_down):
    shape = h.shape
    x = h.reshape(-1, D_MODEL)
    T = x.shape[0]
    logits = (x @ router_w + router_b).astype(jnp.float32)
    top_vals, top_idx = lax.top_k(logits, TOP_K)
    gates = jax.nn.softmax(top_vals, axis=-1)
    n_slots = T * TOP_K
    flat_e = top_idx.reshape(-1)
    flat_tok = jnp.arange(n_slots, dtype=jnp.int32) // TOP_K
    order = jnp.argsort(flat_e)
    se, stok, sgate = flat_e[order], flat_tok[order], gates.reshape(-1)[order]
    counts = jnp.zeros((N_EXPERTS,), jnp.int32).at[flat_e].add(1)
    padded = (counts + MOE_BLOCK - 1) // MOE_BLOCK * MOE_BLOCK
    pad_end = jnp.cumsum(padded)
    pad_start = pad_end - padded
    start = jnp.cumsum(counts) - counts
    dest = pad_start[se] + jnp.arange(n_slots, dtype=jnp.int32) - start[se]
    n_blocks = -(-n_slots // MOE_BLOCK) + N_EXPERTS
    rows = jnp.full((n_blocks * MOE_BLOCK,), T, jnp.int32).at[dest].set(stok)
    block_start = jnp.arange(n_blocks, dtype=jnp.int32) * MOE_BLOCK
    block_expert = jnp.minimum(jnp.sum(pad_end[None, :] <= block_start[:, None], axis=1), N_EXPERTS - 1)
    x_pad = jnp.concatenate([x, jnp.zeros((1, D_MODEL), x.dtype)], axis=0)

    def expert_block(args):
        r, e = args
        hu = x_pad[r] @ w_up[e] + b_up[e]
        glu, lin = jnp.split(hu, 2, axis=-1)
        glu = jnp.minimum(glu, SWIGLU_LIMIT)
        lin = jnp.clip(lin, -SWIGLU_LIMIT, SWIGLU_LIMIT)
        act = glu * jax.nn.sigmoid(SWIGLU_ALPHA * glu) * (lin + 1.0)
        return act @ w_down[e] + b_down[e]

    out = lax.map(expert_block, (rows.reshape(n_blocks, MOE_BLOCK), block_expert)).reshape(-1, D_MODEL)
    y = jax.ops.segment_sum(out[dest] * sgate[:, None].astype(out.dtype), stok, num_segments=T)
    return y.reshape(shape)


def setup_inputs(seed: int = 0) -> dict:
    key = jax.random.key(seed)
    ks = jax.random.split(key, 24)
    n = lambda k, s, sc: jax.random.normal(k, s, jnp.float32) * sc
    return {
        'x_prompt': n(ks[0], (BATCH, SEQ, D_MODEL), 1.0),
        'x_sample': n(ks[1], (DEC_BATCH, DEC_SEQ, D_MODEL), 1.0),
        'cache_swa_k': n(ks[2], (DEPTH, DEC_BATCH, WINDOW, KV_HEADS, HEAD_DIM), 1.0),
        'cache_swa_v': n(ks[3], (DEPTH, DEC_BATCH, WINDOW, KV_HEADS, HEAD_DIM), 1.0),
        'state_gla': n(ks[4], (DEPTH, DEC_BATCH, GLA_HEADS, GLA_DK, GLA_DV), 0.5),
        'g_mix': 1.0 + n(ks[5], (DEPTH, D_MODEL), 0.02),
        'w_in': n(ks[6], (DEPTH, D_MODEL, IN_COLS), D_MODEL ** -0.5),
        'w_gk_up': n(ks[7], (DEPTH, GK_RANK, GLA_KDIM), GK_RANK ** -0.5),
        'b_gk': n(ks[8], (DEPTH, GLA_KDIM), 0.1),
        'sinks': n(ks[9], (DEPTH, N_HEADS), 0.5),
        'gla_norm': 1.0 + n(ks[10], (DEPTH, GLA_DV), 0.02),
        'w_branch_a': n(ks[11], (DEPTH, ATT_WIDTH, D_MODEL), ATT_WIDTH ** -0.5),
        'w_branch_b': n(ks[12], (DEPTH, GLA_VDIM, D_MODEL), GLA_VDIM ** -0.5),
        'w_out': n(ks[13], (DEPTH, D_MODEL, D_MODEL), D_MODEL ** -0.5),
        'g_ffn': 1.0 + n(ks[14], (DEPTH, D_MODEL), 0.02),
        'router_w': n(ks[15], (DEPTH, D_MODEL, N_EXPERTS), D_MODEL ** -0.5),
        'router_b': n(ks[16], (DEPTH, N_EXPERTS), 0.01),
        'w_up': n(ks[17], (DEPTH, N_EXPERTS, D_MODEL, 2 * D_FF), D_MODEL ** -0.5),
        'b_up': n(ks[18], (DEPTH, N_EXPERTS, 2 * D_FF), 0.01),
        'w_down': n(ks[19], (DEPTH, N_EXPERTS, D_FF, D_MODEL), D_FF ** -0.5),
        'b_down': n(ks[20], (DEPTH, N_EXPERTS, D_MODEL), 0.01),
        'g_final': 1.0 + n(ks[21], (D_MODEL,), 0.02),
    }


def reference(x_prompt, x_sample, cache_swa_k, cache_swa_v, state_gla, g_mix, w_in, w_gk_up, b_gk, sinks,
              gla_norm, w_branch_a, w_branch_b, w_out, g_ffn, router_w, router_b, w_up, b_up, w_down, b_down,
              g_final):
    xp, xs = x_prompt, x_sample
    pos_p = jnp.arange(xp.shape[1], dtype=jnp.int32)
    pos_s = PAST_LEN + jnp.arange(xs.shape[1], dtype=jnp.int32)
    kp_l, vp_l, sp_l, ks_l, vs_l, ss_l = [], [], [], [], [], []
    for l in range(DEPTH):
        sink = sinks[l].reshape(KV_HEADS, GROUP)
        qa, ka, va, qg, kg, vg, la, rg, ga, gb = project(rmsnorm(xp, g_mix[l]), pos_p, w_in[l], w_gk_up[l], b_gk[l])
        oa, nk, nv = swa_prompt(qa, ka, va, sink)
        og, S_fin = gla_prompt(qg, kg, vg, la)
        og = gla_output(og, rg, gla_norm[l], xp.dtype)
        xp = xp + merge(oa, og, ga, gb, w_branch_a[l], w_branch_b[l], w_out[l])
        xp = xp + moe(rmsnorm(xp, g_ffn[l]), router_w[l], router_b[l], w_up[l], b_up[l], w_down[l], b_down[l])
        kp_l.append(nk); vp_l.append(nv); sp_l.append(S_fin)
        qa, ka, va, qg, kg, vg, la, rg, ga, gb = project(rmsnorm(xs, g_mix[l]), pos_s, w_in[l], w_gk_up[l], b_gk[l])
        oa, nk, nv = swa_sample(qa, ka, va, cache_swa_k[l], cache_swa_v[l], sink)
        S_new, og = gla_chunk(state_gla[l].astype(jnp.float32), qg, kg, vg, la)
        og = gla_output(og, rg, gla_norm[l], xs.dtype)
        xs = xs + merge(oa, og, ga, gb, w_branch_a[l], w_branch_b[l], w_out[l])
        xs = xs + moe(rmsnorm(xs, g_ffn[l]), router_w[l], router_b[l], w_up[l], b_up[l], w_down[l], b_down[l])
        ks_l.append(nk); vs_l.append(nv); ss_l.append(S_new)
    y_prompt = rmsnorm(xp, g_final)
    y_sample = rmsnorm(xs, g_final)
    return (y_prompt, y_sample, jnp.stack(kp_l), jnp.stack(vp_l), jnp.stack(sp_l),
            jnp.stack(ks_l), jnp.stack(vs_l), jnp.stack(ss_l))
```

```python
import functools

import numpy as np
import jax
import jax.numpy as jnp
from jax import lax
from jax.experimental import pallas as pl
from jax.experimental.pallas import tpu as pltpu

D_MODEL = 1024
PAST_LEN = 8192
HEAD_DIM = 64
N_HEADS = 8
KV_HEADS = 2
GROUP = N_HEADS // KV_HEADS
WINDOW = 128
ROT_DIM = HEAD_DIM // 4
ROPE_THETA = 500000.0
ATT_WIDTH = N_HEADS * HEAD_DIM
KV_WIDTH = KV_HEADS * HEAD_DIM
GLA_HEADS = 4
GLA_KDIM = D_MODEL // 2
GLA_VDIM = D_MODEL
GLA_DK = GLA_KDIM // GLA_HEADS
GLA_DV = GLA_VDIM // GLA_HEADS
GK_RANK = 16
GK_NORMALIZER = 16.0
N_EXPERTS = 32
TOP_K = 4
D_FF = D_MODEL
SWIGLU_LIMIT = 7.0
SWIGLU_ALPHA = 1.702
EPS = 1e-5
NEG_INF = -1e30

LANES = 128
SUBLANES = 8
VMEM_LIMIT_BYTES = 56 * 1024 * 1024

ROW_TILE = 256
CHUNK = 128
FFN_BLOCK = 256

BF16 = jnp.bfloat16
F32 = jnp.float32

_C_QA, _C_KA, _C_VA, _C_QG, _C_KG, _C_VG, _C_RG, _C_GA, _C_GB, _C_GL, _C_END = (
    0, 512, 640, 768, 1280, 1792, 2816, 3840, 4864, 5888, 6016)


def _const_spec(shape):
    nd = len(shape)
    return pl.BlockSpec(shape, lambda *_: (0,) * nd)


def _params(sem, vmem=VMEM_LIMIT_BYTES):
    return pltpu.CompilerParams(dimension_semantics=sem, vmem_limit_bytes=vmem)


def _nt_dot(a, b):
    return lax.dot_general(a, b, (((1,), (1,)), ((), ())), preferred_element_type=F32)


def _dot(a, b):
    return jnp.dot(a, b, preferred_element_type=F32)


def _split3(x):
    hi = x.astype(BF16)
    r1 = x - hi.astype(F32)
    mid = r1.astype(BF16)
    lo = (r1 - mid.astype(F32)).astype(BF16)
    return hi, mid, lo


def _rope(x, cos_t, sin_t, n_rep):
    width = x.shape[1]
    cos_f = jnp.concatenate([cos_t] * n_rep, axis=1) if n_rep > 1 else cos_t
    sin_f = jnp.concatenate([sin_t] * n_rep, axis=1) if n_rep > 1 else sin_t
    lane = lax.broadcasted_iota(jnp.int32, x.shape, 1) % HEAD_DIM
    up = pltpu.roll(x, width - ROT_DIM // 2, 1)
    down = pltpu.roll(x, ROT_DIM // 2, 1)
    partner = jnp.where(lane < ROT_DIM // 2, up, down)
    return x * cos_f + partner * sin_f


def _proj_kernel(n_prompt_tiles, xp_ref, xs_ref, g_ref, cos_ref, sin_ref, w_ref, wup_ref, bgk_ref,
                 qa_ref, ka_ref, va_ref, qg_ref, kg_ref, vg_ref, la_ref, rg_ref, ga_ref, gb_ref):
    i = pl.program_id(0)
    x = jnp.where(i < n_prompt_tiles, xp_ref[...], xs_ref[...])
    ms = jnp.mean(x * x, axis=-1, keepdims=True)
    h = (x * lax.rsqrt(ms + EPS) * g_ref[...]).astype(BF16)
    cos_t = cos_ref[...]
    sin_t = sin_ref[...]

    def seg(a, b):
        return _dot(h, w_ref[:, a:b])

    qa = _rope(seg(_C_QA, _C_KA), cos_t, sin_t, ATT_WIDTH // LANES)
    qa_ref[...] = (qa * (HEAD_DIM ** -0.5)).astype(BF16)
    ka_ref[...] = _rope(seg(_C_KA, _C_VA), cos_t, sin_t, 1)
    va_ref[...] = seg(_C_VA, _C_QG)
    qg_ref[...] = seg(_C_QG, _C_KG) * (GLA_DK ** -0.5)
    kg_ref[...] = seg(_C_KG, _C_VG)
    vg_ref[...] = seg(_C_VG, _C_RG).astype(BF16)
    rg_ref[...] = seg(_C_RG, _C_GA).astype(BF16)
    ga_ref[...] = seg(_C_GA, _C_GB).astype(BF16)
    gb_ref[...] = seg(_C_GB, _C_GL).astype(BF16)
    gk_low = seg(_C_GL, _C_END).astype(BF16)
    z = _dot(gk_low, wup_ref[...]) + bgk_ref[...]
    log_sig = jnp.minimum(z, 0.0) - jnp.log1p(jnp.exp(-jnp.abs(z)))
    la_ref[...] = log_sig / GK_NORMALIZER


def _proj(xp2, xs2, g_mix, cos_tab, sin_tab, w_all, wup_pad, b_gk, seq_len):
    n_p, n_s = xp2.shape[0], xs2.shape[0]
    n_all = n_p + n_s
    npt, nst = n_p // ROW_TILE, n_s // ROW_TILE
    tiles_per_seq = seq_len // ROW_TILE

    def tab_map(i):
        return (jnp.where(i < npt, i % tiles_per_seq, tiles_per_seq), 0)

    row = lambda w: pl.BlockSpec((ROW_TILE, w), lambda i: (i, 0))
    widths = [(ATT_WIDTH, BF16), (KV_WIDTH, F32), (KV_WIDTH, F32), (GLA_KDIM, F32), (GLA_KDIM, F32),
              (GLA_VDIM, BF16), (GLA_KDIM, F32), (GLA_VDIM, BF16), (D_MODEL, BF16), (D_MODEL, BF16)]
    return pl.pallas_call(
        functools.partial(_proj_kernel, npt),
        grid=(npt + nst,),
        in_specs=[
            pl.BlockSpec((ROW_TILE, D_MODEL), lambda i: (jnp.minimum(i, npt - 1), 0)),
            pl.BlockSpec((ROW_TILE, D_MODEL), lambda i: (jnp.maximum(i - npt, 0), 0)),
            _const_spec((1, D_MODEL)),
            pl.BlockSpec((ROW_TILE, LANES), tab_map),
            pl.BlockSpec((ROW_TILE, LANES), tab_map),
            _const_spec(w_all.shape),
            _const_spec(wup_pad.shape),
            _const_spec((1, GLA_KDIM)),
        ],
        out_specs=[row(w) for w, _ in widths],
        out_shape=[jax.ShapeDtypeStruct((n_all, w), dt) for w, dt in widths],
        compiler_params=_params(("arbitrary",)),
        name="proj",
    )(xp2, xs2, g_mix, cos_tab, sin_tab, w_all, wup_pad, b_gk)


def _pair_blocks(kk):
    lane = lax.broadcasted_iota(jnp.int32, kk.shape, 1)
    lo = lane < HEAD_DIM
    swapped = pltpu.roll(kk, HEAD_DIM, 1)
    zero = jnp.zeros_like(kk)
    blocks = []
    for kh in range(KV_HEADS):
        left = jnp.where(lo, kk if kh == 0 else swapped, zero)
        right = jnp.where(lo, zero, swapped if kh == 0 else kk)
        blocks.append(jnp.concatenate([left, right], axis=0).astype(BF16))
    return blocks


def _sink_softmax(s, valid, sink):
    s = jnp.where(valid, s, NEG_INF)
    m = jnp.maximum(jnp.max(s, axis=-1, keepdims=True), sink)
    p = jnp.exp(s - m)
    denom = jnp.sum(p, axis=-1, keepdims=True) + jnp.exp(sink - m)
    return (p * (1.0 / denom)).astype(BF16)


def _attend(q, kk, vv, valid, sink_ref, o_ref):
    rows, keys = valid.shape
    kblocks = _pair_blocks(kk)
    vblocks = _pair_blocks(vv)
    for kh in range(KV_HEADS):
        base = kh * GROUP * HEAD_DIM
        qq = jnp.concatenate([q[:, base:base + LANES], q[:, base + LANES:base + 2 * LANES]], axis=0)
        s = _nt_dot(qq, kblocks[kh])
        for r in range(2):
            probs = []
            for c in range(2):
                head = kh * GROUP + 2 * r + c
                probs.append(_sink_softmax(s[r * rows:(r + 1) * rows, c * keys:(c + 1) * keys],
                                           valid, sink_ref[head]))
            p = jnp.concatenate(probs, axis=1)
            o_ref[:, base + r * LANES:base + (r + 1) * LANES] = _dot(p, vblocks[kh]).astype(o_ref.dtype)


def _swa_prompt_kernel(sink_ref, q_ref, kp_ref, kc_ref, vp_ref, vc_ref, o_ref):
    j = pl.program_id(1)
    kk = jnp.concatenate([kp_ref[...], kc_ref[...]], axis=0)
    vv = jnp.concatenate([vp_ref[...], vc_ref[...]], axis=0)
    row = lax.broadcasted_iota(jnp.int32, (WINDOW, 2 * WINDOW), 0)
    col = lax.broadcasted_iota(jnp.int32, (WINDOW, 2 * WINDOW), 1)
    valid = (col > row) & (col <= row + WINDOW) & ((j > 0) | (col >= WINDOW))
    _attend(q_ref[...], kk, vv, valid, sink_ref, o_ref)


def _swa_prompt(sinks, qa, ka, va, batch, seq_len):
    nb = seq_len // WINDOW
    cur = lambda b, j, s: (b * nb + j, 0)
    prev = lambda b, j, s: (b * nb + jnp.maximum(j - 1, 0), 0)
    return pl.pallas_call(
        _swa_prompt_kernel,
        grid_spec=pltpu.PrefetchScalarGridSpec(
            num_scalar_prefetch=1,
            grid=(batch, nb),
            in_specs=[
                pl.BlockSpec((WINDOW, ATT_WIDTH), cur),
                pl.BlockSpec((WINDOW, KV_WIDTH), prev),
                pl.BlockSpec((WINDOW, KV_WIDTH), cur),
                pl.BlockSpec((WINDOW, KV_WIDTH), prev),
                pl.BlockSpec((WINDOW, KV_WIDTH), cur),
            ],
            out_specs=pl.BlockSpec((WINDOW, ATT_WIDTH), cur),
        ),
        out_shape=jax.ShapeDtypeStruct((batch * seq_len, ATT_WIDTH), BF16),
        compiler_params=_params(("arbitrary", "arbitrary")),
        name="swa_prompt",
    )(sinks, qa, ka, ka, va, va)


SAMPLE_GROUP = 16


def _swa_sample_kernel(dec_seq, sink_ref, q_ref, kn_ref, vn_ref, ck_ref, cv_ref, o_ref, nk_ref, nv_ref):
    rows = SAMPLE_GROUP * dec_seq
    ck = ck_ref[...]
    cv = cv_ref[...]
    kn = kn_ref[...]
    vn = vn_ref[...]
    nk_ref[:, :WINDOW - dec_seq, :] = ck[:, dec_seq:, :]
    nv_ref[:, :WINDOW - dec_seq, :] = cv[:, dec_seq:, :]
    nk_ref[:, WINDOW - dec_seq:, :] = kn.reshape(SAMPLE_GROUP, dec_seq, KV_WIDTH)
    nv_ref[:, WINDOW - dec_seq:, :] = vn.reshape(SAMPLE_GROUP, dec_seq, KV_WIDTH)
    n_cache = SAMPLE_GROUP * WINDOW
    kk = jnp.concatenate([ck.reshape(n_cache, KV_WIDTH), kn], axis=0)
    vv = jnp.concatenate([cv.reshape(n_cache, KV_WIDTH), vn], axis=0)
    keys = n_cache + rows
    row = lax.broadcasted_iota(jnp.int32, (rows, keys), 0)
    col = lax.broadcasted_iota(jnp.int32, (rows, keys), 1)
    q_b, q_s = row // dec_seq, row % dec_seq
    is_cache = col < n_cache
    new = col - n_cache
    valid_cache = (col // WINDOW == q_b) & (col % WINDOW > q_s)
    valid_new = (new // dec_seq == q_b) & (new % dec_seq <= q_s)
    valid = (is_cache & valid_cache) | (jnp.logical_not(is_cache) & valid_new)
    _attend(q_ref[...], kk, vv, valid, sink_ref, o_ref)


def _swa_sample(sinks, qa, ka, va, cache_k, cache_v, n_prompt_rows, dec_batch, dec_seq):
    rows = SAMPLE_GROUP * dec_seq
    off = n_prompt_rows // rows
    tok = lambda g, s: (off + g, 0)
    cache = lambda g, s: (g, 0, 0)
    cshape = (dec_batch, WINDOW, KV_WIDTH)
    return pl.pallas_call(
        functools.partial(_swa_sample_kernel, dec_seq),
        grid_spec=pltpu.PrefetchScalarGridSpec(
            num_scalar_prefetch=1,
            grid=(dec_batch // SAMPLE_GROUP,),
            in_specs=[
                pl.BlockSpec((rows, ATT_WIDTH), tok),
                pl.BlockSpec((rows, KV_WIDTH), tok),
                pl.BlockSpec((rows, KV_WIDTH), tok),
                pl.BlockSpec((SAMPLE_GROUP, WINDOW, KV_WIDTH), cache),
                pl.BlockSpec((SAMPLE_GROUP, WINDOW, KV_WIDTH), cache),
            ],
            out_specs=[
                pl.BlockSpec((rows, ATT_WIDTH), lambda g, s: (g, 0)),
                pl.BlockSpec((SAMPLE_GROUP, WINDOW, KV_WIDTH), cache),
                pl.BlockSpec((SAMPLE_GROUP, WINDOW, KV_WIDTH), cache),
            ],
        ),
        out_shape=[jax.ShapeDtypeStruct((dec_batch * dec_seq, ATT_WIDTH), BF16),
                   jax.ShapeDtypeStruct(cshape, F32), jax.ShapeDtypeStruct(cshape, F32)],
        compiler_params=_params(("arbitrary",)),
        name="swa_sample",
    )(sinks, qa, ka, va, cache_k.reshape(cshape), cache_v.reshape(cshape))


def _chunk_tables(seg):
    n_lev = int(np.log2(seg))
    t = np.arange(CHUNK)
    seg_start = (t // seg) * seg
    u = np.arange(CHUNK)[None, :]

    def prefix(end):
        return ((u >= seg_start[:, None]) & (u <= end[:, None])).astype(np.float32)

    blocks = [prefix(t), prefix(seg_start + seg - 1)]
    for d in range(n_lev):
        m = 1 << d
        ref = (t >> (d + 1) << (d + 1)) + m - 1
        blocks.append(prefix(ref))
    lhs = np.concatenate(blocks, axis=0)
    lhs3 = np.concatenate([lhs, lhs, lhs], axis=1)
    tt, ss = t[:, None], t[None, :]
    x = tt ^ ss
    lev = np.where(x > 0, np.floor(np.log2(np.maximum(x, 1))).astype(np.int32), n_lev)
    lev = np.where((ss > tt) | (tt // seg != ss // seg), -1, lev)
    lev = np.where(tt == ss, n_lev, lev)
    return jnp.asarray(lhs3, BF16), jnp.asarray(lev, jnp.int32), n_lev


def _gla_chunk_terms(q, k, la, lhs3, level, n_lev):
    hi, mid, lo = _split3(la)
    rhs = jnp.concatenate([hi, mid, lo], axis=0)
    sums = _dot(lhs3, rhs)
    b = sums[0:CHUNK]
    b_last = sums[CHUNK:2 * CHUNK]
    q_main = (q * jnp.exp(b)).astype(BF16)
    k_upd = k * jnp.exp(b_last - b)
    q_lev = [None] * n_lev
    k_lev = [None] * n_lev
    for d in range(n_lev):
        ref = sums[(2 + d) * CHUNK:(3 + d) * CHUNK]
        q_lev[d] = (q * jnp.exp(jnp.minimum(b - ref, 0.0))).astype(BF16)
        k_lev[d] = (k * jnp.exp(jnp.minimum(ref - b, 0.0))).astype(BF16)
    q_b, k_b = q.astype(BF16), k.astype(BF16)

    def att(h):
        hs = slice(h * GLA_DK, (h + 1) * GLA_DK)
        acc = jnp.where(level == n_lev, _nt_dot(q_b[:, hs], k_b[:, hs]), 0.0)
        for d in range(n_lev):
            acc = jnp.where(level == d, _nt_dot(q_lev[d][:, hs], k_lev[d][:, hs]), acc)
        return acc

    return q_main, k_upd, att, b_last


def _gla_out(o, r, norm):
    parts = []
    for h in range(GLA_HEADS):
        oh = o[:, h * GLA_DV:(h + 1) * GLA_DV]
        ms = jnp.mean(oh * oh, axis=-1, keepdims=True)
        parts.append(oh * lax.rsqrt(ms + EPS))
    y = jnp.concatenate(parts, axis=1) * norm
    rf = r.astype(F32)
    return y * (rf * jax.nn.sigmoid(rf))


def _gla_prompt_kernel(n_lev, q_ref, k_ref, la_ref, v_ref, r_ref, norm_ref, lhs_ref, lev_ref,
                       o_ref, sfin_ref, s_scr):
    c = pl.program_id(1)

    @pl.when(c == 0)
    def _():
        s_scr[...] = jnp.zeros_like(s_scr)

    q_main, k_upd, att, b_last = _gla_chunk_terms(q_ref[...], k_ref[...], la_ref[...], lhs_ref[...],
                                                  lev_ref[...], n_lev)
    v = v_ref[...]
    outs = []
    for h in range(GLA_HEADS):
        hs = slice(h * GLA_DK, (h + 1) * GLA_DK)
        vh = v[:, h * GLA_DV:(h + 1) * GLA_DV]
        s0 = s_scr[h]
        o_h = _dot(q_main[:, hs], s0.astype(BF16)) + _dot(att(h).astype(BF16), vh)
        outs.append(o_h)
        decay = jnp.exp(b_last[:, hs]).T
        k_t = k_upd[:, hs].T.astype(BF16)
        s_scr[h] = jnp.concatenate([decay, decay], axis=1) * s0 + _dot(k_t, vh)
    o_ref[...] = _gla_out(jnp.concatenate(outs, axis=1), r_ref[...], norm_ref[...]).astype(o_ref.dtype)

    @pl.when(c == pl.num_programs(1) - 1)
    def _():
        sfin_ref[0] = s_scr[...]


def _gla_prompt(qg, kg, la, vg, rg, norm_row, batch, seq_len):
    nc = seq_len // CHUNK
    lhs3, level, n_lev = _chunk_tables(CHUNK)
    tok = lambda w: pl.BlockSpec((CHUNK, w), lambda b, c: (b * nc + c, 0))
    return pl.pallas_call(
        functools.partial(_gla_prompt_kernel, n_lev),
        grid=(batch, nc),
        in_specs=[tok(GLA_KDIM), tok(GLA_KDIM), tok(GLA_KDIM), tok(GLA_VDIM), tok(GLA_VDIM),
                  _const_spec((1, GLA_VDIM)), _const_spec(lhs3.shape), _const_spec(level.shape)],
        out_specs=[tok(GLA_VDIM),
                   pl.BlockSpec((1, GLA_HEADS, GLA_DK, GLA_DV), lambda b, c: (b, 0, 0, 0))],
        out_shape=[jax.ShapeDtypeStruct((batch * seq_len, GLA_VDIM), BF16),
                   jax.ShapeDtypeStruct((batch, GLA_HEADS, GLA_DK, GLA_DV), F32)],
        scratch_shapes=[pltpu.VMEM((GLA_HEADS, GLA_DK, GLA_DV), F32)],
        compiler_params=_params(("arbitrary", "arbitrary")),
        name="gla_prompt",
    )(qg, kg, la, vg, rg, norm_row, lhs3, level)


def _gla_sample_kernel(n_lev, dec_seq, q_ref, k_ref, la_ref, v_ref, r_ref, norm_ref, lhs_ref, lev_ref,
                       s0_ref, o_ref, snew_ref):
    q_main, k_upd, att, b_last = _gla_chunk_terms(q_ref[...], k_ref[...], la_ref[...], lhs_ref[...],
                                                  lev_ref[...], n_lev)
    v = v_ref[...]
    n_b = CHUNK // dec_seq
    row_b = lax.broadcasted_iota(jnp.int32, (CHUNK, GLA_DK), 0) // dec_seq
    col_b = lax.broadcasted_iota(jnp.int32, (GLA_DK, CHUNK), 1) // dec_seq
    outs = []
    for h in range(GLA_HEADS):
        hs = slice(h * GLA_DK, (h + 1) * GLA_DK)
        vh = v[:, h * GLA_DV:(h + 1) * GLA_DV]
        qm = q_main[:, hs]
        decay_t = jnp.exp(b_last[:, hs]).T
        k_t = k_upd[:, hs].T.astype(BF16)
        o_h = _dot(att(h).astype(BF16), vh)
        for bi in range(n_b):
            s0 = s0_ref[bi, h]
            o_h = o_h + _dot(jnp.where(row_b == bi, qm, jnp.zeros_like(qm)), s0.astype(BF16))
            decay = jnp.broadcast_to(decay_t[:, bi * dec_seq:bi * dec_seq + 1], (GLA_DK, GLA_DV))
            k_b = jnp.where(col_b == bi, k_t, jnp.zeros_like(k_t))
            snew_ref[bi, h] = decay * s0 + _dot(k_b, vh)
        outs.append(o_h)
    o_ref[...] = _gla_out(jnp.concatenate(outs, axis=1), r_ref[...], norm_ref[...]).astype(o_ref.dtype)


def _gla_sample(qg, kg, la, vg, rg, norm_row, state, n_prompt_rows, dec_batch, dec_seq):
    n_b = CHUNK // dec_seq
    off = n_prompt_rows // CHUNK
    lhs3, level, n_lev = _chunk_tables(dec_seq)
    tok = lambda w: pl.BlockSpec((CHUNK, w), lambda g: (off + g, 0))
    st = pl.BlockSpec((n_b, GLA_HEADS, GLA_DK, GLA_DV), lambda g: (g, 0, 0, 0))
    return pl.pallas_call(
        functools.partial(_gla_sample_kernel, n_lev, dec_seq),
        grid=(dec_batch // n_b,),
        in_specs=[tok(GLA_KDIM), tok(GLA_KDIM), tok(GLA_KDIM), tok(GLA_VDIM), tok(GLA_VDIM),
                  _const_spec((1, GLA_VDIM)), _const_spec(lhs3.shape), _const_spec(level.shape), st],
        out_specs=[pl.BlockSpec((CHUNK, GLA_VDIM), lambda g: (g, 0)), st],
        out_shape=[jax.ShapeDtypeStruct((dec_batch * dec_seq, GLA_VDIM), BF16),
                   jax.ShapeDtypeStruct(state.shape, F32)],
        compiler_params=_params(("arbitrary",)),
        name="gla_sample",
    )(qg, kg, la, vg, rg, norm_row, lhs3, level, state)


def _merge_kernel(n_prompt_tiles, xp_ref, xs_ref, oap_ref, oas_ref, ogp_ref, ogs_ref, ga_ref, gb_ref,
                  wa_ref, wb_ref, wo_ref, x1_ref):
    i = pl.program_id(0)
    is_p = i < n_prompt_tiles
    x = jnp.where(is_p, xp_ref[...], xs_ref[...])
    oa = jnp.where(is_p, oap_ref[...], oas_ref[...])
    og = jnp.where(is_p, ogp_ref[...], ogs_ref[...])
    m = (jax.nn.sigmoid(ga_ref[...].astype(F32)) * _dot(oa, wa_ref[...])
         + jax.nn.sigmoid(gb_ref[...].astype(F32)) * _dot(og, wb_ref[...]))
    x1_ref[...] = x + _dot(m.astype(BF16), wo_ref[...])


def _merge(xp2, xs2, oa_p, oa_s, og_p, og_s, ga, gb, wa, wb, wo):
    n_p, n_s = xp2.shape[0], xs2.shape[0]
    npt, nst = n_p // ROW_TILE, n_s // ROW_TILE
    p_map = lambda i: (jnp.minimum(i, npt - 1), 0)
    s_map = lambda i: (jnp.maximum(i - npt, 0), 0)
    row = lambda w: pl.BlockSpec((ROW_TILE, w), lambda i: (i, 0))
    return pl.pallas_call(
        functools.partial(_merge_kernel, npt),
        grid=(npt + nst,),
        in_specs=[
            pl.BlockSpec((ROW_TILE, D_MODEL), p_map), pl.BlockSpec((ROW_TILE, D_MODEL), s_map),
            pl.BlockSpec((ROW_TILE, ATT_WIDTH), p_map), pl.BlockSpec((ROW_TILE, ATT_WIDTH), s_map),
            pl.BlockSpec((ROW_TILE, GLA_VDIM), p_map), pl.BlockSpec((ROW_TILE, GLA_VDIM), s_map),
            row(D_MODEL), row(D_MODEL),
            _const_spec(wa.shape), _const_spec(wb.shape), _const_spec(wo.shape),
        ],
        out_specs=row(D_MODEL),
        out_shape=jax.ShapeDtypeStruct((n_p + n_s, D_MODEL), F32),
        compiler_params=_params(("arbitrary",)),
        name="merge",
    )(xp2, xs2, oa_p, oa_s, og_p, og_s, ga, gb, wa, wb, wo)


def _route_kernel(x1_ref, gffn_ref, rwh_ref, rwl_ref, rb_ref, upper_ref, eidx_ref, rank_ref, gate_ref,
                  cnt_ref, cnt_scr):
    i = pl.program_id(0)

    @pl.when(i == 0)
    def _():
        cnt_scr[...] = jnp.zeros_like(cnt_scr)

    x1 = x1_ref[...]
    ms = jnp.mean(x1 * x1, axis=-1, keepdims=True)
    h2 = x1 * lax.rsqrt(ms + EPS) * gffn_ref[...]
    h_hi = h2.astype(BF16)
    h_lo = (h2 - h_hi.astype(F32)).astype(BF16)
    rwh, rwl = rwh_ref[...], rwl_ref[...]
    logits = _nt_dot(rwh, h_hi) + _nt_dot(rwl, h_hi) + _nt_dot(rwh, h_lo) + rb_ref[...]
    eid = lax.broadcasted_iota(jnp.int32, logits.shape, 0)
    upper = upper_ref[...]
    base = cnt_scr[...]
    vals, rows_e, rows_r = [], [], []
    lg = logits
    for _ in range(TOP_K):
        mx = jnp.max(lg, axis=0, keepdims=True)
        sel = jnp.min(jnp.where(lg == mx, eid, N_EXPERTS), axis=0, keepdims=True)
        onehot = eid == sel
        oh = onehot.astype(F32)
        before = _dot(onehot.astype(BF16), upper)
        rows_r.append(jnp.sum(oh * (base + before), axis=0, keepdims=True))
        base = base + jnp.sum(oh, axis=1, keepdims=True)
        vals.append(mx)
        rows_e.append(sel)
        lg = jnp.where(onehot, -jnp.inf, lg)
    cnt_scr[...] = base
    ex = [jnp.exp(v - vals[0]) for v in vals]
    inv = 1.0 / (ex[0] + ex[1] + ex[2] + ex[3])
    eidx_ref[...] = jnp.concatenate(rows_e, axis=0)
    rank_ref[...] = jnp.concatenate(rows_r, axis=0).astype(jnp.int32)
    gate_ref[...] = jnp.concatenate([e * inv for e in ex], axis=0)
    cnt_ref[...] = jnp.broadcast_to(base, cnt_ref.shape)


def _route(x1, g_ffn, rw_hi, rw_lo, rb_col):
    n = x1.shape[0]
    upper = jnp.asarray(np.triu(np.ones((ROW_TILE, ROW_TILE), np.float32), 1), BF16)
    col = lambda dt: (pl.BlockSpec((TOP_K, ROW_TILE), lambda i: (0, i)), jax.ShapeDtypeStruct((TOP_K, n), dt))
    outs = [col(jnp.int32), col(jnp.int32), col(F32),
            (_const_spec((N_EXPERTS, LANES)), jax.ShapeDtypeStruct((N_EXPERTS, LANES), F32))]
    return pl.pallas_call(
        _route_kernel,
        grid=(n // ROW_TILE,),
        in_specs=[pl.BlockSpec((ROW_TILE, D_MODEL), lambda i: (i, 0)), _const_spec((1, D_MODEL)),
                  _const_spec(rw_hi.shape), _const_spec(rw_lo.shape), _const_spec((N_EXPERTS, 1)),
                  _const_spec(upper.shape)],
        out_specs=[o[0] for o in outs],
        out_shape=[o[1] for o in outs],
        scratch_shapes=[pltpu.VMEM((N_EXPERTS, 1), F32)],
        compiler_params=_params(("arbitrary",)),
        name="route",
    )(x1, g_ffn, rw_hi, rw_lo, rb_col, upper)


def _ffn_kernel(be_ref, nused_ref, src_cur_ref, src_nxt_ref, dst_ref, gate_ref, gffn_ref, wup_ref, bup_ref,
                wdn_ref, bdn_ref, x1_hbm, out_hbm, xbuf, obuf, gsem, ssem):
    i = pl.program_id(0)
    n_used = nused_ref[0]
    slot = i % 2

    def gather_copy(src_ref, r, s):
        return pltpu.make_async_copy(x1_hbm.at[pl.ds(src_ref[0, 0, r], 1), :],
                                     xbuf.at[s, pl.ds(r, 1), :], gsem.at[s])

    def scatter_copy(r, s):
        return pltpu.make_async_copy(obuf.at[s, pl.ds(r, 1), :],
                                     out_hbm.at[pl.ds(dst_ref[0, 0, r], 1), :], ssem.at[s])

    def start_gather(src_ref, s):
        def body(r, carry):
            gather_copy(src_ref, r, s).start()
            return carry
        lax.fori_loop(0, FFN_BLOCK, body, 0, unroll=8)

    def wait_rows(copy_fn):
        def body(r, carry):
            copy_fn(r).wait()
            return carry
        lax.fori_loop(0, FFN_BLOCK, body, 0, unroll=8)

    @pl.when(i == 0)
    def _():
        obuf[1] = jnp.zeros((FFN_BLOCK, D_MODEL), obuf.dtype)
        n_slot_rows = out_hbm.shape[0] - 2 * FFN_BLOCK
        for half in range(2):
            init = pltpu.make_async_copy(
                obuf.at[1], out_hbm.at[pl.ds(n_slot_rows + half * FFN_BLOCK, FFN_BLOCK), :], ssem.at[1])
            init.start()
            init.wait()

    @pl.when((i == 0) & (n_used > 0))
    def _():
        start_gather(src_cur_ref, 0)

    @pl.when(i + 1 < n_used)
    def _():
        start_gather(src_nxt_ref, 1 - slot)

    @pl.when(i < n_used)
    def _():
        wait_rows(lambda r: gather_copy(src_cur_ref, r, slot))
        x1 = xbuf[slot]
        ms = jnp.mean(x1 * x1, axis=-1, keepdims=True)
        x = (x1 * lax.rsqrt(ms + EPS) * gffn_ref[...]).astype(BF16)
        hu = _dot(x, wup_ref[0]) + bup_ref[0]
        glu = jnp.minimum(hu[:, :D_FF], SWIGLU_LIMIT)
        lin = jnp.clip(hu[:, D_FF:], -SWIGLU_LIMIT, SWIGLU_LIMIT)
        act = glu * jax.nn.sigmoid(SWIGLU_ALPHA * glu) * (lin + 1.0)
        out = (_dot(act.astype(BF16), wdn_ref[0]) + bdn_ref[0]) * gate_ref[...]

        @pl.when(i >= 2)
        def _():
            wait_rows(lambda r: scatter_copy(r, slot))

        obuf[slot] = out.astype(obuf.dtype)

        def body(r, carry):
            scatter_copy(r, slot).start()
            return carry
        lax.fori_loop(0, FFN_BLOCK, body, 0, unroll=8)

        @pl.when(i == n_used - 1)
        def _():
            @pl.when(i >= 1)
            def _():
                wait_rows(lambda r: scatter_copy(r, 1 - slot))
            wait_rows(lambda r: scatter_copy(r, slot))


def _ffn(block_expert, n_used, src, dst, gate_sorted, x1, g_ffn, w_up, b_up, w_down, b_down, n_out_rows):
    n_blocks = block_expert.shape[0]
    idx_cur = pl.BlockSpec((1, 1, FFN_BLOCK), lambda i, be, nu: (i, 0, 0), memory_space=pltpu.SMEM)
    idx_nxt = pl.BlockSpec((1, 1, FFN_BLOCK), lambda i, be, nu: (jnp.minimum(i + 1, n_blocks - 1), 0, 0),
                           memory_space=pltpu.SMEM)
    ex3 = lambda i, be, nu: (be[i], 0, 0)
    return pl.pallas_call(
        _ffn_kernel,
        grid_spec=pltpu.PrefetchScalarGridSpec(
            num_scalar_prefetch=2,
            grid=(n_blocks,),
            in_specs=[
                idx_cur, idx_nxt, idx_cur,
                pl.BlockSpec((FFN_BLOCK, 1), lambda i, be, nu: (i, 0)),
                pl.BlockSpec((1, D_MODEL), lambda i, be, nu: (0, 0)),
                pl.BlockSpec((1, D_MODEL, 2 * D_FF), ex3),
                pl.BlockSpec((1, 1, 2 * D_FF), ex3),
                pl.BlockSpec((1, D_FF, D_MODEL), ex3),
                pl.BlockSpec((1, 1, D_MODEL), ex3),
                pl.BlockSpec(memory_space=pl.ANY),
            ],
            out_specs=pl.BlockSpec(memory_space=pl.ANY),
            scratch_shapes=[pltpu.VMEM((2, FFN_BLOCK, D_MODEL), F32), pltpu.VMEM((2, FFN_BLOCK, D_MODEL), F32),
                            pltpu.SemaphoreType.DMA((2,)), pltpu.SemaphoreType.DMA((2,))],
        ),
        out_shape=jax.ShapeDtypeStruct((n_out_rows, D_MODEL), F32),
        compiler_params=_params(("arbitrary",)),
        name="ffn",
    )(block_expert, n_used, src, src, dst, gate_sorted, g_ffn, w_up, b_up, w_down, b_down, x1)


def _final_kernel(x1_ref, s0_ref, s1_ref, s2_ref, s3_ref, g_ref, y_ref):
    x2 = x1_ref[...] + ((s0_ref[...] + s1_ref[...]) + (s2_ref[...] + s3_ref[...]))
    ms = jnp.mean(x2 * x2, axis=-1, keepdims=True)
    y_ref[...] = x2 * lax.rsqrt(ms + EPS) * g_ref[...]


def _final(x1, slots, g_final, row0, n_rows, n_all):
    t0 = row0 // ROW_TILE
    per_slot = n_all // ROW_TILE
    slot = lambda k: pl.BlockSpec((ROW_TILE, D_MODEL), lambda i: (k * per_slot + t0 + i, 0))
    return pl.pallas_call(
        _final_kernel,
        grid=(n_rows // ROW_TILE,),
        in_specs=[pl.BlockSpec((ROW_TILE, D_MODEL), lambda i: (t0 + i, 0)),
                  slot(0), slot(1), slot(2), slot(3), _const_spec((1, D_MODEL))],
        out_specs=pl.BlockSpec((ROW_TILE, D_MODEL), lambda i: (i, 0)),
        out_shape=jax.ShapeDtypeStruct((n_rows, D_MODEL), F32),
        compiler_params=_params(("arbitrary",)),
        name="final",
    )(x1, slots, slots, slots, slots, g_final)


def _rope_tables(seq_len, dec_seq):
    half = ROT_DIM // 2
    inv = ROPE_THETA ** (-jnp.arange(0, ROT_DIM, 2, dtype=F32) / ROT_DIM)
    pos_p = jnp.arange(seq_len, dtype=jnp.int32)
    pos_s = PAST_LEN + (jnp.arange(ROW_TILE, dtype=jnp.int32) % dec_seq)
    pos = jnp.concatenate([pos_p, pos_s]).astype(F32)
    ang = pos[:, None] * inv[None, :]
    cos, sin = jnp.cos(ang), jnp.sin(ang)
    ones = jnp.ones((pos.shape[0], HEAD_DIM - ROT_DIM), F32)
    cos_h = jnp.concatenate([cos, cos, ones], axis=1)
    sin_h = jnp.concatenate([-sin, sin, 0.0 * ones], axis=1)
    return jnp.concatenate([cos_h, cos_h], axis=1), jnp.concatenate([sin_h, sin_h], axis=1)


def _dispatch_tables(eidx, rank, gate, counts, n_all):
    n_slots = TOP_K * n_all
    n_blocks = -(-n_slots // FFN_BLOCK) + N_EXPERTS
    n_pos = n_blocks * FFN_BLOCK
    padded = (counts + FFN_BLOCK - 1) // FFN_BLOCK * FFN_BLOCK
    pad_end = jnp.cumsum(padded)
    pad_start = pad_end - padded
    pos = (pad_start[eidx] + rank).reshape(-1)
    slot_row = jnp.arange(n_slots, dtype=jnp.int32)
    tok = slot_row % n_all
    dump = n_slots + jnp.arange(n_pos, dtype=jnp.int32) % (2 * FFN_BLOCK)
    src = jnp.zeros((n_pos,), jnp.int32).at[pos].set(tok)
    dst = dump.at[pos].set(slot_row)
    gate_sorted = jnp.zeros((n_pos,), F32).at[pos].set(gate.reshape(-1))
    block_start = jnp.arange(n_blocks, dtype=jnp.int32) * FFN_BLOCK
    block_expert = jnp.minimum(jnp.sum(pad_end[None, :] <= block_start[:, None], axis=1), N_EXPERTS - 1)
    n_used = (pad_end[-1] // FFN_BLOCK).astype(jnp.int32).reshape(1)
    return (block_expert.astype(jnp.int32), n_used, src.reshape(n_blocks, 1, FFN_BLOCK),
            dst.reshape(n_blocks, 1, FFN_BLOCK), gate_sorted.reshape(n_pos, 1), n_slots + 2 * FFN_BLOCK)


def kernel(x_prompt, x_sample, cache_swa_k, cache_swa_v, state_gla, g_mix, w_in, w_gk_up, b_gk, sinks,
           gla_norm, w_branch_a, w_branch_b, w_out, g_ffn, router_w, router_b, w_up, b_up, w_down, b_down,
           g_final):
    batch, seq_len, _ = x_prompt.shape
    dec_batch, dec_seq, _ = x_sample.shape
    n_p, n_s = batch * seq_len, dec_batch * dec_seq
    n_all = n_p + n_s
    assert w_in.shape[0] == 1, "one layer: the final norm is fused after the only MoE"
    assert seq_len % ROW_TILE == 0 and n_s % ROW_TILE == 0 and ROW_TILE % dec_seq == 0
    assert dec_seq % SUBLANES == 0 and dec_batch % SAMPLE_GROUP == 0
    assert SAMPLE_GROUP * dec_seq == CHUNK and (dec_seq & (dec_seq - 1)) == 0

    xp2 = x_prompt.reshape(n_p, D_MODEL)
    xs2 = x_sample.reshape(n_s, D_MODEL)
    cos_tab, sin_tab = _rope_tables(seq_len, dec_seq)
    w = w_in[0]
    gl0 = _C_RG
    w_all = jnp.concatenate(
        [w[:, :gl0], w[:, gl0 + GK_RANK:], w[:, gl0:gl0 + GK_RANK],
         jnp.zeros((D_MODEL, LANES - GK_RANK), w.dtype)], axis=1).astype(BF16)
    wup_pad = jnp.concatenate([w_gk_up[0], jnp.zeros((LANES - GK_RANK, GLA_KDIM), F32)], axis=0).astype(BF16)
    qa, ka, va, qg, kg, vg, la, rg, ga, gb = _proj(
        xp2, xs2, g_mix[0].reshape(1, D_MODEL), cos_tab, sin_tab, w_all, wup_pad,
        b_gk[0].reshape(1, GLA_KDIM), seq_len)

    oa_p = _swa_prompt(sinks[0], qa, ka, va, batch, seq_len)
    oa_s, nk_s, nv_s = _swa_sample(sinks[0], qa, ka, va, cache_swa_k[0], cache_swa_v[0], n_p,
                                   dec_batch, dec_seq)
    norm_row = jnp.tile(gla_norm[0], GLA_HEADS).reshape(1, GLA_VDIM)
    og_p, s_fin = _gla_prompt(qg, kg, la, vg, rg, norm_row, batch, seq_len)
    og_s, s_new = _gla_sample(qg, kg, la, vg, rg, norm_row, state_gla[0], n_p, dec_batch, dec_seq)

    x1 = _merge(xp2, xs2, oa_p, oa_s, og_p, og_s, ga, gb, w_branch_a[0].astype(BF16),
                w_branch_b[0].astype(BF16), w_out[0].astype(BF16))

    g_ffn_row = g_ffn[0].reshape(1, D_MODEL)
    rw_t = router_w[0].T
    rw_hi = rw_t.astype(BF16)
    rw_lo = (rw_t - rw_hi.astype(F32)).astype(BF16)
    eidx, rank, gate, cnt = _route(x1, g_ffn_row, rw_hi, rw_lo, router_b[0].reshape(N_EXPERTS, 1))
    counts = cnt[:, 0].astype(jnp.int32)
    block_expert, n_used, src, dst, gate_sorted, n_out_rows = _dispatch_tables(eidx, rank, gate, counts, n_all)
    slots = _ffn(block_expert, n_used, src, dst, gate_sorted, x1, g_ffn_row, w_up[0].astype(BF16),
                 b_up[0].reshape(N_EXPERTS, 1, 2 * D_FF), w_down[0].astype(BF16),
                 b_down[0].reshape(N_EXPERTS, 1, D_MODEL), n_out_rows)

    g_out = g_final.reshape(1, D_MODEL)
    y_p = _final(x1, slots, g_out, 0, n_p, n_all)
    y_s = _final(x1, slots, g_out, n_p, n_s, n_all)

    kv_shape = (1, -1, WINDOW, KV_HEADS, HEAD_DIM)
    new_k_p = ka[:n_p].reshape(batch, seq_len, KV_WIDTH)[:, -WINDOW:].reshape(kv_shape)
    new_v_p = va[:n_p].reshape(batch, seq_len, KV_WIDTH)[:, -WINDOW:].reshape(kv_shape)
    return (y_p.reshape(batch, seq_len, D_MODEL), y_s.reshape(dec_batch, dec_seq, D_MODEL),
            new_k_p, new_v_p, s_fin[None], nk_s.reshape(kv_shape), nv_s.reshape(kv_shape), s_new[None])
```

```python
import functools

import numpy as np
import jax
import jax.numpy as jnp
from jax import lax
from jax.experimental import pallas as pl
from jax.experimental.pallas import tpu as pltpu

D_MODEL = 1024
PAST_LEN = 8192
HEAD_DIM = 64
N_HEADS = 8
KV_HEADS = 2
GROUP = N_HEADS // KV_HEADS
WINDOW = 128
ROT_DIM = HEAD_DIM // 4
ROPE_THETA = 500000.0
ATT_WIDTH = N_HEADS * HEAD_DIM
KV_WIDTH = KV_HEADS * HEAD_DIM
GLA_HEADS = 4
GLA_KDIM = D_MODEL // 2
GLA_VDIM = D_MODEL
GLA_DK = GLA_KDIM // GLA_HEADS
GLA_DV = GLA_VDIM // GLA_HEADS
GK_RANK = 16
GK_NORMALIZER = 16.0
N_EXPERTS = 32
TOP_K = 4
D_FF = D_MODEL
SWIGLU_LIMIT = 7.0
SWIGLU_ALPHA = 1.702
EPS = 1e-5
NEG_INF = -1e30

LANES = 128
SUBLANES = 8
VMEM_LIMIT_BYTES = 56 * 1024 * 1024

ROW_TILE = 256
CHUNK = 128
FFN_BLOCK = 256

BF16 = jnp.bfloat16
F32 = jnp.float32

_C_QA, _C_KA, _C_VA, _C_QG, _C_KG, _C_VG, _C_RG, _C_GA, _C_GB, _C_GL, _C_END = (
    0, 512, 640, 768, 1280, 1792, 2816, 3840, 4864, 5888, 6016)


def _const_spec(shape):
    nd = len(shape)
    return pl.BlockSpec(shape, lambda *_: (0,) * nd)


def _params(sem, vmem=VMEM_LIMIT_BYTES):
    return pltpu.CompilerParams(dimension_semantics=sem, vmem_limit_bytes=vmem)


def _nt_dot(a, b):
    return lax.dot_general(a, b, (((1,), (1,)), ((), ())), preferred_element_type=F32)


def _dot(a, b):
    return jnp.dot(a, b, preferred_element_type=F32)


def _split3(x):
    hi = x.astype(BF16)
    r1 = x - hi.astype(F32)
    mid = r1.astype(BF16)
    lo = (r1 - mid.astype(F32)).astype(BF16)
    return hi, mid, lo


def _rope(x, cos_t, sin_t, n_rep):
    width = x.shape[1]
    cos_f = jnp.concatenate([cos_t] * n_rep, axis=1) if n_rep > 1 else cos_t
    sin_f = jnp.concatenate([sin_t] * n_rep, axis=1) if n_rep > 1 else sin_t
    lane = lax.broadcasted_iota(jnp.int32, x.shape, 1) % HEAD_DIM
    up = pltpu.roll(x, width - ROT_DIM // 2, 1)
    down = pltpu.roll(x, ROT_DIM // 2, 1)
    partner = jnp.where(lane < ROT_DIM // 2, up, down)
    return x * cos_f + partner * sin_f


def _proj_kernel(n_prompt_tiles, xp_ref, xs_ref, g_ref, cos_ref, sin_ref, w_ref, wup_ref, bgk_ref,
                 qa_ref, ka_ref, va_ref, qg_ref, kg_ref, vg_ref, la_ref, rg_ref, ga_ref, gb_ref):
    i = pl.program_id(0)
    x = jnp.where(i < n_prompt_tiles, xp_ref[...], xs_ref[...])
    ms = jnp.mean(x * x, axis=-1, keepdims=True)
    h = (x * lax.rsqrt(ms + EPS) * g_ref[...]).astype(BF16)
    cos_t = cos_ref[...]
    sin_t = sin_ref[...]

    def seg(a, b):
        return _dot(h, w_ref[:, a:b])

    qa = _rope(seg(_C_QA, _C_KA), cos_t, sin_t, ATT_WIDTH // LANES)
    qa_ref[...] = (qa * (HEAD_DIM ** -0.5)).astype(BF16)
    ka_ref[...] = _rope(seg(_C_KA, _C_VA), cos_t, sin_t, 1)
    va_ref[...] = seg(_C_VA, _C_QG)
    qg_ref[...] = seg(_C_QG, _C_KG) * (GLA_DK ** -0.5)
    kg_ref[...] = seg(_C_KG, _C_VG)
    vg_ref[...] = seg(_C_VG, _C_RG).astype(BF16)
    rg_ref[...] = seg(_C_RG, _C_GA).astype(BF16)
    ga_ref[...] = seg(_C_GA, _C_GB).astype(BF16)
    gb_ref[...] = seg(_C_GB, _C_GL).astype(BF16)
    gk_low = seg(_C_GL, _C_END).astype(BF16)
    z = _dot(gk_low, wup_ref[...]) + bgk_ref[...]
    log_sig = jnp.minimum(z, 0.0) - jnp.log1p(jnp.exp(-jnp.abs(z)))
    la_ref[...] = log_sig / GK_NORMALIZER


def _proj(xp2, xs2, g_mix, cos_tab, sin_tab, w_all, wup_pad, b_gk, seq_len):
    n_p, n_s = xp2.shape[0], xs2.shape[0]
    n_all = n_p + n_s
    npt, nst = n_p // ROW_TILE, n_s // ROW_TILE
    tiles_per_seq = seq_len // ROW_TILE

    def tab_map(i):
        return (jnp.where(i < npt, i % tiles_per_seq, tiles_per_seq), 0)

    row = lambda w: pl.BlockSpec((ROW_TILE, w), lambda i: (i, 0))
    widths = [(ATT_WIDTH, BF16), (KV_WIDTH, F32), (KV_WIDTH, F32), (GLA_KDIM, F32), (GLA_KDIM, F32),
              (GLA_VDIM, BF16), (GLA_KDIM, F32), (GLA_VDIM, BF16), (D_MODEL, BF16), (D_MODEL, BF16)]
    return pl.pallas_call(
        functools.partial(_proj_kernel, npt),
        grid=(npt + nst,),
        in_specs=[
            pl.BlockSpec((ROW_TILE, D_MODEL), lambda i: (jnp.minimum(i, npt - 1), 0)),
            pl.BlockSpec((ROW_TILE, D_MODEL), lambda i: (jnp.maximum(i - npt, 0), 0)),
            _const_spec((1, D_MODEL)),
            pl.BlockSpec((ROW_TILE, LANES), tab_map),
            pl.BlockSpec((ROW_TILE, LANES), tab_map),
            _const_spec(w_all.shape),
            _const_spec(wup_pad.shape),
            _const_spec((1, GLA_KDIM)),
        ],
        out_specs=[row(w) for w, _ in widths],
        out_shape=[jax.ShapeDtypeStruct((n_all, w), dt) for w, dt in widths],
        compiler_params=_params(("arbitrary",)),
        name="proj",
    )(xp2, xs2, g_mix, cos_tab, sin_tab, w_all, wup_pad, b_gk)


def _pair_blocks(kk):
    lane = lax.broadcasted_iota(jnp.int32, kk.shape, 1)
    lo = lane < HEAD_DIM
    swapped = pltpu.roll(kk, HEAD_DIM, 1)
    zero = jnp.zeros_like(kk)
    blocks = []
    for kh in range(KV_HEADS):
        left = jnp.where(lo, kk if kh == 0 else swapped, zero)
        right = jnp.where(lo, zero, swapped if kh == 0 else kk)
        blocks.append(jnp.concatenate([left, right], axis=0).astype(BF16))
    return blocks


def _sink_softmax(s, valid, sink):
    s = jnp.where(valid, s, NEG_INF)
    m = jnp.maximum(jnp.max(s, axis=-1, keepdims=True), sink)
    p = jnp.exp(s - m)
    denom = jnp.sum(p, axis=-1, keepdims=True) + jnp.exp(sink - m)
    return (p * (1.0 / denom)).astype(BF16)


def _attend(q, kk, vv, valid, sink_ref, o_ref):
    rows, keys = valid.shape
    kblocks = _pair_blocks(kk)
    vblocks = _pair_blocks(vv)
    for kh in range(KV_HEADS):
        base = kh * GROUP * HEAD_DIM
        qq = jnp.concatenate([q[:, base:base + LANES], q[:, base + LANES:base + 2 * LANES]], axis=0)
        s = _nt_dot(qq, kblocks[kh])
        for r in range(2):
            probs = []
            for c in range(2):
                head = kh * GROUP + 2 * r + c
                probs.append(_sink_softmax(s[r * rows:(r + 1) * rows, c * keys:(c + 1) * keys],
                                           valid, sink_ref[head]))
            p = jnp.concatenate(probs, axis=1)
            o_ref[:, base + r * LANES:base + (r + 1) * LANES] = _dot(p, vblocks[kh]).astype(o_ref.dtype)


def _swa_prompt_kernel(sink_ref, q_ref, kp_ref, kc_ref, vp_ref, vc_ref, o_ref):
    j = pl.program_id(1)
    kk = jnp.concatenate([kp_ref[...], kc_ref[...]], axis=0)
    vv = jnp.concatenate([vp_ref[...], vc_ref[...]], axis=0)
    row = lax.broadcasted_iota(jnp.int32, (WINDOW, 2 * WINDOW), 0)
    col = lax.broadcasted_iota(jnp.int32, (WINDOW, 2 * WINDOW), 1)
    valid = (col > row) & (col <= row + WINDOW) & ((j > 0) | (col >= WINDOW))
    _attend(q_ref[...], kk, vv, valid, sink_ref, o_ref)


def _swa_prompt(sinks, qa, ka, va, batch, seq_len):
    nb = seq_len // WINDOW
    cur = lambda b, j, s: (b * nb + j, 0)
    prev = lambda b, j, s: (b * nb + jnp.maximum(j - 1, 0), 0)
    return pl.pallas_call(
        _swa_prompt_kernel,
        grid_spec=pltpu.PrefetchScalarGridSpec(
            num_scalar_prefetch=1,
            grid=(batch, nb),
            in_specs=[
                pl.BlockSpec((WINDOW, ATT_WIDTH), cur),
                pl.BlockSpec((WINDOW, KV_WIDTH), prev),
                pl.BlockSpec((WINDOW, KV_WIDTH), cur),
                pl.BlockSpec((WINDOW, KV_WIDTH), prev),
                pl.BlockSpec((WINDOW, KV_WIDTH), cur),
            ],
            out_specs=pl.BlockSpec((WINDOW, ATT_WIDTH), cur),
        ),
        out_shape=jax.ShapeDtypeStruct((batch * seq_len, ATT_WIDTH), BF16),
        compiler_params=_params(("arbitrary", "arbitrary")),
        name="swa_prompt",
    )(sinks, qa, ka, ka, va, va)


SAMPLE_GROUP = 16


def _swa_sample_kernel(dec_seq, sink_ref, q_ref, kn_ref, vn_ref, ck_ref, cv_ref, o_ref, nk_ref, nv_ref):
    rows = SAMPLE_GROUP * dec_seq
    ck = ck_ref[...]
    cv = cv_ref[...]
    kn = kn_ref[...]
    vn = vn_ref[...]
    nk_ref[:, :WINDOW - dec_seq, :] = ck[:, dec_seq:, :]
    nv_ref[:, :WINDOW - dec_seq, :] = cv[:, dec_seq:, :]
    nk_ref[:, WINDOW - dec_seq:, :] = kn.reshape(SAMPLE_GROUP, dec_seq, KV_WIDTH)
    nv_ref[:, WINDOW - dec_seq:, :] = vn.reshape(SAMPLE_GROUP, dec_seq, KV_WIDTH)
    n_cache = SAMPLE_GROUP * WINDOW
    kk = jnp.concatenate([ck.reshape(n_cache, KV_WIDTH), kn], axis=0)
    vv = jnp.concatenate([cv.reshape(n_cache, KV_WIDTH), vn], axis=0)
    keys = n_cache + rows
    row = lax.broadcasted_iota(jnp.int32, (rows, keys), 0)
    col = lax.broadcasted_iota(jnp.int32, (rows, keys), 1)
    q_b, q_s = row // dec_seq, row % dec_seq
    is_cache = col < n_cache
    new = col - n_cache
    valid_cache = (col // WINDOW == q_b) & (col % WINDOW > q_s)
    valid_new = (new // dec_seq == q_b) & (new % dec_seq <= q_s)
    valid = (is_cache & valid_cache) | (jnp.logical_not(is_cache) & valid_new)
    _attend(q_ref[...], kk, vv, valid, sink_ref, o_ref)


def _swa_sample(sinks, qa, ka, va, cache_k, cache_v, n_prompt_rows, dec_batch, dec_seq):
    rows = SAMPLE_GROUP * dec_seq
    off = n_prompt_rows // rows
    tok = lambda g, s: (off + g, 0)
    cache = lambda g, s: (g, 0, 0)
    cshape = (dec_batch, WINDOW, KV_WIDTH)
    return pl.pallas_call(
        functools.partial(_swa_sample_kernel, dec_seq),
        grid_spec=pltpu.PrefetchScalarGridSpec(
            num_scalar_prefetch=1,
            grid=(dec_batch // SAMPLE_GROUP,),
            in_specs=[
                pl.BlockSpec((rows, ATT_WIDTH), tok),
                pl.BlockSpec((rows, KV_WIDTH), tok),
                pl.BlockSpec((rows, KV_WIDTH), tok),
                pl.BlockSpec((SAMPLE_GROUP, WINDOW, KV_WIDTH), cache),
                pl.BlockSpec((SAMPLE_GROUP, WINDOW, KV_WIDTH), cache),
            ],
            out_specs=[
                pl.BlockSpec((rows, ATT_WIDTH), lambda g, s: (g, 0)),
                pl.BlockSpec((SAMPLE_GROUP, WINDOW, KV_WIDTH), cache),
                pl.BlockSpec((SAMPLE_GROUP, WINDOW, KV_WIDTH), cache),
            ],
        ),
        out_shape=[jax.ShapeDtypeStruct((dec_batch * dec_seq, ATT_WIDTH), BF16),
                   jax.ShapeDtypeStruct(cshape, F32), jax.ShapeDtypeStruct(cshape, F32)],
        compiler_params=_params(("arbitrary",)),
        name="swa_sample",
    )(sinks, qa, ka, va, cache_k.reshape(cshape), cache_v.reshape(cshape))


def _chunk_tables(seg):
    n_lev = int(np.log2(seg))
    t = np.arange(CHUNK)
    seg_start = (t // seg) * seg
    u = np.arange(CHUNK)[None, :]

    def prefix(end):
        return ((u >= seg_start[:, None]) & (u <= end[:, None])).astype(np.float32)

    blocks = [prefix(t), prefix(seg_start + seg - 1)]
    for d in range(n_lev):
        m = 1 << d
        ref = (t >> (d + 1) << (d + 1)) + m - 1
        blocks.append(prefix(ref))
    lhs = np.concatenate(blocks, axis=0)
    lhs3 = np.concatenate([lhs, lhs, lhs], axis=1)
    tt, ss = t[:, None], t[None, :]
    x = tt ^ ss
    lev = np.where(x > 0, np.floor(np.log2(np.maximum(x, 1))).astype(np.int32), n_lev)
    lev = np.where((ss > tt) | (tt // seg != ss // seg), -1, lev)
    lev = np.where(tt == ss, n_lev, lev)
    return jnp.asarray(lhs3, BF16), jnp.asarray(lev, jnp.int32), n_lev


def _gla_chunk_terms(q, k, la, lhs3, level, n_lev):
    hi, mid, lo = _split3(la)
    rhs = jnp.concatenate([hi, mid, lo], axis=0)
    sums = _dot(lhs3, rhs)
    b = sums[0:CHUNK]
    b_last = sums[CHUNK:2 * CHUNK]
    q_main = (q * jnp.exp(b)).astype(BF16)
    k_upd = k * jnp.exp(b_last - b)
    q_lev = [None] * n_lev
    k_lev = [None] * n_lev
    for d in range(n_lev):
        ref = sums[(2 + d) * CHUNK:(3 + d) * CHUNK]
        q_lev[d] = (q * jnp.exp(jnp.minimum(b - ref, 0.0))).astype(BF16)
        k_lev[d] = (k * jnp.exp(jnp.minimum(ref - b, 0.0))).astype(BF16)
    q_b, k_b = q.astype(BF16), k.astype(BF16)

    def att(h):
        hs = slice(h * GLA_DK, (h + 1) * GLA_DK)
        acc = jnp.where(level == n_lev, _nt_dot(q_b[:, hs], k_b[:, hs]), 0.0)
        for d in range(n_lev):
            acc = jnp.where(level == d, _nt_dot(q_lev[d][:, hs], k_lev[d][:, hs]), acc)
        return acc

    return q_main, k_upd, att, b_last


def _gla_out(o, r, norm):
    parts = []
    for h in range(GLA_HEADS):
        oh = o[:, h * GLA_DV:(h + 1) * GLA_DV]
        ms = jnp.mean(oh * oh, axis=-1, keepdims=True)
        parts.append(oh * lax.rsqrt(ms + EPS))
    y = jnp.concatenate(parts, axis=1) * norm
    rf = r.astype(F32)
    return y * (rf * jax.nn.sigmoid(rf))


def _gla_prompt_kernel(n_lev, q_ref, k_ref, la_ref, v_ref, r_ref, norm_ref, lhs_ref, lev_ref,
                       o_ref, sfin_ref, s_scr):
    c = pl.program_id(1)

    @pl.when(c == 0)
    def _():
        s_scr[...] = jnp.zeros_like(s_scr)

    q_main, k_upd, att, b_last = _gla_chunk_terms(q_ref[...], k_ref[...], la_ref[...], lhs_ref[...],
                                                  lev_ref[...], n_lev)
    v = v_ref[...]
    outs = []
    for h in range(GLA_HEADS):
        hs = slice(h * GLA_DK, (h + 1) * GLA_DK)
        vh = v[:, h * GLA_DV:(h + 1) * GLA_DV]
        s0 = s_scr[h]
        o_h = _dot(q_main[:, hs], s0.astype(BF16)) + _dot(att(h).astype(BF16), vh)
        outs.append(o_h)
        decay = jnp.exp(b_last[:, hs]).T
        k_t = k_upd[:, hs].T.astype(BF16)
        s_scr[h] = jnp.concatenate([decay, decay], axis=1) * s0 + _dot(k_t, vh)
    o_ref[...] = _gla_out(jnp.concatenate(outs, axis=1), r_ref[...], norm_ref[...]).astype(o_ref.dtype)

    @pl.when(c == pl.num_programs(1) - 1)
    def _():
        sfin_ref[0] = s_scr[...]


def _gla_prompt(qg, kg, la, vg, rg, norm_row, batch, seq_len):
    nc = seq_len // CHUNK
    lhs3, level, n_lev = _chunk_tables(CHUNK)
    tok = lambda w: pl.BlockSpec((CHUNK, w), lambda b, c: (b * nc + c, 0))
    return pl.pallas_call(
        functools.partial(_gla_prompt_kernel, n_lev),
        grid=(batch, nc),
        in_specs=[tok(GLA_KDIM), tok(GLA_KDIM), tok(GLA_KDIM), tok(GLA_VDIM), tok(GLA_VDIM),
                  _const_spec((1, GLA_VDIM)), _const_spec(lhs3.shape), _const_spec(level.shape)],
        out_specs=[tok(GLA_VDIM),
                   pl.BlockSpec((1, GLA_HEADS, GLA_DK, GLA_DV), lambda b, c: (b, 0, 0, 0))],
        out_shape=[jax.ShapeDtypeStruct((batch * seq_len, GLA_VDIM), BF16),
                   jax.ShapeDtypeStruct((batch, GLA_HEADS, GLA_DK, GLA_DV), F32)],
        scratch_shapes=[pltpu.VMEM((GLA_HEADS, GLA_DK, GLA_DV), F32)],
        compiler_params=_params(("arbitrary", "arbitrary")),
        name="gla_prompt",
    )(qg, kg, la, vg, rg, norm_row, lhs3, level)


def _gla_sample_kernel(n_lev, dec_seq, q_ref, k_ref, la_ref, v_ref, r_ref, norm_ref, lhs_ref, lev_ref,
                       s0_ref, o_ref, snew_ref):
    q_main, k_upd, att, b_last = _gla_chunk_terms(q_ref[...], k_ref[...], la_ref[...], lhs_ref[...],
                                                  lev_ref[...], n_lev)
    v = v_ref[...]
    n_b = CHUNK // dec_seq
    row_b = lax.broadcasted_iota(jnp.int32, (CHUNK, GLA_DK), 0) // dec_seq
    col_b = lax.broadcasted_iota(jnp.int32, (GLA_DK, CHUNK), 1) // dec_seq
    outs = []
    for h in range(GLA_HEADS):
        hs = slice(h * GLA_DK, (h + 1) * GLA_DK)
        vh = v[:, h * GLA_DV:(h + 1) * GLA_DV]
        qm = q_main[:, hs]
        decay_t = jnp.exp(b_last[:, hs]).T
        k_t = k_upd[:, hs].T.astype(BF16)
        o_h = _dot(att(h).astype(BF16), vh)
        for bi in range(n_b):
            s0 = s0_ref[bi, h]
            o_h = o_h + _dot(jnp.where(row_b == bi, qm, jnp.zeros_like(qm)), s0.astype(BF16))
            decay = jnp.broadcast_to(decay_t[:, bi * dec_seq:bi * dec_seq + 1], (GLA_DK, GLA_DV))
            k_b = jnp.where(col_b == bi, k_t, jnp.zeros_like(k_t))
            snew_ref[bi, h] = decay * s0 + _dot(k_b, vh)
        outs.append(o_h)
    o_ref[...] = _gla_out(jnp.concatenate(outs, axis=1), r_ref[...], norm_ref[...]).astype(o_ref.dtype)


def _gla_sample(qg, kg, la, vg, rg, norm_row, state, n_prompt_rows, dec_batch, dec_seq):
    n_b = CHUNK // dec_seq
    off = n_prompt_rows // CHUNK
    lhs3, level, n_lev = _chunk_tables(dec_seq)
    tok = lambda w: pl.BlockSpec((CHUNK, w), lambda g: (off + g, 0))
    st = pl.BlockSpec((n_b, GLA_HEADS, GLA_DK, GLA_DV), lambda g: (g, 0, 0, 0))
    return pl.pallas_call(
        functools.partial(_gla_sample_kernel, n_lev, dec_seq),
        grid=(dec_batch // n_b,),
        in_specs=[tok(GLA_KDIM), tok(GLA_KDIM), tok(GLA_KDIM), tok(GLA_VDIM), tok(GLA_VDIM),
                  _const_spec((1, GLA_VDIM)), _const_spec(lhs3.shape), _const_spec(level.shape), st],
        out_specs=[pl.BlockSpec((CHUNK, GLA_VDIM), lambda g: (g, 0)), st],
        out_shape=[jax.ShapeDtypeStruct((dec_batch * dec_seq, GLA_VDIM), BF16),
                   jax.ShapeDtypeStruct(state.shape, F32)],
        compiler_params=_params(("arbitrary",)),
        name="gla_sample",
    )(qg, kg, la, vg, rg, norm_row, lhs3, level, state)


def _merge_kernel(n_prompt_tiles, xp_ref, xs_ref, oap_ref, oas_ref, ogp_ref, ogs_ref, ga_ref, gb_ref,
                  wa_ref, wb_ref, wo_ref, x1_ref):
    i = pl.program_id(0)
    is_p = i < n_prompt_tiles
    x = jnp.where(is_p, xp_ref[...], xs_ref[...])
    oa = jnp.where(is_p, oap_ref[...], oas_ref[...])
    og = jnp.where(is_p, ogp_ref[...], ogs_ref[...])
    m = (jax.nn.sigmoid(ga_ref[...].astype(F32)) * _dot(oa, wa_ref[...])
         + jax.nn.sigmoid(gb_ref[...].astype(F32)) * _dot(og, wb_ref[...]))
    x1_ref[...] = x + _dot(m.astype(BF16), wo_ref[...])


def _merge(xp2, xs2, oa_p, oa_s, og_p, og_s, ga, gb, wa, wb, wo):
    n_p, n_s = xp2.shape[0], xs2.shape[0]
    npt, nst = n_p // ROW_TILE, n_s // ROW_TILE
    p_map = lambda i: (jnp.minimum(i, npt - 1), 0)
    s_map = lambda i: (jnp.maximum(i - npt, 0), 0)
    row = lambda w: pl.BlockSpec((ROW_TILE, w), lambda i: (i, 0))
    return pl.pallas_call(
        functools.partial(_merge_kernel, npt),
        grid=(npt + nst,),
        in_specs=[
            pl.BlockSpec((ROW_TILE, D_MODEL), p_map), pl.BlockSpec((ROW_TILE, D_MODEL), s_map),
            pl.BlockSpec((ROW_TILE, ATT_WIDTH), p_map), pl.BlockSpec((ROW_TILE, ATT_WIDTH), s_map),
            pl.BlockSpec((ROW_TILE, GLA_VDIM), p_map), pl.BlockSpec((ROW_TILE, GLA_VDIM), s_map),
            row(D_MODEL), row(D_MODEL),
            _const_spec(wa.shape), _const_spec(wb.shape), _const_spec(wo.shape),
        ],
        out_specs=row(D_MODEL),
        out_shape=jax.ShapeDtypeStruct((n_p + n_s, D_MODEL), F32),
        compiler_params=_params(("arbitrary",)),
        name="merge",
    )(xp2, xs2, oa_p, oa_s, og_p, og_s, ga, gb, wa, wb, wo)


def _route_kernel(x1_ref, gffn_ref, rwh_ref, rwl_ref, rb_ref, upper_ref, eidx_ref, rank_ref, gate_ref,
                  cnt_ref, cnt_scr):
    i = pl.program_id(0)

    @pl.when(i == 0)
    def _():
        cnt_scr[...] = jnp.zeros_like(cnt_scr)

    x1 = x1_ref[...]
    ms = jnp.mean(x1 * x1, axis=-1, keepdims=True)
    h2 = x1 * lax.rsqrt(ms + EPS) * gffn_ref[...]
    h_hi = h2.astype(BF16)
    h_lo = (h2 - h_hi.astype(F32)).astype(BF16)
    rwh, rwl = rwh_ref[...], rwl_ref[...]
    logits = _nt_dot(rwh, h_hi) + _nt_dot(rwl, h_hi) + _nt_dot(rwh, h_lo) + rb_ref[...]
    eid = lax.broadcasted_iota(jnp.int32, logits.shape, 0)
    upper = upper_ref[...]
    base = cnt_scr[...]
    vals, rows_e, rows_r = [], [], []
    lg = logits
    for _ in range(TOP_K):
        mx = jnp.max(lg, axis=0, keepdims=True)
        sel = jnp.min(jnp.where(lg == mx, eid, N_EXPERTS), axis=0, keepdims=True)
        onehot = eid == sel
        oh = onehot.astype(F32)
        before = _dot(onehot.astype(BF16), upper)
        rows_r.append(jnp.sum(oh * (base + before), axis=0, keepdims=True))
        base = base + jnp.sum(oh, axis=1, keepdims=True)
        vals.append(mx)
        rows_e.append(sel)
        lg = jnp.where(onehot, -jnp.inf, lg)
    cnt_scr[...] = base
    ex = [jnp.exp(v - vals[0]) for v in vals]
    inv = 1.0 / (ex[0] + ex[1] + ex[2] + ex[3])
    eidx_ref[...] = jnp.concatenate(rows_e, axis=0)
    rank_ref[...] = jnp.concatenate(rows_r, axis=0).astype(jnp.int32)
    gate_ref[...] = jnp.concatenate([e * inv for e in ex], axis=0)
    cnt_ref[...] = jnp.broadcast_to(base, cnt_ref.shape)


def _route(x1, g_ffn, rw_hi, rw_lo, rb_col):
    n = x1.shape[0]
    upper = jnp.asarray(np.triu(np.ones((ROW_TILE, ROW_TILE), np.float32), 1), BF16)
    col = lambda dt: (pl.BlockSpec((TOP_K, ROW_TILE), lambda i: (0, i)), jax.ShapeDtypeStruct((TOP_K, n), dt))
    outs = [col(jnp.int32), col(jnp.int32), col(F32),
            (_const_spec((N_EXPERTS, LANES)), jax.ShapeDtypeStruct((N_EXPERTS, LANES), F32))]
    return pl.pallas_call(
        _route_kernel,
        grid=(n // ROW_TILE,),
        in_specs=[pl.BlockSpec((ROW_TILE, D_MODEL), lambda i: (i, 0)), _const_spec((1, D_MODEL)),
                  _const_spec(rw_hi.shape), _const_spec(rw_lo.shape), _const_spec((N_EXPERTS, 1)),
                  _const_spec(upper.shape)],
        out_specs=[o[0] for o in outs],
        out_shape=[o[1] for o in outs],
        scratch_shapes=[pltpu.VMEM((N_EXPERTS, 1), F32)],
        compiler_params=_params(("arbitrary",)),
        name="route",
    )(x1, g_ffn, rw_hi, rw_lo, rb_col, upper)


_INV_FIELDS = 6


def _split3_exact(x):
    def trunc(v):
        return pltpu.bitcast(pltpu.bitcast(v, jnp.uint32) & jnp.uint32(0xFFFF0000), F32)
    hi = trunc(x)
    rest = x - hi
    mid = trunc(rest)
    return hi, mid, rest - mid


def _invert_kernel(n_blocks, eidx_ref, rank_ref, gate_ref, pstart_ref, acc_ref):
    i = pl.program_id(0)

    @pl.when(i == 0)
    def _():
        acc_ref[...] = jnp.zeros(acc_ref.shape, acc_ref.dtype)

    eidx, rank, gate = eidx_ref[...], rank_ref[...], gate_ref[...]
    pstart = pstart_ref[...]
    rows = eidx.shape[1]
    eid = lax.broadcasted_iota(jnp.int32, (N_EXPERTS, rows), 0)
    blk_id = lax.broadcasted_iota(jnp.int32, (n_blocks, rows), 0)
    off_id = lax.broadcasted_iota(jnp.int32, (FFN_BLOCK, rows), 0)
    tok = i * rows + lax.broadcasted_iota(jnp.int32, (1, rows), 1)
    a_parts, b_parts = [], []
    shift = FFN_BLOCK.bit_length() - 1
    tok_lo = (tok & 255).astype(F32)
    tok_hi = lax.shift_right_logical(tok, 8).astype(F32)
    for k in range(TOP_K):
        base = jnp.sum(jnp.where(eid == eidx[k:k + 1], pstart, 0), axis=0, keepdims=True)
        pos = base + rank[k:k + 1]
        a_parts.append(jnp.where(blk_id == lax.shift_right_logical(pos, shift), 1.0, 0.0).astype(BF16))
        hit = off_id == (pos & (FFN_BLOCK - 1))
        g_hi, g_mid, g_lo = _split3_exact(gate[k:k + 1])
        fields = [tok_lo, tok_hi, jnp.full((1, rows), k + 1.0, F32), g_hi, g_mid, g_lo]
        b_parts.append([jnp.where(hit, f, 0.0).astype(BF16) for f in fields])
    a = jnp.concatenate(a_parts, axis=1)
    for j in range(_INV_FIELDS):
        b = jnp.concatenate([b_parts[k][j] for k in range(TOP_K)], axis=1)
        acc_ref[j] += _nt_dot(a, b)


def _invert(eidx, rank, gate, pad_start, n_blocks):
    n_all = eidx.shape[1]
    assert n_all // 256 < 256, "token index is carried as two byte-sized fields"
    col = pl.BlockSpec((TOP_K, ROW_TILE), lambda i: (0, i))
    shape = (_INV_FIELDS, n_blocks, FFN_BLOCK)
    f = pl.pallas_call(
        functools.partial(_invert_kernel, n_blocks),
        grid=(n_all // ROW_TILE,),
        in_specs=[col, col, col, _const_spec((N_EXPERTS, 1))],
        out_specs=_const_spec(shape),
        out_shape=jax.ShapeDtypeStruct(shape, F32),
        compiler_params=_params(("arbitrary",)),
        name="invert",
    )(eidx, rank, gate, pad_start)
    t = (f[0] + 256.0 * f[1]).astype(jnp.int32)
    kp1 = f[2].astype(jnp.int32)
    blk = lax.broadcasted_iota(jnp.int32, t.shape, 0)
    off = lax.broadcasted_iota(jnp.int32, t.shape, 1)
    dump = TOP_K * n_all + (blk & 1) * FFN_BLOCK + off
    return t, jnp.where(kp1 > 0, (kp1 - 1) * n_all + t, dump), (f[3] + f[4]) + f[5]


def _ffn_kernel(be_ref, nused_ref, src_cur_ref, src_nxt_ref, dst_cur_ref, dst_prev_ref, gate_ref, gffn_ref,
                wup_ref, bup_ref, wdn_ref, bdn_ref, x1_hbm, out_hbm, xbuf0, xbuf1, obuf0, obuf1, wup_bf, wdn_bf,
                gsem, ssem):
    i = pl.program_id(0)
    n_used = nused_ref[0]
    xbufs, obufs = (xbuf0, xbuf1), (obuf0, obuf1)

    def gather_copy(src_ref, r, s):
        return pltpu.make_async_copy(x1_hbm.at[pl.ds(src_ref[0, 0, r], 1), :],
                                     xbufs[s].at[pl.ds(r, 1), :], gsem.at[s])

    def scatter_copy(dst_ref, r, s):
        return pltpu.make_async_copy(obufs[s].at[pl.ds(r, 1), :],
                                     out_hbm.at[pl.ds(dst_ref[0, 0, r], 1), :], ssem.at[s])

    def start_rows(copy_fn):
        for r in range(FFN_BLOCK):
            copy_fn(r).start()

    def wait_rows(copy_fn):
        for r in range(FFN_BLOCK):
            copy_fn(r).wait()

    first = i == 0
    changed = first | (be_ref[i] != be_ref[jnp.maximum(i - 1, 0)])

    @pl.when(first)
    def _():
        obuf0[...] = jnp.zeros(obuf0.shape, obuf0.dtype)
        obuf1[...] = jnp.zeros(obuf1.shape, obuf1.dtype)
        n_slot_rows = out_hbm.shape[0] - 3 * FFN_BLOCK
        init = pltpu.make_async_copy(obuf1, out_hbm.at[pl.ds(n_slot_rows + FFN_BLOCK, FFN_BLOCK), :], ssem.at[1])
        init.start()
        init.wait()
        start_rows(lambda r: pltpu.make_async_copy(
            obuf0.at[pl.ds(r, 1), :], out_hbm.at[pl.ds(n_slot_rows + r, 1), :], ssem.at[0]))
        start_rows(lambda r: gather_copy(src_cur_ref, r, 0))

    @pl.when(changed & (i < n_used))
    def _():
        wup_bf[...] = wup_ref[0].astype(BF16)
        wdn_bf[...] = wdn_ref[0].astype(BF16)

    def step(s):
        wait_rows(lambda r: gather_copy(src_cur_ref, r, s))
        x1 = xbufs[s][...]
        ms = jnp.mean(x1 * x1, axis=-1, keepdims=True)
        x = (x1 * lax.rsqrt(ms + EPS) * gffn_ref[...]).astype(BF16)
        start_rows(lambda r: gather_copy(src_nxt_ref, r, 1 - s))
        start_rows(lambda r: scatter_copy(dst_prev_ref, r, 1 - s))
        hu = _dot(x, wup_bf[...]) + bup_ref[0]
        glu = jnp.minimum(hu[:, :D_FF], SWIGLU_LIMIT)
        lin = jnp.clip(hu[:, D_FF:], -SWIGLU_LIMIT, SWIGLU_LIMIT)
        act = glu * jax.nn.sigmoid(SWIGLU_ALPHA * glu) * (lin + 1.0)
        out = (_dot(act.astype(BF16), wdn_bf[...]) + bdn_ref[0]) * gate_ref[...]
        wait_rows(lambda r: scatter_copy(dst_cur_ref, r, s))
        obufs[s][...] = out

    def drain(s):
        start_rows(lambda r: scatter_copy(dst_cur_ref, r, s))
        wait_rows(lambda r: scatter_copy(dst_cur_ref, r, 1 - s))
        wait_rows(lambda r: scatter_copy(dst_cur_ref, r, s))
        wait_rows(lambda r: gather_copy(src_cur_ref, r, 1 - s))

    for s in range(2):
        pl.when((i < n_used) & (i % 2 == s))(functools.partial(step, s))
    for s in range(2):
        pl.when((i == n_used - 1) & (i % 2 == s))(functools.partial(drain, s))


def _ffn(block_expert, n_used, src, dst, gate_sorted, x1, g_ffn, w_up, b_up, w_down, b_down, n_out_rows):
    n_blocks = block_expert.shape[0]
    smem_block = lambda fn: pl.BlockSpec((1, 1, FFN_BLOCK), fn, memory_space=pltpu.SMEM)
    cur = smem_block(lambda i, be, nu: (i, 0, 0))
    nxt = smem_block(lambda i, be, nu: (jnp.minimum(i + 1, n_blocks - 1), 0, 0))
    prev = smem_block(lambda i, be, nu: (jnp.where(i == 0, n_blocks, i - 1), 0, 0))
    ex3 = lambda i, be, nu: (be[i], 0, 0)
    return pl.pallas_call(
        _ffn_kernel,
        grid_spec=pltpu.PrefetchScalarGridSpec(
            num_scalar_prefetch=2,
            grid=(n_blocks,),
            in_specs=[
                cur, nxt, cur, prev,
                pl.BlockSpec((FFN_BLOCK, 1), lambda i, be, nu: (i, 0)),
                pl.BlockSpec((1, D_MODEL), lambda i, be, nu: (0, 0)),
                pl.BlockSpec((1, D_MODEL, 2 * D_FF), ex3),
                pl.BlockSpec((1, 1, 2 * D_FF), ex3),
                pl.BlockSpec((1, D_FF, D_MODEL), ex3),
                pl.BlockSpec((1, 1, D_MODEL), ex3),
                pl.BlockSpec(memory_space=pl.ANY),
            ],
            out_specs=pl.BlockSpec(memory_space=pl.ANY),
            scratch_shapes=[pltpu.VMEM((FFN_BLOCK, D_MODEL), F32), pltpu.VMEM((FFN_BLOCK, D_MODEL), F32),
                            pltpu.VMEM((FFN_BLOCK, D_MODEL), F32), pltpu.VMEM((FFN_BLOCK, D_MODEL), F32),
                            pltpu.VMEM((D_MODEL, 2 * D_FF), BF16), pltpu.VMEM((D_FF, D_MODEL), BF16),
                            pltpu.SemaphoreType.DMA((2,)), pltpu.SemaphoreType.DMA((2,))],
        ),
        out_shape=jax.ShapeDtypeStruct((n_out_rows, D_MODEL), F32),
        compiler_params=_params(("arbitrary",)),
        name="ffn",
    )(block_expert, n_used, src, src, dst, dst, gate_sorted, g_ffn, w_up, b_up, w_down, b_down, x1)


def _final_kernel(x1_ref, s0_ref, s1_ref, s2_ref, s3_ref, g_ref, y_ref):
    x2 = x1_ref[...] + ((s0_ref[...] + s1_ref[...]) + (s2_ref[...] + s3_ref[...]))
    ms = jnp.mean(x2 * x2, axis=-1, keepdims=True)
    y_ref[...] = x2 * lax.rsqrt(ms + EPS) * g_ref[...]


def _final(x1, slots, g_final, row0, n_rows, n_all):
    t0 = row0 // ROW_TILE
    per_slot = n_all // ROW_TILE
    slot = lambda k: pl.BlockSpec((ROW_TILE, D_MODEL), lambda i: (k * per_slot + t0 + i, 0))
    return pl.pallas_call(
        _final_kernel,
        grid=(n_rows // ROW_TILE,),
        in_specs=[pl.BlockSpec((ROW_TILE, D_MODEL), lambda i: (t0 + i, 0)),
                  slot(0), slot(1), slot(2), slot(3), _const_spec((1, D_MODEL))],
        out_specs=pl.BlockSpec((ROW_TILE, D_MODEL), lambda i: (i, 0)),
        out_shape=jax.ShapeDtypeStruct((n_rows, D_MODEL), F32),
        compiler_params=_params(("arbitrary",)),
        name="final",
    )(x1, slots, slots, slots, slots, g_final)


def _rope_tables(seq_len, dec_seq):
    half = ROT_DIM // 2
    inv = ROPE_THETA ** (-jnp.arange(0, ROT_DIM, 2, dtype=F32) / ROT_DIM)
    pos_p = jnp.arange(seq_len, dtype=jnp.int32)
    pos_s = PAST_LEN + (jnp.arange(ROW_TILE, dtype=jnp.int32) % dec_seq)
    pos = jnp.concatenate([pos_p, pos_s]).astype(F32)
    ang = pos[:, None] * inv[None, :]
    cos, sin = jnp.cos(ang), jnp.sin(ang)
    ones = jnp.ones((pos.shape[0], HEAD_DIM - ROT_DIM), F32)
    cos_h = jnp.concatenate([cos, cos, ones], axis=1)
    sin_h = jnp.concatenate([-sin, sin, 0.0 * ones], axis=1)
    return jnp.concatenate([cos_h, cos_h], axis=1), jnp.concatenate([sin_h, sin_h], axis=1)


def _block_tables(counts, n_all):
    n_slots = TOP_K * n_all
    n_blocks = -(-n_slots // FFN_BLOCK) + N_EXPERTS
    padded = (counts + FFN_BLOCK - 1) // FFN_BLOCK * FFN_BLOCK
    pad_end = jnp.cumsum(padded)
    pad_start = (pad_end - padded).astype(jnp.int32).reshape(N_EXPERTS, 1)
    block_start = jnp.arange(n_blocks, dtype=jnp.int32) * FFN_BLOCK
    block_expert = jnp.minimum(jnp.sum(pad_end[None, :] <= block_start[:, None], axis=1), N_EXPERTS - 1)
    n_used = (pad_end[-1] // FFN_BLOCK).astype(jnp.int32).reshape(1)
    return pad_start, block_expert.astype(jnp.int32), n_used, n_blocks


def kernel(x_prompt, x_sample, cache_swa_k, cache_swa_v, state_gla, g_mix, w_in, w_gk_up, b_gk, sinks,
           gla_norm, w_branch_a, w_branch_b, w_out, g_ffn, router_w, router_b, w_up, b_up, w_down, b_down,
           g_final):
    batch, seq_len, _ = x_prompt.shape
    dec_batch, dec_seq, _ = x_sample.shape
    n_p, n_s = batch * seq_len, dec_batch * dec_seq
    n_all = n_p + n_s
    assert w_in.shape[0] == 1, "one layer: the final norm is fused after the only MoE"
    assert seq_len % ROW_TILE == 0 and n_s % ROW_TILE == 0 and ROW_TILE % dec_seq == 0
    assert dec_seq % SUBLANES == 0 and dec_batch % SAMPLE_GROUP == 0
    assert SAMPLE_GROUP * dec_seq == CHUNK and (dec_seq & (dec_seq - 1)) == 0

    xp2 = x_prompt.reshape(n_p, D_MODEL)
    xs2 = x_sample.reshape(n_s, D_MODEL)
    cos_tab, sin_tab = _rope_tables(seq_len, dec_seq)
    w = w_in[0]
    gl0 = _C_RG
    w_all = jnp.concatenate(
        [w[:, :gl0], w[:, gl0 + GK_RANK:], w[:, gl0:gl0 + GK_RANK],
         jnp.zeros((D_MODEL, LANES - GK_RANK), w.dtype)], axis=1).astype(BF16)
    wup_pad = jnp.concatenate([w_gk_up[0], jnp.zeros((LANES - GK_RANK, GLA_KDIM), F32)], axis=0).astype(BF16)
    qa, ka, va, qg, kg, vg, la, rg, ga, gb = _proj(
        xp2, xs2, g_mix[0].reshape(1, D_MODEL), cos_tab, sin_tab, w_all, wup_pad,
        b_gk[0].reshape(1, GLA_KDIM), seq_len)

    oa_p = _swa_prompt(sinks[0], qa, ka, va, batch, seq_len)
    oa_s, nk_s, nv_s = _swa_sample(sinks[0], qa, ka, va, cache_swa_k[0], cache_swa_v[0], n_p,
                                   dec_batch, dec_seq)
    norm_row = jnp.tile(gla_norm[0], GLA_HEADS).reshape(1, GLA_VDIM)
    og_p, s_fin = _gla_prompt(qg, kg, la, vg, rg, norm_row, batch, seq_len)
    og_s, s_new = _gla_sample(qg, kg, la, vg, rg, norm_row, state_gla[0], n_p, dec_batch, dec_seq)

    x1 = _merge(xp2, xs2, oa_p, oa_s, og_p, og_s, ga, gb, w_branch_a[0].astype(BF16),
                w_branch_b[0].astype(BF16), w_out[0].astype(BF16))

    g_ffn_row = g_ffn[0].reshape(1, D_MODEL)
    rw_t = router_w[0].T
    rw_hi = rw_t.astype(BF16)
    rw_lo = (rw_t - rw_hi.astype(F32)).astype(BF16)
    eidx, rank, gate, cnt = _route(x1, g_ffn_row, rw_hi, rw_lo, router_b[0].reshape(N_EXPERTS, 1))
    counts = cnt[:, 0].astype(jnp.int32)
    pad_start, block_expert, n_used, n_blocks = _block_tables(counts, n_all)
    src, dst, gate_sorted = _invert(eidx, rank, gate, pad_start, n_blocks)
    prime = (TOP_K * n_all + 2 * FFN_BLOCK + jnp.arange(FFN_BLOCK, dtype=jnp.int32)).reshape(1, FFN_BLOCK)
    dst = jnp.concatenate([dst, prime], axis=0)
    slots = _ffn(block_expert, n_used, src.reshape(n_blocks, 1, FFN_BLOCK),
                 dst.reshape(n_blocks + 1, 1, FFN_BLOCK), gate_sorted.reshape(n_blocks * FFN_BLOCK, 1),
                 x1, g_ffn_row, w_up[0], b_up[0].reshape(N_EXPERTS, 1, 2 * D_FF), w_down[0],
                 b_down[0].reshape(N_EXPERTS, 1, D_MODEL), TOP_K * n_all + 3 * FFN_BLOCK)

    g_out = g_final.reshape(1, D_MODEL)
    y_p = _final(x1, slots, g_out, 0, n_p, n_all)
    y_s = _final(x1, slots, g_out, n_p, n_s, n_all)

    kv_shape = (1, -1, WINDOW, KV_HEADS, HEAD_DIM)
    new_k_p = ka[:n_p].reshape(batch, seq_len, KV_WIDTH)[:, -WINDOW:].reshape(kv_shape)
    new_v_p = va[:n_p].reshape(batch, seq_len, KV_WIDTH)[:, -WINDOW:].reshape(kv_shape)
    return (y_p.reshape(batch, seq_len, D_MODEL), y_s.reshape(dec_batch, dec_seq, D_MODEL),
            new_k_p, new_v_p, s_fin[None], nk_s.reshape(kv_shape), nv_s.reshape(kv_shape), s_new[None])
```

```python
import functools

import numpy as np
import jax
import jax.numpy as jnp
from jax import lax
from jax.experimental import pallas as pl
from jax.experimental.pallas import tpu as pltpu

D_MODEL = 1024
PAST_LEN = 8192
HEAD_DIM = 64
N_HEADS = 8
KV_HEADS = 2
GROUP = N_HEADS // KV_HEADS
WINDOW = 128
ROT_DIM = HEAD_DIM // 4
ROPE_THETA = 500000.0
ATT_WIDTH = N_HEADS * HEAD_DIM
KV_WIDTH = KV_HEADS * HEAD_DIM
GLA_HEADS = 4
GLA_KDIM = D_MODEL // 2
GLA_VDIM = D_MODEL
GLA_DK = GLA_KDIM // GLA_HEADS
GLA_DV = GLA_VDIM // GLA_HEADS
GK_RANK = 16
GK_NORMALIZER = 16.0
N_EXPERTS = 32
TOP_K = 4
D_FF = D_MODEL
SWIGLU_LIMIT = 7.0
SWIGLU_ALPHA = 1.702
EPS = 1e-5
NEG_INF = -1e30

LANES = 128
SUBLANES = 8
VMEM_LIMIT_BYTES = 56 * 1024 * 1024

ROW_TILE = 256
CHUNK = 128
FFN_BLOCK = 256

BF16 = jnp.bfloat16
F32 = jnp.float32

_C_QA, _C_KA, _C_VA, _C_QG, _C_KG, _C_VG, _C_RG, _C_GA, _C_GB, _C_GL, _C_END = (
    0, 512, 640, 768, 1280, 1792, 2816, 3840, 4864, 5888, 6016)


def _const_spec(shape):
    nd = len(shape)
    return pl.BlockSpec(shape, lambda *_: (0,) * nd)


def _params(sem, vmem=VMEM_LIMIT_BYTES):
    return pltpu.CompilerParams(dimension_semantics=sem, vmem_limit_bytes=vmem)


def _nt_dot(a, b):
    return lax.dot_general(a, b, (((1,), (1,)), ((), ())), preferred_element_type=F32)


def _dot(a, b):
    return jnp.dot(a, b, preferred_element_type=F32)


TILE_ROWS = D_MODEL // LANES
assert TILE_ROWS == SUBLANES


def _load_token_tiles(ref, n_tokens):
    return jnp.concatenate([ref[pl.ds(c, n_tokens, stride=TILE_ROWS), :] for c in range(TILE_ROWS)], axis=1)


def _store_token_tiles(ref, x):
    for c in range(TILE_ROWS):
        ref[pl.ds(c, x.shape[0], stride=TILE_ROWS), :] = x[:, c * LANES:(c + 1) * LANES]


def _split3(x):
    hi = x.astype(BF16)
    r1 = x - hi.astype(F32)
    mid = r1.astype(BF16)
    lo = (r1 - mid.astype(F32)).astype(BF16)
    return hi, mid, lo


def _rope(x, cos_t, sin_t, n_rep):
    width = x.shape[1]
    cos_f = jnp.concatenate([cos_t] * n_rep, axis=1) if n_rep > 1 else cos_t
    sin_f = jnp.concatenate([sin_t] * n_rep, axis=1) if n_rep > 1 else sin_t
    lane = lax.broadcasted_iota(jnp.int32, x.shape, 1) % HEAD_DIM
    up = pltpu.roll(x, width - ROT_DIM // 2, 1)
    down = pltpu.roll(x, ROT_DIM // 2, 1)
    partner = jnp.where(lane < ROT_DIM // 2, up, down)
    return x * cos_f + partner * sin_f


def _proj_kernel(n_prompt_tiles, xp_ref, xs_ref, g_ref, cos_ref, sin_ref, w_ref, wup_ref, bgk_ref,
                 qa_ref, ka_ref, va_ref, qg_ref, kg_ref, vg_ref, la_ref, rg_ref, ga_ref, gb_ref):
    i = pl.program_id(0)
    x = jnp.where(i < n_prompt_tiles, xp_ref[...], xs_ref[...])
    ms = jnp.mean(x * x, axis=-1, keepdims=True)
    h = (x * lax.rsqrt(ms + EPS) * g_ref[...]).astype(BF16)
    cos_t = cos_ref[...]
    sin_t = sin_ref[...]

    def seg(a, b):
        return _dot(h, w_ref[:, a:b])

    qa = _rope(seg(_C_QA, _C_KA), cos_t, sin_t, ATT_WIDTH // LANES)
    qa_ref[...] = (qa * (HEAD_DIM ** -0.5)).astype(BF16)
    ka_ref[...] = _rope(seg(_C_KA, _C_VA), cos_t, sin_t, 1)
    va_ref[...] = seg(_C_VA, _C_QG)
    qg_ref[...] = seg(_C_QG, _C_KG) * (GLA_DK ** -0.5)
    kg_ref[...] = seg(_C_KG, _C_VG)
    vg_ref[...] = seg(_C_VG, _C_RG).astype(BF16)
    rg_ref[...] = seg(_C_RG, _C_GA).astype(BF16)
    ga_ref[...] = seg(_C_GA, _C_GB).astype(BF16)
    gb_ref[...] = seg(_C_GB, _C_GL).astype(BF16)
    gk_low = seg(_C_GL, _C_END).astype(BF16)
    z = _dot(gk_low, wup_ref[...]) + bgk_ref[...]
    log_sig = jnp.minimum(z, 0.0) - jnp.log1p(jnp.exp(-jnp.abs(z)))
    la_ref[...] = log_sig / GK_NORMALIZER


def _proj(xp2, xs2, g_mix, cos_tab, sin_tab, w_all, wup_pad, b_gk, seq_len):
    n_p, n_s = xp2.shape[0], xs2.shape[0]
    n_all = n_p + n_s
    npt, nst = n_p // ROW_TILE, n_s // ROW_TILE
    tiles_per_seq = seq_len // ROW_TILE

    def tab_map(i):
        return (jnp.where(i < npt, i % tiles_per_seq, tiles_per_seq), 0)

    row = lambda w: pl.BlockSpec((ROW_TILE, w), lambda i: (i, 0))
    widths = [(ATT_WIDTH, BF16), (KV_WIDTH, F32), (KV_WIDTH, F32), (GLA_KDIM, F32), (GLA_KDIM, F32),
              (GLA_VDIM, BF16), (GLA_KDIM, F32), (GLA_VDIM, BF16), (D_MODEL, BF16), (D_MODEL, BF16)]
    return pl.pallas_call(
        functools.partial(_proj_kernel, npt),
        grid=(npt + nst,),
        in_specs=[
            pl.BlockSpec((ROW_TILE, D_MODEL), lambda i: (jnp.minimum(i, npt - 1), 0)),
            pl.BlockSpec((ROW_TILE, D_MODEL), lambda i: (jnp.maximum(i - npt, 0), 0)),
            _const_spec((1, D_MODEL)),
            pl.BlockSpec((ROW_TILE, LANES), tab_map),
            pl.BlockSpec((ROW_TILE, LANES), tab_map),
            _const_spec(w_all.shape),
            _const_spec(wup_pad.shape),
            _const_spec((1, GLA_KDIM)),
        ],
        out_specs=[row(w) for w, _ in widths],
        out_shape=[jax.ShapeDtypeStruct((n_all, w), dt) for w, dt in widths],
        compiler_params=_params(("arbitrary",)),
        name="proj",
    )(xp2, xs2, g_mix, cos_tab, sin_tab, w_all, wup_pad, b_gk)


def _pair_blocks(kk):
    lane = lax.broadcasted_iota(jnp.int32, kk.shape, 1)
    lo = lane < HEAD_DIM
    swapped = pltpu.roll(kk, HEAD_DIM, 1)
    zero = jnp.zeros_like(kk)
    blocks = []
    for kh in range(KV_HEADS):
        left = jnp.where(lo, kk if kh == 0 else swapped, zero)
        right = jnp.where(lo, zero, swapped if kh == 0 else kk)
        blocks.append(jnp.concatenate([left, right], axis=0).astype(BF16))
    return blocks


def _sink_softmax(s, valid, sink):
    s = jnp.where(valid, s, NEG_INF)
    m = jnp.maximum(jnp.max(s, axis=-1, keepdims=True), sink)
    p = jnp.exp(s - m)
    denom = jnp.sum(p, axis=-1, keepdims=True) + jnp.exp(sink - m)
    return (p * (1.0 / denom)).astype(BF16)


def _attend(q, kk, vv, valid, sink_ref, o_ref):
    rows, keys = valid.shape
    kblocks = _pair_blocks(kk)
    vblocks = _pair_blocks(vv)
    for kh in range(KV_HEADS):
        base = kh * GROUP * HEAD_DIM
        qq = jnp.concatenate([q[:, base:base + LANES], q[:, base + LANES:base + 2 * LANES]], axis=0)
        s = _nt_dot(qq, kblocks[kh])
        for r in range(2):
            probs = []
            for c in range(2):
                head = kh * GROUP + 2 * r + c
                probs.append(_sink_softmax(s[r * rows:(r + 1) * rows, c * keys:(c + 1) * keys],
                                           valid, sink_ref[head]))
            p = jnp.concatenate(probs, axis=1)
            o_ref[:, base + r * LANES:base + (r + 1) * LANES] = _dot(p, vblocks[kh]).astype(o_ref.dtype)


def _swa_prompt_kernel(sink_ref, q_ref, kp_ref, kc_ref, vp_ref, vc_ref, o_ref):
    j = pl.program_id(1)
    kk = jnp.concatenate([kp_ref[...], kc_ref[...]], axis=0)
    vv = jnp.concatenate([vp_ref[...], vc_ref[...]], axis=0)
    row = lax.broadcasted_iota(jnp.int32, (WINDOW, 2 * WINDOW), 0)
    col = lax.broadcasted_iota(jnp.int32, (WINDOW, 2 * WINDOW), 1)
    valid = (col > row) & (col <= row + WINDOW) & ((j > 0) | (col >= WINDOW))
    _attend(q_ref[...], kk, vv, valid, sink_ref, o_ref)


def _swa_prompt(sinks, qa, ka, va, batch, seq_len):
    nb = seq_len // WINDOW
    cur = lambda b, j, s: (b * nb + j, 0)
    prev = lambda b, j, s: (b * nb + jnp.maximum(j - 1, 0), 0)
    return pl.pallas_call(
        _swa_prompt_kernel,
        grid_spec=pltpu.PrefetchScalarGridSpec(
            num_scalar_prefetch=1,
            grid=(batch, nb),
            in_specs=[
                pl.BlockSpec((WINDOW, ATT_WIDTH), cur),
                pl.BlockSpec((WINDOW, KV_WIDTH), prev),
                pl.BlockSpec((WINDOW, KV_WIDTH), cur),
                pl.BlockSpec((WINDOW, KV_WIDTH), prev),
                pl.BlockSpec((WINDOW, KV_WIDTH), cur),
            ],
            out_specs=pl.BlockSpec((WINDOW, ATT_WIDTH), cur),
        ),
        out_shape=jax.ShapeDtypeStruct((batch * seq_len, ATT_WIDTH), BF16),
        compiler_params=_params(("arbitrary", "arbitrary")),
        name="swa_prompt",
    )(sinks, qa, ka, ka, va, va)


SAMPLE_GROUP = 16


def _swa_sample_kernel(dec_seq, sink_ref, q_ref, kn_ref, vn_ref, ck_ref, cv_ref, o_ref, nk_ref, nv_ref):
    rows = SAMPLE_GROUP * dec_seq
    ck = ck_ref[...]
    cv = cv_ref[...]
    kn = kn_ref[...]
    vn = vn_ref[...]
    nk_ref[:, :WINDOW - dec_seq, :] = ck[:, dec_seq:, :]
    nv_ref[:, :WINDOW - dec_seq, :] = cv[:, dec_seq:, :]
    nk_ref[:, WINDOW - dec_seq:, :] = kn.reshape(SAMPLE_GROUP, dec_seq, KV_WIDTH)
    nv_ref[:, WINDOW - dec_seq:, :] = vn.reshape(SAMPLE_GROUP, dec_seq, KV_WIDTH)
    n_cache = SAMPLE_GROUP * WINDOW
    kk = jnp.concatenate([ck.reshape(n_cache, KV_WIDTH), kn], axis=0)
    vv = jnp.concatenate([cv.reshape(n_cache, KV_WIDTH), vn], axis=0)
    keys = n_cache + rows
    row = lax.broadcasted_iota(jnp.int32, (rows, keys), 0)
    col = lax.broadcasted_iota(jnp.int32, (rows, keys), 1)
    q_b, q_s = row // dec_seq, row % dec_seq
    is_cache = col < n_cache
    new = col - n_cache
    valid_cache = (col // WINDOW == q_b) & (col % WINDOW > q_s)
    valid_new = (new // dec_seq == q_b) & (new % dec_seq <= q_s)
    valid = (is_cache & valid_cache) | (jnp.logical_not(is_cache) & valid_new)
    _attend(q_ref[...], kk, vv, valid, sink_ref, o_ref)


def _swa_sample(sinks, qa, ka, va, cache_k, cache_v, n_prompt_rows, dec_batch, dec_seq):
    rows = SAMPLE_GROUP * dec_seq
    off = n_prompt_rows // rows
    tok = lambda g, s: (off + g, 0)
    cache = lambda g, s: (g, 0, 0)
    cshape = (dec_batch, WINDOW, KV_WIDTH)
    return pl.pallas_call(
        functools.partial(_swa_sample_kernel, dec_seq),
        grid_spec=pltpu.PrefetchScalarGridSpec(
            num_scalar_prefetch=1,
            grid=(dec_batch // SAMPLE_GROUP,),
            in_specs=[
                pl.BlockSpec((rows, ATT_WIDTH), tok),
                pl.BlockSpec((rows, KV_WIDTH), tok),
                pl.BlockSpec((rows, KV_WIDTH), tok),
                pl.BlockSpec((SAMPLE_GROUP, WINDOW, KV_WIDTH), cache),
                pl.BlockSpec((SAMPLE_GROUP, WINDOW, KV_WIDTH), cache),
            ],
            out_specs=[
                pl.BlockSpec((rows, ATT_WIDTH), lambda g, s: (g, 0)),
                pl.BlockSpec((SAMPLE_GROUP, WINDOW, KV_WIDTH), cache),
                pl.BlockSpec((SAMPLE_GROUP, WINDOW, KV_WIDTH), cache),
            ],
        ),
        out_shape=[jax.ShapeDtypeStruct((dec_batch * dec_seq, ATT_WIDTH), BF16),
                   jax.ShapeDtypeStruct(cshape, F32), jax.ShapeDtypeStruct(cshape, F32)],
        compiler_params=_params(("arbitrary",)),
        name="swa_sample",
    )(sinks, qa, ka, va, cache_k.reshape(cshape), cache_v.reshape(cshape))


def _chunk_tables(seg):
    n_lev = int(np.log2(seg))
    t = np.arange(CHUNK)
    seg_start = (t // seg) * seg
    u = np.arange(CHUNK)[None, :]

    def prefix(end):
        return ((u >= seg_start[:, None]) & (u <= end[:, None])).astype(np.float32)

    blocks = [prefix(t), prefix(seg_start + seg - 1)]
    for d in range(n_lev):
        m = 1 << d
        ref = (t >> (d + 1) << (d + 1)) + m - 1
        blocks.append(prefix(ref))
    lhs = np.concatenate(blocks, axis=0)
    lhs3 = np.concatenate([lhs, lhs, lhs], axis=1)
    tt, ss = t[:, None], t[None, :]
    x = tt ^ ss
    lev = np.where(x > 0, np.floor(np.log2(np.maximum(x, 1))).astype(np.int32), n_lev)
    lev = np.where((ss > tt) | (tt // seg != ss // seg), -1, lev)
    lev = np.where(tt == ss, n_lev, lev)
    return jnp.asarray(lhs3, BF16), jnp.asarray(lev, jnp.int32), n_lev


def _gla_chunk_terms(q, k, la, lhs3, level, n_lev):
    hi, mid, lo = _split3(la)
    rhs = jnp.concatenate([hi, mid, lo], axis=0)
    sums = _dot(lhs3, rhs)
    b = sums[0:CHUNK]
    b_last = sums[CHUNK:2 * CHUNK]
    q_main = (q * jnp.exp(b)).astype(BF16)
    k_upd = k * jnp.exp(b_last - b)
    q_lev = [None] * n_lev
    k_lev = [None] * n_lev
    for d in range(n_lev):
        ref = sums[(2 + d) * CHUNK:(3 + d) * CHUNK]
        q_lev[d] = (q * jnp.exp(jnp.minimum(b - ref, 0.0))).astype(BF16)
        k_lev[d] = (k * jnp.exp(jnp.minimum(ref - b, 0.0))).astype(BF16)
    q_b, k_b = q.astype(BF16), k.astype(BF16)

    def att(h):
        hs = slice(h * GLA_DK, (h + 1) * GLA_DK)
        acc = jnp.where(level == n_lev, _nt_dot(q_b[:, hs], k_b[:, hs]), 0.0)
        for d in range(n_lev):
            acc = jnp.where(level == d, _nt_dot(q_lev[d][:, hs], k_lev[d][:, hs]), acc)
        return acc

    return q_main, k_upd, att, b_last


def _gla_out(o, r, norm):
    parts = []
    for h in range(GLA_HEADS):
        oh = o[:, h * GLA_DV:(h + 1) * GLA_DV]
        ms = jnp.mean(oh * oh, axis=-1, keepdims=True)
        parts.append(oh * lax.rsqrt(ms + EPS))
    y = jnp.concatenate(parts, axis=1) * norm
    rf = r.astype(F32)
    return y * (rf * jax.nn.sigmoid(rf))


def _gla_prompt_kernel(n_lev, q_ref, k_ref, la_ref, v_ref, r_ref, norm_ref, lhs_ref, lev_ref,
                       o_ref, sfin_ref, s_scr):
    c = pl.program_id(1)

    @pl.when(c == 0)
    def _():
        s_scr[...] = jnp.zeros_like(s_scr)

    q_main, k_upd, att, b_last = _gla_chunk_terms(q_ref[...], k_ref[...], la_ref[...], lhs_ref[...],
                                                  lev_ref[...], n_lev)
    v = v_ref[...]
    outs = []
    for h in range(GLA_HEADS):
        hs = slice(h * GLA_DK, (h + 1) * GLA_DK)
        vh = v[:, h * GLA_DV:(h + 1) * GLA_DV]
        s0 = s_scr[h]
        o_h = _dot(q_main[:, hs], s0.astype(BF16)) + _dot(att(h).astype(BF16), vh)
        outs.append(o_h)
        decay = jnp.exp(b_last[:, hs]).T
        k_t = k_upd[:, hs].T.astype(BF16)
        s_scr[h] = jnp.concatenate([decay, decay], axis=1) * s0 + _dot(k_t, vh)
    o_ref[...] = _gla_out(jnp.concatenate(outs, axis=1), r_ref[...], norm_ref[...]).astype(o_ref.dtype)

    @pl.when(c == pl.num_programs(1) - 1)
    def _():
        sfin_ref[0] = s_scr[...]


def _gla_prompt(qg, kg, la, vg, rg, norm_row, batch, seq_len):
    nc = seq_len // CHUNK
    lhs3, level, n_lev = _chunk_tables(CHUNK)
    tok = lambda w: pl.BlockSpec((CHUNK, w), lambda b, c: (b * nc + c, 0))
    return pl.pallas_call(
        functools.partial(_gla_prompt_kernel, n_lev),
        grid=(batch, nc),
        in_specs=[tok(GLA_KDIM), tok(GLA_KDIM), tok(GLA_KDIM), tok(GLA_VDIM), tok(GLA_VDIM),
                  _const_spec((1, GLA_VDIM)), _const_spec(lhs3.shape), _const_spec(level.shape)],
        out_specs=[tok(GLA_VDIM),
                   pl.BlockSpec((1, GLA_HEADS, GLA_DK, GLA_DV), lambda b, c: (b, 0, 0, 0))],
        out_shape=[jax.ShapeDtypeStruct((batch * seq_len, GLA_VDIM), BF16),
                   jax.ShapeDtypeStruct((batch, GLA_HEADS, GLA_DK, GLA_DV), F32)],
        scratch_shapes=[pltpu.VMEM((GLA_HEADS, GLA_DK, GLA_DV), F32)],
        compiler_params=_params(("arbitrary", "arbitrary")),
        name="gla_prompt",
    )(qg, kg, la, vg, rg, norm_row, lhs3, level)


def _gla_sample_kernel(n_lev, dec_seq, q_ref, k_ref, la_ref, v_ref, r_ref, norm_ref, lhs_ref, lev_ref,
                       s0_ref, o_ref, snew_ref):
    q_main, k_upd, att, b_last = _gla_chunk_terms(q_ref[...], k_ref[...], la_ref[...], lhs_ref[...],
                                                  lev_ref[...], n_lev)
    v = v_ref[...]
    n_b = CHUNK // dec_seq
    row_b = lax.broadcasted_iota(jnp.int32, (CHUNK, GLA_DK), 0) // dec_seq
    col_b = lax.broadcasted_iota(jnp.int32, (GLA_DK, CHUNK), 1) // dec_seq
    outs = []
    for h in range(GLA_HEADS):
        hs = slice(h * GLA_DK, (h + 1) * GLA_DK)
        vh = v[:, h * GLA_DV:(h + 1) * GLA_DV]
        qm = q_main[:, hs]
        decay_t = jnp.exp(b_last[:, hs]).T
        k_t = k_upd[:, hs].T.astype(BF16)
        o_h = _dot(att(h).astype(BF16), vh)
        for bi in range(n_b):
            s0 = s0_ref[bi, h]
            o_h = o_h + _dot(jnp.where(row_b == bi, qm, jnp.zeros_like(qm)), s0.astype(BF16))
            decay = jnp.broadcast_to(decay_t[:, bi * dec_seq:bi * dec_seq + 1], (GLA_DK, GLA_DV))
            k_b = jnp.where(col_b == bi, k_t, jnp.zeros_like(k_t))
            snew_ref[bi, h] = decay * s0 + _dot(k_b, vh)
        outs.append(o_h)
    o_ref[...] = _gla_out(jnp.concatenate(outs, axis=1), r_ref[...], norm_ref[...]).astype(o_ref.dtype)


def _gla_sample(qg, kg, la, vg, rg, norm_row, state, n_prompt_rows, dec_batch, dec_seq):
    n_b = CHUNK // dec_seq
    off = n_prompt_rows // CHUNK
    lhs3, level, n_lev = _chunk_tables(dec_seq)
    tok = lambda w: pl.BlockSpec((CHUNK, w), lambda g: (off + g, 0))
    st = pl.BlockSpec((n_b, GLA_HEADS, GLA_DK, GLA_DV), lambda g: (g, 0, 0, 0))
    return pl.pallas_call(
        functools.partial(_gla_sample_kernel, n_lev, dec_seq),
        grid=(dec_batch // n_b,),
        in_specs=[tok(GLA_KDIM), tok(GLA_KDIM), tok(GLA_KDIM), tok(GLA_VDIM), tok(GLA_VDIM),
                  _const_spec((1, GLA_VDIM)), _const_spec(lhs3.shape), _const_spec(level.shape), st],
        out_specs=[pl.BlockSpec((CHUNK, GLA_VDIM), lambda g: (g, 0)), st],
        out_shape=[jax.ShapeDtypeStruct((dec_batch * dec_seq, GLA_VDIM), BF16),
                   jax.ShapeDtypeStruct(state.shape, F32)],
        compiler_params=_params(("arbitrary",)),
        name="gla_sample",
    )(qg, kg, la, vg, rg, norm_row, lhs3, level, state)


def _merge_kernel(n_prompt_tiles, xp_ref, xs_ref, oap_ref, oas_ref, ogp_ref, ogs_ref, ga_ref, gb_ref,
                  wa_ref, wb_ref, wo_ref, x1_ref):
    i = pl.program_id(0)
    is_p = i < n_prompt_tiles
    x = jnp.where(is_p, xp_ref[...], xs_ref[...])
    oa = jnp.where(is_p, oap_ref[...], oas_ref[...])
    og = jnp.where(is_p, ogp_ref[...], ogs_ref[...])
    m = (jax.nn.sigmoid(ga_ref[...].astype(F32)) * _dot(oa, wa_ref[...])
         + jax.nn.sigmoid(gb_ref[...].astype(F32)) * _dot(og, wb_ref[...]))
    _store_token_tiles(x1_ref, x + _dot(m.astype(BF16), wo_ref[...]))


def _merge(xp2, xs2, oa_p, oa_s, og_p, og_s, ga, gb, wa, wb, wo):
    n_p, n_s = xp2.shape[0], xs2.shape[0]
    npt, nst = n_p // ROW_TILE, n_s // ROW_TILE
    p_map = lambda i: (jnp.minimum(i, npt - 1), 0)
    s_map = lambda i: (jnp.maximum(i - npt, 0), 0)
    row = lambda w: pl.BlockSpec((ROW_TILE, w), lambda i: (i, 0))
    return pl.pallas_call(
        functools.partial(_merge_kernel, npt),
        grid=(npt + nst,),
        in_specs=[
            pl.BlockSpec((ROW_TILE, D_MODEL), p_map), pl.BlockSpec((ROW_TILE, D_MODEL), s_map),
            pl.BlockSpec((ROW_TILE, ATT_WIDTH), p_map), pl.BlockSpec((ROW_TILE, ATT_WIDTH), s_map),
            pl.BlockSpec((ROW_TILE, GLA_VDIM), p_map), pl.BlockSpec((ROW_TILE, GLA_VDIM), s_map),
            row(D_MODEL), row(D_MODEL),
            _const_spec(wa.shape), _const_spec(wb.shape), _const_spec(wo.shape),
        ],
        out_specs=pl.BlockSpec((ROW_TILE * TILE_ROWS, LANES), lambda i: (i, 0)),
        out_shape=jax.ShapeDtypeStruct(((n_p + n_s) * TILE_ROWS, LANES), F32),
        compiler_params=_params(("arbitrary",)),
        name="merge",
    )(xp2, xs2, oa_p, oa_s, og_p, og_s, ga, gb, wa, wb, wo)


def _route_kernel(x1_ref, gffn_ref, rwh_ref, rwl_ref, rb_ref, upper_ref, eidx_ref, rank_ref, gate_ref,
                  cnt_ref, cnt_scr):
    i = pl.program_id(0)

    @pl.when(i == 0)
    def _():
        cnt_scr[...] = jnp.zeros_like(cnt_scr)

    x1 = _load_token_tiles(x1_ref, ROW_TILE)
    ms = jnp.mean(x1 * x1, axis=-1, keepdims=True)
    h2 = x1 * lax.rsqrt(ms + EPS) * gffn_ref[...]
    h_hi = h2.astype(BF16)
    h_lo = (h2 - h_hi.astype(F32)).astype(BF16)
    rwh, rwl = rwh_ref[...], rwl_ref[...]
    logits = _nt_dot(rwh, h_hi) + _nt_dot(rwl, h_hi) + _nt_dot(rwh, h_lo) + rb_ref[...]
    eid = lax.broadcasted_iota(jnp.int32, logits.shape, 0)
    upper = upper_ref[...]
    base = cnt_scr[...]
    vals, rows_e, rows_r = [], [], []
    lg = logits
    for _ in range(TOP_K):
        mx = jnp.max(lg, axis=0, keepdims=True)
        sel = jnp.min(jnp.where(lg == mx, eid, N_EXPERTS), axis=0, keepdims=True)
        onehot = eid == sel
        oh = onehot.astype(F32)
        before = _dot(onehot.astype(BF16), upper)
        rows_r.append(jnp.sum(oh * (base + before), axis=0, keepdims=True))
        base = base + jnp.sum(oh, axis=1, keepdims=True)
        vals.append(mx)
        rows_e.append(sel)
        lg = jnp.where(onehot, -jnp.inf, lg)
    cnt_scr[...] = base
    ex = [jnp.exp(v - vals[0]) for v in vals]
    inv = 1.0 / (ex[0] + ex[1] + ex[2] + ex[3])
    eidx_ref[...] = jnp.concatenate(rows_e, axis=0)
    rank_ref[...] = jnp.concatenate(rows_r, axis=0).astype(jnp.int32)
    gate_ref[...] = jnp.concatenate([e * inv for e in ex], axis=0)
    cnt_ref[...] = jnp.broadcast_to(base, cnt_ref.shape)


def _route(x1, g_ffn, rw_hi, rw_lo, rb_col):
    n = x1.shape[0] // TILE_ROWS
    upper = jnp.asarray(np.triu(np.ones((ROW_TILE, ROW_TILE), np.float32), 1), BF16)
    col = lambda dt: (pl.BlockSpec((TOP_K, ROW_TILE), lambda i: (0, i)), jax.ShapeDtypeStruct((TOP_K, n), dt))
    outs = [col(jnp.int32), col(jnp.int32), col(F32),
            (_const_spec((N_EXPERTS, LANES)), jax.ShapeDtypeStruct((N_EXPERTS, LANES), F32))]
    return pl.pallas_call(
        _route_kernel,
        grid=(n // ROW_TILE,),
        in_specs=[pl.BlockSpec((ROW_TILE * TILE_ROWS, LANES), lambda i: (i, 0)), _const_spec((1, D_MODEL)),
                  _const_spec(rw_hi.shape), _const_spec(rw_lo.shape), _const_spec((N_EXPERTS, 1)),
                  _const_spec(upper.shape)],
        out_specs=[o[0] for o in outs],
        out_shape=[o[1] for o in outs],
        scratch_shapes=[pltpu.VMEM((N_EXPERTS, 1), F32)],
        compiler_params=_params(("arbitrary",)),
        name="route",
    )(x1, g_ffn, rw_hi, rw_lo, rb_col, upper)


_INV_FIELDS = 6


def _split3_exact(x):
    def trunc(v):
        return pltpu.bitcast(pltpu.bitcast(v, jnp.uint32) & jnp.uint32(0xFFFF0000), F32)
    hi = trunc(x)
    rest = x - hi
    mid = trunc(rest)
    return hi, mid, rest - mid


def _invert_kernel(n_blocks, eidx_ref, rank_ref, gate_ref, pstart_ref, acc_ref):
    i = pl.program_id(0)

    @pl.when(i == 0)
    def _():
        acc_ref[...] = jnp.zeros(acc_ref.shape, acc_ref.dtype)

    eidx, rank, gate = eidx_ref[...], rank_ref[...], gate_ref[...]
    pstart = pstart_ref[...]
    rows = eidx.shape[1]
    eid = lax.broadcasted_iota(jnp.int32, (N_EXPERTS, rows), 0)
    blk_id = lax.broadcasted_iota(jnp.int32, (n_blocks, rows), 0)
    off_id = lax.broadcasted_iota(jnp.int32, (FFN_BLOCK, rows), 0)
    tok = i * rows + lax.broadcasted_iota(jnp.int32, (1, rows), 1)
    a_parts, b_parts = [], []
    shift = FFN_BLOCK.bit_length() - 1
    tok_lo = (tok & 255).astype(F32)
    tok_hi = lax.shift_right_logical(tok, 8).astype(F32)
    for k in range(TOP_K):
        base = jnp.sum(jnp.where(eid == eidx[k:k + 1], pstart, 0), axis=0, keepdims=True)
        pos = base + rank[k:k + 1]
        a_parts.append(jnp.where(blk_id == lax.shift_right_logical(pos, shift), 1.0, 0.0).astype(BF16))
        hit = off_id == (pos & (FFN_BLOCK - 1))
        g_hi, g_mid, g_lo = _split3_exact(gate[k:k + 1])
        fields = [tok_lo, tok_hi, jnp.full((1, rows), k + 1.0, F32), g_hi, g_mid, g_lo]
        b_parts.append([jnp.where(hit, f, 0.0).astype(BF16) for f in fields])
    a = jnp.concatenate(a_parts, axis=1)
    for j in range(_INV_FIELDS):
        b = jnp.concatenate([b_parts[k][j] for k in range(TOP_K)], axis=1)
        acc_ref[j] += _nt_dot(a, b)


def _invert(eidx, rank, gate, pad_start, n_blocks):
    n_all = eidx.shape[1]
    assert n_all // 256 < 256, "token index is carried as two byte-sized fields"
    col = pl.BlockSpec((TOP_K, ROW_TILE), lambda i: (0, i))
    shape = (_INV_FIELDS, n_blocks, FFN_BLOCK)
    f = pl.pallas_call(
        functools.partial(_invert_kernel, n_blocks),
        grid=(n_all // ROW_TILE,),
        in_specs=[col, col, col, _const_spec((N_EXPERTS, 1))],
        out_specs=_const_spec(shape),
        out_shape=jax.ShapeDtypeStruct(shape, F32),
        compiler_params=_params(("arbitrary",)),
        name="invert",
    )(eidx, rank, gate, pad_start)
    t = (f[0] + 256.0 * f[1]).astype(jnp.int32)
    kp1 = f[2].astype(jnp.int32)
    blk = lax.broadcasted_iota(jnp.int32, t.shape, 0)
    off = lax.broadcasted_iota(jnp.int32, t.shape, 1)
    dump = TOP_K * n_all + (blk & 1) * FFN_BLOCK + off
    return t, jnp.where(kp1 > 0, (kp1 - 1) * n_all + t, dump), (f[3] + f[4]) + f[5]


def _ffn_kernel(be_ref, nused_ref, src_cur_ref, src_nxt_ref, dst_cur_ref, dst_prev_ref, gate_ref, gffn_ref,
                wup_ref, bup_ref, wdn_ref, bdn_ref, x1_hbm, out_hbm, xbuf0, xbuf1, obuf0, obuf1, wup_bf, wdn_bf,
                gsem, ssem):
    i = pl.program_id(0)
    n_used = nused_ref[0]
    xbufs, obufs = (xbuf0, xbuf1), (obuf0, obuf1)

    def hbm_tile(ref, token):
        return ref.at[pl.ds(pl.multiple_of(token * TILE_ROWS, TILE_ROWS), TILE_ROWS), :]

    def vmem_tile(ref, r):
        return ref.at[pl.ds(r * TILE_ROWS, TILE_ROWS), :]

    def gather_copy(src_ref, r, s):
        return pltpu.make_async_copy(hbm_tile(x1_hbm, src_ref[0, 0, r]), vmem_tile(xbufs[s], r), gsem.at[s])

    def scatter_copy(dst_ref, r, s):
        return pltpu.make_async_copy(vmem_tile(obufs[s], r), hbm_tile(out_hbm, dst_ref[0, 0, r]), ssem.at[s])

    def start_rows(copy_fn):
        for r in range(FFN_BLOCK):
            copy_fn(r).start(priority=r % 2)

    def wait_rows(copy_fn):
        for r in range(FFN_BLOCK):
            copy_fn(r).wait()

    first = i == 0
    changed = first | (be_ref[i] != be_ref[jnp.maximum(i - 1, 0)])

    @pl.when(first)
    def _():
        obuf0[...] = jnp.zeros(obuf0.shape, obuf0.dtype)
        obuf1[...] = jnp.zeros(obuf1.shape, obuf1.dtype)
        n_slot_rows = out_hbm.shape[0] // TILE_ROWS - 3 * FFN_BLOCK
        init = pltpu.make_async_copy(
            obuf1, out_hbm.at[pl.ds((n_slot_rows + FFN_BLOCK) * TILE_ROWS, FFN_BLOCK * TILE_ROWS), :], ssem.at[1])
        init.start()
        init.wait()
        start_rows(lambda r: pltpu.make_async_copy(
            vmem_tile(obuf0, r), out_hbm.at[pl.ds((n_slot_rows + r) * TILE_ROWS, TILE_ROWS), :], ssem.at[0]))
        start_rows(lambda r: gather_copy(src_cur_ref, r, 0))

    @pl.when(changed & (i < n_used))
    def _():
        wup_bf[...] = wup_ref[0].astype(BF16)
        wdn_bf[...] = wdn_ref[0].astype(BF16)

    def step(s):
        wait_rows(lambda r: gather_copy(src_cur_ref, r, s))
        x1 = _load_token_tiles(xbufs[s], FFN_BLOCK)
        ms = jnp.mean(x1 * x1, axis=-1, keepdims=True)
        x = (x1 * lax.rsqrt(ms + EPS) * gffn_ref[...]).astype(BF16)
        start_rows(lambda r: gather_copy(src_nxt_ref, r, 1 - s))
        start_rows(lambda r: scatter_copy(dst_prev_ref, r, 1 - s))
        hu = _dot(x, wup_bf[...]) + bup_ref[0]
        glu = jnp.minimum(hu[:, :D_FF], SWIGLU_LIMIT)
        lin = jnp.clip(hu[:, D_FF:], -SWIGLU_LIMIT, SWIGLU_LIMIT)
        act = glu * jax.nn.sigmoid(SWIGLU_ALPHA * glu) * (lin + 1.0)
        out = (_dot(act.astype(BF16), wdn_bf[...]) + bdn_ref[0]) * gate_ref[...]
        wait_rows(lambda r: scatter_copy(dst_cur_ref, r, s))
        _store_token_tiles(obufs[s], out)

    def drain(s):
        start_rows(lambda r: scatter_copy(dst_cur_ref, r, s))
        wait_rows(lambda r: scatter_copy(dst_cur_ref, r, 1 - s))
        wait_rows(lambda r: scatter_copy(dst_cur_ref, r, s))
        wait_rows(lambda r: gather_copy(src_cur_ref, r, 1 - s))

    for s in range(2):
        pl.when((i < n_used) & (i % 2 == s))(functools.partial(step, s))
    for s in range(2):
        pl.when((i == n_used - 1) & (i % 2 == s))(functools.partial(drain, s))


def _ffn(block_expert, n_used, src, dst, gate_sorted, x1, g_ffn, w_up, b_up, w_down, b_down, n_out_rows):
    n_blocks = block_expert.shape[0]
    smem_block = lambda fn: pl.BlockSpec((1, 1, FFN_BLOCK), fn, memory_space=pltpu.SMEM)
    cur = smem_block(lambda i, be, nu: (i, 0, 0))
    nxt = smem_block(lambda i, be, nu: (jnp.minimum(i + 1, n_blocks - 1), 0, 0))
    prev = smem_block(lambda i, be, nu: (jnp.where(i == 0, n_blocks, i - 1), 0, 0))
    ex3 = lambda i, be, nu: (be[i], 0, 0)
    return pl.pallas_call(
        _ffn_kernel,
        grid_spec=pltpu.PrefetchScalarGridSpec(
            num_scalar_prefetch=2,
            grid=(n_blocks,),
            in_specs=[
                cur, nxt, cur, prev,
                pl.BlockSpec((FFN_BLOCK, 1), lambda i, be, nu: (i, 0)),
                pl.BlockSpec((1, D_MODEL), lambda i, be, nu: (0, 0)),
                pl.BlockSpec((1, D_MODEL, 2 * D_FF), ex3),
                pl.BlockSpec((1, 1, 2 * D_FF), ex3),
                pl.BlockSpec((1, D_FF, D_MODEL), ex3),
                pl.BlockSpec((1, 1, D_MODEL), ex3),
                pl.BlockSpec(memory_space=pl.ANY),
            ],
            out_specs=pl.BlockSpec(memory_space=pl.ANY),
            scratch_shapes=[pltpu.VMEM((FFN_BLOCK * TILE_ROWS, LANES), F32)] * 4 + [
                pltpu.VMEM((D_MODEL, 2 * D_FF), BF16), pltpu.VMEM((D_FF, D_MODEL), BF16),
                pltpu.SemaphoreType.DMA((2,)), pltpu.SemaphoreType.DMA((2,))],
        ),
        out_shape=jax.ShapeDtypeStruct((n_out_rows * TILE_ROWS, LANES), F32),
        compiler_params=_params(("arbitrary",)),
        name="ffn",
    )(block_expert, n_used, src, src, dst, dst, gate_sorted, g_ffn, w_up, b_up, w_down, b_down, x1)


def _final_kernel(x1_ref, s0_ref, s1_ref, s2_ref, s3_ref, g_ref, y_ref, sum_scr):
    sum_scr[...] = x1_ref[...] + ((s0_ref[...] + s1_ref[...]) + (s2_ref[...] + s3_ref[...]))
    x2 = _load_token_tiles(sum_scr, ROW_TILE)
    ms = jnp.mean(x2 * x2, axis=-1, keepdims=True)
    y_ref[...] = x2 * lax.rsqrt(ms + EPS) * g_ref[...]


def _final(x1, slots, g_final, row0, n_rows, n_all):
    t0 = row0 // ROW_TILE
    per_slot = n_all // ROW_TILE
    tile_block = lambda b0: pl.BlockSpec((ROW_TILE * TILE_ROWS, LANES), lambda i: (b0 + i, 0))
    slot = lambda k: tile_block(k * per_slot + t0)
    return pl.pallas_call(
        _final_kernel,
        grid=(n_rows // ROW_TILE,),
        in_specs=[tile_block(t0),
                  slot(0), slot(1), slot(2), slot(3), _const_spec((1, D_MODEL))],
        out_specs=pl.BlockSpec((ROW_TILE, D_MODEL), lambda i: (i, 0)),
        out_shape=jax.ShapeDtypeStruct((n_rows, D_MODEL), F32),
        scratch_shapes=[pltpu.VMEM((ROW_TILE * TILE_ROWS, LANES), F32)],
        compiler_params=_params(("arbitrary",)),
        name="final",
    )(x1, slots, slots, slots, slots, g_final)


def _rope_tables(seq_len, dec_seq):
    half = ROT_DIM // 2
    inv = ROPE_THETA ** (-jnp.arange(0, ROT_DIM, 2, dtype=F32) / ROT_DIM)
    pos_p = jnp.arange(seq_len, dtype=jnp.int32)
    pos_s = PAST_LEN + (jnp.arange(ROW_TILE, dtype=jnp.int32) % dec_seq)
    pos = jnp.concatenate([pos_p, pos_s]).astype(F32)
    ang = pos[:, None] * inv[None, :]
    cos, sin = jnp.cos(ang), jnp.sin(ang)
    ones = jnp.ones((pos.shape[0], HEAD_DIM - ROT_DIM), F32)
    cos_h = jnp.concatenate([cos, cos, ones], axis=1)
    sin_h = jnp.concatenate([-sin, sin, 0.0 * ones], axis=1)
    return jnp.concatenate([cos_h, cos_h], axis=1), jnp.concatenate([sin_h, sin_h], axis=1)


def _block_tables(counts, n_all):
    n_slots = TOP_K * n_all
    n_blocks = -(-n_slots // FFN_BLOCK) + N_EXPERTS
    padded = (counts + FFN_BLOCK - 1) // FFN_BLOCK * FFN_BLOCK
    pad_end = jnp.cumsum(padded)
    pad_start = (pad_end - padded).astype(jnp.int32).reshape(N_EXPERTS, 1)
    block_start = jnp.arange(n_blocks, dtype=jnp.int32) * FFN_BLOCK
    block_expert = jnp.minimum(jnp.sum(pad_end[None, :] <= block_start[:, None], axis=1), N_EXPERTS - 1)
    n_used = (pad_end[-1] // FFN_BLOCK).astype(jnp.int32).reshape(1)
    return pad_start, block_expert.astype(jnp.int32), n_used, n_blocks


def kernel(x_prompt, x_sample, cache_swa_k, cache_swa_v, state_gla, g_mix, w_in, w_gk_up, b_gk, sinks,
           gla_norm, w_branch_a, w_branch_b, w_out, g_ffn, router_w, router_b, w_up, b_up, w_down, b_down,
           g_final):
    batch, seq_len, _ = x_prompt.shape
    dec_batch, dec_seq, _ = x_sample.shape
    n_p, n_s = batch * seq_len, dec_batch * dec_seq
    n_all = n_p + n_s
    assert w_in.shape[0] == 1, "one layer: the final norm is fused after the only MoE"
    assert seq_len % ROW_TILE == 0 and n_s % ROW_TILE == 0 and ROW_TILE % dec_seq == 0
    assert dec_seq % SUBLANES == 0 and dec_batch % SAMPLE_GROUP == 0
    assert SAMPLE_GROUP * dec_seq == CHUNK and (dec_seq & (dec_seq - 1)) == 0

    xp2 = x_prompt.reshape(n_p, D_MODEL)
    xs2 = x_sample.reshape(n_s, D_MODEL)
    cos_tab, sin_tab = _rope_tables(seq_len, dec_seq)
    w = w_in[0]
    gl0 = _C_RG
    w_all = jnp.concatenate(
        [w[:, :gl0], w[:, gl0 + GK_RANK:], w[:, gl0:gl0 + GK_RANK],
         jnp.zeros((D_MODEL, LANES - GK_RANK), w.dtype)], axis=1).astype(BF16)
    wup_pad = jnp.concatenate([w_gk_up[0], jnp.zeros((LANES - GK_RANK, GLA_KDIM), F32)], axis=0).astype(BF16)
    qa, ka, va, qg, kg, vg, la, rg, ga, gb = _proj(
        xp2, xs2, g_mix[0].reshape(1, D_MODEL), cos_tab, sin_tab, w_all, wup_pad,
        b_gk[0].reshape(1, GLA_KDIM), seq_len)

    oa_p = _swa_prompt(sinks[0], qa, ka, va, batch, seq_len)
    oa_s, nk_s, nv_s = _swa_sample(sinks[0], qa, ka, va, cache_swa_k[0], cache_swa_v[0], n_p,
                                   dec_batch, dec_seq)
    norm_row = jnp.tile(gla_norm[0], GLA_HEADS).reshape(1, GLA_VDIM)
    og_p, s_fin = _gla_prompt(qg, kg, la, vg, rg, norm_row, batch, seq_len)
    og_s, s_new = _gla_sample(qg, kg, la, vg, rg, norm_row, state_gla[0], n_p, dec_batch, dec_seq)

    x1 = _merge(xp2, xs2, oa_p, oa_s, og_p, og_s, ga, gb, w_branch_a[0].astype(BF16),
                w_branch_b[0].astype(BF16), w_out[0].astype(BF16))

    g_ffn_row = g_ffn[0].reshape(1, D_MODEL)
    rw_t = router_w[0].T
    rw_hi = rw_t.astype(BF16)
    rw_lo = (rw_t - rw_hi.astype(F32)).astype(BF16)
    eidx, rank, gate, cnt = _route(x1, g_ffn_row, rw_hi, rw_lo, router_b[0].reshape(N_EXPERTS, 1))
    counts = cnt[:, 0].astype(jnp.int32)
    pad_start, block_expert, n_used, n_blocks = _block_tables(counts, n_all)
    src, dst, gate_sorted = _invert(eidx, rank, gate, pad_start, n_blocks)
    prime = (TOP_K * n_all + 2 * FFN_BLOCK + jnp.arange(FFN_BLOCK, dtype=jnp.int32)).reshape(1, FFN_BLOCK)
    dst = jnp.concatenate([dst, prime], axis=0)
    slots = _ffn(block_expert, n_used, src.reshape(n_blocks, 1, FFN_BLOCK),
                 dst.reshape(n_blocks + 1, 1, FFN_BLOCK), gate_sorted.reshape(n_blocks * FFN_BLOCK, 1),
                 x1, g_ffn_row, w_up[0], b_up[0].reshape(N_EXPERTS, 1, 2 * D_FF), w_down[0],
                 b_down[0].reshape(N_EXPERTS, 1, D_MODEL), TOP_K * n_all + 3 * FFN_BLOCK)

    g_out = g_final.reshape(1, D_MODEL)
    y_p = _final(x1, slots, g_out, 0, n_p, n_all)
    y_s = _final(x1, slots, g_out, n_p, n_s, n_all)

    kv_shape = (1, -1, WINDOW, KV_HEADS, HEAD_DIM)
    new_k_p = ka[:n_p].reshape(batch, seq_len, KV_WIDTH)[:, -WINDOW:].reshape(kv_shape)
    new_v_p = va[:n_p].reshape(batch, seq_len, KV_WIDTH)[:, -WINDOW:].reshape(kv_shape)
    return (y_p.reshape(batch, seq_len, D_MODEL), y_s.reshape(dec_batch, dec_seq, D_MODEL),
            new_k_p, new_v_p, s_fin[None], nk_s.reshape(kv_shape), nv_s.reshape(kv_shape), s_new[None])
```

```python
import functools

import numpy as np
import jax
import jax.numpy as jnp
from jax import lax
from jax.experimental import pallas as pl
from jax.experimental.pallas import tpu as pltpu

D_MODEL = 1024
PAST_LEN = 8192
HEAD_DIM = 64
N_HEADS = 8
KV_HEADS = 2
GROUP = N_HEADS // KV_HEADS
WINDOW = 128
ROT_DIM = HEAD_DIM // 4
ROPE_THETA = 500000.0
ATT_WIDTH = N_HEADS * HEAD_DIM
KV_WIDTH = KV_HEADS * HEAD_DIM
GLA_HEADS = 4
GLA_KDIM = D_MODEL // 2
GLA_VDIM = D_MODEL
GLA_DK = GLA_KDIM // GLA_HEADS
GLA_DV = GLA_VDIM // GLA_HEADS
GK_RANK = 16
GK_NORMALIZER = 16.0
N_EXPERTS = 32
TOP_K = 4
D_FF = D_MODEL
SWIGLU_LIMIT = 7.0
SWIGLU_ALPHA = 1.702
EPS = 1e-5
NEG_INF = -1e30

LANES = 128
SUBLANES = 8
VMEM_LIMIT_BYTES = 56 * 1024 * 1024

ROW_TILE = 256
CHUNK = 128
FFN_BLOCK = 256

BF16 = jnp.bfloat16
F32 = jnp.float32

_C_QA, _C_KA, _C_VA, _C_QG, _C_KG, _C_VG, _C_RG, _C_GA, _C_GB, _C_GL, _C_END = (
    0, 512, 640, 768, 1280, 1792, 2816, 3840, 4864, 5888, 6016)


def _const_spec(shape):
    nd = len(shape)
    return pl.BlockSpec(shape, lambda *_: (0,) * nd)


def _params(sem, vmem=VMEM_LIMIT_BYTES):
    return pltpu.CompilerParams(dimension_semantics=sem, vmem_limit_bytes=vmem)


def _nt_dot(a, b):
    return lax.dot_general(a, b, (((1,), (1,)), ((), ())), preferred_element_type=F32)


def _dot(a, b):
    return jnp.dot(a, b, preferred_element_type=F32)


TILE_ROWS = D_MODEL // LANES
assert TILE_ROWS == SUBLANES


def _load_token_tiles(ref, n_tokens):
    return jnp.concatenate([ref[pl.ds(c, n_tokens, stride=TILE_ROWS), :] for c in range(TILE_ROWS)], axis=1)


def _store_token_tiles(ref, x):
    for c in range(TILE_ROWS):
        ref[pl.ds(c, x.shape[0], stride=TILE_ROWS), :] = x[:, c * LANES:(c + 1) * LANES]


def _rope(x, cos_t, sin_t, n_rep):
    width = x.shape[1]
    cos_f = jnp.concatenate([cos_t] * n_rep, axis=1) if n_rep > 1 else cos_t
    sin_f = jnp.concatenate([sin_t] * n_rep, axis=1) if n_rep > 1 else sin_t
    lane = lax.broadcasted_iota(jnp.int32, x.shape, 1) % HEAD_DIM
    up = pltpu.roll(x, width - ROT_DIM // 2, 1)
    down = pltpu.roll(x, ROT_DIM // 2, 1)
    partner = jnp.where(lane < ROT_DIM // 2, up, down)
    return x * cos_f + partner * sin_f


def _proj_kernel(n_prompt_tiles, xp_ref, xs_ref, g_ref, cos_ref, sin_ref, w_ref, wup_ref, bgk_ref,
                 qa_ref, ka_ref, va_ref, qg_ref, kg_ref, vg_ref, la_ref, rg_ref, ga_ref, gb_ref):
    i = pl.program_id(0)
    x = jnp.where(i < n_prompt_tiles, xp_ref[...], xs_ref[...])
    ms = jnp.mean(x * x, axis=-1, keepdims=True)
    h = (x * lax.rsqrt(ms + EPS) * g_ref[...]).astype(BF16)
    cos_t = cos_ref[...]
    sin_t = sin_ref[...]

    def seg(a, b):
        return _dot(h, w_ref[:, a:b])

    qa = _rope(seg(_C_QA, _C_KA), cos_t, sin_t, ATT_WIDTH // LANES)
    qa_ref[...] = (qa * (HEAD_DIM ** -0.5)).astype(BF16)
    ka_ref[...] = _rope(seg(_C_KA, _C_VA), cos_t, sin_t, 1)
    va_ref[...] = seg(_C_VA, _C_QG)
    qg_ref[...] = seg(_C_QG, _C_KG) * (GLA_DK ** -0.5)
    kg_ref[...] = seg(_C_KG, _C_VG)
    vg_ref[...] = seg(_C_VG, _C_RG).astype(BF16)
    rg_ref[...] = seg(_C_RG, _C_GA).astype(BF16)
    ga_ref[...] = seg(_C_GA, _C_GB).astype(BF16)
    gb_ref[...] = seg(_C_GB, _C_GL).astype(BF16)
    gk_low = seg(_C_GL, _C_END).astype(BF16)
    z = _dot(gk_low, wup_ref[...]) + bgk_ref[...]
    log_sig = jnp.minimum(z, 0.0) - jnp.log1p(jnp.exp(-jnp.abs(z)))
    la_ref[...] = log_sig / GK_NORMALIZER


def _proj(xp2, xs2, g_mix, cos_tab, sin_tab, w_all, wup_pad, b_gk, seq_len):
    n_p, n_s = xp2.shape[0], xs2.shape[0]
    n_all = n_p + n_s
    npt, nst = n_p // ROW_TILE, n_s // ROW_TILE
    tiles_per_seq = seq_len // ROW_TILE

    def tab_map(i):
        return (jnp.where(i < npt, i % tiles_per_seq, tiles_per_seq), 0)

    row = lambda w: pl.BlockSpec((ROW_TILE, w), lambda i: (i, 0))
    widths = [(ATT_WIDTH, BF16), (KV_WIDTH, F32), (KV_WIDTH, F32), (GLA_KDIM, F32), (GLA_KDIM, F32),
              (GLA_VDIM, BF16), (GLA_KDIM, F32), (GLA_VDIM, BF16), (D_MODEL, BF16), (D_MODEL, BF16)]
    return pl.pallas_call(
        functools.partial(_proj_kernel, npt),
        grid=(npt + nst,),
        in_specs=[
            pl.BlockSpec((ROW_TILE, D_MODEL), lambda i: (jnp.minimum(i, npt - 1), 0)),
            pl.BlockSpec((ROW_TILE, D_MODEL), lambda i: (jnp.maximum(i - npt, 0), 0)),
            _const_spec((1, D_MODEL)),
            pl.BlockSpec((ROW_TILE, LANES), tab_map),
            pl.BlockSpec((ROW_TILE, LANES), tab_map),
            _const_spec(w_all.shape),
            _const_spec(wup_pad.shape),
            _const_spec((1, GLA_KDIM)),
        ],
        out_specs=[row(w) for w, _ in widths],
        out_shape=[jax.ShapeDtypeStruct((n_all, w), dt) for w, dt in widths],
        compiler_params=_params(("arbitrary",)),
        name="proj",
    )(xp2, xs2, g_mix, cos_tab, sin_tab, w_all, wup_pad, b_gk)


def _pair_blocks(kk):
    lane = lax.broadcasted_iota(jnp.int32, kk.shape, 1)
    lo = lane < HEAD_DIM
    swapped = pltpu.roll(kk, HEAD_DIM, 1)
    zero = jnp.zeros_like(kk)
    blocks = []
    for kh in range(KV_HEADS):
        left = jnp.where(lo, kk if kh == 0 else swapped, zero)
        right = jnp.where(lo, zero, swapped if kh == 0 else kk)
        blocks.append(jnp.concatenate([left, right], axis=0).astype(BF16))
    return blocks


def _sink_softmax(s, valid, sink):
    s = jnp.where(valid, s, NEG_INF)
    m = jnp.maximum(jnp.max(s, axis=-1, keepdims=True), sink)
    p = jnp.exp(s - m)
    denom = jnp.sum(p, axis=-1, keepdims=True) + jnp.exp(sink - m)
    return (p * (1.0 / denom)).astype(BF16)


def _attend(q, kk, vv, valid, sink_ref, o_ref):
    rows, keys = valid.shape
    kblocks = _pair_blocks(kk)
    vblocks = _pair_blocks(vv)
    for kh in range(KV_HEADS):
        base = kh * GROUP * HEAD_DIM
        qq = jnp.concatenate([q[:, base:base + LANES], q[:, base + LANES:base + 2 * LANES]], axis=0)
        s = _nt_dot(qq, kblocks[kh])
        for r in range(2):
            probs = []
            for c in range(2):
                head = kh * GROUP + 2 * r + c
                probs.append(_sink_softmax(s[r * rows:(r + 1) * rows, c * keys:(c + 1) * keys],
                                           valid, sink_ref[head]))
            p = jnp.concatenate(probs, axis=1)
            o_ref[:, base + r * LANES:base + (r + 1) * LANES] = _dot(p, vblocks[kh]).astype(o_ref.dtype)


def _swa_prompt_kernel(sink_ref, q_ref, kp_ref, kc_ref, vp_ref, vc_ref, o_ref):
    j = pl.program_id(1)
    kk = jnp.concatenate([kp_ref[...], kc_ref[...]], axis=0)
    vv = jnp.concatenate([vp_ref[...], vc_ref[...]], axis=0)
    row = lax.broadcasted_iota(jnp.int32, (WINDOW, 2 * WINDOW), 0)
    col = lax.broadcasted_iota(jnp.int32, (WINDOW, 2 * WINDOW), 1)
    valid = (col > row) & (col <= row + WINDOW) & ((j > 0) | (col >= WINDOW))
    _attend(q_ref[...], kk, vv, valid, sink_ref, o_ref)


def _swa_prompt(sinks, qa, ka, va, batch, seq_len):
    nb = seq_len // WINDOW
    cur = lambda b, j, s: (b * nb + j, 0)
    prev = lambda b, j, s: (b * nb + jnp.maximum(j - 1, 0), 0)
    return pl.pallas_call(
        _swa_prompt_kernel,
        grid_spec=pltpu.PrefetchScalarGridSpec(
            num_scalar_prefetch=1,
            grid=(batch, nb),
            in_specs=[
                pl.BlockSpec((WINDOW, ATT_WIDTH), cur),
                pl.BlockSpec((WINDOW, KV_WIDTH), prev),
                pl.BlockSpec((WINDOW, KV_WIDTH), cur),
                pl.BlockSpec((WINDOW, KV_WIDTH), prev),
                pl.BlockSpec((WINDOW, KV_WIDTH), cur),
            ],
            out_specs=pl.BlockSpec((WINDOW, ATT_WIDTH), cur),
        ),
        out_shape=jax.ShapeDtypeStruct((batch * seq_len, ATT_WIDTH), BF16),
        compiler_params=_params(("arbitrary", "arbitrary")),
        name="swa_prompt",
    )(sinks, qa, ka, ka, va, va)


SAMPLE_GROUP = 16


def _swa_sample_kernel(dec_seq, sink_ref, q_ref, kn_ref, vn_ref, ck_ref, cv_ref, o_ref, nk_ref, nv_ref):
    rows = SAMPLE_GROUP * dec_seq
    ck = ck_ref[...]
    cv = cv_ref[...]
    kn = kn_ref[...]
    vn = vn_ref[...]
    nk_ref[:, :WINDOW - dec_seq, :] = ck[:, dec_seq:, :]
    nv_ref[:, :WINDOW - dec_seq, :] = cv[:, dec_seq:, :]
    nk_ref[:, WINDOW - dec_seq:, :] = kn.reshape(SAMPLE_GROUP, dec_seq, KV_WIDTH)
    nv_ref[:, WINDOW - dec_seq:, :] = vn.reshape(SAMPLE_GROUP, dec_seq, KV_WIDTH)
    n_cache = SAMPLE_GROUP * WINDOW
    kk = jnp.concatenate([ck.reshape(n_cache, KV_WIDTH), kn], axis=0)
    vv = jnp.concatenate([cv.reshape(n_cache, KV_WIDTH), vn], axis=0)
    keys = n_cache + rows
    row = lax.broadcasted_iota(jnp.int32, (rows, keys), 0)
    col = lax.broadcasted_iota(jnp.int32, (rows, keys), 1)
    q_b, q_s = row // dec_seq, row % dec_seq
    is_cache = col < n_cache
    new = col - n_cache
    valid_cache = (col // WINDOW == q_b) & (col % WINDOW > q_s)
    valid_new = (new // dec_seq == q_b) & (new % dec_seq <= q_s)
    valid = (is_cache & valid_cache) | (jnp.logical_not(is_cache) & valid_new)
    _attend(q_ref[...], kk, vv, valid, sink_ref, o_ref)


def _swa_sample(sinks, qa, ka, va, cache_k, cache_v, n_prompt_rows, dec_batch, dec_seq):
    rows = SAMPLE_GROUP * dec_seq
    off = n_prompt_rows // rows
    tok = lambda g, s: (off + g, 0)
    cache = lambda g, s: (g, 0, 0)
    cshape = (dec_batch, WINDOW, KV_WIDTH)
    return pl.pallas_call(
        functools.partial(_swa_sample_kernel, dec_seq),
        grid_spec=pltpu.PrefetchScalarGridSpec(
            num_scalar_prefetch=1,
            grid=(dec_batch // SAMPLE_GROUP,),
            in_specs=[
                pl.BlockSpec((rows, ATT_WIDTH), tok),
                pl.BlockSpec((rows, KV_WIDTH), tok),
                pl.BlockSpec((rows, KV_WIDTH), tok),
                pl.BlockSpec((SAMPLE_GROUP, WINDOW, KV_WIDTH), cache),
                pl.BlockSpec((SAMPLE_GROUP, WINDOW, KV_WIDTH), cache),
            ],
            out_specs=[
                pl.BlockSpec((rows, ATT_WIDTH), lambda g, s: (g, 0)),
                pl.BlockSpec((SAMPLE_GROUP, WINDOW, KV_WIDTH), cache),
                pl.BlockSpec((SAMPLE_GROUP, WINDOW, KV_WIDTH), cache),
            ],
        ),
        out_shape=[jax.ShapeDtypeStruct((dec_batch * dec_seq, ATT_WIDTH), BF16),
                   jax.ShapeDtypeStruct(cshape, F32), jax.ShapeDtypeStruct(cshape, F32)],
        compiler_params=_params(("arbitrary",)),
        name="swa_sample",
    )(sinks, qa, ka, va, cache_k.reshape(cshape), cache_v.reshape(cshape))


def _chunk_tables(seg):
    n_lev = int(np.log2(seg))
    t = np.arange(CHUNK)
    seg_start = (t // seg) * seg
    u = np.arange(CHUNK)[None, :]

    def prefix(end):
        return ((u >= seg_start[:, None]) & (u <= end[:, None])).astype(np.float32)

    blocks = [prefix(t)]
    for d in range(min(n_lev, _MATMUL_LEVELS)):
        m = 1 << d
        ref = (t >> (d + 1) << (d + 1)) + m - 1
        blocks.append(prefix(ref))
    lhs = np.concatenate(blocks, axis=0)
    lhs2 = np.concatenate([lhs, lhs], axis=1)
    tt, ss = t[:, None], t[None, :]
    x = tt ^ ss
    lev = np.where(x > 0, np.floor(np.log2(np.maximum(x, 1))).astype(np.int32), n_lev)
    lev = np.where((ss > tt) | (tt // seg != ss // seg), -1, lev)
    lev = np.where(tt == ss, n_lev, lev)
    return jnp.asarray(lhs2, BF16), jnp.asarray(lev, jnp.int32), n_lev


_MATMUL_LEVELS = 3


def _group_row(x, group, row):
    width = x.shape[1]
    parts = [jnp.broadcast_to(x[g * group + row:g * group + row + 1, :], (group, width))
             for g in range(x.shape[0] // group)]
    return parts[0] if len(parts) == 1 else jnp.concatenate(parts, axis=0)


_HALF_ROW_LEVELS = 4
LOG2_E = 1.4426950408889634


def _halves(x, m, which):
    parts = [x[(2 * g + which) * m:(2 * g + which + 1) * m] for g in range(x.shape[0] // (2 * m))]
    return parts[0] if len(parts) == 1 else jnp.concatenate(parts, axis=0)


def _unhalve(xh, m, which):
    zero = jnp.zeros((m, xh.shape[1]), xh.dtype)
    parts = []
    for g in range(xh.shape[0] // m):
        blk = xh[g * m:(g + 1) * m]
        parts += [zero, blk] if which else [blk, zero]
    return jnp.concatenate(parts, axis=0)


def _gla_chunk_terms(q, k, la, lhs2, level, n_lev, seg):
    la2 = la * LOG2_E
    hi = la2.astype(BF16)
    lo = (la2 - hi.astype(F32)).astype(BF16)
    sums = _dot(lhs2, jnp.concatenate([hi, lo], axis=0))
    b = sums[0:CHUNK]
    b_last = _group_row(b, seg, seg - 1)
    q_main = (q * jnp.exp2(b)).astype(BF16)
    k_upd = k * jnp.exp2(b_last - b)
    q_lev = [None] * n_lev
    k_lev = [None] * n_lev
    for d in range(n_lev):
        m = 1 << d
        if d < _MATMUL_LEVELS:
            ref = sums[(1 + d) * CHUNK:(2 + d) * CHUNK]
        elif d < _HALF_ROW_LEVELS:
            ref = _group_row(b, 2 * m, m - 1)
        if d < _HALF_ROW_LEVELS:
            q_lev[d] = (q * jnp.exp2(b - ref)).astype(BF16)
            k_lev[d] = (k * jnp.exp2(ref - b)).astype(BF16)
        else:
            ref_h = _group_row(_halves(b, m, 0), m, m - 1)
            q_lev[d] = _unhalve((_halves(q, m, 1) * jnp.exp2(_halves(b, m, 1) - ref_h)).astype(BF16), m, 1)
            k_lev[d] = _unhalve((_halves(k, m, 0) * jnp.exp2(ref_h - _halves(b, m, 0))).astype(BF16), m, 0)
    q_b, k_b = q.astype(BF16), k.astype(BF16)

    def att(h):
        hs = slice(h * GLA_DK, (h + 1) * GLA_DK)
        acc = jnp.where(level == n_lev, _nt_dot(q_b[:, hs], k_b[:, hs]), 0.0)
        for d in range(n_lev):
            acc = jnp.where(level == d, _nt_dot(q_lev[d][:, hs], k_lev[d][:, hs]), acc)
        return acc

    return q_main, k_upd, att, b_last


def _gla_out(o, r, norm):
    parts = []
    for h in range(GLA_HEADS):
        oh = o[:, h * GLA_DV:(h + 1) * GLA_DV]
        ms = jnp.mean(oh * oh, axis=-1, keepdims=True)
        parts.append(oh * lax.rsqrt(ms + EPS))
    y = jnp.concatenate(parts, axis=1) * norm
    rf = r.astype(F32)
    return y * (rf * jax.nn.sigmoid(rf))


def _gla_prompt_kernel(n_lev, n_par, *refs):
    seq_refs = [refs[5 * j:5 * j + 5] for j in range(n_par)]
    norm_ref, lhs_ref, lev_ref = refs[5 * n_par:5 * n_par + 3]
    o_ref, sfin_ref, s_scr = refs[5 * n_par + 3:]
    c = pl.program_id(1)

    @pl.when(c == 0)
    def _():
        s_scr[...] = jnp.zeros_like(s_scr)

    for j, (q_ref, k_ref, la_ref, v_ref, r_ref) in enumerate(seq_refs):
        q_main, k_upd, att, b_last = _gla_chunk_terms(q_ref[...], k_ref[...], la_ref[...], lhs_ref[...],
                                                      lev_ref[...], n_lev, CHUNK)
        v = v_ref[...]
        outs = []
        for h in range(GLA_HEADS):
            hs = slice(h * GLA_DK, (h + 1) * GLA_DK)
            vh = v[:, h * GLA_DV:(h + 1) * GLA_DV]
            s0 = s_scr[j, h]
            o_h = _dot(q_main[:, hs], s0.astype(BF16)) + _dot(att(h).astype(BF16), vh)
            outs.append(o_h)
            decay = jnp.exp2(b_last[:, hs]).T
            k_t = k_upd[:, hs].T.astype(BF16)
            s_scr[j, h] = jnp.concatenate([decay, decay], axis=1) * s0 + _dot(k_t, vh)
        o_ref[j] = _gla_out(jnp.concatenate(outs, axis=1), r_ref[...], norm_ref[...]).astype(BF16)

    @pl.when(c == pl.num_programs(1) - 1)
    def _():
        sfin_ref[...] = s_scr[...]


def _gla_prompt(qg, kg, la, vg, rg, norm_row, batch, seq_len):
    nc = seq_len // CHUNK
    n_par = 2 if batch % 2 == 0 else 1
    lhs2, level, n_lev = _chunk_tables(CHUNK)
    tok = lambda j, w: pl.BlockSpec((CHUNK, w), lambda b, c: ((b * n_par + j) * nc + c, 0))
    seq_specs, seq_args = [], []
    for j in range(n_par):
        seq_specs += [tok(j, GLA_KDIM), tok(j, GLA_KDIM), tok(j, GLA_KDIM), tok(j, GLA_VDIM), tok(j, GLA_VDIM)]
        seq_args += [qg, kg, la, vg, rg]
    og, s_fin = pl.pallas_call(
        functools.partial(_gla_prompt_kernel, n_lev, n_par),
        grid=(batch // n_par, nc),
        in_specs=seq_specs + [_const_spec((1, GLA_VDIM)), _const_spec(lhs2.shape), _const_spec(level.shape)],
        out_specs=[pl.BlockSpec((n_par, CHUNK, GLA_VDIM), lambda b, c: (b, c, 0)),
                   pl.BlockSpec((n_par, GLA_HEADS, GLA_DK, GLA_DV), lambda b, c: (b, 0, 0, 0))],
        out_shape=[jax.ShapeDtypeStruct((batch, seq_len, GLA_VDIM), BF16),
                   jax.ShapeDtypeStruct((batch, GLA_HEADS, GLA_DK, GLA_DV), F32)],
        scratch_shapes=[pltpu.VMEM((n_par, GLA_HEADS, GLA_DK, GLA_DV), F32)],
        compiler_params=_params(("arbitrary", "arbitrary")),
        name="gla_prompt",
    )(*seq_args, norm_row, lhs2, level)
    return og.reshape(batch * seq_len, GLA_VDIM), s_fin


def _gla_sample_kernel(n_lev, dec_seq, q_ref, k_ref, la_ref, v_ref, r_ref, norm_ref, lhs_ref, lev_ref,
                       s0_ref, o_ref, snew_ref):
    q_main, k_upd, att, b_last = _gla_chunk_terms(q_ref[...], k_ref[...], la_ref[...], lhs_ref[...],
                                                  lev_ref[...], n_lev, dec_seq)
    v = v_ref[...]
    n_b = CHUNK // dec_seq
    row_b = lax.broadcasted_iota(jnp.int32, (CHUNK, GLA_DK), 0) // dec_seq
    col_b = lax.broadcasted_iota(jnp.int32, (GLA_DK, CHUNK), 1) // dec_seq
    outs = []
    for h in range(GLA_HEADS):
        hs = slice(h * GLA_DK, (h + 1) * GLA_DK)
        vh = v[:, h * GLA_DV:(h + 1) * GLA_DV]
        qm = q_main[:, hs]
        decay_t = jnp.exp2(b_last[:, hs]).T
        k_t = k_upd[:, hs].T.astype(BF16)
        o_h = _dot(att(h).astype(BF16), vh)
        for bi in range(n_b):
            s0 = s0_ref[bi, h]
            o_h = o_h + _dot(jnp.where(row_b == bi, qm, jnp.zeros_like(qm)), s0.astype(BF16))
            decay = jnp.broadcast_to(decay_t[:, bi * dec_seq:bi * dec_seq + 1], (GLA_DK, GLA_DV))
            k_b = jnp.where(col_b == bi, k_t, jnp.zeros_like(k_t))
            snew_ref[bi, h] = decay * s0 + _dot(k_b, vh)
        outs.append(o_h)
    o_ref[...] = _gla_out(jnp.concatenate(outs, axis=1), r_ref[...], norm_ref[...]).astype(o_ref.dtype)


def _gla_sample(qg, kg, la, vg, rg, norm_row, state, n_prompt_rows, dec_batch, dec_seq):
    n_b = CHUNK // dec_seq
    off = n_prompt_rows // CHUNK
    lhs3, level, n_lev = _chunk_tables(dec_seq)
    tok = lambda w: pl.BlockSpec((CHUNK, w), lambda g: (off + g, 0))
    st = pl.BlockSpec((n_b, GLA_HEADS, GLA_DK, GLA_DV), lambda g: (g, 0, 0, 0))
    return pl.pallas_call(
        functools.partial(_gla_sample_kernel, n_lev, dec_seq),
        grid=(dec_batch // n_b,),
        in_specs=[tok(GLA_KDIM), tok(GLA_KDIM), tok(GLA_KDIM), tok(GLA_VDIM), tok(GLA_VDIM),
                  _const_spec((1, GLA_VDIM)), _const_spec(lhs3.shape), _const_spec(level.shape), st],
        out_specs=[pl.BlockSpec((CHUNK, GLA_VDIM), lambda g: (g, 0)), st],
        out_shape=[jax.ShapeDtypeStruct((dec_batch * dec_seq, GLA_VDIM), BF16),
                   jax.ShapeDtypeStruct(state.shape, F32)],
        compiler_params=_params(("arbitrary",)),
        name="gla_sample",
    )(qg, kg, la, vg, rg, norm_row, lhs3, level, state)


def _merge_kernel(n_prompt_tiles, xp_ref, xs_ref, oap_ref, oas_ref, ogp_ref, ogs_ref, ga_ref, gb_ref,
                  wa_ref, wb_ref, wo_ref, x1_ref):
    i = pl.program_id(0)
    is_p = i < n_prompt_tiles
    x = jnp.where(is_p, xp_ref[...], xs_ref[...])
    oa = jnp.where(is_p, oap_ref[...], oas_ref[...])
    og = jnp.where(is_p, ogp_ref[...], ogs_ref[...])
    m = (jax.nn.sigmoid(ga_ref[...].astype(F32)) * _dot(oa, wa_ref[...])
         + jax.nn.sigmoid(gb_ref[...].astype(F32)) * _dot(og, wb_ref[...]))
    _store_token_tiles(x1_ref, x + _dot(m.astype(BF16), wo_ref[...]))


def _merge(xp2, xs2, oa_p, oa_s, og_p, og_s, ga, gb, wa, wb, wo):
    n_p, n_s = xp2.shape[0], xs2.shape[0]
    npt, nst = n_p // ROW_TILE, n_s // ROW_TILE
    p_map = lambda i: (jnp.minimum(i, npt - 1), 0)
    s_map = lambda i: (jnp.maximum(i - npt, 0), 0)
    row = lambda w: pl.BlockSpec((ROW_TILE, w), lambda i: (i, 0))
    return pl.pallas_call(
        functools.partial(_merge_kernel, npt),
        grid=(npt + nst,),
        in_specs=[
            pl.BlockSpec((ROW_TILE, D_MODEL), p_map), pl.BlockSpec((ROW_TILE, D_MODEL), s_map),
            pl.BlockSpec((ROW_TILE, ATT_WIDTH), p_map), pl.BlockSpec((ROW_TILE, ATT_WIDTH), s_map),
            pl.BlockSpec((ROW_TILE, GLA_VDIM), p_map), pl.BlockSpec((ROW_TILE, GLA_VDIM), s_map),
            row(D_MODEL), row(D_MODEL),
            _const_spec(wa.shape), _const_spec(wb.shape), _const_spec(wo.shape),
        ],
        out_specs=pl.BlockSpec((ROW_TILE * TILE_ROWS, LANES), lambda i: (i, 0)),
        out_shape=jax.ShapeDtypeStruct(((n_p + n_s) * TILE_ROWS, LANES), F32),
        compiler_params=_params(("arbitrary",)),
        name="merge",
    )(xp2, xs2, oa_p, oa_s, og_p, og_s, ga, gb, wa, wb, wo)


def _route_kernel(x1_ref, gffn_ref, rwh_ref, rwl_ref, rb_ref, upper_ref, eidx_ref, rank_ref, gate_ref,
                  cnt_ref, cnt_scr):
    i = pl.program_id(0)

    @pl.when(i == 0)
    def _():
        cnt_scr[...] = jnp.zeros_like(cnt_scr)

    x1 = _load_token_tiles(x1_ref, ROW_TILE)
    ms = jnp.mean(x1 * x1, axis=-1, keepdims=True)
    h2 = x1 * lax.rsqrt(ms + EPS) * gffn_ref[...]
    h_hi = h2.astype(BF16)
    h_lo = (h2 - h_hi.astype(F32)).astype(BF16)
    rwh, rwl = rwh_ref[...], rwl_ref[...]
    logits = _nt_dot(rwh, h_hi) + _nt_dot(rwl, h_hi) + _nt_dot(rwh, h_lo) + rb_ref[...]
    eid = lax.broadcasted_iota(jnp.int32, logits.shape, 0)
    upper = upper_ref[...]
    base = cnt_scr[...]
    vals, rows_e, rows_r = [], [], []
    lg = logits
    for _ in range(TOP_K):
        mx = jnp.max(lg, axis=0, keepdims=True)
        sel = jnp.min(jnp.where(lg == mx, eid, N_EXPERTS), axis=0, keepdims=True)
        onehot = eid == sel
        oh = onehot.astype(F32)
        before = _dot(onehot.astype(BF16), upper)
        rows_r.append(jnp.sum(oh * (base + before), axis=0, keepdims=True))
        base = base + jnp.sum(oh, axis=1, keepdims=True)
        vals.append(mx)
        rows_e.append(sel)
        lg = jnp.where(onehot, -jnp.inf, lg)
    cnt_scr[...] = base
    ex = [jnp.exp(v - vals[0]) for v in vals]
    inv = 1.0 / (ex[0] + ex[1] + ex[2] + ex[3])
    eidx_ref[...] = jnp.concatenate(rows_e, axis=0)
    rank_ref[...] = jnp.concatenate(rows_r, axis=0).astype(jnp.int32)
    gate_ref[...] = jnp.concatenate([e * inv for e in ex], axis=0)
    cnt_ref[...] = jnp.broadcast_to(base, cnt_ref.shape)


def _route(x1, g_ffn, rw_hi, rw_lo, rb_col):
    n = x1.shape[0] // TILE_ROWS
    upper = jnp.asarray(np.triu(np.ones((ROW_TILE, ROW_TILE), np.float32), 1), BF16)
    col = lambda dt: (pl.BlockSpec((TOP_K, ROW_TILE), lambda i: (0, i)), jax.ShapeDtypeStruct((TOP_K, n), dt))
    outs = [col(jnp.int32), col(jnp.int32), col(F32),
            (_const_spec((N_EXPERTS, LANES)), jax.ShapeDtypeStruct((N_EXPERTS, LANES), F32))]
    return pl.pallas_call(
        _route_kernel,
        grid=(n // ROW_TILE,),
        in_specs=[pl.BlockSpec((ROW_TILE * TILE_ROWS, LANES), lambda i: (i, 0)), _const_spec((1, D_MODEL)),
                  _const_spec(rw_hi.shape), _const_spec(rw_lo.shape), _const_spec((N_EXPERTS, 1)),
                  _const_spec(upper.shape)],
        out_specs=[o[0] for o in outs],
        out_shape=[o[1] for o in outs],
        scratch_shapes=[pltpu.VMEM((N_EXPERTS, 1), F32)],
        compiler_params=_params(("arbitrary",)),
        name="route",
    )(x1, g_ffn, rw_hi, rw_lo, rb_col, upper)


_INV_FIELDS = 6


def _split3_exact(x):
    def trunc(v):
        return pltpu.bitcast(pltpu.bitcast(v, jnp.uint32) & jnp.uint32(0xFFFF0000), F32)
    hi = trunc(x)
    rest = x - hi
    mid = trunc(rest)
    return hi, mid, rest - mid


def _invert_kernel(n_blocks, eidx_ref, rank_ref, gate_ref, pstart_ref, acc_ref):
    i = pl.program_id(0)

    @pl.when(i == 0)
    def _():
        acc_ref[...] = jnp.zeros(acc_ref.shape, acc_ref.dtype)

    eidx, rank, gate = eidx_ref[...], rank_ref[...], gate_ref[...]
    pstart = pstart_ref[...]
    rows = eidx.shape[1]
    eid = lax.broadcasted_iota(jnp.int32, (N_EXPERTS, rows), 0)
    blk_id = lax.broadcasted_iota(jnp.int32, (n_blocks, rows), 0)
    off_id = lax.broadcasted_iota(jnp.int32, (FFN_BLOCK, rows), 0)
    tok = i * rows + lax.broadcasted_iota(jnp.int32, (1, rows), 1)
    a_parts, b_parts = [], []
    shift = FFN_BLOCK.bit_length() - 1
    tok_lo = (tok & 255).astype(F32)
    tok_hi = lax.shift_right_logical(tok, 8).astype(F32)
    for k in range(TOP_K):
        base = jnp.sum(jnp.where(eid == eidx[k:k + 1], pstart, 0), axis=0, keepdims=True)
        pos = base + rank[k:k + 1]
        a_parts.append(jnp.where(blk_id == lax.shift_right_logical(pos, shift), 1.0, 0.0).astype(BF16))
        hit = off_id == (pos & (FFN_BLOCK - 1))
        g_hi, g_mid, g_lo = _split3_exact(gate[k:k + 1])
        fields = [tok_lo, tok_hi, jnp.full((1, rows), k + 1.0, F32), g_hi, g_mid, g_lo]
        b_parts.append([jnp.where(hit, f, 0.0).astype(BF16) for f in fields])
    a = jnp.concatenate(a_parts, axis=1)
    for j in range(_INV_FIELDS):
        b = jnp.concatenate([b_parts[k][j] for k in range(TOP_K)], axis=1)
        acc_ref[j] += _nt_dot(a, b)


def _invert(eidx, rank, gate, pad_start, n_blocks):
    n_all = eidx.shape[1]
    assert n_all // 256 < 256, "token index is carried as two byte-sized fields"
    col = pl.BlockSpec((TOP_K, ROW_TILE), lambda i: (0, i))
    shape = (_INV_FIELDS, n_blocks, FFN_BLOCK)
    f = pl.pallas_call(
        functools.partial(_invert_kernel, n_blocks),
        grid=(n_all // ROW_TILE,),
        in_specs=[col, col, col, _const_spec((N_EXPERTS, 1))],
        out_specs=_const_spec(shape),
        out_shape=jax.ShapeDtypeStruct(shape, F32),
        compiler_params=_params(("arbitrary",)),
        name="invert",
    )(eidx, rank, gate, pad_start)
    t = (f[0] + 256.0 * f[1]).astype(jnp.int32)
    kp1 = f[2].astype(jnp.int32)
    blk = lax.broadcasted_iota(jnp.int32, t.shape, 0)
    off = lax.broadcasted_iota(jnp.int32, t.shape, 1)
    dump = TOP_K * n_all + (blk & 1) * FFN_BLOCK + off
    return t, jnp.where(kp1 > 0, (kp1 - 1) * n_all + t, dump), (f[3] + f[4]) + f[5]


def _ffn_kernel(be_ref, nused_ref, src_cur_ref, src_nxt_ref, dst_cur_ref, dst_prev_ref, gate_ref, gffn_ref,
                wup_ref, bup_ref, wdn_ref, bdn_ref, x1_hbm, out_hbm, xbuf0, xbuf1, obuf0, obuf1, wup_bf, wdn_bf,
                gsem, ssem):
    i = pl.program_id(0)
    n_used = nused_ref[0]
    xbufs, obufs = (xbuf0, xbuf1), (obuf0, obuf1)

    def hbm_tile(ref, token):
        return ref.at[pl.ds(pl.multiple_of(token * TILE_ROWS, TILE_ROWS), TILE_ROWS), :]

    def vmem_tile(ref, r):
        return ref.at[pl.ds(r * TILE_ROWS, TILE_ROWS), :]

    def gather_copy(src_ref, r, s):
        return pltpu.make_async_copy(hbm_tile(x1_hbm, src_ref[0, 0, r]), vmem_tile(xbufs[s], r), gsem.at[s])

    def scatter_copy(dst_ref, r, s):
        return pltpu.make_async_copy(vmem_tile(obufs[s], r), hbm_tile(out_hbm, dst_ref[0, 0, r]), ssem.at[s])

    def start_rows(copy_fn):
        for r in range(FFN_BLOCK):
            copy_fn(r).start(priority=r % 2)

    def wait_rows(copy_fn):
        for r in range(FFN_BLOCK):
            copy_fn(r).wait()

    first = i == 0
    changed = first | (be_ref[i] != be_ref[jnp.maximum(i - 1, 0)])

    @pl.when(first)
    def _():
        obuf0[...] = jnp.zeros(obuf0.shape, obuf0.dtype)
        obuf1[...] = jnp.zeros(obuf1.shape, obuf1.dtype)
        n_slot_rows = out_hbm.shape[0] // TILE_ROWS - 3 * FFN_BLOCK
        init = pltpu.make_async_copy(
            obuf1, out_hbm.at[pl.ds((n_slot_rows + FFN_BLOCK) * TILE_ROWS, FFN_BLOCK * TILE_ROWS), :], ssem.at[1])
        init.start()
        init.wait()
        start_rows(lambda r: pltpu.make_async_copy(
            vmem_tile(obuf0, r), out_hbm.at[pl.ds((n_slot_rows + r) * TILE_ROWS, TILE_ROWS), :], ssem.at[0]))
        start_rows(lambda r: gather_copy(src_cur_ref, r, 0))

    @pl.when(changed & (i < n_used))
    def _():
        wup_bf[...] = wup_ref[0].astype(BF16)
        wdn_bf[...] = wdn_ref[0].astype(BF16)

    def step(s):
        wait_rows(lambda r: gather_copy(src_cur_ref, r, s))
        x1 = _load_token_tiles(xbufs[s], FFN_BLOCK)
        ms = jnp.mean(x1 * x1, axis=-1, keepdims=True)
        x = (x1 * lax.rsqrt(ms + EPS) * gffn_ref[...]).astype(BF16)
        start_rows(lambda r: gather_copy(src_nxt_ref, r, 1 - s))
        start_rows(lambda r: scatter_copy(dst_prev_ref, r, 1 - s))
        hu = _dot(x, wup_bf[...]) + bup_ref[0]
        glu = jnp.minimum(hu[:, :D_FF], SWIGLU_LIMIT)
        lin = jnp.clip(hu[:, D_FF:], -SWIGLU_LIMIT, SWIGLU_LIMIT)
        act = glu * jax.nn.sigmoid(SWIGLU_ALPHA * glu) * (lin + 1.0)
        out = (_dot(act.astype(BF16), wdn_bf[...]) + bdn_ref[0]) * gate_ref[...]
        wait_rows(lambda r: scatter_copy(dst_cur_ref, r, s))
        _store_token_tiles(obufs[s], out)

    def drain(s):
        start_rows(lambda r: scatter_copy(dst_cur_ref, r, s))
        wait_rows(lambda r: scatter_copy(dst_cur_ref, r, 1 - s))
        wait_rows(lambda r: scatter_copy(dst_cur_ref, r, s))
        wait_rows(lambda r: gather_copy(src_cur_ref, r, 1 - s))

    for s in range(2):
        pl.when((i < n_used) & (i % 2 == s))(functools.partial(step, s))
    for s in range(2):
        pl.when((i == n_used - 1) & (i % 2 == s))(functools.partial(drain, s))


def _ffn(block_expert, n_used, src, dst, gate_sorted, x1, g_ffn, w_up, b_up, w_down, b_down, n_out_rows):
    n_blocks = block_expert.shape[0]
    smem_block = lambda fn: pl.BlockSpec((1, 1, FFN_BLOCK), fn, memory_space=pltpu.SMEM)
    cur = smem_block(lambda i, be, nu: (i, 0, 0))
    nxt = smem_block(lambda i, be, nu: (jnp.minimum(i + 1, n_blocks - 1), 0, 0))
    prev = smem_block(lambda i, be, nu: (jnp.where(i == 0, n_blocks, i - 1), 0, 0))
    ex3 = lambda i, be, nu: (be[i], 0, 0)
    return pl.pallas_call(
        _ffn_kernel,
        grid_spec=pltpu.PrefetchScalarGridSpec(
            num_scalar_prefetch=2,
            grid=(n_blocks,),
            in_specs=[
                cur, nxt, cur, prev,
                pl.BlockSpec((FFN_BLOCK, 1), lambda i, be, nu: (i, 0)),
                pl.BlockSpec((1, D_MODEL), lambda i, be, nu: (0, 0)),
                pl.BlockSpec((1, D_MODEL, 2 * D_FF), ex3),
                pl.BlockSpec((1, 1, 2 * D_FF), ex3),
                pl.BlockSpec((1, D_FF, D_MODEL), ex3),
                pl.BlockSpec((1, 1, D_MODEL), ex3),
                pl.BlockSpec(memory_space=pl.ANY),
            ],
            out_specs=pl.BlockSpec(memory_space=pl.ANY),
            scratch_shapes=[pltpu.VMEM((FFN_BLOCK * TILE_ROWS, LANES), F32)] * 4 + [
                pltpu.VMEM((D_MODEL, 2 * D_FF), BF16), pltpu.VMEM((D_FF, D_MODEL), BF16),
                pltpu.SemaphoreType.DMA((2,)), pltpu.SemaphoreType.DMA((2,))],
        ),
        out_shape=jax.ShapeDtypeStruct((n_out_rows * TILE_ROWS, LANES), F32),
        compiler_params=_params(("arbitrary",)),
        name="ffn",
    )(block_expert, n_used, src, src, dst, dst, gate_sorted, g_ffn, w_up, b_up, w_down, b_down, x1)


def _final_kernel(x1_ref, s0_ref, s1_ref, s2_ref, s3_ref, g_ref, y_ref, sum_scr):
    sum_scr[...] = x1_ref[...] + ((s0_ref[...] + s1_ref[...]) + (s2_ref[...] + s3_ref[...]))
    x2 = _load_token_tiles(sum_scr, ROW_TILE)
    ms = jnp.mean(x2 * x2, axis=-1, keepdims=True)
    y_ref[...] = x2 * lax.rsqrt(ms + EPS) * g_ref[...]


def _final(x1, slots, g_final, row0, n_rows, n_all):
    t0 = row0 // ROW_TILE
    per_slot = n_all // ROW_TILE
    tile_block = lambda b0: pl.BlockSpec((ROW_TILE * TILE_ROWS, LANES), lambda i: (b0 + i, 0))
    slot = lambda k: tile_block(k * per_slot + t0)
    return pl.pallas_call(
        _final_kernel,
        grid=(n_rows // ROW_TILE,),
        in_specs=[tile_block(t0),
                  slot(0), slot(1), slot(2), slot(3), _const_spec((1, D_MODEL))],
        out_specs=pl.BlockSpec((ROW_TILE, D_MODEL), lambda i: (i, 0)),
        out_shape=jax.ShapeDtypeStruct((n_rows, D_MODEL), F32),
        scratch_shapes=[pltpu.VMEM((ROW_TILE * TILE_ROWS, LANES), F32)],
        compiler_params=_params(("arbitrary",)),
        name="final",
    )(x1, slots, slots, slots, slots, g_final)


def _rope_tables(seq_len, dec_seq):
    inv = ROPE_THETA ** (-np.arange(0, ROT_DIM, 2, dtype=np.float64) / ROT_DIM)
    pos = np.concatenate([np.arange(seq_len), PAST_LEN + np.arange(ROW_TILE) % dec_seq]).astype(np.float64)
    ang = pos[:, None] * inv[None, :]
    cos, sin = np.cos(ang), np.sin(ang)
    ones = np.ones((pos.shape[0], HEAD_DIM - ROT_DIM))
    cos_h = np.concatenate([cos, cos, ones], axis=1)
    sin_h = np.concatenate([-sin, sin, 0.0 * ones], axis=1)
    return (jnp.asarray(np.concatenate([cos_h, cos_h], axis=1), F32),
            jnp.asarray(np.concatenate([sin_h, sin_h], axis=1), F32))


def _block_tables(counts, n_all):
    n_slots = TOP_K * n_all
    n_blocks = -(-n_slots // FFN_BLOCK) + N_EXPERTS
    padded = (counts + FFN_BLOCK - 1) // FFN_BLOCK * FFN_BLOCK
    pad_end = jnp.cumsum(padded)
    pad_start = (pad_end - padded).astype(jnp.int32).reshape(N_EXPERTS, 1)
    block_start = jnp.arange(n_blocks, dtype=jnp.int32) * FFN_BLOCK
    block_expert = jnp.minimum(jnp.sum(pad_end[None, :] <= block_start[:, None], axis=1), N_EXPERTS - 1)
    n_used = (pad_end[-1] // FFN_BLOCK).astype(jnp.int32).reshape(1)
    return pad_start, block_expert.astype(jnp.int32), n_used, n_blocks


def kernel(x_prompt, x_sample, cache_swa_k, cache_swa_v, state_gla, g_mix, w_in, w_gk_up, b_gk, sinks,
           gla_norm, w_branch_a, w_branch_b, w_out, g_ffn, router_w, router_b, w_up, b_up, w_down, b_down,
           g_final):
    batch, seq_len, _ = x_prompt.shape
    dec_batch, dec_seq, _ = x_sample.shape
    n_p, n_s = batch * seq_len, dec_batch * dec_seq
    n_all = n_p + n_s
    assert w_in.shape[0] == 1, "one layer: the final norm is fused after the only MoE"
    assert seq_len % ROW_TILE == 0 and n_s % ROW_TILE == 0 and ROW_TILE % dec_seq == 0
    assert dec_seq % SUBLANES == 0 and dec_batch % SAMPLE_GROUP == 0
    assert SAMPLE_GROUP * dec_seq == CHUNK and (dec_seq & (dec_seq - 1)) == 0

    xp2 = x_prompt.reshape(n_p, D_MODEL)
    xs2 = x_sample.reshape(n_s, D_MODEL)
    cos_tab, sin_tab = _rope_tables(seq_len, dec_seq)
    w = w_in[0]
    gl0 = _C_RG
    w_all = jnp.concatenate(
        [w[:, :gl0], w[:, gl0 + GK_RANK:], w[:, gl0:gl0 + GK_RANK],
         jnp.zeros((D_MODEL, LANES - GK_RANK), w.dtype)], axis=1).astype(BF16)
    wup_pad = jnp.concatenate([w_gk_up[0], jnp.zeros((LANES - GK_RANK, GLA_KDIM), F32)], axis=0).astype(BF16)
    qa, ka, va, qg, kg, vg, la, rg, ga, gb = _proj(
        xp2, xs2, g_mix[0].reshape(1, D_MODEL), cos_tab, sin_tab, w_all, wup_pad,
        b_gk[0].reshape(1, GLA_KDIM), seq_len)

    oa_p = _swa_prompt(sinks[0], qa, ka, va, batch, seq_len)
    oa_s, nk_s, nv_s = _swa_sample(sinks[0], qa, ka, va, cache_swa_k[0], cache_swa_v[0], n_p,
                                   dec_batch, dec_seq)
    norm_row = jnp.tile(gla_norm[0], GLA_HEADS).reshape(1, GLA_VDIM)
    og_p, s_fin = _gla_prompt(qg, kg, la, vg, rg, norm_row, batch, seq_len)
    og_s, s_new = _gla_sample(qg, kg, la, vg, rg, norm_row, state_gla[0], n_p, dec_batch, dec_seq)

    x1 = _merge(xp2, xs2, oa_p, oa_s, og_p, og_s, ga, gb, w_branch_a[0].astype(BF16),
                w_branch_b[0].astype(BF16), w_out[0].astype(BF16))

    g_ffn_row = g_ffn[0].reshape(1, D_MODEL)
    rw_t = router_w[0].T
    rw_hi = rw_t.astype(BF16)
    rw_lo = (rw_t - rw_hi.astype(F32)).astype(BF16)
    eidx, rank, gate, cnt = _route(x1, g_ffn_row, rw_hi, rw_lo, router_b[0].reshape(N_EXPERTS, 1))
    counts = cnt[:, 0].astype(jnp.int32)
    pad_start, block_expert, n_used, n_blocks = _block_tables(counts, n_all)
    src, dst, gate_sorted = _invert(eidx, rank, gate, pad_start, n_blocks)
    prime = (TOP_K * n_all + 2 * FFN_BLOCK + jnp.arange(FFN_BLOCK, dtype=jnp.int32)).reshape(1, FFN_BLOCK)
    dst = jnp.concatenate([dst, prime], axis=0)
    slots = _ffn(block_expert, n_used, src.reshape(n_blocks, 1, FFN_BLOCK),
                 dst.reshape(n_blocks + 1, 1, FFN_BLOCK), gate_sorted.reshape(n_blocks * FFN_BLOCK, 1),
                 x1, g_ffn_row, w_up[0], b_up[0].reshape(N_EXPERTS, 1, 2 * D_FF), w_down[0],
                 b_down[0].reshape(N_EXPERTS, 1, D_MODEL), TOP_K * n_all + 3 * FFN_BLOCK)

    g_out = g_final.reshape(1, D_MODEL)
    y_p = _final(x1, slots, g_out, 0, n_p, n_all)
    y_s = _final(x1, slots, g_out, n_p, n_s, n_all)

    kv_shape = (1, -1, WINDOW, KV_HEADS, HEAD_DIM)
    new_k_p = ka[:n_p].reshape(batch, seq_len, KV_WIDTH)[:, -WINDOW:].reshape(kv_shape)
    new_v_p = va[:n_p].reshape(batch, seq_len, KV_WIDTH)[:, -WINDOW:].reshape(kv_shape)
    return (y_p.reshape(batch, seq_len, D_MODEL), y_s.reshape(dec_batch, dec_seq, D_MODEL),
            new_k_p, new_v_p, s_fin[None], nk_s.reshape(kv_shape), nv_s.reshape(kv_shape), s_new[None])
```

```python
import functools

import numpy as np
import jax
import jax.numpy as jnp
from jax import lax
from jax.experimental import pallas as pl
from jax.experimental.pallas import tpu as pltpu

D_MODEL = 1024
PAST_LEN = 8192
HEAD_DIM = 64
N_HEADS = 8
KV_HEADS = 2
GROUP = N_HEADS // KV_HEADS
WINDOW = 128
ROT_DIM = HEAD_DIM // 4
ROPE_THETA = 500000.0
ATT_WIDTH = N_HEADS * HEAD_DIM
KV_WIDTH = KV_HEADS * HEAD_DIM
GLA_HEADS = 4
GLA_KDIM = D_MODEL // 2
GLA_VDIM = D_MODEL
GLA_DK = GLA_KDIM // GLA_HEADS
GLA_DV = GLA_VDIM // GLA_HEADS
GK_RANK = 16
GK_NORMALIZER = 16.0
N_EXPERTS = 32
TOP_K = 4
D_FF = D_MODEL
SWIGLU_LIMIT = 7.0
SWIGLU_ALPHA = 1.702
EPS = 1e-5
NEG_INF = -1e30

LANES = 128
SUBLANES = 8
VMEM_LIMIT_BYTES = 56 * 1024 * 1024

ROW_TILE = 512
CHUNK = 128
FFN_BLOCK = 256

BF16 = jnp.bfloat16
F32 = jnp.float32

_C_QA, _C_KA, _C_VA, _C_QG, _C_KG, _C_VG, _C_RG, _C_GA, _C_GB, _C_GL, _C_END = (
    0, 512, 640, 768, 1280, 1792, 2816, 3840, 4864, 5888, 6016)


def _const_spec(shape):
    nd = len(shape)
    return pl.BlockSpec(shape, lambda *_: (0,) * nd, pipeline_mode=pl.Buffered(1))


def _resident_out_spec(shape):
    nd = len(shape)
    return pl.BlockSpec(shape, lambda *_: (0,) * nd)


def _params(sem, vmem=VMEM_LIMIT_BYTES):
    return pltpu.CompilerParams(dimension_semantics=sem, vmem_limit_bytes=vmem)


def _nt_dot(a, b):
    return lax.dot_general(a, b, (((1,), (1,)), ((), ())), preferred_element_type=F32)


def _dot(a, b):
    return jnp.dot(a, b, preferred_element_type=F32)


TILE_ROWS = D_MODEL // LANES
assert TILE_ROWS == SUBLANES


def _load_token_tiles(ref, n_tokens):
    return jnp.concatenate([ref[pl.ds(c, n_tokens, stride=TILE_ROWS), :] for c in range(TILE_ROWS)], axis=1)


def _store_token_tiles(ref, x):
    for c in range(TILE_ROWS):
        ref[pl.ds(c, x.shape[0], stride=TILE_ROWS), :] = x[:, c * LANES:(c + 1) * LANES]


def _rope(x, cos_t, sin_t, n_rep):
    width = x.shape[1]
    cos_f = jnp.concatenate([cos_t] * n_rep, axis=1) if n_rep > 1 else cos_t
    sin_f = jnp.concatenate([sin_t] * n_rep, axis=1) if n_rep > 1 else sin_t
    lane = lax.broadcasted_iota(jnp.int32, x.shape, 1) % HEAD_DIM
    up = pltpu.roll(x, width - ROT_DIM // 2, 1)
    down = pltpu.roll(x, ROT_DIM // 2, 1)
    partner = jnp.where(lane < ROT_DIM // 2, up, down)
    return x * cos_f + partner * sin_f


def _proj_kernel(n_prompt_tiles, xp_ref, xs_ref, g_ref, cos_ref, sin_ref, w_ref, wup_ref, bgk_ref,
                 qa_ref, ka_ref, va_ref, qg_ref, kg_ref, vg_ref, la_ref, rg_ref, ga_ref, gb_ref):
    i = pl.program_id(0)
    x = jnp.where(i < n_prompt_tiles, xp_ref[...], xs_ref[...])
    ms = jnp.mean(x * x, axis=-1, keepdims=True)
    h = (x * lax.rsqrt(ms + EPS) * g_ref[...]).astype(BF16)
    cos_t = cos_ref[...]
    sin_t = sin_ref[...]

    def seg(a, b):
        return _dot(h, w_ref[:, a:b])

    qa = _rope(seg(_C_QA, _C_KA), cos_t, sin_t, ATT_WIDTH // LANES)
    qa_ref[...] = (qa * (HEAD_DIM ** -0.5)).astype(BF16)
    ka_ref[...] = _rope(seg(_C_KA, _C_VA), cos_t, sin_t, 1)
    va_ref[...] = seg(_C_VA, _C_QG)
    qg_ref[...] = seg(_C_QG, _C_KG) * (GLA_DK ** -0.5)
    kg_ref[...] = seg(_C_KG, _C_VG)
    vg_ref[...] = seg(_C_VG, _C_RG).astype(BF16)
    rg_ref[...] = seg(_C_RG, _C_GA).astype(BF16)
    ga_ref[...] = seg(_C_GA, _C_GB).astype(BF16)
    gb_ref[...] = seg(_C_GB, _C_GL).astype(BF16)
    gk_low = seg(_C_GL, _C_END).astype(BF16)
    z = _dot(gk_low, wup_ref[...]) + bgk_ref[...]
    log_sig = jnp.minimum(z, 0.0) - jnp.log1p(jnp.exp(-jnp.abs(z)))
    la_ref[...] = log_sig / GK_NORMALIZER


def _proj(xp2, xs2, g_mix, cos_tab, sin_tab, w_all, wup_pad, b_gk, seq_len):
    n_p, n_s = xp2.shape[0], xs2.shape[0]
    n_all = n_p + n_s
    npt, nst = n_p // ROW_TILE, n_s // ROW_TILE
    tiles_per_seq = seq_len // ROW_TILE

    def tab_map(i):
        return (jnp.where(i < npt, i % tiles_per_seq, tiles_per_seq), 0)

    row = lambda w: pl.BlockSpec((ROW_TILE, w), lambda i: (i, 0))
    widths = [(ATT_WIDTH, BF16), (KV_WIDTH, F32), (KV_WIDTH, F32), (GLA_KDIM, F32), (GLA_KDIM, F32),
              (GLA_VDIM, BF16), (GLA_KDIM, F32), (GLA_VDIM, BF16), (D_MODEL, BF16), (D_MODEL, BF16)]
    return pl.pallas_call(
        functools.partial(_proj_kernel, npt),
        grid=(npt + nst,),
        in_specs=[
            pl.BlockSpec((ROW_TILE, D_MODEL), lambda i: (jnp.minimum(i, npt - 1), 0)),
            pl.BlockSpec((ROW_TILE, D_MODEL), lambda i: (jnp.maximum(i - npt, 0), 0)),
            _const_spec((1, D_MODEL)),
            pl.BlockSpec((ROW_TILE, LANES), tab_map),
            pl.BlockSpec((ROW_TILE, LANES), tab_map),
            _const_spec(w_all.shape),
            _const_spec(wup_pad.shape),
            _const_spec((1, GLA_KDIM)),
        ],
        out_specs=[row(w) for w, _ in widths],
        out_shape=[jax.ShapeDtypeStruct((n_all, w), dt) for w, dt in widths],
        compiler_params=_params(("arbitrary",)),
        name="proj",
    )(xp2, xs2, g_mix, cos_tab, sin_tab, w_all, wup_pad, b_gk)


def _pair_blocks(kk):
    lane = lax.broadcasted_iota(jnp.int32, kk.shape, 1)
    lo = lane < HEAD_DIM
    swapped = pltpu.roll(kk, HEAD_DIM, 1)
    zero = jnp.zeros_like(kk)
    blocks = []
    for kh in range(KV_HEADS):
        left = jnp.where(lo, kk if kh == 0 else swapped, zero)
        right = jnp.where(lo, zero, swapped if kh == 0 else kk)
        blocks.append(jnp.concatenate([left, right], axis=0).astype(BF16))
    return blocks


def _sink_softmax(s, valid, sink):
    s = jnp.where(valid, s, NEG_INF)
    m = jnp.maximum(jnp.max(s, axis=-1, keepdims=True), sink)
    p = jnp.exp(s - m)
    denom = jnp.sum(p, axis=-1, keepdims=True) + jnp.exp(sink - m)
    return (p * (1.0 / denom)).astype(BF16)


def _attend(q, kk, vv, valid, sink_ref, o_ref, row0=0):
    rows, keys = valid.shape
    kblocks = _pair_blocks(kk)
    vblocks = _pair_blocks(vv)
    for kh in range(KV_HEADS):
        base = kh * GROUP * HEAD_DIM
        qq = jnp.concatenate([q[:, base:base + LANES], q[:, base + LANES:base + 2 * LANES]], axis=0)
        s = _nt_dot(qq, kblocks[kh])
        for r in range(2):
            probs = []
            for c in range(2):
                head = kh * GROUP + 2 * r + c
                probs.append(_sink_softmax(s[r * rows:(r + 1) * rows, c * keys:(c + 1) * keys],
                                           valid, sink_ref[head]))
            p = jnp.concatenate(probs, axis=1)
            o_ref[pl.ds(row0, rows), base + r * LANES:base + (r + 1) * LANES] = (
                _dot(p, vblocks[kh]).astype(o_ref.dtype))


def _swa_prompt_kernel(sink_ref, q_ref, kp_ref, k0_ref, k1_ref, vp_ref, v0_ref, v1_ref, o_ref):
    j = pl.program_id(1)
    row = lax.broadcasted_iota(jnp.int32, (WINDOW, 2 * WINDOW), 0)
    col = lax.broadcasted_iota(jnp.int32, (WINDOW, 2 * WINDOW), 1)
    band = (col > row) & (col <= row + WINDOW)
    k_blocks = (kp_ref[...], k0_ref[...], k1_ref[...])
    v_blocks = (vp_ref[...], v0_ref[...], v1_ref[...])
    q = q_ref[...]
    for half in range(2):
        kk = jnp.concatenate(k_blocks[half:half + 2], axis=0)
        vv = jnp.concatenate(v_blocks[half:half + 2], axis=0)
        valid = band & ((j > 0) | (col >= WINDOW)) if half == 0 else band
        _attend(q[half * WINDOW:(half + 1) * WINDOW], kk, vv, valid, sink_ref, o_ref, half * WINDOW)


def _swa_prompt(sinks, qa, ka, va, batch, seq_len):
    nb = seq_len // WINDOW
    assert nb % 2 == 0
    kv = lambda off: pl.BlockSpec((WINDOW, KV_WIDTH), lambda b, j, s: (b * nb + jnp.maximum(2 * j + off, 0), 0))
    pair = pl.BlockSpec((2 * WINDOW, ATT_WIDTH), lambda b, j, s: (b * (nb // 2) + j, 0))
    return pl.pallas_call(
        _swa_prompt_kernel,
        grid_spec=pltpu.PrefetchScalarGridSpec(
            num_scalar_prefetch=1,
            grid=(batch, nb // 2),
            in_specs=[pair, kv(-1), kv(0), kv(1), kv(-1), kv(0), kv(1)],
            out_specs=pair,
        ),
        out_shape=jax.ShapeDtypeStruct((batch * seq_len, ATT_WIDTH), BF16),
        compiler_params=_params(("arbitrary", "arbitrary")),
        name="swa_prompt",
    )(sinks, qa, ka, ka, ka, va, va, va)


SAMPLE_GROUP = 16


def _swa_sample_kernel(dec_seq, sink_ref, q_ref, kn_ref, vn_ref, ck_ref, cv_ref, o_ref, nk_ref, nv_ref):
    rows = SAMPLE_GROUP * dec_seq
    ck = ck_ref[...]
    cv = cv_ref[...]
    kn = kn_ref[...]
    vn = vn_ref[...]
    nk_ref[:, :WINDOW - dec_seq, :] = ck[:, dec_seq:, :]
    nv_ref[:, :WINDOW - dec_seq, :] = cv[:, dec_seq:, :]
    nk_ref[:, WINDOW - dec_seq:, :] = kn.reshape(SAMPLE_GROUP, dec_seq, KV_WIDTH)
    nv_ref[:, WINDOW - dec_seq:, :] = vn.reshape(SAMPLE_GROUP, dec_seq, KV_WIDTH)
    n_cache = SAMPLE_GROUP * WINDOW
    kk = jnp.concatenate([ck.reshape(n_cache, KV_WIDTH), kn], axis=0)
    vv = jnp.concatenate([cv.reshape(n_cache, KV_WIDTH), vn], axis=0)
    keys = n_cache + rows
    row = lax.broadcasted_iota(jnp.int32, (rows, keys), 0)
    col = lax.broadcasted_iota(jnp.int32, (rows, keys), 1)
    q_b, q_s = row // dec_seq, row % dec_seq
    is_cache = col < n_cache
    new = col - n_cache
    valid_cache = (col // WINDOW == q_b) & (col % WINDOW > q_s)
    valid_new = (new // dec_seq == q_b) & (new % dec_seq <= q_s)
    valid = (is_cache & valid_cache) | (jnp.logical_not(is_cache) & valid_new)
    _attend(q_ref[...], kk, vv, valid, sink_ref, o_ref)


def _swa_sample(sinks, qa, ka, va, cache_k, cache_v, n_prompt_rows, dec_batch, dec_seq):
    rows = SAMPLE_GROUP * dec_seq
    off = n_prompt_rows // rows
    tok = lambda g, s: (off + g, 0)
    cache = lambda g, s: (g, 0, 0)
    cshape = (dec_batch, WINDOW, KV_WIDTH)
    return pl.pallas_call(
        functools.partial(_swa_sample_kernel, dec_seq),
        grid_spec=pltpu.PrefetchScalarGridSpec(
            num_scalar_prefetch=1,
            grid=(dec_batch // SAMPLE_GROUP,),
            in_specs=[
                pl.BlockSpec((rows, ATT_WIDTH), tok),
                pl.BlockSpec((rows, KV_WIDTH), tok),
                pl.BlockSpec((rows, KV_WIDTH), tok),
                pl.BlockSpec((SAMPLE_GROUP, WINDOW, KV_WIDTH), cache),
                pl.BlockSpec((SAMPLE_GROUP, WINDOW, KV_WIDTH), cache),
            ],
            out_specs=[
                pl.BlockSpec((rows, ATT_WIDTH), lambda g, s: (g, 0)),
                pl.BlockSpec((SAMPLE_GROUP, WINDOW, KV_WIDTH), cache),
                pl.BlockSpec((SAMPLE_GROUP, WINDOW, KV_WIDTH), cache),
            ],
        ),
        out_shape=[jax.ShapeDtypeStruct((dec_batch * dec_seq, ATT_WIDTH), BF16),
                   jax.ShapeDtypeStruct(cshape, F32), jax.ShapeDtypeStruct(cshape, F32)],
        compiler_params=_params(("arbitrary",)),
        name="swa_sample",
    )(sinks, qa, ka, va, cache_k.reshape(cshape), cache_v.reshape(cshape))


def _chunk_tables(seg):
    n_lev = int(np.log2(seg))
    t = np.arange(CHUNK)
    seg_start = (t // seg) * seg
    u = np.arange(CHUNK)[None, :]

    def prefix(end):
        return ((u >= seg_start[:, None]) & (u <= end[:, None])).astype(np.float32)

    blocks = [prefix(t)]
    for d in range(min(n_lev, _MATMUL_LEVELS)):
        m = 1 << d
        ref = (t >> (d + 1) << (d + 1)) + m - 1
        blocks.append(prefix(ref))
    lhs = np.concatenate(blocks, axis=0)
    lhs2 = np.concatenate([lhs, lhs], axis=1)
    tt, ss = t[:, None], t[None, :]
    x = tt ^ ss
    lev = np.where(x > 0, np.floor(np.log2(np.maximum(x, 1))).astype(np.int32), n_lev)
    lev = np.where((ss > tt) | (tt // seg != ss // seg), -1, lev)
    lev = np.where(tt == ss, n_lev, lev)
    return jnp.asarray(lhs2, BF16), jnp.asarray(lev, jnp.int32), n_lev


_MATMUL_LEVELS = 3


def _group_row(x, group, row):
    width = x.shape[1]
    parts = [jnp.broadcast_to(x[g * group + row:g * group + row + 1, :], (group, width))
             for g in range(x.shape[0] // group)]
    return parts[0] if len(parts) == 1 else jnp.concatenate(parts, axis=0)


_HALF_ROW_LEVELS = 4
LOG2_E = 1.4426950408889634


def _halves(x, m, which):
    parts = [x[(2 * g + which) * m:(2 * g + which + 1) * m] for g in range(x.shape[0] // (2 * m))]
    return parts[0] if len(parts) == 1 else jnp.concatenate(parts, axis=0)


def _unhalve(xh, m, which):
    zero = jnp.zeros((m, xh.shape[1]), xh.dtype)
    parts = []
    for g in range(xh.shape[0] // m):
        blk = xh[g * m:(g + 1) * m]
        parts += [zero, blk] if which else [blk, zero]
    return jnp.concatenate(parts, axis=0)


def _gla_chunk_terms(q, k, la, lhs2, level, n_lev, seg):
    la2 = la * LOG2_E
    hi = la2.astype(BF16)
    lo = (la2 - hi.astype(F32)).astype(BF16)
    sums = _dot(lhs2, jnp.concatenate([hi, lo], axis=0))
    b = sums[0:CHUNK]
    b_last = _group_row(b, seg, seg - 1)
    q_main = (q * jnp.exp2(b)).astype(BF16)
    k_upd = k * jnp.exp2(b_last - b)
    q_lev = [None] * n_lev
    k_lev = [None] * n_lev
    for d in range(n_lev):
        m = 1 << d
        if d < _MATMUL_LEVELS:
            ref = sums[(1 + d) * CHUNK:(2 + d) * CHUNK]
        elif d < _HALF_ROW_LEVELS:
            ref = _group_row(b, 2 * m, m - 1)
        if d < _HALF_ROW_LEVELS:
            q_lev[d] = (q * jnp.exp2(b - ref)).astype(BF16)
            k_lev[d] = (k * jnp.exp2(ref - b)).astype(BF16)
        else:
            ref_h = _group_row(_halves(b, m, 0), m, m - 1)
            q_lev[d] = _unhalve((_halves(q, m, 1) * jnp.exp2(_halves(b, m, 1) - ref_h)).astype(BF16), m, 1)
            k_lev[d] = _unhalve((_halves(k, m, 0) * jnp.exp2(ref_h - _halves(b, m, 0))).astype(BF16), m, 0)
    q_b, k_b = q.astype(BF16), k.astype(BF16)

    def att(h):
        hs = slice(h * GLA_DK, (h + 1) * GLA_DK)
        acc = jnp.where(level == n_lev, _nt_dot(q_b[:, hs], k_b[:, hs]), 0.0)
        for d in range(n_lev):
            acc = jnp.where(level == d, _nt_dot(q_lev[d][:, hs], k_lev[d][:, hs]), acc)
        return acc

    return q_main, k_upd, att, b_last


def _gla_out(o, r, norm):
    parts = []
    for h in range(GLA_HEADS):
        oh = o[:, h * GLA_DV:(h + 1) * GLA_DV]
        ms = jnp.mean(oh * oh, axis=-1, keepdims=True)
        parts.append(oh * lax.rsqrt(ms + EPS))
    y = jnp.concatenate(parts, axis=1) * norm
    rf = r.astype(F32)
    return y * (rf * jax.nn.sigmoid(rf))


def _gla_prompt_kernel(n_lev, n_par, *refs):
    seq_refs = [refs[5 * j:5 * j + 5] for j in range(n_par)]
    norm_ref, lhs_ref, lev_ref = refs[5 * n_par:5 * n_par + 3]
    o_ref, sfin_ref, s_scr = refs[5 * n_par + 3:]
    c = pl.program_id(1)

    @pl.when(c == 0)
    def _():
        s_scr[...] = jnp.zeros_like(s_scr)

    for j, (q_ref, k_ref, la_ref, v_ref, r_ref) in enumerate(seq_refs):
        q_main, k_upd, att, b_last = _gla_chunk_terms(q_ref[...], k_ref[...], la_ref[...], lhs_ref[...],
                                                      lev_ref[...], n_lev, CHUNK)
        v = v_ref[...]
        outs = []
        for h in range(GLA_HEADS):
            hs = slice(h * GLA_DK, (h + 1) * GLA_DK)
            vh = v[:, h * GLA_DV:(h + 1) * GLA_DV]
            s0 = s_scr[j, h]
            o_h = _dot(q_main[:, hs], s0.astype(BF16)) + _dot(att(h).astype(BF16), vh)
            outs.append(o_h)
            decay = jnp.exp2(b_last[:, hs]).T
            k_t = k_upd[:, hs].T.astype(BF16)
            s_scr[j, h] = jnp.concatenate([decay, decay], axis=1) * s0 + _dot(k_t, vh)
        o_ref[j] = _gla_out(jnp.concatenate(outs, axis=1), r_ref[...], norm_ref[...]).astype(BF16)

    @pl.when(c == pl.num_programs(1) - 1)
    def _():
        sfin_ref[...] = s_scr[...]


def _gla_prompt(qg, kg, la, vg, rg, norm_row, batch, seq_len):
    nc = seq_len // CHUNK
    n_par = 2 if batch % 2 == 0 else 1
    lhs2, level, n_lev = _chunk_tables(CHUNK)
    tok = lambda j, w: pl.BlockSpec((CHUNK, w), lambda b, c: ((b * n_par + j) * nc + c, 0))
    seq_specs, seq_args = [], []
    for j in range(n_par):
        seq_specs += [tok(j, GLA_KDIM), tok(j, GLA_KDIM), tok(j, GLA_KDIM), tok(j, GLA_VDIM), tok(j, GLA_VDIM)]
        seq_args += [qg, kg, la, vg, rg]
    og, s_fin = pl.pallas_call(
        functools.partial(_gla_prompt_kernel, n_lev, n_par),
        grid=(batch // n_par, nc),
        in_specs=seq_specs + [_const_spec((1, GLA_VDIM)), _const_spec(lhs2.shape), _const_spec(level.shape)],
        out_specs=[pl.BlockSpec((n_par, CHUNK, GLA_VDIM), lambda b, c: (b, c, 0)),
                   pl.BlockSpec((n_par, GLA_HEADS, GLA_DK, GLA_DV), lambda b, c: (b, 0, 0, 0))],
        out_shape=[jax.ShapeDtypeStruct((batch, seq_len, GLA_VDIM), BF16),
                   jax.ShapeDtypeStruct((batch, GLA_HEADS, GLA_DK, GLA_DV), F32)],
        scratch_shapes=[pltpu.VMEM((n_par, GLA_HEADS, GLA_DK, GLA_DV), F32)],
        compiler_params=_params(("arbitrary", "arbitrary")),
        name="gla_prompt",
    )(*seq_args, norm_row, lhs2, level)
    return og.reshape(batch * seq_len, GLA_VDIM), s_fin


def _gla_sample_kernel(n_lev, dec_seq, q_ref, k_ref, la_ref, v_ref, r_ref, norm_ref, lhs_ref, lev_ref,
                       s0_ref, o_ref, snew_ref):
    q_main, k_upd, att, b_last = _gla_chunk_terms(q_ref[...], k_ref[...], la_ref[...], lhs_ref[...],
                                                  lev_ref[...], n_lev, dec_seq)
    v = v_ref[...]
    n_b = CHUNK // dec_seq
    row_b = lax.broadcasted_iota(jnp.int32, (CHUNK, GLA_DK), 0) // dec_seq
    col_b = lax.broadcasted_iota(jnp.int32, (GLA_DK, CHUNK), 1) // dec_seq
    outs = []
    for h in range(GLA_HEADS):
        hs = slice(h * GLA_DK, (h + 1) * GLA_DK)
        vh = v[:, h * GLA_DV:(h + 1) * GLA_DV]
        qm = q_main[:, hs]
        decay_t = jnp.exp2(b_last[:, hs]).T
        k_t = k_upd[:, hs].T.astype(BF16)
        o_h = _dot(att(h).astype(BF16), vh)
        for bi in range(n_b):
            s0 = s0_ref[bi, h]
            o_h = o_h + _dot(jnp.where(row_b == bi, qm, jnp.zeros_like(qm)), s0.astype(BF16))
            decay = jnp.broadcast_to(decay_t[:, bi * dec_seq:bi * dec_seq + 1], (GLA_DK, GLA_DV))
            k_b = jnp.where(col_b == bi, k_t, jnp.zeros_like(k_t))
            snew_ref[bi, h] = decay * s0 + _dot(k_b, vh)
        outs.append(o_h)
    o_ref[...] = _gla_out(jnp.concatenate(outs, axis=1), r_ref[...], norm_ref[...]).astype(o_ref.dtype)


def _gla_sample(qg, kg, la, vg, rg, norm_row, state, n_prompt_rows, dec_batch, dec_seq):
    n_b = CHUNK // dec_seq
    off = n_prompt_rows // CHUNK
    lhs3, level, n_lev = _chunk_tables(dec_seq)
    tok = lambda w: pl.BlockSpec((CHUNK, w), lambda g: (off + g, 0))
    st = pl.BlockSpec((n_b, GLA_HEADS, GLA_DK, GLA_DV), lambda g: (g, 0, 0, 0))
    return pl.pallas_call(
        functools.partial(_gla_sample_kernel, n_lev, dec_seq),
        grid=(dec_batch // n_b,),
        in_specs=[tok(GLA_KDIM), tok(GLA_KDIM), tok(GLA_KDIM), tok(GLA_VDIM), tok(GLA_VDIM),
                  _const_spec((1, GLA_VDIM)), _const_spec(lhs3.shape), _const_spec(level.shape), st],
        out_specs=[pl.BlockSpec((CHUNK, GLA_VDIM), lambda g: (g, 0)), st],
        out_shape=[jax.ShapeDtypeStruct((dec_batch * dec_seq, GLA_VDIM), BF16),
                   jax.ShapeDtypeStruct(state.shape, F32)],
        compiler_params=_params(("arbitrary",)),
        name="gla_sample",
    )(qg, kg, la, vg, rg, norm_row, lhs3, level, state)


def _merge_kernel(n_prompt_tiles, xp_ref, xs_ref, oap_ref, oas_ref, ogp_ref, ogs_ref, ga_ref, gb_ref,
                  wa_ref, wb_ref, wo_ref, x1_ref):
    i = pl.program_id(0)
    is_p = i < n_prompt_tiles
    x = jnp.where(is_p, xp_ref[...], xs_ref[...])
    oa = jnp.where(is_p, oap_ref[...], oas_ref[...])
    og = jnp.where(is_p, ogp_ref[...], ogs_ref[...])
    m = (jax.nn.sigmoid(ga_ref[...].astype(F32)) * _dot(oa, wa_ref[...])
         + jax.nn.sigmoid(gb_ref[...].astype(F32)) * _dot(og, wb_ref[...]))
    _store_token_tiles(x1_ref, x + _dot(m.astype(BF16), wo_ref[...]))


def _merge(xp2, xs2, oa_p, oa_s, og_p, og_s, ga, gb, wa, wb, wo):
    n_p, n_s = xp2.shape[0], xs2.shape[0]
    npt, nst = n_p // ROW_TILE, n_s // ROW_TILE
    p_map = lambda i: (jnp.minimum(i, npt - 1), 0)
    s_map = lambda i: (jnp.maximum(i - npt, 0), 0)
    row = lambda w: pl.BlockSpec((ROW_TILE, w), lambda i: (i, 0))
    return pl.pallas_call(
        functools.partial(_merge_kernel, npt),
        grid=(npt + nst,),
        in_specs=[
            pl.BlockSpec((ROW_TILE, D_MODEL), p_map), pl.BlockSpec((ROW_TILE, D_MODEL), s_map),
            pl.BlockSpec((ROW_TILE, ATT_WIDTH), p_map), pl.BlockSpec((ROW_TILE, ATT_WIDTH), s_map),
            pl.BlockSpec((ROW_TILE, GLA_VDIM), p_map), pl.BlockSpec((ROW_TILE, GLA_VDIM), s_map),
            row(D_MODEL), row(D_MODEL),
            _const_spec(wa.shape), _const_spec(wb.shape), _const_spec(wo.shape),
        ],
        out_specs=pl.BlockSpec((ROW_TILE * TILE_ROWS, LANES), lambda i: (i, 0)),
        out_shape=jax.ShapeDtypeStruct(((n_p + n_s) * TILE_ROWS, LANES), F32),
        compiler_params=_params(("arbitrary",)),
        name="merge",
    )(xp2, xs2, oa_p, oa_s, og_p, og_s, ga, gb, wa, wb, wo)


def _route_kernel(x1_ref, gffn_ref, rwh_ref, rwl_ref, rb_ref, upper_ref, eidx_ref, rank_ref, gate_ref,
                  cnt_ref, cnt_scr):
    i = pl.program_id(0)

    @pl.when(i == 0)
    def _():
        cnt_scr[...] = jnp.zeros_like(cnt_scr)

    x1 = _load_token_tiles(x1_ref, ROW_TILE)
    ms = jnp.mean(x1 * x1, axis=-1, keepdims=True)
    h2 = x1 * lax.rsqrt(ms + EPS) * gffn_ref[...]
    h_hi = h2.astype(BF16)
    h_lo = (h2 - h_hi.astype(F32)).astype(BF16)
    rwh, rwl = rwh_ref[...], rwl_ref[...]
    logits = _nt_dot(rwh, h_hi) + _nt_dot(rwl, h_hi) + _nt_dot(rwh, h_lo) + rb_ref[...]
    eid = lax.broadcasted_iota(jnp.int32, logits.shape, 0)
    upper = upper_ref[...]
    base = cnt_scr[...]
    vals, rows_e, rows_r = [], [], []
    lg = logits
    for _ in range(TOP_K):
        mx = jnp.max(lg, axis=0, keepdims=True)
        sel = jnp.min(jnp.where(lg == mx, eid, N_EXPERTS), axis=0, keepdims=True)
        onehot = eid == sel
        oh = onehot.astype(F32)
        before = _dot(onehot.astype(BF16), upper)
        rows_r.append(jnp.sum(oh * (base + before), axis=0, keepdims=True))
        base = base + jnp.sum(oh, axis=1, keepdims=True)
        vals.append(mx)
        rows_e.append(sel)
        lg = jnp.where(onehot, -jnp.inf, lg)
    cnt_scr[...] = base
    ex = [jnp.exp(v - vals[0]) for v in vals]
    inv = 1.0 / (ex[0] + ex[1] + ex[2] + ex[3])
    eidx_ref[...] = jnp.concatenate(rows_e, axis=0)
    rank_ref[...] = jnp.concatenate(rows_r, axis=0).astype(jnp.int32)
    gate_ref[...] = jnp.concatenate([e * inv for e in ex], axis=0)
    cnt_ref[...] = jnp.broadcast_to(base, cnt_ref.shape)


def _route(x1, g_ffn, rw_hi, rw_lo, rb_col):
    n = x1.shape[0] // TILE_ROWS
    upper = jnp.asarray(np.triu(np.ones((ROW_TILE, ROW_TILE), np.float32), 1), BF16)
    col = lambda dt: (pl.BlockSpec((TOP_K, ROW_TILE), lambda i: (0, i)), jax.ShapeDtypeStruct((TOP_K, n), dt))
    outs = [col(jnp.int32), col(jnp.int32), col(F32),
            (_resident_out_spec((N_EXPERTS, LANES)), jax.ShapeDtypeStruct((N_EXPERTS, LANES), F32))]
    return pl.pallas_call(
        _route_kernel,
        grid=(n // ROW_TILE,),
        in_specs=[pl.BlockSpec((ROW_TILE * TILE_ROWS, LANES), lambda i: (i, 0)), _const_spec((1, D_MODEL)),
                  _const_spec(rw_hi.shape), _const_spec(rw_lo.shape), _const_spec((N_EXPERTS, 1)),
                  _const_spec(upper.shape)],
        out_specs=[o[0] for o in outs],
        out_shape=[o[1] for o in outs],
        scratch_shapes=[pltpu.VMEM((N_EXPERTS, 1), F32)],
        compiler_params=_params(("arbitrary",)),
        name="route",
    )(x1, g_ffn, rw_hi, rw_lo, rb_col, upper)


_INV_FIELDS = 6


def _split3_exact(x):
    def trunc(v):
        return pltpu.bitcast(pltpu.bitcast(v, jnp.uint32) & jnp.uint32(0xFFFF0000), F32)
    hi = trunc(x)
    rest = x - hi
    mid = trunc(rest)
    return hi, mid, rest - mid


def _invert_kernel(n_blocks, eidx_ref, rank_ref, gate_ref, pstart_ref, acc_ref):
    i = pl.program_id(0)

    @pl.when(i == 0)
    def _():
        acc_ref[...] = jnp.zeros(acc_ref.shape, acc_ref.dtype)

    eidx, rank, gate = eidx_ref[...], rank_ref[...], gate_ref[...]
    pstart = pstart_ref[...]
    rows = eidx.shape[1]
    eid = lax.broadcasted_iota(jnp.int32, (N_EXPERTS, rows), 0)
    blk_id = lax.broadcasted_iota(jnp.int32, (n_blocks, rows), 0)
    off_id = lax.broadcasted_iota(jnp.int32, (FFN_BLOCK, rows), 0)
    tok = i * rows + lax.broadcasted_iota(jnp.int32, (1, rows), 1)
    a_parts, b_parts = [], []
    shift = FFN_BLOCK.bit_length() - 1
    tok_lo = (tok & 255).astype(F32)
    tok_hi = lax.shift_right_logical(tok, 8).astype(F32)
    for k in range(TOP_K):
        base = jnp.sum(jnp.where(eid == eidx[k:k + 1], pstart, 0), axis=0, keepdims=True)
        pos = base + rank[k:k + 1]
        a_parts.append(jnp.where(blk_id == lax.shift_right_logical(pos, shift), 1.0, 0.0).astype(BF16))
        hit = off_id == (pos & (FFN_BLOCK - 1))
        g_hi, g_mid, g_lo = _split3_exact(gate[k:k + 1])
        fields = [tok_lo, tok_hi, jnp.full((1, rows), k + 1.0, F32), g_hi, g_mid, g_lo]
        b_parts.append([jnp.where(hit, f, 0.0).astype(BF16) for f in fields])
    a = jnp.concatenate(a_parts, axis=1)
    for j in range(_INV_FIELDS):
        b = jnp.concatenate([b_parts[k][j] for k in range(TOP_K)], axis=1)
        acc_ref[j] += _nt_dot(a, b)


def _invert(eidx, rank, gate, pad_start, n_blocks):
    n_all = eidx.shape[1]
    assert n_all // 256 < 256, "token index is carried as two byte-sized fields"
    col = pl.BlockSpec((TOP_K, ROW_TILE), lambda i: (0, i))
    shape = (_INV_FIELDS, n_blocks, FFN_BLOCK)
    f = pl.pallas_call(
        functools.partial(_invert_kernel, n_blocks),
        grid=(n_all // ROW_TILE,),
        in_specs=[col, col, col, _const_spec((N_EXPERTS, 1))],
        out_specs=_resident_out_spec(shape),
        out_shape=jax.ShapeDtypeStruct(shape, F32),
        compiler_params=_params(("arbitrary",)),
        name="invert",
    )(eidx, rank, gate, pad_start)
    t = (f[0] + 256.0 * f[1]).astype(jnp.int32)
    kp1 = f[2].astype(jnp.int32)
    blk = lax.broadcasted_iota(jnp.int32, t.shape, 0)
    off = lax.broadcasted_iota(jnp.int32, t.shape, 1)
    dump = TOP_K * n_all + (blk & 1) * FFN_BLOCK + off
    return t, jnp.where(kp1 > 0, (kp1 - 1) * n_all + t, dump), (f[3] + f[4]) + f[5]


def _ffn_kernel(be_ref, nused_ref, src_cur_ref, src_nxt_ref, dst_cur_ref, dst_prev_ref, gate_ref, gffn_ref,
                wup_ref, bup_ref, wdn_ref, bdn_ref, x1_hbm, out_hbm, xbuf0, xbuf1, obuf0, obuf1, wup_bf, wdn_bf,
                gsem, ssem):
    i = pl.program_id(0)
    n_used = nused_ref[0]
    xbufs, obufs = (xbuf0, xbuf1), (obuf0, obuf1)

    def hbm_tile(ref, token):
        return ref.at[pl.ds(pl.multiple_of(token * TILE_ROWS, TILE_ROWS), TILE_ROWS), :]

    def vmem_tile(ref, r):
        return ref.at[pl.ds(r * TILE_ROWS, TILE_ROWS), :]

    def gather_copy(src_ref, r, s):
        return pltpu.make_async_copy(hbm_tile(x1_hbm, src_ref[0, 0, r]), vmem_tile(xbufs[s], r), gsem.at[s])

    def scatter_copy(dst_ref, r, s):
        return pltpu.make_async_copy(vmem_tile(obufs[s], r), hbm_tile(out_hbm, dst_ref[0, 0, r]), ssem.at[s])

    def start_rows(copy_fn):
        for r in range(FFN_BLOCK):
            copy_fn(r).start(priority=r % 2)

    def wait_rows(copy_fn):
        for r in range(FFN_BLOCK):
            copy_fn(r).wait()

    first = i == 0
    changed = first | (be_ref[i] != be_ref[jnp.maximum(i - 1, 0)])

    @pl.when(first)
    def _():
        obuf0[...] = jnp.zeros(obuf0.shape, obuf0.dtype)
        obuf1[...] = jnp.zeros(obuf1.shape, obuf1.dtype)
        n_slot_rows = out_hbm.shape[0] // TILE_ROWS - 3 * FFN_BLOCK
        init = pltpu.make_async_copy(
            obuf1, out_hbm.at[pl.ds((n_slot_rows + FFN_BLOCK) * TILE_ROWS, FFN_BLOCK * TILE_ROWS), :], ssem.at[1])
        init.start()
        init.wait()
        start_rows(lambda r: pltpu.make_async_copy(
            vmem_tile(obuf0, r), out_hbm.at[pl.ds((n_slot_rows + r) * TILE_ROWS, TILE_ROWS), :], ssem.at[0]))
        start_rows(lambda r: gather_copy(src_cur_ref, r, 0))

    @pl.when(changed & (i < n_used))
    def _():
        wup_bf[...] = wup_ref[0].astype(BF16)
        wdn_bf[...] = wdn_ref[0].astype(BF16)

    def step(s):
        wait_rows(lambda r: gather_copy(src_cur_ref, r, s))
        x1 = _load_token_tiles(xbufs[s], FFN_BLOCK)
        ms = jnp.mean(x1 * x1, axis=-1, keepdims=True)
        x = (x1 * lax.rsqrt(ms + EPS) * gffn_ref[...]).astype(BF16)
        start_rows(lambda r: gather_copy(src_nxt_ref, r, 1 - s))
        start_rows(lambda r: scatter_copy(dst_prev_ref, r, 1 - s))
        hu = _dot(x, wup_bf[...]) + bup_ref[0]
        glu = jnp.minimum(hu[:, :D_FF], SWIGLU_LIMIT)
        lin = jnp.clip(hu[:, D_FF:], -SWIGLU_LIMIT, SWIGLU_LIMIT)
        act = glu * jax.nn.sigmoid(SWIGLU_ALPHA * glu) * (lin + 1.0)
        out = (_dot(act.astype(BF16), wdn_bf[...]) + bdn_ref[0]) * gate_ref[...]
        wait_rows(lambda r: scatter_copy(dst_cur_ref, r, s))
        _store_token_tiles(obufs[s], out)

    def drain(s):
        start_rows(lambda r: scatter_copy(dst_cur_ref, r, s))
        wait_rows(lambda r: scatter_copy(dst_cur_ref, r, 1 - s))
        wait_rows(lambda r: scatter_copy(dst_cur_ref, r, s))
        wait_rows(lambda r: gather_copy(src_cur_ref, r, 1 - s))

    for s in range(2):
        pl.when((i < n_used) & (i % 2 == s))(functools.partial(step, s))
    for s in range(2):
        pl.when((i == n_used - 1) & (i % 2 == s))(functools.partial(drain, s))


def _ffn(block_expert, n_used, src, dst, gate_sorted, x1, g_ffn, w_up, b_up, w_down, b_down, n_out_rows):
    n_blocks = block_expert.shape[0]
    smem_block = lambda fn: pl.BlockSpec((1, 1, FFN_BLOCK), fn, memory_space=pltpu.SMEM)
    cur = smem_block(lambda i, be, nu: (i, 0, 0))
    nxt = smem_block(lambda i, be, nu: (jnp.minimum(i + 1, n_blocks - 1), 0, 0))
    prev = smem_block(lambda i, be, nu: (jnp.where(i == 0, n_blocks, i - 1), 0, 0))
    ex3 = lambda i, be, nu: (be[i], 0, 0)
    return pl.pallas_call(
        _ffn_kernel,
        grid_spec=pltpu.PrefetchScalarGridSpec(
            num_scalar_prefetch=2,
            grid=(n_blocks,),
            in_specs=[
                cur, nxt, cur, prev,
                pl.BlockSpec((FFN_BLOCK, 1), lambda i, be, nu: (i, 0)),
                pl.BlockSpec((1, D_MODEL), lambda i, be, nu: (0, 0)),
                pl.BlockSpec((1, D_MODEL, 2 * D_FF), ex3),
                pl.BlockSpec((1, 1, 2 * D_FF), ex3),
                pl.BlockSpec((1, D_FF, D_MODEL), ex3),
                pl.BlockSpec((1, 1, D_MODEL), ex3),
                pl.BlockSpec(memory_space=pl.ANY),
            ],
            out_specs=pl.BlockSpec(memory_space=pl.ANY),
            scratch_shapes=[pltpu.VMEM((FFN_BLOCK * TILE_ROWS, LANES), F32)] * 4 + [
                pltpu.VMEM((D_MODEL, 2 * D_FF), BF16), pltpu.VMEM((D_FF, D_MODEL), BF16),
                pltpu.SemaphoreType.DMA((2,)), pltpu.SemaphoreType.DMA((2,))],
        ),
        out_shape=jax.ShapeDtypeStruct((n_out_rows * TILE_ROWS, LANES), F32),
        compiler_params=_params(("arbitrary",)),
        name="ffn",
    )(block_expert, n_used, src, src, dst, dst, gate_sorted, g_ffn, w_up, b_up, w_down, b_down, x1)


def _final_kernel(x1_ref, s0_ref, s1_ref, s2_ref, s3_ref, g_ref, y_ref, sum_scr):
    sum_scr[...] = x1_ref[...] + ((s0_ref[...] + s1_ref[...]) + (s2_ref[...] + s3_ref[...]))
    x2 = _load_token_tiles(sum_scr, ROW_TILE)
    ms = jnp.mean(x2 * x2, axis=-1, keepdims=True)
    y_ref[...] = x2 * lax.rsqrt(ms + EPS) * g_ref[...]


def _final(x1, slots, g_final, row0, n_rows, n_all):
    t0 = row0 // ROW_TILE
    per_slot = n_all // ROW_TILE
    tile_block = lambda b0: pl.BlockSpec((ROW_TILE * TILE_ROWS, LANES), lambda i: (b0 + i, 0))
    slot = lambda k: tile_block(k * per_slot + t0)
    return pl.pallas_call(
        _final_kernel,
        grid=(n_rows // ROW_TILE,),
        in_specs=[tile_block(t0),
                  slot(0), slot(1), slot(2), slot(3), _const_spec((1, D_MODEL))],
        out_specs=pl.BlockSpec((ROW_TILE, D_MODEL), lambda i: (i, 0)),
        out_shape=jax.ShapeDtypeStruct((n_rows, D_MODEL), F32),
        scratch_shapes=[pltpu.VMEM((ROW_TILE * TILE_ROWS, LANES), F32)],
        compiler_params=_params(("arbitrary",)),
        name="final",
    )(x1, slots, slots, slots, slots, g_final)


def _rope_tables(seq_len, dec_seq):
    inv = ROPE_THETA ** (-np.arange(0, ROT_DIM, 2, dtype=np.float64) / ROT_DIM)
    pos = np.concatenate([np.arange(seq_len), PAST_LEN + np.arange(ROW_TILE) % dec_seq]).astype(np.float64)
    ang = pos[:, None] * inv[None, :]
    cos, sin = np.cos(ang), np.sin(ang)
    ones = np.ones((pos.shape[0], HEAD_DIM - ROT_DIM))
    cos_h = np.concatenate([cos, cos, ones], axis=1)
    sin_h = np.concatenate([-sin, sin, 0.0 * ones], axis=1)
    return (jnp.asarray(np.concatenate([cos_h, cos_h], axis=1), F32),
            jnp.asarray(np.concatenate([sin_h, sin_h], axis=1), F32))


def _block_tables(counts, n_all):
    n_slots = TOP_K * n_all
    n_blocks = -(-n_slots // FFN_BLOCK) + N_EXPERTS
    padded = (counts + FFN_BLOCK - 1) // FFN_BLOCK * FFN_BLOCK
    pad_end = jnp.cumsum(padded)
    pad_start = (pad_end - padded).astype(jnp.int32).reshape(N_EXPERTS, 1)
    block_start = jnp.arange(n_blocks, dtype=jnp.int32) * FFN_BLOCK
    block_expert = jnp.minimum(jnp.sum(pad_end[None, :] <= block_start[:, None], axis=1), N_EXPERTS - 1)
    n_used = (pad_end[-1] // FFN_BLOCK).astype(jnp.int32).reshape(1)
    return pad_start, block_expert.astype(jnp.int32), n_used, n_blocks


def kernel(x_prompt, x_sample, cache_swa_k, cache_swa_v, state_gla, g_mix, w_in, w_gk_up, b_gk, sinks,
           gla_norm, w_branch_a, w_branch_b, w_out, g_ffn, router_w, router_b, w_up, b_up, w_down, b_down,
           g_final):
    batch, seq_len, _ = x_prompt.shape
    dec_batch, dec_seq, _ = x_sample.shape
    n_p, n_s = batch * seq_len, dec_batch * dec_seq
    n_all = n_p + n_s
    assert w_in.shape[0] == 1, "one layer: the final norm is fused after the only MoE"
    assert seq_len % ROW_TILE == 0 and n_s % ROW_TILE == 0 and ROW_TILE % dec_seq == 0
    assert dec_seq % SUBLANES == 0 and dec_batch % SAMPLE_GROUP == 0
    assert SAMPLE_GROUP * dec_seq == CHUNK and (dec_seq & (dec_seq - 1)) == 0

    xp2 = x_prompt.reshape(n_p, D_MODEL)
    xs2 = x_sample.reshape(n_s, D_MODEL)
    cos_tab, sin_tab = _rope_tables(seq_len, dec_seq)
    w = w_in[0]
    gl0 = _C_RG
    w_all = jnp.concatenate(
        [w[:, :gl0], w[:, gl0 + GK_RANK:], w[:, gl0:gl0 + GK_RANK],
         jnp.zeros((D_MODEL, LANES - GK_RANK), w.dtype)], axis=1).astype(BF16)
    wup_pad = jnp.concatenate([w_gk_up[0], jnp.zeros((LANES - GK_RANK, GLA_KDIM), F32)], axis=0).astype(BF16)
    qa, ka, va, qg, kg, vg, la, rg, ga, gb = _proj(
        xp2, xs2, g_mix[0].reshape(1, D_MODEL), cos_tab, sin_tab, w_all, wup_pad,
        b_gk[0].reshape(1, GLA_KDIM), seq_len)

    oa_p = _swa_prompt(sinks[0], qa, ka, va, batch, seq_len)
    oa_s, nk_s, nv_s = _swa_sample(sinks[0], qa, ka, va, cache_swa_k[0], cache_swa_v[0], n_p,
                                   dec_batch, dec_seq)
    norm_row = jnp.tile(gla_norm[0], GLA_HEADS).reshape(1, GLA_VDIM)
    og_p, s_fin = _gla_prompt(qg, kg, la, vg, rg, norm_row, batch, seq_len)
    og_s, s_new = _gla_sample(qg, kg, la, vg, rg, norm_row, state_gla[0], n_p, dec_batch, dec_seq)

    x1 = _merge(xp2, xs2, oa_p, oa_s, og_p, og_s, ga, gb, w_branch_a[0].astype(BF16),
                w_branch_b[0].astype(BF16), w_out[0].astype(BF16))

    g_ffn_row = g_ffn[0].reshape(1, D_MODEL)
    rw_t = router_w[0].T
    rw_hi = rw_t.astype(BF16)
    rw_lo = (rw_t - rw_hi.astype(F32)).astype(BF16)
    eidx, rank, gate, cnt = _route(x1, g_ffn_row, rw_hi, rw_lo, router_b[0].reshape(N_EXPERTS, 1))
    counts = cnt[:, 0].astype(jnp.int32)
    pad_start, block_expert, n_used, n_blocks = _block_tables(counts, n_all)
    src, dst, gate_sorted = _invert(eidx, rank, gate, pad_start, n_blocks)
    prime = (TOP_K * n_all + 2 * FFN_BLOCK + jnp.arange(FFN_BLOCK, dtype=jnp.int32)).reshape(1, FFN_BLOCK)
    dst = jnp.concatenate([dst, prime], axis=0)
    slots = _ffn(block_expert, n_used, src.reshape(n_blocks, 1, FFN_BLOCK),
                 dst.reshape(n_blocks + 1, 1, FFN_BLOCK), gate_sorted.reshape(n_blocks * FFN_BLOCK, 1),
                 x1, g_ffn_row, w_up[0], b_up[0].reshape(N_EXPERTS, 1, 2 * D_FF), w_down[0],
                 b_down[0].reshape(N_EXPERTS, 1, D_MODEL), TOP_K * n_all + 3 * FFN_BLOCK)

    g_out = g_final.reshape(1, D_MODEL)
    y_p = _final(x1, slots, g_out, 0, n_p, n_all)
    y_s = _final(x1, slots, g_out, n_p, n_s, n_all)

    kv_shape = (1, -1, WINDOW, KV_HEADS, HEAD_DIM)
    new_k_p = ka[:n_p].reshape(batch, seq_len, KV_WIDTH)[:, -WINDOW:].reshape(kv_shape)
    new_v_p = va[:n_p].reshape(batch, seq_len, KV_WIDTH)[:, -WINDOW:].reshape(kv_shape)
    return (y_p.reshape(batch, seq_len, D_MODEL), y_s.reshape(dec_batch, dec_seq, D_MODEL),
            new_k_p, new_v_p, s_fin[None], nk_s.reshape(kv_shape), nv_s.reshape(kv_shape), s_new[None])
```

```python
import functools

import numpy as np
import jax
import jax.numpy as jnp
from jax import lax
from jax.experimental import pallas as pl
from jax.experimental.pallas import tpu as pltpu

D_MODEL = 1024
PAST_LEN = 8192
HEAD_DIM = 64
N_HEADS = 8
KV_HEADS = 2
GROUP = N_HEADS // KV_HEADS
WINDOW = 128
ROT_DIM = HEAD_DIM // 4
ROPE_THETA = 500000.0
ATT_WIDTH = N_HEADS * HEAD_DIM
KV_WIDTH = KV_HEADS * HEAD_DIM
GLA_HEADS = 4
GLA_KDIM = D_MODEL // 2
GLA_VDIM = D_MODEL
GLA_DK = GLA_KDIM // GLA_HEADS
GLA_DV = GLA_VDIM // GLA_HEADS
GK_RANK = 16
GK_NORMALIZER = 16.0
N_EXPERTS = 32
TOP_K = 4
D_FF = D_MODEL
SWIGLU_LIMIT = 7.0
SWIGLU_ALPHA = 1.702
EPS = 1e-5
NEG_INF = -1e30

LANES = 128
SUBLANES = 8
VMEM_LIMIT_BYTES = 56 * 1024 * 1024

ROW_TILE = 512
CHUNK = 128
FFN_BLOCK = 256

BF16 = jnp.bfloat16
F32 = jnp.float32

_C_QA, _C_KA, _C_VA, _C_QG, _C_KG, _C_VG, _C_RG, _C_GA, _C_GB, _C_GL, _C_END = (
    0, 512, 640, 768, 1280, 1792, 2816, 3840, 4864, 5888, 6016)


def _const_spec(shape):
    nd = len(shape)
    return pl.BlockSpec(shape, lambda *_: (0,) * nd, pipeline_mode=pl.Buffered(1))


def _resident_out_spec(shape):
    nd = len(shape)
    return pl.BlockSpec(shape, lambda *_: (0,) * nd)


def _params(sem, vmem=VMEM_LIMIT_BYTES):
    return pltpu.CompilerParams(dimension_semantics=sem, vmem_limit_bytes=vmem)


def _nt_dot(a, b):
    return lax.dot_general(a, b, (((1,), (1,)), ((), ())), preferred_element_type=F32)


def _dot(a, b):
    return jnp.dot(a, b, preferred_element_type=F32)


TILE_ROWS = D_MODEL // LANES
assert TILE_ROWS == SUBLANES


def _load_token_tiles(ref, n_tokens):
    return jnp.concatenate([ref[pl.ds(c, n_tokens, stride=TILE_ROWS), :] for c in range(TILE_ROWS)], axis=1)


def _store_token_tiles(ref, x):
    for c in range(TILE_ROWS):
        ref[pl.ds(c, x.shape[0], stride=TILE_ROWS), :] = x[:, c * LANES:(c + 1) * LANES]


def _rope(x, cos_t, sin_t, n_rep):
    width = x.shape[1]
    cos_f = jnp.concatenate([cos_t] * n_rep, axis=1) if n_rep > 1 else cos_t
    sin_f = jnp.concatenate([sin_t] * n_rep, axis=1) if n_rep > 1 else sin_t
    lane = lax.broadcasted_iota(jnp.int32, x.shape, 1) % HEAD_DIM
    up = pltpu.roll(x, width - ROT_DIM // 2, 1)
    down = pltpu.roll(x, ROT_DIM // 2, 1)
    partner = jnp.where(lane < ROT_DIM // 2, up, down)
    return x * cos_f + partner * sin_f


def _proj_kernel(n_prompt_tiles, xp_ref, xs_ref, g_ref, cos_ref, sin_ref, w_ref, wup_ref, bgk_ref,
                 qa_ref, ka_ref, va_ref, qg_ref, kg_ref, vg_ref, la_ref, rg_ref, ga_ref, gb_ref):
    i = pl.program_id(0)
    x = jnp.where(i < n_prompt_tiles, xp_ref[...], xs_ref[...])
    ms = jnp.mean(x * x, axis=-1, keepdims=True)
    h = (x * lax.rsqrt(ms + EPS) * g_ref[...]).astype(BF16)
    cos_t = cos_ref[...]
    sin_t = sin_ref[...]

    def seg(a, b):
        return _dot(h, w_ref[:, a:b])

    qa = _rope(seg(_C_QA, _C_KA), cos_t, sin_t, ATT_WIDTH // LANES)
    qa_ref[...] = (qa * (HEAD_DIM ** -0.5)).astype(BF16)
    ka_ref[...] = _rope(seg(_C_KA, _C_VA), cos_t, sin_t, 1)
    va_ref[...] = seg(_C_VA, _C_QG)
    qg_ref[...] = seg(_C_QG, _C_KG) * (GLA_DK ** -0.5)
    kg_ref[...] = seg(_C_KG, _C_VG)
    vg_ref[...] = seg(_C_VG, _C_RG).astype(BF16)
    rg_ref[...] = seg(_C_RG, _C_GA).astype(BF16)
    ga_ref[...] = seg(_C_GA, _C_GB).astype(BF16)
    gb_ref[...] = seg(_C_GB, _C_GL).astype(BF16)
    gk_low = seg(_C_GL, _C_END).astype(BF16)
    z = _dot(gk_low, wup_ref[...]) + bgk_ref[...]
    log_sig = jnp.minimum(z, 0.0) - jnp.log1p(jnp.exp(-jnp.abs(z)))
    la_ref[...] = log_sig / GK_NORMALIZER


def _proj(xp2, xs2, g_mix, cos_tab, sin_tab, w_all, wup_pad, b_gk, seq_len):
    n_p, n_s = xp2.shape[0], xs2.shape[0]
    n_all = n_p + n_s
    npt, nst = n_p // ROW_TILE, n_s // ROW_TILE
    tiles_per_seq = seq_len // ROW_TILE

    def tab_map(i):
        return (jnp.where(i < npt, i % tiles_per_seq, tiles_per_seq), 0)

    row = lambda w: pl.BlockSpec((ROW_TILE, w), lambda i: (i, 0))
    widths = [(ATT_WIDTH, BF16), (KV_WIDTH, F32), (KV_WIDTH, F32), (GLA_KDIM, F32), (GLA_KDIM, F32),
              (GLA_VDIM, BF16), (GLA_KDIM, F32), (GLA_VDIM, BF16), (D_MODEL, BF16), (D_MODEL, BF16)]
    return pl.pallas_call(
        functools.partial(_proj_kernel, npt),
        grid=(npt + nst,),
        in_specs=[
            pl.BlockSpec((ROW_TILE, D_MODEL), lambda i: (jnp.minimum(i, npt - 1), 0)),
            pl.BlockSpec((ROW_TILE, D_MODEL), lambda i: (jnp.maximum(i - npt, 0), 0)),
            _const_spec((1, D_MODEL)),
            pl.BlockSpec((ROW_TILE, LANES), tab_map),
            pl.BlockSpec((ROW_TILE, LANES), tab_map),
            _const_spec(w_all.shape),
            _const_spec(wup_pad.shape),
            _const_spec((1, GLA_KDIM)),
        ],
        out_specs=[row(w) for w, _ in widths],
        out_shape=[jax.ShapeDtypeStruct((n_all, w), dt) for w, dt in widths],
        compiler_params=_params(("arbitrary",)),
        name="proj",
    )(xp2, xs2, g_mix, cos_tab, sin_tab, w_all, wup_pad, b_gk)


def _pair_blocks(kk):
    lane = lax.broadcasted_iota(jnp.int32, kk.shape, 1)
    lo = lane < HEAD_DIM
    swapped = pltpu.roll(kk, HEAD_DIM, 1)
    zero = jnp.zeros_like(kk)
    blocks = []
    for kh in range(KV_HEADS):
        left = jnp.where(lo, kk if kh == 0 else swapped, zero)
        right = jnp.where(lo, zero, swapped if kh == 0 else kk)
        blocks.append(jnp.concatenate([left, right], axis=0).astype(BF16))
    return blocks


def _sink_softmax(s, valid, sink):
    s = jnp.where(valid, s, NEG_INF)
    m = jnp.maximum(jnp.max(s, axis=-1, keepdims=True), sink)
    p = jnp.exp(s - m)
    denom = jnp.sum(p, axis=-1, keepdims=True) + jnp.exp(sink - m)
    return (p * (1.0 / denom)).astype(BF16)


def _attend(q, kk, vv, valid, sink_ref, o_ref, row0=0):
    rows, keys = valid.shape
    kblocks = _pair_blocks(kk)
    vblocks = _pair_blocks(vv)
    for kh in range(KV_HEADS):
        base = kh * GROUP * HEAD_DIM
        qq = jnp.concatenate([q[:, base:base + LANES], q[:, base + LANES:base + 2 * LANES]], axis=0)
        s = _nt_dot(qq, kblocks[kh])
        for r in range(2):
            probs = []
            for c in range(2):
                head = kh * GROUP + 2 * r + c
                probs.append(_sink_softmax(s[r * rows:(r + 1) * rows, c * keys:(c + 1) * keys],
                                           valid, sink_ref[head]))
            p = jnp.concatenate(probs, axis=1)
            o_ref[pl.ds(row0, rows), base + r * LANES:base + (r + 1) * LANES] = (
                _dot(p, vblocks[kh]).astype(o_ref.dtype))


def _swa_prompt_kernel(sink_ref, q_ref, kp_ref, k0_ref, k1_ref, vp_ref, v0_ref, v1_ref, o_ref):
    j = pl.program_id(1)
    row = lax.broadcasted_iota(jnp.int32, (WINDOW, 2 * WINDOW), 0)
    col = lax.broadcasted_iota(jnp.int32, (WINDOW, 2 * WINDOW), 1)
    band = (col > row) & (col <= row + WINDOW)
    k_blocks = (kp_ref[...], k0_ref[...], k1_ref[...])
    v_blocks = (vp_ref[...], v0_ref[...], v1_ref[...])
    q = q_ref[...]
    for half in range(2):
        kk = jnp.concatenate(k_blocks[half:half + 2], axis=0)
        vv = jnp.concatenate(v_blocks[half:half + 2], axis=0)
        valid = band & ((j > 0) | (col >= WINDOW)) if half == 0 else band
        _attend(q[half * WINDOW:(half + 1) * WINDOW], kk, vv, valid, sink_ref, o_ref, half * WINDOW)


def _swa_prompt(sinks, qa, ka, va, batch, seq_len):
    nb = seq_len // WINDOW
    assert nb % 2 == 0
    kv = lambda off: pl.BlockSpec((WINDOW, KV_WIDTH), lambda b, j, s: (b * nb + jnp.maximum(2 * j + off, 0), 0))
    pair = pl.BlockSpec((2 * WINDOW, ATT_WIDTH), lambda b, j, s: (b * (nb // 2) + j, 0))
    return pl.pallas_call(
        _swa_prompt_kernel,
        grid_spec=pltpu.PrefetchScalarGridSpec(
            num_scalar_prefetch=1,
            grid=(batch, nb // 2),
            in_specs=[pair, kv(-1), kv(0), kv(1), kv(-1), kv(0), kv(1)],
            out_specs=pair,
        ),
        out_shape=jax.ShapeDtypeStruct((batch * seq_len, ATT_WIDTH), BF16),
        compiler_params=_params(("arbitrary", "arbitrary")),
        name="swa_prompt",
    )(sinks, qa, ka, ka, ka, va, va, va)


SAMPLE_GROUP = 16


def _swa_sample_kernel(dec_seq, sink_ref, q_ref, kn_ref, vn_ref, ck_ref, cv_ref, o_ref, nk_ref, nv_ref):
    rows = SAMPLE_GROUP * dec_seq
    ck = ck_ref[...]
    cv = cv_ref[...]
    kn = kn_ref[...]
    vn = vn_ref[...]
    nk_ref[:, :WINDOW - dec_seq, :] = ck[:, dec_seq:, :]
    nv_ref[:, :WINDOW - dec_seq, :] = cv[:, dec_seq:, :]
    nk_ref[:, WINDOW - dec_seq:, :] = kn.reshape(SAMPLE_GROUP, dec_seq, KV_WIDTH)
    nv_ref[:, WINDOW - dec_seq:, :] = vn.reshape(SAMPLE_GROUP, dec_seq, KV_WIDTH)
    n_cache = SAMPLE_GROUP * WINDOW
    kk = jnp.concatenate([ck.reshape(n_cache, KV_WIDTH), kn], axis=0)
    vv = jnp.concatenate([cv.reshape(n_cache, KV_WIDTH), vn], axis=0)
    keys = n_cache + rows
    row = lax.broadcasted_iota(jnp.int32, (rows, keys), 0)
    col = lax.broadcasted_iota(jnp.int32, (rows, keys), 1)
    q_b, q_s = row // dec_seq, row % dec_seq
    is_cache = col < n_cache
    new = col - n_cache
    valid_cache = (col // WINDOW == q_b) & (col % WINDOW > q_s)
    valid_new = (new // dec_seq == q_b) & (new % dec_seq <= q_s)
    valid = (is_cache & valid_cache) | (jnp.logical_not(is_cache) & valid_new)
    _attend(q_ref[...], kk, vv, valid, sink_ref, o_ref)


def _swa_sample(sinks, qa, ka, va, cache_k, cache_v, n_prompt_rows, dec_batch, dec_seq):
    rows = SAMPLE_GROUP * dec_seq
    off = n_prompt_rows // rows
    tok = lambda g, s: (off + g, 0)
    cache = lambda g, s: (g, 0, 0)
    cshape = (dec_batch, WINDOW, KV_WIDTH)
    return pl.pallas_call(
        functools.partial(_swa_sample_kernel, dec_seq),
        grid_spec=pltpu.PrefetchScalarGridSpec(
            num_scalar_prefetch=1,
            grid=(dec_batch // SAMPLE_GROUP,),
            in_specs=[
                pl.BlockSpec((rows, ATT_WIDTH), tok),
                pl.BlockSpec((rows, KV_WIDTH), tok),
                pl.BlockSpec((rows, KV_WIDTH), tok),
                pl.BlockSpec((SAMPLE_GROUP, WINDOW, KV_WIDTH), cache),
                pl.BlockSpec((SAMPLE_GROUP, WINDOW, KV_WIDTH), cache),
            ],
            out_specs=[
                pl.BlockSpec((rows, ATT_WIDTH), lambda g, s: (g, 0)),
                pl.BlockSpec((SAMPLE_GROUP, WINDOW, KV_WIDTH), cache),
                pl.BlockSpec((SAMPLE_GROUP, WINDOW, KV_WIDTH), cache),
            ],
        ),
        out_shape=[jax.ShapeDtypeStruct((dec_batch * dec_seq, ATT_WIDTH), BF16),
                   jax.ShapeDtypeStruct(cshape, F32), jax.ShapeDtypeStruct(cshape, F32)],
        compiler_params=_params(("arbitrary",)),
        name="swa_sample",
    )(sinks, qa, ka, va, cache_k.reshape(cshape), cache_v.reshape(cshape))


def _chunk_tables(seg):
    n_lev = int(np.log2(seg))
    t = np.arange(CHUNK)
    seg_start = (t // seg) * seg
    u = np.arange(CHUNK)[None, :]

    def prefix(end):
        return ((u >= seg_start[:, None]) & (u <= end[:, None])).astype(np.float32)

    blocks = [prefix(t)]
    for d in range(min(n_lev, _MATMUL_LEVELS)):
        m = 1 << d
        ref = (t >> (d + 1) << (d + 1)) + m - 1
        blocks.append(prefix(ref))
    lhs = np.concatenate(blocks, axis=0)
    lhs2 = np.concatenate([lhs, lhs], axis=1)
    tt, ss = t[:, None], t[None, :]
    x = tt ^ ss
    lev = np.where(x > 0, np.floor(np.log2(np.maximum(x, 1))).astype(np.int32), n_lev)
    lev = np.where((ss > tt) | (tt // seg != ss // seg), -1, lev)
    lev = np.where(tt == ss, n_lev, lev)
    return jnp.asarray(lhs2, BF16), jnp.asarray(lev, jnp.int32), n_lev


_MATMUL_LEVELS = 3


def _group_row(x, group, row):
    width = x.shape[1]
    parts = [jnp.broadcast_to(x[g * group + row:g * group + row + 1, :], (group, width))
             for g in range(x.shape[0] // group)]
    return parts[0] if len(parts) == 1 else jnp.concatenate(parts, axis=0)


_HALF_ROW_LEVELS = 4
LOG2_E = 1.4426950408889634


def _halves(x, m, which):
    parts = [x[(2 * g + which) * m:(2 * g + which + 1) * m] for g in range(x.shape[0] // (2 * m))]
    return parts[0] if len(parts) == 1 else jnp.concatenate(parts, axis=0)


def _unhalve(xh, m, which):
    zero = jnp.zeros((m, xh.shape[1]), xh.dtype)
    parts = []
    for g in range(xh.shape[0] // m):
        blk = xh[g * m:(g + 1) * m]
        parts += [zero, blk] if which else [blk, zero]
    return jnp.concatenate(parts, axis=0)


def _gla_chunk_terms(q, k, la, lhs2, level, n_lev, seg):
    la2 = la * LOG2_E
    hi = la2.astype(BF16)
    lo = (la2 - hi.astype(F32)).astype(BF16)
    sums = _dot(lhs2, jnp.concatenate([hi, lo], axis=0))
    b = sums[0:CHUNK]
    b_last = _group_row(b, seg, seg - 1)
    q_main = (q * jnp.exp2(b)).astype(BF16)
    k_upd = k * jnp.exp2(b_last - b)
    q_lev = [None] * n_lev
    k_lev = [None] * n_lev
    for d in range(n_lev):
        m = 1 << d
        if d < _MATMUL_LEVELS:
            ref = sums[(1 + d) * CHUNK:(2 + d) * CHUNK]
        elif d < _HALF_ROW_LEVELS:
            ref = _group_row(b, 2 * m, m - 1)
        if d < _HALF_ROW_LEVELS:
            q_lev[d] = (q * jnp.exp2(b - ref)).astype(BF16)
            k_lev[d] = (k * jnp.exp2(ref - b)).astype(BF16)
        else:
            ref_h = _group_row(_halves(b, m, 0), m, m - 1)
            q_lev[d] = _unhalve((_halves(q, m, 1) * jnp.exp2(_halves(b, m, 1) - ref_h)).astype(BF16), m, 1)
            k_lev[d] = _unhalve((_halves(k, m, 0) * jnp.exp2(ref_h - _halves(b, m, 0))).astype(BF16), m, 0)
    q_b, k_b = q.astype(BF16), k.astype(BF16)

    def att(h):
        hs = slice(h * GLA_DK, (h + 1) * GLA_DK)
        acc = jnp.where(level == n_lev, _nt_dot(q_b[:, hs], k_b[:, hs]), 0.0)
        for d in range(n_lev):
            acc = jnp.where(level == d, _nt_dot(q_lev[d][:, hs], k_lev[d][:, hs]), acc)
        return acc

    return q_main, k_upd, att, b_last


def _gla_out(o, r, norm):
    parts = []
    for h in range(GLA_HEADS):
        oh = o[:, h * GLA_DV:(h + 1) * GLA_DV]
        ms = jnp.mean(oh * oh, axis=-1, keepdims=True)
        parts.append(oh * lax.rsqrt(ms + EPS))
    y = jnp.concatenate(parts, axis=1) * norm
    rf = r.astype(F32)
    return y * (rf * jax.nn.sigmoid(rf))


def _gla_prompt_kernel(n_lev, n_par, *refs):
    seq_refs = [refs[5 * j:5 * j + 5] for j in range(n_par)]
    norm_ref, lhs_ref, lev_ref = refs[5 * n_par:5 * n_par + 3]
    o_ref, sfin_ref, s_scr = refs[5 * n_par + 3:]
    c = pl.program_id(1)

    @pl.when(c == 0)
    def _():
        s_scr[...] = jnp.zeros_like(s_scr)

    for j, (q_ref, k_ref, la_ref, v_ref, r_ref) in enumerate(seq_refs):
        q_main, k_upd, att, b_last = _gla_chunk_terms(q_ref[...], k_ref[...], la_ref[...], lhs_ref[...],
                                                      lev_ref[...], n_lev, CHUNK)
        v = v_ref[...]
        outs = []
        for h in range(GLA_HEADS):
            hs = slice(h * GLA_DK, (h + 1) * GLA_DK)
            vh = v[:, h * GLA_DV:(h + 1) * GLA_DV]
            s0 = s_scr[j, h]
            o_h = _dot(q_main[:, hs], s0.astype(BF16)) + _dot(att(h).astype(BF16), vh)
            outs.append(o_h)
            decay = jnp.exp2(b_last[:, hs]).T
            k_t = k_upd[:, hs].T.astype(BF16)
            s_scr[j, h] = jnp.concatenate([decay, decay], axis=1) * s0 + _dot(k_t, vh)
        o_ref[j] = _gla_out(jnp.concatenate(outs, axis=1), r_ref[...], norm_ref[...]).astype(BF16)

    @pl.when(c == pl.num_programs(1) - 1)
    def _():
        sfin_ref[...] = s_scr[...]


def _gla_prompt(qg, kg, la, vg, rg, norm_row, batch, seq_len):
    nc = seq_len // CHUNK
    n_par = 2 if batch % 2 == 0 else 1
    lhs2, level, n_lev = _chunk_tables(CHUNK)
    tok = lambda j, w: pl.BlockSpec((CHUNK, w), lambda b, c: ((b * n_par + j) * nc + c, 0))
    seq_specs, seq_args = [], []
    for j in range(n_par):
        seq_specs += [tok(j, GLA_KDIM), tok(j, GLA_KDIM), tok(j, GLA_KDIM), tok(j, GLA_VDIM), tok(j, GLA_VDIM)]
        seq_args += [qg, kg, la, vg, rg]
    og, s_fin = pl.pallas_call(
        functools.partial(_gla_prompt_kernel, n_lev, n_par),
        grid=(batch // n_par, nc),
        in_specs=seq_specs + [_const_spec((1, GLA_VDIM)), _const_spec(lhs2.shape), _const_spec(level.shape)],
        out_specs=[pl.BlockSpec((n_par, CHUNK, GLA_VDIM), lambda b, c: (b, c, 0)),
                   pl.BlockSpec((n_par, GLA_HEADS, GLA_DK, GLA_DV), lambda b, c: (b, 0, 0, 0))],
        out_shape=[jax.ShapeDtypeStruct((batch, seq_len, GLA_VDIM), BF16),
                   jax.ShapeDtypeStruct((batch, GLA_HEADS, GLA_DK, GLA_DV), F32)],
        scratch_shapes=[pltpu.VMEM((n_par, GLA_HEADS, GLA_DK, GLA_DV), F32)],
        compiler_params=_params(("arbitrary", "arbitrary")),
        name="gla_prompt",
    )(*seq_args, norm_row, lhs2, level)
    return og.reshape(batch * seq_len, GLA_VDIM), s_fin


def _gla_sample_kernel(n_lev, dec_seq, q_ref, k_ref, la_ref, v_ref, r_ref, norm_ref, lhs_ref, lev_ref,
                       s0_ref, o_ref, snew_ref):
    q_main, k_upd, att, b_last = _gla_chunk_terms(q_ref[...], k_ref[...], la_ref[...], lhs_ref[...],
                                                  lev_ref[...], n_lev, dec_seq)
    v = v_ref[...]
    n_b = CHUNK // dec_seq
    row_b = lax.broadcasted_iota(jnp.int32, (CHUNK, GLA_DK), 0) // dec_seq
    col_b = lax.broadcasted_iota(jnp.int32, (GLA_DK, CHUNK), 1) // dec_seq
    outs = []
    for h in range(GLA_HEADS):
        hs = slice(h * GLA_DK, (h + 1) * GLA_DK)
        vh = v[:, h * GLA_DV:(h + 1) * GLA_DV]
        qm = q_main[:, hs]
        decay_t = jnp.exp2(b_last[:, hs]).T
        k_t = k_upd[:, hs].T.astype(BF16)
        o_h = _dot(att(h).astype(BF16), vh)
        for bi in range(n_b):
            s0 = s0_ref[bi, h]
            o_h = o_h + _dot(jnp.where(row_b == bi, qm, jnp.zeros_like(qm)), s0.astype(BF16))
            decay = jnp.broadcast_to(decay_t[:, bi * dec_seq:bi * dec_seq + 1], (GLA_DK, GLA_DV))
            k_b = jnp.where(col_b == bi, k_t, jnp.zeros_like(k_t))
            snew_ref[bi, h] = decay * s0 + _dot(k_b, vh)
        outs.append(o_h)
    o_ref[...] = _gla_out(jnp.concatenate(outs, axis=1), r_ref[...], norm_ref[...]).astype(o_ref.dtype)


def _gla_sample(qg, kg, la, vg, rg, norm_row, state, n_prompt_rows, dec_batch, dec_seq):
    n_b = CHUNK // dec_seq
    off = n_prompt_rows // CHUNK
    lhs3, level, n_lev = _chunk_tables(dec_seq)
    tok = lambda w: pl.BlockSpec((CHUNK, w), lambda g: (off + g, 0))
    st = pl.BlockSpec((n_b, GLA_HEADS, GLA_DK, GLA_DV), lambda g: (g, 0, 0, 0))
    return pl.pallas_call(
        functools.partial(_gla_sample_kernel, n_lev, dec_seq),
        grid=(dec_batch // n_b,),
        in_specs=[tok(GLA_KDIM), tok(GLA_KDIM), tok(GLA_KDIM), tok(GLA_VDIM), tok(GLA_VDIM),
                  _const_spec((1, GLA_VDIM)), _const_spec(lhs3.shape), _const_spec(level.shape), st],
        out_specs=[pl.BlockSpec((CHUNK, GLA_VDIM), lambda g: (g, 0)), st],
        out_shape=[jax.ShapeDtypeStruct((dec_batch * dec_seq, GLA_VDIM), BF16),
                   jax.ShapeDtypeStruct(state.shape, F32)],
        compiler_params=_params(("arbitrary",)),
        name="gla_sample",
    )(qg, kg, la, vg, rg, norm_row, lhs3, level, state)


def _merge_kernel(n_prompt_tiles, xp_ref, xs_ref, oap_ref, oas_ref, ogp_ref, ogs_ref, ga_ref, gb_ref,
                  wa_ref, wb_ref, wo_ref, x1_ref):
    i = pl.program_id(0)
    is_p = i < n_prompt_tiles
    x = jnp.where(is_p, xp_ref[...], xs_ref[...])
    oa = jnp.where(is_p, oap_ref[...], oas_ref[...])
    og = jnp.where(is_p, ogp_ref[...], ogs_ref[...])
    m = (jax.nn.sigmoid(ga_ref[...].astype(F32)) * _dot(oa, wa_ref[...])
         + jax.nn.sigmoid(gb_ref[...].astype(F32)) * _dot(og, wb_ref[...]))
    _store_token_tiles(x1_ref, x + _dot(m.astype(BF16), wo_ref[...]))


def _merge(xp2, xs2, oa_p, oa_s, og_p, og_s, ga, gb, wa, wb, wo):
    n_p, n_s = xp2.shape[0], xs2.shape[0]
    npt, nst = n_p // ROW_TILE, n_s // ROW_TILE
    p_map = lambda i: (jnp.minimum(i, npt - 1), 0)
    s_map = lambda i: (jnp.maximum(i - npt, 0), 0)
    row = lambda w: pl.BlockSpec((ROW_TILE, w), lambda i: (i, 0))
    return pl.pallas_call(
        functools.partial(_merge_kernel, npt),
        grid=(npt + nst,),
        in_specs=[
            pl.BlockSpec((ROW_TILE, D_MODEL), p_map), pl.BlockSpec((ROW_TILE, D_MODEL), s_map),
            pl.BlockSpec((ROW_TILE, ATT_WIDTH), p_map), pl.BlockSpec((ROW_TILE, ATT_WIDTH), s_map),
            pl.BlockSpec((ROW_TILE, GLA_VDIM), p_map), pl.BlockSpec((ROW_TILE, GLA_VDIM), s_map),
            row(D_MODEL), row(D_MODEL),
            _const_spec(wa.shape), _const_spec(wb.shape), _const_spec(wo.shape),
        ],
        out_specs=pl.BlockSpec((ROW_TILE * TILE_ROWS, LANES), lambda i: (i, 0)),
        out_shape=jax.ShapeDtypeStruct(((n_p + n_s) * TILE_ROWS, LANES), F32),
        compiler_params=_params(("arbitrary",)),
        name="merge",
    )(xp2, xs2, oa_p, oa_s, og_p, og_s, ga, gb, wa, wb, wo)


CHUNK_ROWS = 8
CHUNKS_PER_BLOCK = FFN_BLOCK // CHUNK_ROWS
TILE_PACK = TOP_K * ROW_TILE + N_EXPERTS * (CHUNK_ROWS - 1)
TILE_PACK += -TILE_PACK % CHUNK_ROWS


def _route_kernel(x1_ref, gffn_ref, rwh_ref, rwl_ref, rb_ref, upper_ref, lower_ref, eidx_ref, prel_ref, qloc_ref,
                  gate_ref, cnt_ref, cnt_scr):
    i = pl.program_id(0)

    @pl.when(i == 0)
    def _():
        cnt_scr[...] = jnp.zeros_like(cnt_scr)

    x1 = _load_token_tiles(x1_ref, ROW_TILE)
    ms = jnp.mean(x1 * x1, axis=-1, keepdims=True)
    h2 = x1 * lax.rsqrt(ms + EPS) * gffn_ref[...]
    h_hi = h2.astype(BF16)
    h_lo = (h2 - h_hi.astype(F32)).astype(BF16)
    rwh, rwl = rwh_ref[...], rwl_ref[...]
    logits = _nt_dot(rwh, h_hi) + _nt_dot(rwl, h_hi) + _nt_dot(rwh, h_lo) + rb_ref[...]
    eid = lax.broadcasted_iota(jnp.int32, logits.shape, 0)
    upper = upper_ref[...]
    seen = cnt_scr[...]
    vals, rows_e, hots, befores, counts = [], [], [], [], []
    lg = logits
    for _ in range(TOP_K):
        mx = jnp.max(lg, axis=0, keepdims=True)
        sel = jnp.min(jnp.where(lg == mx, eid, N_EXPERTS), axis=0, keepdims=True)
        onehot = eid == sel
        oh = onehot.astype(F32)
        hots.append(oh)
        befores.append(_dot(onehot.astype(BF16), upper))
        counts.append(jnp.sum(oh, axis=1, keepdims=True))
        vals.append(mx)
        rows_e.append(sel)
        lg = jnp.where(onehot, -jnp.inf, lg)
    run = (counts[0] + counts[1]) + (counts[2] + counts[3])
    run_pad = jnp.floor((run + (CHUNK_ROWS - 1)) * (1.0 / CHUNK_ROWS)) * CHUNK_ROWS
    tile_off = _dot(lower_ref[...], jnp.broadcast_to(run_pad, (N_EXPERTS, LANES)).astype(BF16))[:, 0:1]
    rows_p, rows_q = [], []
    ahead = jnp.zeros_like(run)
    for k in range(TOP_K):
        local = ahead + befores[k]
        rows_p.append(jnp.sum(hots[k] * (seen + local), axis=0, keepdims=True))
        rows_q.append(jnp.sum(hots[k] * (tile_off + local), axis=0, keepdims=True))
        ahead = ahead + counts[k]
    cnt_scr[...] = seen + run_pad
    ex = [jnp.exp(v - vals[0]) for v in vals]
    inv = 1.0 / (ex[0] + ex[1] + ex[2] + ex[3])
    eidx_ref[...] = jnp.concatenate(rows_e, axis=0)
    prel_ref[...] = jnp.concatenate(rows_p, axis=0).astype(jnp.int32)
    qloc_ref[...] = jnp.concatenate(rows_q, axis=0).astype(jnp.int32)
    gate_ref[...] = jnp.concatenate([e * inv for e in ex], axis=0)
    cnt_ref[...] = jnp.broadcast_to(seen + run_pad, cnt_ref.shape)


def _route(x1, g_ffn, rw_hi, rw_lo, rb_col):
    n = x1.shape[0] // TILE_ROWS
    upper = jnp.asarray(np.triu(np.ones((ROW_TILE, ROW_TILE), np.float32), 1), BF16)
    lower = jnp.asarray(np.tril(np.ones((N_EXPERTS, N_EXPERTS), np.float32), -1), BF16)
    col = lambda dt: (pl.BlockSpec((TOP_K, ROW_TILE), lambda i: (0, i)), jax.ShapeDtypeStruct((TOP_K, n), dt))
    outs = [col(jnp.int32), col(jnp.int32), col(jnp.int32), col(F32),
            (_resident_out_spec((N_EXPERTS, LANES)), jax.ShapeDtypeStruct((N_EXPERTS, LANES), F32))]
    return pl.pallas_call(
        _route_kernel,
        grid=(n // ROW_TILE,),
        in_specs=[pl.BlockSpec((ROW_TILE * TILE_ROWS, LANES), lambda i: (i, 0)), _const_spec((1, D_MODEL)),
                  _const_spec(rw_hi.shape), _const_spec(rw_lo.shape), _const_spec((N_EXPERTS, 1)),
                  _const_spec(upper.shape), _const_spec(lower.shape)],
        out_specs=[o[0] for o in outs],
        out_shape=[o[1] for o in outs],
        scratch_shapes=[pltpu.VMEM((N_EXPERTS, 1), F32)],
        compiler_params=_params(("arbitrary",)),
        name="route",
    )(x1, g_ffn, rw_hi, rw_lo, rb_col, upper, lower)


def _tile_rows(ref, row):
    return ref.at[pl.ds(pl.multiple_of(row * TILE_ROWS, TILE_ROWS), TILE_ROWS), :]


def _pack_kernel(q_ref, x1_ref, o_ref):
    o_ref[...] = jnp.zeros(o_ref.shape, o_ref.dtype)

    @pl.when(pl.program_id(0) < pl.num_programs(0) - 1)
    def _():
        def body(t, carry):
            row = _tile_rows(x1_ref, t)[...]
            for k in range(TOP_K):
                _tile_rows(o_ref, q_ref[0, 0, k * ROW_TILE + t])[...] = row
            return carry

        lax.fori_loop(0, ROW_TILE, body, 0, unroll=4)


def _pack(qloc_tiles, x1):
    n_tiles = qloc_tiles.shape[0]
    assert TILE_PACK >= 3 * FFN_BLOCK
    last = n_tiles - 1
    return pl.pallas_call(
        _pack_kernel,
        grid=(n_tiles + 1,),
        in_specs=[pl.BlockSpec((1, 1, TOP_K * ROW_TILE), lambda i: (jnp.minimum(i, last), 0, 0),
                               memory_space=pltpu.SMEM),
                  pl.BlockSpec((ROW_TILE * TILE_ROWS, LANES), lambda i: (jnp.minimum(i, last), 0))],
        out_specs=pl.BlockSpec((TILE_PACK * TILE_ROWS, LANES), lambda i: (i, 0)),
        out_shape=jax.ShapeDtypeStruct(((n_tiles + 1) * TILE_PACK * TILE_ROWS, LANES), F32),
        compiler_params=_params(("arbitrary",)),
        name="pack",
    )(qloc_tiles, x1)


_INV_FIELDS = 3


def _invert_kernel(n_blocks, eidx_ref, prel_ref, qloc_ref, rstart_ref, acc_ref):
    i = pl.program_id(0)

    @pl.when(i == 0)
    def _():
        acc_ref[...] = jnp.zeros(acc_ref.shape, acc_ref.dtype)

    eidx, prel, qloc = eidx_ref[...], prel_ref[...], qloc_ref[...]
    rstart = rstart_ref[...]
    rows = eidx.shape[1]
    eid = lax.broadcasted_iota(jnp.int32, (N_EXPERTS, rows), 0)
    blk_id = lax.broadcasted_iota(jnp.int32, (n_blocks, rows), 0)
    off_id = lax.broadcasted_iota(jnp.int32, (LANES, rows), 0)
    blk_shift = FFN_BLOCK.bit_length() - 1
    chunk_shift = CHUNK_ROWS.bit_length() - 1
    a_parts, b_parts = [], []
    for k in range(TOP_K):
        base = jnp.sum(jnp.where(eid == eidx[k:k + 1], rstart, 0), axis=0, keepdims=True)
        pos = base + prel[k:k + 1]
        leader = (prel[k:k + 1] & (CHUNK_ROWS - 1)) == 0
        in_blk = (blk_id == lax.shift_right_logical(pos, blk_shift)) & leader
        a_parts.append(jnp.where(in_blk, 1.0, 0.0).astype(BF16))
        hit = off_id == (lax.shift_right_logical(pos, chunk_shift) & (CHUNKS_PER_BLOCK - 1))
        cid = lax.shift_right_logical(i * TILE_PACK + qloc[k:k + 1], chunk_shift)
        fields = [(cid & 255).astype(F32), lax.shift_right_logical(cid, 8).astype(F32), jnp.ones((1, rows), F32)]
        b_parts.append([jnp.where(hit, f, 0.0).astype(BF16) for f in fields])
    a = jnp.concatenate(a_parts, axis=1)
    for j in range(_INV_FIELDS):
        b = jnp.concatenate([b_parts[k][j] for k in range(TOP_K)], axis=1)
        acc_ref[j] += _nt_dot(a, b)


def _invert(eidx, prel, qloc, region_start, n_blocks, n_tiles):
    n_all = eidx.shape[1]
    n_chunks = n_tiles * TILE_PACK // CHUNK_ROWS
    assert n_chunks < 256 * 256, "chunk id is carried as two byte-sized fields"
    col = pl.BlockSpec((TOP_K, ROW_TILE), lambda i: (0, i))
    shape = (_INV_FIELDS, n_blocks, LANES)
    f = pl.pallas_call(
        functools.partial(_invert_kernel, n_blocks),
        grid=(n_all // ROW_TILE,),
        in_specs=[col, col, col, _const_spec((N_EXPERTS, 1))],
        out_specs=_resident_out_spec(shape),
        out_shape=jax.ShapeDtypeStruct(shape, F32),
        compiler_params=_params(("arbitrary",)),
        name="invert",
    )(eidx, prel, qloc, region_start)
    cid = (f[0] + 256.0 * f[1]).astype(jnp.int32)
    valid = f[2] > 0.0
    blk = lax.broadcasted_iota(jnp.int32, cid.shape, 0)
    off = lax.broadcasted_iota(jnp.int32, cid.shape, 1)
    dump = n_chunks + (blk & 1) * CHUNKS_PER_BLOCK + jnp.minimum(off, CHUNKS_PER_BLOCK - 1)
    return jnp.where(valid, cid, n_chunks + 3 * CHUNKS_PER_BLOCK), jnp.where(valid, cid, dump)


def _ffn_kernel(be_ref, nused_ref, src_cur_ref, src_nxt_ref, dst_cur_ref, dst_prev_ref, gffn_ref,
                wup_ref, bup_ref, wdn_ref, bdn_ref, xts_hbm, out_hbm, xbuf0, xbuf1, obuf0, obuf1, wup_bf, wdn_bf,
                gsem, ssem):
    i = pl.program_id(0)
    n_used = nused_ref[0]
    xbufs, obufs = (xbuf0, xbuf1), (obuf0, obuf1)
    chunk_len = CHUNK_ROWS * TILE_ROWS

    def hbm_chunk(ref, cid):
        return ref.at[pl.ds(pl.multiple_of(cid * chunk_len, chunk_len), chunk_len), :]

    def vmem_chunk(ref, c):
        return ref.at[pl.ds(c * chunk_len, chunk_len), :]

    def gather_copy(src_ref, c, s):
        return pltpu.make_async_copy(hbm_chunk(xts_hbm, src_ref[0, 0, c]), vmem_chunk(xbufs[s], c), gsem.at[s])

    def scatter_copy(dst_ref, c, s):
        return pltpu.make_async_copy(vmem_chunk(obufs[s], c), hbm_chunk(out_hbm, dst_ref[0, 0, c]), ssem.at[s])

    def start_rows(copy_fn):
        for c in range(CHUNKS_PER_BLOCK):
            copy_fn(c).start(priority=c % 2)

    def wait_rows(copy_fn):
        for c in range(CHUNKS_PER_BLOCK):
            copy_fn(c).wait()

    first = i == 0
    changed = first | (be_ref[i] != be_ref[jnp.maximum(i - 1, 0)])

    @pl.when(first)
    def _():
        obuf0[...] = jnp.zeros(obuf0.shape, obuf0.dtype)
        obuf1[...] = jnp.zeros(obuf1.shape, obuf1.dtype)
        spare0 = out_hbm.shape[0] // chunk_len - TILE_PACK // CHUNK_ROWS
        start_rows(lambda c: pltpu.make_async_copy(
            vmem_chunk(obuf0, c), out_hbm.at[pl.ds((spare0 + c) * chunk_len, chunk_len), :], ssem.at[0]))
        start_rows(lambda c: gather_copy(src_cur_ref, c, 0))

    @pl.when(changed & (i < n_used))
    def _():
        wup_bf[...] = wup_ref[0].astype(BF16)
        wdn_bf[...] = wdn_ref[0].astype(BF16)

    def step(s):
        wait_rows(lambda r: gather_copy(src_cur_ref, r, s))
        x1 = _load_token_tiles(xbufs[s], FFN_BLOCK)
        ms = jnp.mean(x1 * x1, axis=-1, keepdims=True)
        x = (x1 * lax.rsqrt(ms + EPS) * gffn_ref[...]).astype(BF16)
        start_rows(lambda r: gather_copy(src_nxt_ref, r, 1 - s))
        start_rows(lambda r: scatter_copy(dst_prev_ref, r, 1 - s))
        hu = _dot(x, wup_bf[...]) + bup_ref[0]
        glu = jnp.minimum(hu[:, :D_FF], SWIGLU_LIMIT)
        lin = jnp.clip(hu[:, D_FF:], -SWIGLU_LIMIT, SWIGLU_LIMIT)
        act = glu * jax.nn.sigmoid(SWIGLU_ALPHA * glu) * (lin + 1.0)
        out = _dot(act.astype(BF16), wdn_bf[...]) + bdn_ref[0]
        wait_rows(lambda r: scatter_copy(dst_cur_ref, r, s))
        _store_token_tiles(obufs[s], out)

    def drain(s):
        start_rows(lambda r: scatter_copy(dst_cur_ref, r, s))
        wait_rows(lambda r: scatter_copy(dst_cur_ref, r, 1 - s))
        wait_rows(lambda r: scatter_copy(dst_cur_ref, r, s))
        wait_rows(lambda r: gather_copy(src_cur_ref, r, 1 - s))

    for s in range(2):
        pl.when((i < n_used) & (i % 2 == s))(functools.partial(step, s))
    for s in range(2):
        pl.when((i == n_used - 1) & (i % 2 == s))(functools.partial(drain, s))


def _ffn(block_expert, n_used, src, dst, x_ts, g_ffn, w_up, b_up, w_down, b_down):
    n_blocks = block_expert.shape[0]
    smem_block = lambda fn: pl.BlockSpec((1, 1, LANES), fn, memory_space=pltpu.SMEM)
    cur = smem_block(lambda i, be, nu: (i, 0, 0))
    nxt = smem_block(lambda i, be, nu: (jnp.minimum(i + 1, n_blocks - 1), 0, 0))
    prev = smem_block(lambda i, be, nu: (jnp.where(i == 0, n_blocks, i - 1), 0, 0))
    ex3 = lambda i, be, nu: (be[i], 0, 0)
    return pl.pallas_call(
        _ffn_kernel,
        grid_spec=pltpu.PrefetchScalarGridSpec(
            num_scalar_prefetch=2,
            grid=(n_blocks,),
            in_specs=[
                cur, nxt, cur, prev,
                pl.BlockSpec((1, D_MODEL), lambda i, be, nu: (0, 0)),
                pl.BlockSpec((1, D_MODEL, 2 * D_FF), ex3),
                pl.BlockSpec((1, 1, 2 * D_FF), ex3),
                pl.BlockSpec((1, D_FF, D_MODEL), ex3),
                pl.BlockSpec((1, 1, D_MODEL), ex3),
                pl.BlockSpec(memory_space=pl.ANY),
            ],
            out_specs=pl.BlockSpec(memory_space=pl.ANY),
            scratch_shapes=[pltpu.VMEM((FFN_BLOCK * TILE_ROWS, LANES), F32)] * 4 + [
                pltpu.VMEM((D_MODEL, 2 * D_FF), BF16), pltpu.VMEM((D_FF, D_MODEL), BF16),
                pltpu.SemaphoreType.DMA((2,)), pltpu.SemaphoreType.DMA((2,))],
        ),
        out_shape=jax.ShapeDtypeStruct(x_ts.shape, F32),
        input_output_aliases={11: 0},
        compiler_params=_params(("arbitrary",)),
        name="ffn",
    )(block_expert, n_used, src, src, dst, dst, g_ffn, w_up, b_up, w_down, b_down, x_ts)


def _final_kernel(q_ref, gate_ref, x1_ref, yts_ref, g_ref, y_ref, sum_scr):
    def body(t, carry):
        acc = _tile_rows(x1_ref, t)[...]
        for k in range(TOP_K):
            slot = k * ROW_TILE + t
            acc = acc + gate_ref[0, 0, slot] * _tile_rows(yts_ref, q_ref[0, 0, slot])[...]
        _tile_rows(sum_scr, t)[...] = acc
        return carry

    lax.fori_loop(0, ROW_TILE, body, 0, unroll=4)
    x2 = _load_token_tiles(sum_scr, ROW_TILE)
    ms = jnp.mean(x2 * x2, axis=-1, keepdims=True)
    y_ref[...] = x2 * lax.rsqrt(ms + EPS) * g_ref[...]


def _final(qloc_tiles, gate_tiles, x1, y_ts, g_final, row0, n_rows):
    t0 = row0 // ROW_TILE
    smem_tile = pl.BlockSpec((1, 1, TOP_K * ROW_TILE), lambda i: (t0 + i, 0, 0), memory_space=pltpu.SMEM)
    return pl.pallas_call(
        _final_kernel,
        grid=(n_rows // ROW_TILE,),
        in_specs=[smem_tile, smem_tile,
                  pl.BlockSpec((ROW_TILE * TILE_ROWS, LANES), lambda i: (t0 + i, 0)),
                  pl.BlockSpec((TILE_PACK * TILE_ROWS, LANES), lambda i: (t0 + i, 0)),
                  _const_spec((1, D_MODEL))],
        out_specs=pl.BlockSpec((ROW_TILE, D_MODEL), lambda i: (i, 0)),
        out_shape=jax.ShapeDtypeStruct((n_rows, D_MODEL), F32),
        scratch_shapes=[pltpu.VMEM((ROW_TILE * TILE_ROWS, LANES), F32)],
        compiler_params=_params(("arbitrary",)),
        name="final",
    )(qloc_tiles, gate_tiles, x1, y_ts, g_final)


def _rope_tables(seq_len, dec_seq):
    inv = ROPE_THETA ** (-np.arange(0, ROT_DIM, 2, dtype=np.float64) / ROT_DIM)
    pos = np.concatenate([np.arange(seq_len), PAST_LEN + np.arange(ROW_TILE) % dec_seq]).astype(np.float64)
    ang = pos[:, None] * inv[None, :]
    cos, sin = np.cos(ang), np.sin(ang)
    ones = np.ones((pos.shape[0], HEAD_DIM - ROT_DIM))
    cos_h = np.concatenate([cos, cos, ones], axis=1)
    sin_h = np.concatenate([-sin, sin, 0.0 * ones], axis=1)
    return (jnp.asarray(np.concatenate([cos_h, cos_h], axis=1), F32),
            jnp.asarray(np.concatenate([sin_h, sin_h], axis=1), F32))


def _block_tables(counts, n_all):
    max_rows = TOP_K * n_all + (n_all // ROW_TILE) * N_EXPERTS * (CHUNK_ROWS - 1)
    n_blocks = -(-max_rows // FFN_BLOCK) + N_EXPERTS
    padded = (counts + FFN_BLOCK - 1) // FFN_BLOCK * FFN_BLOCK
    pad_end = jnp.cumsum(padded)
    pad_start = (pad_end - padded).astype(jnp.int32).reshape(N_EXPERTS, 1)
    block_start = jnp.arange(n_blocks, dtype=jnp.int32) * FFN_BLOCK
    block_expert = jnp.minimum(jnp.sum(pad_end[None, :] <= block_start[:, None], axis=1), N_EXPERTS - 1)
    n_used = (pad_end[-1] // FFN_BLOCK).astype(jnp.int32).reshape(1)
    return pad_start, block_expert.astype(jnp.int32), n_used, n_blocks


def kernel(x_prompt, x_sample, cache_swa_k, cache_swa_v, state_gla, g_mix, w_in, w_gk_up, b_gk, sinks,
           gla_norm, w_branch_a, w_branch_b, w_out, g_ffn, router_w, router_b, w_up, b_up, w_down, b_down,
           g_final):
    batch, seq_len, _ = x_prompt.shape
    dec_batch, dec_seq, _ = x_sample.shape
    n_p, n_s = batch * seq_len, dec_batch * dec_seq
    n_all = n_p + n_s
    assert w_in.shape[0] == 1, "one layer: the final norm is fused after the only MoE"
    assert seq_len % ROW_TILE == 0 and n_s % ROW_TILE == 0 and ROW_TILE % dec_seq == 0
    assert dec_seq % SUBLANES == 0 and dec_batch % SAMPLE_GROUP == 0
    assert SAMPLE_GROUP * dec_seq == CHUNK and (dec_seq & (dec_seq - 1)) == 0

    xp2 = x_prompt.reshape(n_p, D_MODEL)
    xs2 = x_sample.reshape(n_s, D_MODEL)
    cos_tab, sin_tab = _rope_tables(seq_len, dec_seq)
    w = w_in[0]
    gl0 = _C_RG
    w_all = jnp.concatenate(
        [w[:, :gl0], w[:, gl0 + GK_RANK:], w[:, gl0:gl0 + GK_RANK],
         jnp.zeros((D_MODEL, LANES - GK_RANK), w.dtype)], axis=1).astype(BF16)
    wup_pad = jnp.concatenate([w_gk_up[0], jnp.zeros((LANES - GK_RANK, GLA_KDIM), F32)], axis=0).astype(BF16)
    qa, ka, va, qg, kg, vg, la, rg, ga, gb = _proj(
        xp2, xs2, g_mix[0].reshape(1, D_MODEL), cos_tab, sin_tab, w_all, wup_pad,
        b_gk[0].reshape(1, GLA_KDIM), seq_len)

    oa_p = _swa_prompt(sinks[0], qa, ka, va, batch, seq_len)
    oa_s, nk_s, nv_s = _swa_sample(sinks[0], qa, ka, va, cache_swa_k[0], cache_swa_v[0], n_p,
                                   dec_batch, dec_seq)
    norm_row = jnp.tile(gla_norm[0], GLA_HEADS).reshape(1, GLA_VDIM)
    og_p, s_fin = _gla_prompt(qg, kg, la, vg, rg, norm_row, batch, seq_len)
    og_s, s_new = _gla_sample(qg, kg, la, vg, rg, norm_row, state_gla[0], n_p, dec_batch, dec_seq)

    x1 = _merge(xp2, xs2, oa_p, oa_s, og_p, og_s, ga, gb, w_branch_a[0].astype(BF16),
                w_branch_b[0].astype(BF16), w_out[0].astype(BF16))

    g_ffn_row = g_ffn[0].reshape(1, D_MODEL)
    rw_t = router_w[0].T
    rw_hi = rw_t.astype(BF16)
    rw_lo = (rw_t - rw_hi.astype(F32)).astype(BF16)
    eidx, prel, qloc, gate, cnt = _route(x1, g_ffn_row, rw_hi, rw_lo, router_b[0].reshape(N_EXPERTS, 1))
    n_tiles = n_all // ROW_TILE
    region_start, block_expert, n_used, n_blocks = _block_tables(cnt[:, 0].astype(jnp.int32), n_all)
    per_tile = lambda a: a.reshape(TOP_K, n_tiles, ROW_TILE).transpose(1, 0, 2).reshape(n_tiles, 1, TOP_K * ROW_TILE)
    qloc_tiles, gate_tiles = per_tile(qloc), per_tile(gate)
    x_ts = _pack(qloc_tiles, x1)
    src, dst = _invert(eidx, prel, qloc, region_start, n_blocks, n_tiles)
    prime = (n_tiles * TILE_PACK // CHUNK_ROWS + 2 * CHUNKS_PER_BLOCK
             + jnp.minimum(jnp.arange(LANES, dtype=jnp.int32), CHUNKS_PER_BLOCK - 1)).reshape(1, LANES)
    dst = jnp.concatenate([dst, prime], axis=0)
    y_ts = _ffn(block_expert, n_used, src.reshape(n_blocks, 1, LANES), dst.reshape(n_blocks + 1, 1, LANES),
                x_ts, g_ffn_row, w_up[0], b_up[0].reshape(N_EXPERTS, 1, 2 * D_FF), w_down[0],
                b_down[0].reshape(N_EXPERTS, 1, D_MODEL))

    g_out = g_final.reshape(1, D_MODEL)
    y_p = _final(qloc_tiles, gate_tiles, x1, y_ts, g_out, 0, n_p)
    y_s = _final(qloc_tiles, gate_tiles, x1, y_ts, g_out, n_p, n_s)

    kv_shape = (1, -1, WINDOW, KV_HEADS, HEAD_DIM)
    new_k_p = ka[:n_p].reshape(batch, seq_len, KV_WIDTH)[:, -WINDOW:].reshape(kv_shape)
    new_v_p = va[:n_p].reshape(batch, seq_len, KV_WIDTH)[:, -WINDOW:].reshape(kv_shape)
    return (y_p.reshape(batch, seq_len, D_MODEL), y_s.reshape(dec_batch, dec_seq, D_MODEL),
            new_k_p, new_v_p, s_fin[None], nk_s.reshape(kv_shape), nv_s.reshape(kv_shape), s_new[None])
```

```python
import functools

import numpy as np
import jax
import jax.numpy as jnp
from jax import lax
from jax.experimental import pallas as pl
from jax.experimental.pallas import tpu as pltpu

D_MODEL = 1024
PAST_LEN = 8192
HEAD_DIM = 64
N_HEADS = 8
KV_HEADS = 2
GROUP = N_HEADS // KV_HEADS
WINDOW = 128
ROT_DIM = HEAD_DIM // 4
ROPE_THETA = 500000.0
ATT_WIDTH = N_HEADS * HEAD_DIM
KV_WIDTH = KV_HEADS * HEAD_DIM
GLA_HEADS = 4
GLA_KDIM = D_MODEL // 2
GLA_VDIM = D_MODEL
GLA_DK = GLA_KDIM // GLA_HEADS
GLA_DV = GLA_VDIM // GLA_HEADS
GK_RANK = 16
GK_NORMALIZER = 16.0
N_EXPERTS = 32
TOP_K = 4
D_FF = D_MODEL
SWIGLU_LIMIT = 7.0
SWIGLU_ALPHA = 1.702
EPS = 1e-5
NEG_INF = -1e30

LANES = 128
SUBLANES = 8
VMEM_LIMIT_BYTES = 56 * 1024 * 1024

ROW_TILE = 512
CHUNK = 128
FFN_BLOCK = 256

BF16 = jnp.bfloat16
F32 = jnp.float32

_C_QA, _C_KA, _C_VA, _C_QG, _C_KG, _C_VG, _C_RG, _C_GA, _C_GB, _C_GL, _C_END = (
    0, 512, 640, 768, 1280, 1792, 2816, 3840, 4864, 5888, 6016)


def _const_spec(shape):
    nd = len(shape)
    return pl.BlockSpec(shape, lambda *_: (0,) * nd, pipeline_mode=pl.Buffered(1))


def _resident_out_spec(shape):
    nd = len(shape)
    return pl.BlockSpec(shape, lambda *_: (0,) * nd)


def _params(sem, vmem=VMEM_LIMIT_BYTES):
    return pltpu.CompilerParams(dimension_semantics=sem, vmem_limit_bytes=vmem)


def _nt_dot(a, b):
    return lax.dot_general(a, b, (((1,), (1,)), ((), ())), preferred_element_type=F32)


def _dot(a, b):
    return jnp.dot(a, b, preferred_element_type=F32)


TILE_ROWS = D_MODEL // LANES
assert TILE_ROWS == SUBLANES


def _load_token_tiles(ref, n_tokens):
    return jnp.concatenate([ref[pl.ds(c, n_tokens, stride=TILE_ROWS), :] for c in range(TILE_ROWS)], axis=1)


def _store_token_tiles(ref, x):
    for c in range(TILE_ROWS):
        ref[pl.ds(c, x.shape[0], stride=TILE_ROWS), :] = x[:, c * LANES:(c + 1) * LANES]


def _rope(x, cos_t, sin_t, n_rep):
    width = x.shape[1]
    cos_f = jnp.concatenate([cos_t] * n_rep, axis=1) if n_rep > 1 else cos_t
    sin_f = jnp.concatenate([sin_t] * n_rep, axis=1) if n_rep > 1 else sin_t
    lane = lax.broadcasted_iota(jnp.int32, x.shape, 1) % HEAD_DIM
    up = pltpu.roll(x, width - ROT_DIM // 2, 1)
    down = pltpu.roll(x, ROT_DIM // 2, 1)
    partner = jnp.where(lane < ROT_DIM // 2, up, down)
    return x * cos_f + partner * sin_f


def _proj_kernel(n_prompt_tiles, xp_ref, xs_ref, g_ref, cos_ref, sin_ref, w_ref, wup_ref, bgk_ref,
                 qa_ref, ka_ref, va_ref, qg_ref, kg_ref, vg_ref, la_ref, rg_ref, ga_ref, gb_ref):
    i = pl.program_id(0)
    x = jnp.where(i < n_prompt_tiles, xp_ref[...], xs_ref[...])
    ms = jnp.mean(x * x, axis=-1, keepdims=True)
    h = (x * lax.rsqrt(ms + EPS) * g_ref[...]).astype(BF16)
    cos_t = cos_ref[...]
    sin_t = sin_ref[...]

    def seg(a, b):
        return _dot(h, w_ref[:, a:b])

    qa = _rope(seg(_C_QA, _C_KA), cos_t, sin_t, ATT_WIDTH // LANES)
    qa_ref[...] = (qa * (HEAD_DIM ** -0.5)).astype(BF16)
    ka_ref[...] = _rope(seg(_C_KA, _C_VA), cos_t, sin_t, 1)
    va_ref[...] = seg(_C_VA, _C_QG)
    qg_ref[...] = seg(_C_QG, _C_KG) * (GLA_DK ** -0.5)
    kg_ref[...] = seg(_C_KG, _C_VG)
    vg_ref[...] = seg(_C_VG, _C_RG).astype(BF16)
    rg_ref[...] = seg(_C_RG, _C_GA).astype(BF16)
    ga_ref[...] = seg(_C_GA, _C_GB).astype(BF16)
    gb_ref[...] = seg(_C_GB, _C_GL).astype(BF16)
    gk_low = seg(_C_GL, _C_END).astype(BF16)
    z = _dot(gk_low, wup_ref[...]) + bgk_ref[...]
    log_sig = jnp.minimum(z, 0.0) - jnp.log1p(jnp.exp(-jnp.abs(z)))
    la_ref[...] = log_sig / GK_NORMALIZER


def _proj(xp2, xs2, g_mix, cos_tab, sin_tab, w_all, wup_pad, b_gk, seq_len):
    n_p, n_s = xp2.shape[0], xs2.shape[0]
    n_all = n_p + n_s
    npt, nst = n_p // ROW_TILE, n_s // ROW_TILE
    tiles_per_seq = seq_len // ROW_TILE

    def tab_map(i):
        return (jnp.where(i < npt, i % tiles_per_seq, tiles_per_seq), 0)

    row = lambda w: pl.BlockSpec((ROW_TILE, w), lambda i: (i, 0))
    widths = [(ATT_WIDTH, BF16), (KV_WIDTH, F32), (KV_WIDTH, F32), (GLA_KDIM, F32), (GLA_KDIM, F32),
              (GLA_VDIM, BF16), (GLA_KDIM, F32), (GLA_VDIM, BF16), (D_MODEL, BF16), (D_MODEL, BF16)]
    return pl.pallas_call(
        functools.partial(_proj_kernel, npt),
        grid=(npt + nst,),
        in_specs=[
            pl.BlockSpec((ROW_TILE, D_MODEL), lambda i: (jnp.minimum(i, npt - 1), 0)),
            pl.BlockSpec((ROW_TILE, D_MODEL), lambda i: (jnp.maximum(i - npt, 0), 0)),
            _const_spec((1, D_MODEL)),
            pl.BlockSpec((ROW_TILE, LANES), tab_map),
            pl.BlockSpec((ROW_TILE, LANES), tab_map),
            _const_spec(w_all.shape),
            _const_spec(wup_pad.shape),
            _const_spec((1, GLA_KDIM)),
        ],
        out_specs=[row(w) for w, _ in widths],
        out_shape=[jax.ShapeDtypeStruct((n_all, w), dt) for w, dt in widths],
        compiler_params=_params(("arbitrary",)),
        name="proj",
    )(xp2, xs2, g_mix, cos_tab, sin_tab, w_all, wup_pad, b_gk)


def _pair_blocks(kk):
    lane = lax.broadcasted_iota(jnp.int32, kk.shape, 1)
    lo = lane < HEAD_DIM
    swapped = pltpu.roll(kk, HEAD_DIM, 1)
    zero = jnp.zeros_like(kk)
    blocks = []
    for kh in range(KV_HEADS):
        left = jnp.where(lo, kk if kh == 0 else swapped, zero)
        right = jnp.where(lo, zero, swapped if kh == 0 else kk)
        blocks.append(jnp.concatenate([left, right], axis=0).astype(BF16))
    return blocks


def _sink_softmax(s, valid, sink):
    s = jnp.where(valid, s, NEG_INF)
    m = jnp.maximum(jnp.max(s, axis=-1, keepdims=True), sink)
    p = jnp.exp(s - m)
    denom = jnp.sum(p, axis=-1, keepdims=True) + jnp.exp(sink - m)
    return (p * (1.0 / denom)).astype(BF16)


def _attend(q, kk, vv, valid, sink_ref, o_ref, row0=0):
    rows, keys = valid.shape
    kblocks = _pair_blocks(kk)
    vblocks = _pair_blocks(vv)
    for kh in range(KV_HEADS):
        base = kh * GROUP * HEAD_DIM
        qq = jnp.concatenate([q[:, base:base + LANES], q[:, base + LANES:base + 2 * LANES]], axis=0)
        s = _nt_dot(qq, kblocks[kh])
        for r in range(2):
            probs = []
            for c in range(2):
                head = kh * GROUP + 2 * r + c
                probs.append(_sink_softmax(s[r * rows:(r + 1) * rows, c * keys:(c + 1) * keys],
                                           valid, sink_ref[head]))
            p = jnp.concatenate(probs, axis=1)
            o_ref[pl.ds(row0, rows), base + r * LANES:base + (r + 1) * LANES] = (
                _dot(p, vblocks[kh]).astype(o_ref.dtype))


def _swa_prompt_kernel(sink_ref, q_ref, kp_ref, k0_ref, k1_ref, vp_ref, v0_ref, v1_ref, o_ref):
    j = pl.program_id(1)
    row = lax.broadcasted_iota(jnp.int32, (WINDOW, 2 * WINDOW), 0)
    col = lax.broadcasted_iota(jnp.int32, (WINDOW, 2 * WINDOW), 1)
    band = (col > row) & (col <= row + WINDOW)
    k_blocks = (kp_ref[...], k0_ref[...], k1_ref[...])
    v_blocks = (vp_ref[...], v0_ref[...], v1_ref[...])
    q = q_ref[...]
    for half in range(2):
        kk = jnp.concatenate(k_blocks[half:half + 2], axis=0)
        vv = jnp.concatenate(v_blocks[half:half + 2], axis=0)
        valid = band & ((j > 0) | (col >= WINDOW)) if half == 0 else band
        _attend(q[half * WINDOW:(half + 1) * WINDOW], kk, vv, valid, sink_ref, o_ref, half * WINDOW)


def _swa_prompt(sinks, qa, ka, va, batch, seq_len):
    nb = seq_len // WINDOW
    assert nb % 2 == 0
    kv = lambda off: pl.BlockSpec((WINDOW, KV_WIDTH), lambda b, j, s: (b * nb + jnp.maximum(2 * j + off, 0), 0))
    pair = pl.BlockSpec((2 * WINDOW, ATT_WIDTH), lambda b, j, s: (b * (nb // 2) + j, 0))
    return pl.pallas_call(
        _swa_prompt_kernel,
        grid_spec=pltpu.PrefetchScalarGridSpec(
            num_scalar_prefetch=1,
            grid=(batch, nb // 2),
            in_specs=[pair, kv(-1), kv(0), kv(1), kv(-1), kv(0), kv(1)],
            out_specs=pair,
        ),
        out_shape=jax.ShapeDtypeStruct((batch * seq_len, ATT_WIDTH), BF16),
        compiler_params=_params(("arbitrary", "arbitrary")),
        name="swa_prompt",
    )(sinks, qa, ka, ka, ka, va, va, va)


SAMPLE_GROUP = 16


def _swa_sample_kernel(dec_seq, sink_ref, q_ref, kn_ref, vn_ref, ck_ref, cv_ref, o_ref, nk_ref, nv_ref):
    rows = SAMPLE_GROUP * dec_seq
    ck = ck_ref[...]
    cv = cv_ref[...]
    kn = kn_ref[...]
    vn = vn_ref[...]
    nk_ref[:, :WINDOW - dec_seq, :] = ck[:, dec_seq:, :]
    nv_ref[:, :WINDOW - dec_seq, :] = cv[:, dec_seq:, :]
    nk_ref[:, WINDOW - dec_seq:, :] = kn.reshape(SAMPLE_GROUP, dec_seq, KV_WIDTH)
    nv_ref[:, WINDOW - dec_seq:, :] = vn.reshape(SAMPLE_GROUP, dec_seq, KV_WIDTH)
    n_cache = SAMPLE_GROUP * WINDOW
    kk = jnp.concatenate([ck.reshape(n_cache, KV_WIDTH), kn], axis=0)
    vv = jnp.concatenate([cv.reshape(n_cache, KV_WIDTH), vn], axis=0)
    keys = n_cache + rows
    row = lax.broadcasted_iota(jnp.int32, (rows, keys), 0)
    col = lax.broadcasted_iota(jnp.int32, (rows, keys), 1)
    q_b, q_s = row // dec_seq, row % dec_seq
    is_cache = col < n_cache
    new = col - n_cache
    valid_cache = (col // WINDOW == q_b) & (col % WINDOW > q_s)
    valid_new = (new // dec_seq == q_b) & (new % dec_seq <= q_s)
    valid = (is_cache & valid_cache) | (jnp.logical_not(is_cache) & valid_new)
    _attend(q_ref[...], kk, vv, valid, sink_ref, o_ref)


def _swa_sample(sinks, qa, ka, va, cache_k, cache_v, n_prompt_rows, dec_batch, dec_seq):
    rows = SAMPLE_GROUP * dec_seq
    off = n_prompt_rows // rows
    tok = lambda g, s: (off + g, 0)
    cache = lambda g, s: (g, 0, 0)
    cshape = (dec_batch, WINDOW, KV_WIDTH)
    return pl.pallas_call(
        functools.partial(_swa_sample_kernel, dec_seq),
        grid_spec=pltpu.PrefetchScalarGridSpec(
            num_scalar_prefetch=1,
            grid=(dec_batch // SAMPLE_GROUP,),
            in_specs=[
                pl.BlockSpec((rows, ATT_WIDTH), tok),
                pl.BlockSpec((rows, KV_WIDTH), tok),
                pl.BlockSpec((rows, KV_WIDTH), tok),
                pl.BlockSpec((SAMPLE_GROUP, WINDOW, KV_WIDTH), cache),
                pl.BlockSpec((SAMPLE_GROUP, WINDOW, KV_WIDTH), cache),
            ],
            out_specs=[
                pl.BlockSpec((rows, ATT_WIDTH), lambda g, s: (g, 0)),
                pl.BlockSpec((SAMPLE_GROUP, WINDOW, KV_WIDTH), cache),
                pl.BlockSpec((SAMPLE_GROUP, WINDOW, KV_WIDTH), cache),
            ],
        ),
        out_shape=[jax.ShapeDtypeStruct((dec_batch * dec_seq, ATT_WIDTH), BF16),
                   jax.ShapeDtypeStruct(cshape, F32), jax.ShapeDtypeStruct(cshape, F32)],
        compiler_params=_params(("arbitrary",)),
        name="swa_sample",
    )(sinks, qa, ka, va, cache_k.reshape(cshape), cache_v.reshape(cshape))


def _chunk_tables(seg):
    n_lev = int(np.log2(seg))
    t = np.arange(CHUNK)
    seg_start = (t // seg) * seg
    u = np.arange(CHUNK)[None, :]

    def prefix(end):
        return ((u >= seg_start[:, None]) & (u <= end[:, None])).astype(np.float32)

    blocks = [prefix(t)]
    for d in range(min(n_lev, _MATMUL_LEVELS)):
        m = 1 << d
        ref = (t >> (d + 1) << (d + 1)) + m - 1
        blocks.append(prefix(ref))
    lhs = np.concatenate(blocks, axis=0)
    lhs2 = np.concatenate([lhs, lhs], axis=1)
    tt, ss = t[:, None], t[None, :]
    x = tt ^ ss
    lev = np.where(x > 0, np.floor(np.log2(np.maximum(x, 1))).astype(np.int32), n_lev)
    lev = np.where((ss > tt) | (tt // seg != ss // seg), -1, lev)
    lev = np.where(tt == ss, n_lev, lev)
    return jnp.asarray(lhs2, BF16), jnp.asarray(lev, jnp.int32), n_lev


_MATMUL_LEVELS = 3


def _group_row(x, group, row):
    width = x.shape[1]
    parts = [jnp.broadcast_to(x[g * group + row:g * group + row + 1, :], (group, width))
             for g in range(x.shape[0] // group)]
    return parts[0] if len(parts) == 1 else jnp.concatenate(parts, axis=0)


_HALF_ROW_LEVELS = 4
LOG2_E = 1.4426950408889634


def _halves(x, m, which):
    parts = [x[(2 * g + which) * m:(2 * g + which + 1) * m] for g in range(x.shape[0] // (2 * m))]
    return parts[0] if len(parts) == 1 else jnp.concatenate(parts, axis=0)


def _unhalve(xh, m, which):
    zero = jnp.zeros((m, xh.shape[1]), xh.dtype)
    parts = []
    for g in range(xh.shape[0] // m):
        blk = xh[g * m:(g + 1) * m]
        parts += [zero, blk] if which else [blk, zero]
    return jnp.concatenate(parts, axis=0)


def _gla_chunk_terms(q, k, la, lhs2, level, n_lev, seg):
    la2 = la * LOG2_E
    hi = la2.astype(BF16)
    lo = (la2 - hi.astype(F32)).astype(BF16)
    sums = _dot(lhs2, jnp.concatenate([hi, lo], axis=0))
    b = sums[0:CHUNK]
    b_last = _group_row(b, seg, seg - 1)
    q_main = (q * jnp.exp2(b)).astype(BF16)
    k_upd = k * jnp.exp2(b_last - b)
    q_lev = [None] * n_lev
    k_lev = [None] * n_lev
    for d in range(n_lev):
        m = 1 << d
        if d < _MATMUL_LEVELS:
            ref = sums[(1 + d) * CHUNK:(2 + d) * CHUNK]
        elif d < _HALF_ROW_LEVELS:
            ref = _group_row(b, 2 * m, m - 1)
        if d < _HALF_ROW_LEVELS:
            q_lev[d] = (q * jnp.exp2(b - ref)).astype(BF16)
            k_lev[d] = (k * jnp.exp2(ref - b)).astype(BF16)
        else:
            ref_h = _group_row(_halves(b, m, 0), m, m - 1)
            q_lev[d] = _unhalve((_halves(q, m, 1) * jnp.exp2(_halves(b, m, 1) - ref_h)).astype(BF16), m, 1)
            k_lev[d] = _unhalve((_halves(k, m, 0) * jnp.exp2(ref_h - _halves(b, m, 0))).astype(BF16), m, 0)
    q_b, k_b = q.astype(BF16), k.astype(BF16)

    def att(h):
        hs = slice(h * GLA_DK, (h + 1) * GLA_DK)
        acc = jnp.where(level == n_lev, _nt_dot(q_b[:, hs], k_b[:, hs]), 0.0)
        for d in range(n_lev):
            acc = jnp.where(level == d, _nt_dot(q_lev[d][:, hs], k_lev[d][:, hs]), acc)
        return acc

    return q_main, k_upd, att, b_last


def _gla_out(o, r, norm):
    parts = []
    for h in range(GLA_HEADS):
        oh = o[:, h * GLA_DV:(h + 1) * GLA_DV]
        ms = jnp.mean(oh * oh, axis=-1, keepdims=True)
        parts.append(oh * lax.rsqrt(ms + EPS))
    y = jnp.concatenate(parts, axis=1) * norm
    rf = r.astype(F32)
    return y * (rf * jax.nn.sigmoid(rf))


def _gla_prompt_kernel(n_lev, n_par, *refs):
    seq_refs = [refs[5 * j:5 * j + 5] for j in range(n_par)]
    norm_ref, lhs_ref, lev_ref = refs[5 * n_par:5 * n_par + 3]
    o_ref, sfin_ref, s_scr = refs[5 * n_par + 3:]
    c = pl.program_id(1)

    @pl.when(c == 0)
    def _():
        s_scr[...] = jnp.zeros_like(s_scr)

    for j, (q_ref, k_ref, la_ref, v_ref, r_ref) in enumerate(seq_refs):
        q_main, k_upd, att, b_last = _gla_chunk_terms(q_ref[...], k_ref[...], la_ref[...], lhs_ref[...],
                                                      lev_ref[...], n_lev, CHUNK)
        v = v_ref[...]
        outs = []
        for h in range(GLA_HEADS):
            hs = slice(h * GLA_DK, (h + 1) * GLA_DK)
            vh = v[:, h * GLA_DV:(h + 1) * GLA_DV]
            s0 = s_scr[j, h]
            o_h = _dot(q_main[:, hs], s0.astype(BF16)) + _dot(att(h).astype(BF16), vh)
            outs.append(o_h)
            decay = jnp.exp2(b_last[:, hs]).T
            k_t = k_upd[:, hs].T.astype(BF16)
            s_scr[j, h] = jnp.concatenate([decay, decay], axis=1) * s0 + _dot(k_t, vh)
        o_ref[j] = _gla_out(jnp.concatenate(outs, axis=1), r_ref[...], norm_ref[...]).astype(BF16)

    @pl.when(c == pl.num_programs(1) - 1)
    def _():
        sfin_ref[...] = s_scr[...]


def _gla_prompt(qg, kg, la, vg, rg, norm_row, batch, seq_len):
    nc = seq_len // CHUNK
    n_par = 4 if batch % 4 == 0 else (2 if batch % 2 == 0 else 1)
    lhs2, level, n_lev = _chunk_tables(CHUNK)
    tok = lambda j, w: pl.BlockSpec((CHUNK, w), lambda b, c: ((b * n_par + j) * nc + c, 0))
    seq_specs, seq_args = [], []
    for j in range(n_par):
        seq_specs += [tok(j, GLA_KDIM), tok(j, GLA_KDIM), tok(j, GLA_KDIM), tok(j, GLA_VDIM), tok(j, GLA_VDIM)]
        seq_args += [qg, kg, la, vg, rg]
    og, s_fin = pl.pallas_call(
        functools.partial(_gla_prompt_kernel, n_lev, n_par),
        grid=(batch // n_par, nc),
        in_specs=seq_specs + [_const_spec((1, GLA_VDIM)), _const_spec(lhs2.shape), _const_spec(level.shape)],
        out_specs=[pl.BlockSpec((n_par, CHUNK, GLA_VDIM), lambda b, c: (b, c, 0)),
                   pl.BlockSpec((n_par, GLA_HEADS, GLA_DK, GLA_DV), lambda b, c: (b, 0, 0, 0))],
        out_shape=[jax.ShapeDtypeStruct((batch, seq_len, GLA_VDIM), BF16),
                   jax.ShapeDtypeStruct((batch, GLA_HEADS, GLA_DK, GLA_DV), F32)],
        scratch_shapes=[pltpu.VMEM((n_par, GLA_HEADS, GLA_DK, GLA_DV), F32)],
        compiler_params=_params(("arbitrary", "arbitrary")),
        name="gla_prompt",
    )(*seq_args, norm_row, lhs2, level)
    return og.reshape(batch * seq_len, GLA_VDIM), s_fin


def _gla_sample_kernel(n_lev, dec_seq, q_ref, k_ref, la_ref, v_ref, r_ref, norm_ref, lhs_ref, lev_ref,
                       s0_ref, o_ref, snew_ref):
    q_main, k_upd, att, b_last = _gla_chunk_terms(q_ref[...], k_ref[...], la_ref[...], lhs_ref[...],
                                                  lev_ref[...], n_lev, dec_seq)
    v = v_ref[...]
    n_b = CHUNK // dec_seq
    row_b = lax.broadcasted_iota(jnp.int32, (CHUNK, GLA_DK), 0) // dec_seq
    col_b = lax.broadcasted_iota(jnp.int32, (GLA_DK, CHUNK), 1) // dec_seq
    outs = []
    for h in range(GLA_HEADS):
        hs = slice(h * GLA_DK, (h + 1) * GLA_DK)
        vh = v[:, h * GLA_DV:(h + 1) * GLA_DV]
        qm = q_main[:, hs]
        decay_t = jnp.exp2(b_last[:, hs]).T
        k_t = k_upd[:, hs].T.astype(BF16)
        o_h = _dot(att(h).astype(BF16), vh)
        for bi in range(n_b):
            s0 = s0_ref[bi, h]
            o_h = o_h + _dot(jnp.where(row_b == bi, qm, jnp.zeros_like(qm)), s0.astype(BF16))
            decay = jnp.broadcast_to(decay_t[:, bi * dec_seq:bi * dec_seq + 1], (GLA_DK, GLA_DV))
            k_b = jnp.where(col_b == bi, k_t, jnp.zeros_like(k_t))
            snew_ref[bi, h] = decay * s0 + _dot(k_b, vh)
        outs.append(o_h)
    o_ref[...] = _gla_out(jnp.concatenate(outs, axis=1), r_ref[...], norm_ref[...]).astype(o_ref.dtype)


def _gla_sample(qg, kg, la, vg, rg, norm_row, state, n_prompt_rows, dec_batch, dec_seq):
    n_b = CHUNK // dec_seq
    off = n_prompt_rows // CHUNK
    lhs3, level, n_lev = _chunk_tables(dec_seq)
    tok = lambda w: pl.BlockSpec((CHUNK, w), lambda g: (off + g, 0))
    st = pl.BlockSpec((n_b, GLA_HEADS, GLA_DK, GLA_DV), lambda g: (g, 0, 0, 0))
    return pl.pallas_call(
        functools.partial(_gla_sample_kernel, n_lev, dec_seq),
        grid=(dec_batch // n_b,),
        in_specs=[tok(GLA_KDIM), tok(GLA_KDIM), tok(GLA_KDIM), tok(GLA_VDIM), tok(GLA_VDIM),
                  _const_spec((1, GLA_VDIM)), _const_spec(lhs3.shape), _const_spec(level.shape), st],
        out_specs=[pl.BlockSpec((CHUNK, GLA_VDIM), lambda g: (g, 0)), st],
        out_shape=[jax.ShapeDtypeStruct((dec_batch * dec_seq, GLA_VDIM), BF16),
                   jax.ShapeDtypeStruct(state.shape, F32)],
        compiler_params=_params(("arbitrary",)),
        name="gla_sample",
    )(qg, kg, la, vg, rg, norm_row, lhs3, level, state)


def _merge_kernel(n_prompt_tiles, xp_ref, xs_ref, oap_ref, oas_ref, ogp_ref, ogs_ref, ga_ref, gb_ref,
                  wa_ref, wb_ref, wo_ref, x1_ref):
    i = pl.program_id(0)
    is_p = i < n_prompt_tiles
    x = jnp.where(is_p, xp_ref[...], xs_ref[...])
    oa = jnp.where(is_p, oap_ref[...], oas_ref[...])
    og = jnp.where(is_p, ogp_ref[...], ogs_ref[...])
    m = (jax.nn.sigmoid(ga_ref[...].astype(F32)) * _dot(oa, wa_ref[...])
         + jax.nn.sigmoid(gb_ref[...].astype(F32)) * _dot(og, wb_ref[...]))
    _store_token_tiles(x1_ref, x + _dot(m.astype(BF16), wo_ref[...]))


def _merge(xp2, xs2, oa_p, oa_s, og_p, og_s, ga, gb, wa, wb, wo):
    n_p, n_s = xp2.shape[0], xs2.shape[0]
    npt, nst = n_p // ROW_TILE, n_s // ROW_TILE
    p_map = lambda i: (jnp.minimum(i, npt - 1), 0)
    s_map = lambda i: (jnp.maximum(i - npt, 0), 0)
    row = lambda w: pl.BlockSpec((ROW_TILE, w), lambda i: (i, 0))
    return pl.pallas_call(
        functools.partial(_merge_kernel, npt),
        grid=(npt + nst,),
        in_specs=[
            pl.BlockSpec((ROW_TILE, D_MODEL), p_map), pl.BlockSpec((ROW_TILE, D_MODEL), s_map),
            pl.BlockSpec((ROW_TILE, ATT_WIDTH), p_map), pl.BlockSpec((ROW_TILE, ATT_WIDTH), s_map),
            pl.BlockSpec((ROW_TILE, GLA_VDIM), p_map), pl.BlockSpec((ROW_TILE, GLA_VDIM), s_map),
            row(D_MODEL), row(D_MODEL),
            _const_spec(wa.shape), _const_spec(wb.shape), _const_spec(wo.shape),
        ],
        out_specs=pl.BlockSpec((ROW_TILE * TILE_ROWS, LANES), lambda i: (i, 0)),
        out_shape=jax.ShapeDtypeStruct(((n_p + n_s) * TILE_ROWS, LANES), F32),
        compiler_params=_params(("arbitrary",)),
        name="merge",
    )(xp2, xs2, oa_p, oa_s, og_p, og_s, ga, gb, wa, wb, wo)


CHUNK_ROWS = 8
CHUNKS_PER_BLOCK = FFN_BLOCK // CHUNK_ROWS
TILE_PACK = TOP_K * ROW_TILE + N_EXPERTS * (CHUNK_ROWS - 1)
TILE_PACK += -TILE_PACK % CHUNK_ROWS


def _route_kernel(x1_ref, gffn_ref, rwh_ref, rwl_ref, rb_ref, upper_ref, lower_ref, eidx_ref, prel_ref, qloc_ref,
                  gate_ref, cnt_ref, cnt_scr):
    i = pl.program_id(0)

    @pl.when(i == 0)
    def _():
        cnt_scr[...] = jnp.zeros_like(cnt_scr)

    x1 = _load_token_tiles(x1_ref, ROW_TILE)
    ms = jnp.mean(x1 * x1, axis=-1, keepdims=True)
    h2 = x1 * lax.rsqrt(ms + EPS) * gffn_ref[...]
    h_hi = h2.astype(BF16)
    h_lo = (h2 - h_hi.astype(F32)).astype(BF16)
    rwh, rwl = rwh_ref[...], rwl_ref[...]
    logits = _nt_dot(rwh, h_hi) + _nt_dot(rwl, h_hi) + _nt_dot(rwh, h_lo) + rb_ref[...]
    eid = lax.broadcasted_iota(jnp.int32, logits.shape, 0)
    upper = upper_ref[...]
    seen = cnt_scr[...]
    vals, rows_e, hots, befores, counts = [], [], [], [], []
    lg = logits
    for _ in range(TOP_K):
        mx = jnp.max(lg, axis=0, keepdims=True)
        sel = jnp.min(jnp.where(lg == mx, eid, N_EXPERTS), axis=0, keepdims=True)
        onehot = eid == sel
        oh = onehot.astype(F32)
        hots.append(oh)
        befores.append(_dot(onehot.astype(BF16), upper))
        counts.append(jnp.sum(oh, axis=1, keepdims=True))
        vals.append(mx)
        rows_e.append(sel)
        lg = jnp.where(onehot, -jnp.inf, lg)
    run = (counts[0] + counts[1]) + (counts[2] + counts[3])
    run_pad = jnp.floor((run + (CHUNK_ROWS - 1)) * (1.0 / CHUNK_ROWS)) * CHUNK_ROWS
    tile_off = _dot(lower_ref[...], jnp.broadcast_to(run_pad, (N_EXPERTS, LANES)).astype(BF16))[:, 0:1]
    rows_p, rows_q = [], []
    ahead = jnp.zeros_like(run)
    for k in range(TOP_K):
        local = ahead + befores[k]
        rows_p.append(jnp.sum(hots[k] * (seen + local), axis=0, keepdims=True))
        rows_q.append(jnp.sum(hots[k] * (tile_off + local), axis=0, keepdims=True))
        ahead = ahead + counts[k]
    cnt_scr[...] = seen + run_pad
    ex = [jnp.exp(v - vals[0]) for v in vals]
    inv = 1.0 / (ex[0] + ex[1] + ex[2] + ex[3])
    eidx_ref[...] = jnp.concatenate(rows_e, axis=0)
    prel_ref[...] = jnp.concatenate(rows_p, axis=0).astype(jnp.int32)
    qloc_ref[...] = jnp.concatenate(rows_q, axis=0).astype(jnp.int32)
    gate_ref[...] = jnp.concatenate([e * inv for e in ex], axis=0)
    cnt_ref[...] = jnp.broadcast_to(seen + run_pad, cnt_ref.shape)


def _route(x1, g_ffn, rw_hi, rw_lo, rb_col):
    n = x1.shape[0] // TILE_ROWS
    upper = jnp.asarray(np.triu(np.ones((ROW_TILE, ROW_TILE), np.float32), 1), BF16)
    lower = jnp.asarray(np.tril(np.ones((N_EXPERTS, N_EXPERTS), np.float32), -1), BF16)
    col = lambda dt: (pl.BlockSpec((TOP_K, ROW_TILE), lambda i: (0, i)), jax.ShapeDtypeStruct((TOP_K, n), dt))
    outs = [col(jnp.int32), col(jnp.int32), col(jnp.int32), col(F32),
            (_resident_out_spec((N_EXPERTS, LANES)), jax.ShapeDtypeStruct((N_EXPERTS, LANES), F32))]
    return pl.pallas_call(
        _route_kernel,
        grid=(n // ROW_TILE,),
        in_specs=[pl.BlockSpec((ROW_TILE * TILE_ROWS, LANES), lambda i: (i, 0)), _const_spec((1, D_MODEL)),
                  _const_spec(rw_hi.shape), _const_spec(rw_lo.shape), _const_spec((N_EXPERTS, 1)),
                  _const_spec(upper.shape), _const_spec(lower.shape)],
        out_specs=[o[0] for o in outs],
        out_shape=[o[1] for o in outs],
        scratch_shapes=[pltpu.VMEM((N_EXPERTS, 1), F32)],
        compiler_params=_params(("arbitrary",)),
        name="route",
    )(x1, g_ffn, rw_hi, rw_lo, rb_col, upper, lower)


def _tile_rows(ref, row):
    return ref.at[pl.ds(pl.multiple_of(row * TILE_ROWS, TILE_ROWS), TILE_ROWS), :]


def _pack_kernel(q_ref, x1_ref, gffn_ref, o_ref, h_scr):
    o_ref[...] = jnp.zeros(o_ref.shape, o_ref.dtype)

    @pl.when(pl.program_id(0) < pl.num_programs(0) - 1)
    def _():
        x1 = _load_token_tiles(x1_ref, ROW_TILE)
        ms = jnp.mean(x1 * x1, axis=-1, keepdims=True)
        _store_token_tiles(h_scr, x1 * lax.rsqrt(ms + EPS) * gffn_ref[...])

        def body(t, carry):
            row = _tile_rows(h_scr, t)[...]
            for k in range(TOP_K):
                _tile_rows(o_ref, q_ref[0, 0, k * ROW_TILE + t])[...] = row
            return carry

        lax.fori_loop(0, ROW_TILE, body, 0, unroll=4)


def _pack(qloc_tiles, x1, g_ffn):
    n_tiles = qloc_tiles.shape[0]
    assert TILE_PACK >= 3 * FFN_BLOCK
    last = n_tiles - 1
    return pl.pallas_call(
        _pack_kernel,
        grid=(n_tiles + 1,),
        in_specs=[pl.BlockSpec((1, 1, TOP_K * ROW_TILE), lambda i: (jnp.minimum(i, last), 0, 0),
                               memory_space=pltpu.SMEM),
                  pl.BlockSpec((ROW_TILE * TILE_ROWS, LANES), lambda i: (jnp.minimum(i, last), 0)),
                  _const_spec((1, D_MODEL))],
        out_specs=pl.BlockSpec((TILE_PACK * TILE_ROWS, LANES), lambda i: (i, 0)),
        out_shape=jax.ShapeDtypeStruct(((n_tiles + 1) * TILE_PACK * TILE_ROWS, LANES), F32),
        scratch_shapes=[pltpu.VMEM((ROW_TILE * TILE_ROWS, LANES), F32)],
        compiler_params=_params(("arbitrary",)),
        name="pack",
    )(qloc_tiles, x1, g_ffn)


_INV_FIELDS = 3


def _invert_kernel(n_blocks, eidx_ref, prel_ref, qloc_ref, rstart_ref, acc_ref):
    i = pl.program_id(0)

    @pl.when(i == 0)
    def _():
        acc_ref[...] = jnp.zeros(acc_ref.shape, acc_ref.dtype)

    eidx, prel, qloc = eidx_ref[...], prel_ref[...], qloc_ref[...]
    rstart = rstart_ref[...]
    rows = eidx.shape[1]
    eid = lax.broadcasted_iota(jnp.int32, (N_EXPERTS, rows), 0)
    blk_id = lax.broadcasted_iota(jnp.int32, (n_blocks, rows), 0)
    off_id = lax.broadcasted_iota(jnp.int32, (LANES, rows), 0)
    blk_shift = FFN_BLOCK.bit_length() - 1
    chunk_shift = CHUNK_ROWS.bit_length() - 1
    a_parts, b_parts = [], []
    for k in range(TOP_K):
        base = jnp.sum(jnp.where(eid == eidx[k:k + 1], rstart, 0), axis=0, keepdims=True)
        pos = base + prel[k:k + 1]
        leader = (prel[k:k + 1] & (CHUNK_ROWS - 1)) == 0
        in_blk = (blk_id == lax.shift_right_logical(pos, blk_shift)) & leader
        a_parts.append(jnp.where(in_blk, 1.0, 0.0).astype(BF16))
        hit = off_id == (lax.shift_right_logical(pos, chunk_shift) & (CHUNKS_PER_BLOCK - 1))
        cid = lax.shift_right_logical(i * TILE_PACK + qloc[k:k + 1], chunk_shift)
        fields = [(cid & 255).astype(F32), lax.shift_right_logical(cid, 8).astype(F32), jnp.ones((1, rows), F32)]
        b_parts.append([jnp.where(hit, f, 0.0).astype(BF16) for f in fields])
    a = jnp.concatenate(a_parts, axis=1)
    for j in range(_INV_FIELDS):
        b = jnp.concatenate([b_parts[k][j] for k in range(TOP_K)], axis=1)
        acc_ref[j] += _nt_dot(a, b)


def _invert(eidx, prel, qloc, region_start, n_blocks, n_tiles):
    n_all = eidx.shape[1]
    n_chunks = n_tiles * TILE_PACK // CHUNK_ROWS
    assert n_chunks < 256 * 256, "chunk id is carried as two byte-sized fields"
    col = pl.BlockSpec((TOP_K, ROW_TILE), lambda i: (0, i))
    shape = (_INV_FIELDS, n_blocks, LANES)
    f = pl.pallas_call(
        functools.partial(_invert_kernel, n_blocks),
        grid=(n_all // ROW_TILE,),
        in_specs=[col, col, col, _const_spec((N_EXPERTS, 1))],
        out_specs=_resident_out_spec(shape),
        out_shape=jax.ShapeDtypeStruct(shape, F32),
        compiler_params=_params(("arbitrary",)),
        name="invert",
    )(eidx, prel, qloc, region_start)
    cid = (f[0] + 256.0 * f[1]).astype(jnp.int32)
    valid = f[2] > 0.0
    blk = lax.broadcasted_iota(jnp.int32, cid.shape, 0)
    off = lax.broadcasted_iota(jnp.int32, cid.shape, 1)
    dump = n_chunks + (blk & 1) * CHUNKS_PER_BLOCK + jnp.minimum(off, CHUNKS_PER_BLOCK - 1)
    return jnp.where(valid, cid, n_chunks + 3 * CHUNKS_PER_BLOCK), jnp.where(valid, cid, dump)


def _ffn_kernel(be_ref, nused_ref, src_cur_ref, src_nxt_ref, dst_cur_ref, dst_prev_ref,
                wup_ref, bup_ref, wdn_ref, bdn_ref, xts_hbm, out_hbm, xbuf0, xbuf1, obuf0, obuf1, wup_bf, wdn_bf,
                gsem, ssem):
    i = pl.program_id(0)
    n_used = nused_ref[0]
    xbufs, obufs = (xbuf0, xbuf1), (obuf0, obuf1)
    chunk_len = CHUNK_ROWS * TILE_ROWS

    def hbm_chunk(ref, cid):
        return ref.at[pl.ds(pl.multiple_of(cid * chunk_len, chunk_len), chunk_len), :]

    def vmem_chunk(ref, c):
        return ref.at[pl.ds(c * chunk_len, chunk_len), :]

    def gather_copy(src_ref, c, s):
        return pltpu.make_async_copy(hbm_chunk(xts_hbm, src_ref[0, 0, c]), vmem_chunk(xbufs[s], c), gsem.at[s])

    def scatter_copy(dst_ref, c, s):
        return pltpu.make_async_copy(vmem_chunk(obufs[s], c), hbm_chunk(out_hbm, dst_ref[0, 0, c]), ssem.at[s])

    def start_rows(copy_fn):
        for c in range(CHUNKS_PER_BLOCK):
            copy_fn(c).start(priority=c % 2)

    def wait_rows(copy_fn):
        for c in range(CHUNKS_PER_BLOCK):
            copy_fn(c).wait()

    first = i == 0
    changed = first | (be_ref[i] != be_ref[jnp.maximum(i - 1, 0)])

    @pl.when(first)
    def _():
        obuf0[...] = jnp.zeros(obuf0.shape, obuf0.dtype)
        obuf1[...] = jnp.zeros(obuf1.shape, obuf1.dtype)
        spare0 = out_hbm.shape[0] // chunk_len - TILE_PACK // CHUNK_ROWS
        start_rows(lambda c: pltpu.make_async_copy(
            vmem_chunk(obuf0, c), out_hbm.at[pl.ds((spare0 + c) * chunk_len, chunk_len), :], ssem.at[0]))
        start_rows(lambda c: gather_copy(src_cur_ref, c, 0))

    @pl.when(changed & (i < n_used))
    def _():
        wup_bf[...] = wup_ref[0].astype(BF16)
        wdn_bf[...] = wdn_ref[0].astype(BF16)

    def step(s):
        wait_rows(lambda r: gather_copy(src_cur_ref, r, s))
        x = _load_token_tiles(xbufs[s], FFN_BLOCK).astype(BF16)
        start_rows(lambda r: gather_copy(src_nxt_ref, r, 1 - s))
        start_rows(lambda r: scatter_copy(dst_prev_ref, r, 1 - s))
        hu = _dot(x, wup_bf[...]) + bup_ref[0]
        glu = jnp.minimum(hu[:, :D_FF], SWIGLU_LIMIT)
        lin = jnp.clip(hu[:, D_FF:], -SWIGLU_LIMIT, SWIGLU_LIMIT)
        act = glu * jax.nn.sigmoid(SWIGLU_ALPHA * glu) * (lin + 1.0)
        out = _dot(act.astype(BF16), wdn_bf[...]) + bdn_ref[0]
        wait_rows(lambda r: scatter_copy(dst_cur_ref, r, s))
        _store_token_tiles(obufs[s], out)

    def drain(s):
        start_rows(lambda r: scatter_copy(dst_cur_ref, r, s))
        wait_rows(lambda r: scatter_copy(dst_cur_ref, r, 1 - s))
        wait_rows(lambda r: scatter_copy(dst_cur_ref, r, s))
        wait_rows(lambda r: gather_copy(src_cur_ref, r, 1 - s))

    for s in range(2):
        pl.when((i < n_used) & (i % 2 == s))(functools.partial(step, s))
    for s in range(2):
        pl.when((i == n_used - 1) & (i % 2 == s))(functools.partial(drain, s))


def _ffn(block_expert, n_used, src, dst, x_ts, w_up, b_up, w_down, b_down):
    n_blocks = block_expert.shape[0]
    smem_block = lambda fn: pl.BlockSpec((1, 1, LANES), fn, memory_space=pltpu.SMEM)
    cur = smem_block(lambda i, be, nu: (i, 0, 0))
    nxt = smem_block(lambda i, be, nu: (jnp.minimum(i + 1, n_blocks - 1), 0, 0))
    prev = smem_block(lambda i, be, nu: (jnp.where(i == 0, n_blocks, i - 1), 0, 0))
    ex3 = lambda i, be, nu: (be[i], 0, 0)
    return pl.pallas_call(
        _ffn_kernel,
        grid_spec=pltpu.PrefetchScalarGridSpec(
            num_scalar_prefetch=2,
            grid=(n_blocks,),
            in_specs=[
                cur, nxt, cur, prev,
                pl.BlockSpec((1, D_MODEL, 2 * D_FF), ex3),
                pl.BlockSpec((1, 1, 2 * D_FF), ex3),
                pl.BlockSpec((1, D_FF, D_MODEL), ex3),
                pl.BlockSpec((1, 1, D_MODEL), ex3),
                pl.BlockSpec(memory_space=pl.ANY),
            ],
            out_specs=pl.BlockSpec(memory_space=pl.ANY),
            scratch_shapes=[pltpu.VMEM((FFN_BLOCK * TILE_ROWS, LANES), F32)] * 4 + [
                pltpu.VMEM((D_MODEL, 2 * D_FF), BF16), pltpu.VMEM((D_FF, D_MODEL), BF16),
                pltpu.SemaphoreType.DMA((2,)), pltpu.SemaphoreType.DMA((2,))],
        ),
        out_shape=jax.ShapeDtypeStruct(x_ts.shape, F32),
        input_output_aliases={10: 0},
        compiler_params=_params(("arbitrary",)),
        name="ffn",
    )(block_expert, n_used, src, src, dst, dst, w_up, b_up, w_down, b_down, x_ts)


def _final_kernel(q_ref, gate_ref, x1_ref, yts_ref, g_ref, y_ref, sum_scr):
    def body(t, carry):
        acc = _tile_rows(x1_ref, t)[...]
        for k in range(TOP_K):
            slot = k * ROW_TILE + t
            acc = acc + gate_ref[0, 0, slot] * _tile_rows(yts_ref, q_ref[0, 0, slot])[...]
        _tile_rows(sum_scr, t)[...] = acc
        return carry

    lax.fori_loop(0, ROW_TILE, body, 0, unroll=4)
    x2 = _load_token_tiles(sum_scr, ROW_TILE)
    ms = jnp.mean(x2 * x2, axis=-1, keepdims=True)
    y_ref[...] = x2 * lax.rsqrt(ms + EPS) * g_ref[...]


def _final(qloc_tiles, gate_tiles, x1, y_ts, g_final, row0, n_rows):
    t0 = row0 // ROW_TILE
    smem_tile = pl.BlockSpec((1, 1, TOP_K * ROW_TILE), lambda i: (t0 + i, 0, 0), memory_space=pltpu.SMEM)
    return pl.pallas_call(
        _final_kernel,
        grid=(n_rows // ROW_TILE,),
        in_specs=[smem_tile, smem_tile,
                  pl.BlockSpec((ROW_TILE * TILE_ROWS, LANES), lambda i: (t0 + i, 0)),
                  pl.BlockSpec((TILE_PACK * TILE_ROWS, LANES), lambda i: (t0 + i, 0)),
                  _const_spec((1, D_MODEL))],
        out_specs=pl.BlockSpec((ROW_TILE, D_MODEL), lambda i: (i, 0)),
        out_shape=jax.ShapeDtypeStruct((n_rows, D_MODEL), F32),
        scratch_shapes=[pltpu.VMEM((ROW_TILE * TILE_ROWS, LANES), F32)],
        compiler_params=_params(("arbitrary",)),
        name="final",
    )(qloc_tiles, gate_tiles, x1, y_ts, g_final)


def _rope_tables(seq_len, dec_seq):
    inv = ROPE_THETA ** (-np.arange(0, ROT_DIM, 2, dtype=np.float64) / ROT_DIM)
    pos = np.concatenate([np.arange(seq_len), PAST_LEN + np.arange(ROW_TILE) % dec_seq]).astype(np.float64)
    ang = pos[:, None] * inv[None, :]
    cos, sin = np.cos(ang), np.sin(ang)
    ones = np.ones((pos.shape[0], HEAD_DIM - ROT_DIM))
    cos_h = np.concatenate([cos, cos, ones], axis=1)
    sin_h = np.concatenate([-sin, sin, 0.0 * ones], axis=1)
    return (jnp.asarray(np.concatenate([cos_h, cos_h], axis=1), F32),
            jnp.asarray(np.concatenate([sin_h, sin_h], axis=1), F32))


def _block_tables(counts, n_all):
    max_rows = TOP_K * n_all + (n_all // ROW_TILE) * N_EXPERTS * (CHUNK_ROWS - 1)
    n_blocks = -(-max_rows // FFN_BLOCK) + N_EXPERTS
    padded = (counts + FFN_BLOCK - 1) // FFN_BLOCK * FFN_BLOCK
    pad_end = jnp.cumsum(padded)
    pad_start = (pad_end - padded).astype(jnp.int32).reshape(N_EXPERTS, 1)
    block_start = jnp.arange(n_blocks, dtype=jnp.int32) * FFN_BLOCK
    block_expert = jnp.minimum(jnp.sum(pad_end[None, :] <= block_start[:, None], axis=1), N_EXPERTS - 1)
    n_used = (pad_end[-1] // FFN_BLOCK).astype(jnp.int32).reshape(1)
    return pad_start, block_expert.astype(jnp.int32), n_used, n_blocks


def kernel(x_prompt, x_sample, cache_swa_k, cache_swa_v, state_gla, g_mix, w_in, w_gk_up, b_gk, sinks,
           gla_norm, w_branch_a, w_branch_b, w_out, g_ffn, router_w, router_b, w_up, b_up, w_down, b_down,
           g_final):
    batch, seq_len, _ = x_prompt.shape
    dec_batch, dec_seq, _ = x_sample.shape
    n_p, n_s = batch * seq_len, dec_batch * dec_seq
    n_all = n_p + n_s
    assert w_in.shape[0] == 1, "one layer: the final norm is fused after the only MoE"
    assert seq_len % ROW_TILE == 0 and n_s % ROW_TILE == 0 and ROW_TILE % dec_seq == 0
    assert dec_seq % SUBLANES == 0 and dec_batch % SAMPLE_GROUP == 0
    assert SAMPLE_GROUP * dec_seq == CHUNK and (dec_seq & (dec_seq - 1)) == 0

    xp2 = x_prompt.reshape(n_p, D_MODEL)
    xs2 = x_sample.reshape(n_s, D_MODEL)
    cos_tab, sin_tab = _rope_tables(seq_len, dec_seq)
    w = w_in[0]
    gl0 = _C_RG
    w_all = jnp.concatenate(
        [w[:, :gl0], w[:, gl0 + GK_RANK:], w[:, gl0:gl0 + GK_RANK],
         jnp.zeros((D_MODEL, LANES - GK_RANK), w.dtype)], axis=1).astype(BF16)
    wup_pad = jnp.concatenate([w_gk_up[0], jnp.zeros((LANES - GK_RANK, GLA_KDIM), F32)], axis=0).astype(BF16)
    qa, ka, va, qg, kg, vg, la, rg, ga, gb = _proj(
        xp2, xs2, g_mix[0].reshape(1, D_MODEL), cos_tab, sin_tab, w_all, wup_pad,
        b_gk[0].reshape(1, GLA_KDIM), seq_len)

    oa_p = _swa_prompt(sinks[0], qa, ka, va, batch, seq_len)
    oa_s, nk_s, nv_s = _swa_sample(sinks[0], qa, ka, va, cache_swa_k[0], cache_swa_v[0], n_p,
                                   dec_batch, dec_seq)
    norm_row = jnp.tile(gla_norm[0], GLA_HEADS).reshape(1, GLA_VDIM)
    og_p, s_fin = _gla_prompt(qg, kg, la, vg, rg, norm_row, batch, seq_len)
    og_s, s_new = _gla_sample(qg, kg, la, vg, rg, norm_row, state_gla[0], n_p, dec_batch, dec_seq)

    x1 = _merge(xp2, xs2, oa_p, oa_s, og_p, og_s, ga, gb, w_branch_a[0].astype(BF16),
                w_branch_b[0].astype(BF16), w_out[0].astype(BF16))

    g_ffn_row = g_ffn[0].reshape(1, D_MODEL)
    rw_t = router_w[0].T
    rw_hi = rw_t.astype(BF16)
    rw_lo = (rw_t - rw_hi.astype(F32)).astype(BF16)
    eidx, prel, qloc, gate, cnt = _route(x1, g_ffn_row, rw_hi, rw_lo, router_b[0].reshape(N_EXPERTS, 1))
    n_tiles = n_all // ROW_TILE
    region_start, block_expert, n_used, n_blocks = _block_tables(cnt[:, 0].astype(jnp.int32), n_all)
    per_tile = lambda a: a.reshape(TOP_K, n_tiles, ROW_TILE).transpose(1, 0, 2).reshape(n_tiles, 1, TOP_K * ROW_TILE)
    qloc_tiles, gate_tiles = per_tile(qloc), per_tile(gate)
    x_ts = _pack(qloc_tiles, x1, g_ffn_row)
    src, dst = _invert(eidx, prel, qloc, region_start, n_blocks, n_tiles)
    prime = (n_tiles * TILE_PACK // CHUNK_ROWS + 2 * CHUNKS_PER_BLOCK
             + jnp.minimum(jnp.arange(LANES, dtype=jnp.int32), CHUNKS_PER_BLOCK - 1)).reshape(1, LANES)
    dst = jnp.concatenate([dst, prime], axis=0)
    y_ts = _ffn(block_expert, n_used, src.reshape(n_blocks, 1, LANES), dst.reshape(n_blocks + 1, 1, LANES),
                x_ts, w_up[0], b_up[0].reshape(N_EXPERTS, 1, 2 * D_FF), w_down[0],
                b_down[0].reshape(N_EXPERTS, 1, D_MODEL))

    g_out = g_final.reshape(1, D_MODEL)
    y_p = _final(qloc_tiles, gate_tiles, x1, y_ts, g_out, 0, n_p)
    y_s = _final(qloc_tiles, gate_tiles, x1, y_ts, g_out, n_p, n_s)

    kv_shape = (1, -1, WINDOW, KV_HEADS, HEAD_DIM)
    new_k_p = ka[:n_p].reshape(batch, seq_len, KV_WIDTH)[:, -WINDOW:].reshape(kv_shape)
    new_v_p = va[:n_p].reshape(batch, seq_len, KV_WIDTH)[:, -WINDOW:].reshape(kv_shape)
    return (y_p.reshape(batch, seq_len, D_MODEL), y_s.reshape(dec_batch, dec_seq, D_MODEL),
            new_k_p, new_v_p, s_fin[None], nk_s.reshape(kv_shape), nv_s.reshape(kv_shape), s_new[None])
```

```python
import functools

import numpy as np
import jax
import jax.numpy as jnp
from jax import lax
from jax.experimental import pallas as pl
from jax.experimental.pallas import tpu as pltpu

D_MODEL = 1024
PAST_LEN = 8192
HEAD_DIM = 64
N_HEADS = 8
KV_HEADS = 2
GROUP = N_HEADS // KV_HEADS
WINDOW = 128
ROT_DIM = HEAD_DIM // 4
ROPE_THETA = 500000.0
ATT_WIDTH = N_HEADS * HEAD_DIM
KV_WIDTH = KV_HEADS * HEAD_DIM
GLA_HEADS = 4
GLA_KDIM = D_MODEL // 2
GLA_VDIM = D_MODEL
GLA_DK = GLA_KDIM // GLA_HEADS
GLA_DV = GLA_VDIM // GLA_HEADS
GK_RANK = 16
GK_NORMALIZER = 16.0
N_EXPERTS = 32
TOP_K = 4
D_FF = D_MODEL
SWIGLU_LIMIT = 7.0
SWIGLU_ALPHA = 1.702
EPS = 1e-5
NEG_INF = -1e30

LANES = 128
SUBLANES = 8
VMEM_LIMIT_BYTES = 56 * 1024 * 1024

ROW_TILE = 512
CHUNK = 128
FFN_BLOCK = 256

BF16 = jnp.bfloat16
F32 = jnp.float32

_C_QA, _C_KA, _C_VA, _C_QG, _C_KG, _C_VG, _C_RG, _C_GA, _C_GB, _C_GL, _C_END = (
    0, 512, 640, 768, 1280, 1792, 2816, 3840, 4864, 5888, 6016)


def _const_spec(shape):
    nd = len(shape)
    return pl.BlockSpec(shape, lambda *_: (0,) * nd, pipeline_mode=pl.Buffered(1))


def _resident_out_spec(shape):
    nd = len(shape)
    return pl.BlockSpec(shape, lambda *_: (0,) * nd)


def _params(sem, vmem=VMEM_LIMIT_BYTES):
    return pltpu.CompilerParams(dimension_semantics=sem, vmem_limit_bytes=vmem)


def _nt_dot(a, b):
    return lax.dot_general(a, b, (((1,), (1,)), ((), ())), preferred_element_type=F32)


def _dot(a, b):
    return jnp.dot(a, b, preferred_element_type=F32)


TILE_ROWS = D_MODEL // LANES
assert TILE_ROWS == SUBLANES


def _load_token_tiles(ref, n_tokens):
    return jnp.concatenate([ref[pl.ds(c, n_tokens, stride=TILE_ROWS), :] for c in range(TILE_ROWS)], axis=1)


def _store_token_tiles(ref, x):
    for c in range(TILE_ROWS):
        ref[pl.ds(c, x.shape[0], stride=TILE_ROWS), :] = x[:, c * LANES:(c + 1) * LANES]


def _rope(x, cos_t, sin_t, n_rep):
    width = x.shape[1]
    cos_f = jnp.concatenate([cos_t] * n_rep, axis=1) if n_rep > 1 else cos_t
    sin_f = jnp.concatenate([sin_t] * n_rep, axis=1) if n_rep > 1 else sin_t
    lane = lax.broadcasted_iota(jnp.int32, x.shape, 1) % HEAD_DIM
    up = pltpu.roll(x, width - ROT_DIM // 2, 1)
    down = pltpu.roll(x, ROT_DIM // 2, 1)
    partner = jnp.where(lane < ROT_DIM // 2, up, down)
    return x * cos_f + partner * sin_f


def _proj_kernel(n_prompt_tiles, xp_ref, xs_ref, g_ref, cos_ref, sin_ref, w_ref, wup_ref, bgk_ref,
                 qa_ref, ka_ref, va_ref, qg_ref, kg_ref, vg_ref, la_ref, rg_ref, ga_ref, gb_ref):
    i = pl.program_id(0)
    x = jnp.where(i < n_prompt_tiles, xp_ref[...], xs_ref[...])
    ms = jnp.mean(x * x, axis=-1, keepdims=True)
    h = (x * lax.rsqrt(ms + EPS) * g_ref[...]).astype(BF16)
    cos_t = cos_ref[...]
    sin_t = sin_ref[...]

    def seg(a, b):
        return _dot(h, w_ref[:, a:b])

    qa = _rope(seg(_C_QA, _C_KA), cos_t, sin_t, ATT_WIDTH // LANES)
    qa_ref[...] = (qa * (HEAD_DIM ** -0.5)).astype(BF16)
    ka_ref[...] = _rope(seg(_C_KA, _C_VA), cos_t, sin_t, 1)
    va_ref[...] = seg(_C_VA, _C_QG)
    qg_ref[...] = seg(_C_QG, _C_KG) * (GLA_DK ** -0.5)
    kg_ref[...] = seg(_C_KG, _C_VG)
    vg_ref[...] = seg(_C_VG, _C_RG).astype(BF16)
    rg_ref[...] = seg(_C_RG, _C_GA).astype(BF16)
    ga_ref[...] = seg(_C_GA, _C_GB).astype(BF16)
    gb_ref[...] = seg(_C_GB, _C_GL).astype(BF16)
    gk_low = seg(_C_GL, _C_END).astype(BF16)
    z = _dot(gk_low, wup_ref[...]) + bgk_ref[...]
    log_sig = jnp.minimum(z, 0.0) - jnp.log1p(jnp.exp(-jnp.abs(z)))
    la_ref[...] = log_sig / GK_NORMALIZER


def _proj(xp2, xs2, g_mix, cos_tab, sin_tab, w_all, wup_pad, b_gk, seq_len):
    n_p, n_s = xp2.shape[0], xs2.shape[0]
    n_all = n_p + n_s
    npt, nst = n_p // ROW_TILE, n_s // ROW_TILE
    tiles_per_seq = seq_len // ROW_TILE

    def tab_map(i):
        return (jnp.where(i < npt, i % tiles_per_seq, tiles_per_seq), 0)

    row = lambda w: pl.BlockSpec((ROW_TILE, w), lambda i: (i, 0))
    widths = [(ATT_WIDTH, BF16), (KV_WIDTH, F32), (KV_WIDTH, F32), (GLA_KDIM, F32), (GLA_KDIM, F32),
              (GLA_VDIM, BF16), (GLA_KDIM, F32), (GLA_VDIM, BF16), (D_MODEL, BF16), (D_MODEL, BF16)]
    return pl.pallas_call(
        functools.partial(_proj_kernel, npt),
        grid=(npt + nst,),
        in_specs=[
            pl.BlockSpec((ROW_TILE, D_MODEL), lambda i: (jnp.minimum(i, npt - 1), 0)),
            pl.BlockSpec((ROW_TILE, D_MODEL), lambda i: (jnp.maximum(i - npt, 0), 0)),
            _const_spec((1, D_MODEL)),
            pl.BlockSpec((ROW_TILE, LANES), tab_map),
            pl.BlockSpec((ROW_TILE, LANES), tab_map),
            _const_spec(w_all.shape),
            _const_spec(wup_pad.shape),
            _const_spec((1, GLA_KDIM)),
        ],
        out_specs=[row(w) for w, _ in widths],
        out_shape=[jax.ShapeDtypeStruct((n_all, w), dt) for w, dt in widths],
        compiler_params=_params(("arbitrary",)),
        name="proj",
    )(xp2, xs2, g_mix, cos_tab, sin_tab, w_all, wup_pad, b_gk)


def _pair_blocks(kk):
    lane = lax.broadcasted_iota(jnp.int32, kk.shape, 1)
    lo = lane < HEAD_DIM
    swapped = pltpu.roll(kk, HEAD_DIM, 1)
    zero = jnp.zeros_like(kk)
    blocks = []
    for kh in range(KV_HEADS):
        left = jnp.where(lo, kk if kh == 0 else swapped, zero)
        right = jnp.where(lo, zero, swapped if kh == 0 else kk)
        blocks.append(jnp.concatenate([left, right], axis=0).astype(BF16))
    return blocks


def _sink_softmax(s, valid, sink):
    s = jnp.where(valid, s, NEG_INF)
    m = jnp.maximum(jnp.max(s, axis=-1, keepdims=True), sink)
    p = jnp.exp(s - m)
    denom = jnp.sum(p, axis=-1, keepdims=True) + jnp.exp(sink - m)
    return (p * (1.0 / denom)).astype(BF16)


def _attend(q, kk, vv, valid, sink_ref, o_ref, row0=0):
    rows, keys = valid.shape
    kblocks = _pair_blocks(kk)
    vblocks = _pair_blocks(vv)
    for kh in range(KV_HEADS):
        base = kh * GROUP * HEAD_DIM
        qq = jnp.concatenate([q[:, base:base + LANES], q[:, base + LANES:base + 2 * LANES]], axis=0)
        s = _nt_dot(qq, kblocks[kh])
        for r in range(2):
            probs = []
            for c in range(2):
                head = kh * GROUP + 2 * r + c
                probs.append(_sink_softmax(s[r * rows:(r + 1) * rows, c * keys:(c + 1) * keys],
                                           valid, sink_ref[head]))
            p = jnp.concatenate(probs, axis=1)
            o_ref[pl.ds(row0, rows), base + r * LANES:base + (r + 1) * LANES] = (
                _dot(p, vblocks[kh]).astype(o_ref.dtype))


def _swa_prompt_kernel(sink_ref, q_ref, kp_ref, k0_ref, k1_ref, vp_ref, v0_ref, v1_ref, o_ref):
    j = pl.program_id(1)
    row = lax.broadcasted_iota(jnp.int32, (WINDOW, 2 * WINDOW), 0)
    col = lax.broadcasted_iota(jnp.int32, (WINDOW, 2 * WINDOW), 1)
    band = (col > row) & (col <= row + WINDOW)
    k_blocks = (kp_ref[...], k0_ref[...], k1_ref[...])
    v_blocks = (vp_ref[...], v0_ref[...], v1_ref[...])
    q = q_ref[...]
    for half in range(2):
        kk = jnp.concatenate(k_blocks[half:half + 2], axis=0)
        vv = jnp.concatenate(v_blocks[half:half + 2], axis=0)
        valid = band & ((j > 0) | (col >= WINDOW)) if half == 0 else band
        _attend(q[half * WINDOW:(half + 1) * WINDOW], kk, vv, valid, sink_ref, o_ref, half * WINDOW)


def _swa_prompt(sinks, qa, ka, va, batch, seq_len):
    nb = seq_len // WINDOW
    assert nb % 2 == 0
    kv = lambda off: pl.BlockSpec((WINDOW, KV_WIDTH), lambda b, j, s: (b * nb + jnp.maximum(2 * j + off, 0), 0))
    pair = pl.BlockSpec((2 * WINDOW, ATT_WIDTH), lambda b, j, s: (b * (nb // 2) + j, 0))
    return pl.pallas_call(
        _swa_prompt_kernel,
        grid_spec=pltpu.PrefetchScalarGridSpec(
            num_scalar_prefetch=1,
            grid=(batch, nb // 2),
            in_specs=[pair, kv(-1), kv(0), kv(1), kv(-1), kv(0), kv(1)],
            out_specs=pair,
        ),
        out_shape=jax.ShapeDtypeStruct((batch * seq_len, ATT_WIDTH), BF16),
        compiler_params=_params(("arbitrary", "arbitrary")),
        name="swa_prompt",
    )(sinks, qa, ka, ka, ka, va, va, va)


SAMPLE_GROUP = 16


def _swa_sample_kernel(dec_seq, sink_ref, q_ref, kn_ref, vn_ref, ck_ref, cv_ref, o_ref, nk_ref, nv_ref):
    rows = SAMPLE_GROUP * dec_seq
    ck = ck_ref[...]
    cv = cv_ref[...]
    kn = kn_ref[...]
    vn = vn_ref[...]
    nk_ref[:, :WINDOW - dec_seq, :] = ck[:, dec_seq:, :]
    nv_ref[:, :WINDOW - dec_seq, :] = cv[:, dec_seq:, :]
    nk_ref[:, WINDOW - dec_seq:, :] = kn.reshape(SAMPLE_GROUP, dec_seq, KV_WIDTH)
    nv_ref[:, WINDOW - dec_seq:, :] = vn.reshape(SAMPLE_GROUP, dec_seq, KV_WIDTH)
    n_cache = SAMPLE_GROUP * WINDOW
    kk = jnp.concatenate([ck.reshape(n_cache, KV_WIDTH), kn], axis=0)
    vv = jnp.concatenate([cv.reshape(n_cache, KV_WIDTH), vn], axis=0)
    keys = n_cache + rows
    row = lax.broadcasted_iota(jnp.int32, (rows, keys), 0)
    col = lax.broadcasted_iota(jnp.int32, (rows, keys), 1)
    q_b, q_s = row // dec_seq, row % dec_seq
    is_cache = col < n_cache
    new = col - n_cache
    valid_cache = (col // WINDOW == q_b) & (col % WINDOW > q_s)
    valid_new = (new // dec_seq == q_b) & (new % dec_seq <= q_s)
    valid = (is_cache & valid_cache) | (jnp.logical_not(is_cache) & valid_new)
    _attend(q_ref[...], kk, vv, valid, sink_ref, o_ref)


def _swa_sample(sinks, qa, ka, va, cache_k, cache_v, n_prompt_rows, dec_batch, dec_seq):
    rows = SAMPLE_GROUP * dec_seq
    off = n_prompt_rows // rows
    tok = lambda g, s: (off + g, 0)
    cache = lambda g, s: (g, 0, 0)
    cshape = (dec_batch, WINDOW, KV_WIDTH)
    return pl.pallas_call(
        functools.partial(_swa_sample_kernel, dec_seq),
        grid_spec=pltpu.PrefetchScalarGridSpec(
            num_scalar_prefetch=1,
            grid=(dec_batch // SAMPLE_GROUP,),
            in_specs=[
                pl.BlockSpec((rows, ATT_WIDTH), tok),
                pl.BlockSpec((rows, KV_WIDTH), tok),
                pl.BlockSpec((rows, KV_WIDTH), tok),
                pl.BlockSpec((SAMPLE_GROUP, WINDOW, KV_WIDTH), cache),
                pl.BlockSpec((SAMPLE_GROUP, WINDOW, KV_WIDTH), cache),
            ],
            out_specs=[
                pl.BlockSpec((rows, ATT_WIDTH), lambda g, s: (g, 0)),
                pl.BlockSpec((SAMPLE_GROUP, WINDOW, KV_WIDTH), cache),
                pl.BlockSpec((SAMPLE_GROUP, WINDOW, KV_WIDTH), cache),
            ],
        ),
        out_shape=[jax.ShapeDtypeStruct((dec_batch * dec_seq, ATT_WIDTH), BF16),
                   jax.ShapeDtypeStruct(cshape, F32), jax.ShapeDtypeStruct(cshape, F32)],
        compiler_params=_params(("arbitrary",)),
        name="swa_sample",
    )(sinks, qa, ka, va, cache_k.reshape(cshape), cache_v.reshape(cshape))


def _chunk_tables(seg):
    n_lev = int(np.log2(seg))
    t = np.arange(CHUNK)
    seg_start = (t // seg) * seg
    u = np.arange(CHUNK)[None, :]

    def prefix(end):
        return ((u >= seg_start[:, None]) & (u <= end[:, None])).astype(np.float32)

    blocks = [prefix(t)]
    for d in range(min(n_lev, _MATMUL_LEVELS)):
        m = 1 << d
        ref = (t >> (d + 1) << (d + 1)) + m - 1
        blocks.append(prefix(ref))
    lhs = np.concatenate(blocks, axis=0)
    lhs2 = np.concatenate([lhs, lhs], axis=1)
    tt, ss = t[:, None], t[None, :]
    x = tt ^ ss
    lev = np.where(x > 0, np.floor(np.log2(np.maximum(x, 1))).astype(np.int32), n_lev)
    lev = np.where((ss > tt) | (tt // seg != ss // seg), -1, lev)
    lev = np.where(tt == ss, n_lev, lev)
    return jnp.asarray(lhs2, BF16), jnp.asarray(lev, jnp.int32), n_lev


_MATMUL_LEVELS = 3


def _group_row(x, group, row):
    width = x.shape[1]
    parts = [jnp.broadcast_to(x[g * group + row:g * group + row + 1, :], (group, width))
             for g in range(x.shape[0] // group)]
    return parts[0] if len(parts) == 1 else jnp.concatenate(parts, axis=0)


_HALF_ROW_LEVELS = 4
LOG2_E = 1.4426950408889634


def _halves(x, m, which):
    parts = [x[(2 * g + which) * m:(2 * g + which + 1) * m] for g in range(x.shape[0] // (2 * m))]
    return parts[0] if len(parts) == 1 else jnp.concatenate(parts, axis=0)


def _unhalve(xh, m, which):
    zero = jnp.zeros((m, xh.shape[1]), xh.dtype)
    parts = []
    for g in range(xh.shape[0] // m):
        blk = xh[g * m:(g + 1) * m]
        parts += [zero, blk] if which else [blk, zero]
    return jnp.concatenate(parts, axis=0)


def _gla_chunk_terms(q, k, la, lhs2, level, n_lev, seg):
    la2 = la * LOG2_E
    hi = la2.astype(BF16)
    lo = (la2 - hi.astype(F32)).astype(BF16)
    sums = _dot(lhs2, jnp.concatenate([hi, lo], axis=0))
    b = sums[0:CHUNK]
    b_last = _group_row(b, seg, seg - 1)
    q_main = (q * jnp.exp2(b)).astype(BF16)
    k_upd = k * jnp.exp2(b_last - b)
    q_lev = [None] * n_lev
    k_lev = [None] * n_lev
    for d in range(n_lev):
        m = 1 << d
        if d < _MATMUL_LEVELS:
            ref = sums[(1 + d) * CHUNK:(2 + d) * CHUNK]
        elif d < _HALF_ROW_LEVELS:
            ref = _group_row(b, 2 * m, m - 1)
        if d < _HALF_ROW_LEVELS:
            q_lev[d] = (q * jnp.exp2(b - ref)).astype(BF16)
            k_lev[d] = (k * jnp.exp2(ref - b)).astype(BF16)
        else:
            ref_h = _group_row(_halves(b, m, 0), m, m - 1)
            q_lev[d] = _unhalve((_halves(q, m, 1) * jnp.exp2(_halves(b, m, 1) - ref_h)).astype(BF16), m, 1)
            k_lev[d] = _unhalve((_halves(k, m, 0) * jnp.exp2(ref_h - _halves(b, m, 0))).astype(BF16), m, 0)
    q_b, k_b = q.astype(BF16), k.astype(BF16)

    def att(h):
        hs = slice(h * GLA_DK, (h + 1) * GLA_DK)
        acc = jnp.where(level == n_lev, _nt_dot(q_b[:, hs], k_b[:, hs]), 0.0)
        for d in range(n_lev):
            acc = jnp.where(level == d, _nt_dot(q_lev[d][:, hs], k_lev[d][:, hs]), acc)
        return acc

    return q_main, k_upd, att, b_last


def _gla_out(o, r, norm):
    parts = []
    for h in range(GLA_HEADS):
        oh = o[:, h * GLA_DV:(h + 1) * GLA_DV]
        ms = jnp.mean(oh * oh, axis=-1, keepdims=True)
        parts.append(oh * lax.rsqrt(ms + EPS))
    y = jnp.concatenate(parts, axis=1) * norm
    rf = r.astype(F32)
    return y * (rf * jax.nn.sigmoid(rf))


def _gla_prompt_kernel(n_lev, n_par, *refs):
    seq_refs = [refs[5 * j:5 * j + 5] for j in range(n_par)]
    norm_ref, lhs_ref, lev_ref = refs[5 * n_par:5 * n_par + 3]
    o_ref, sfin_ref, s_scr = refs[5 * n_par + 3:]
    c = pl.program_id(1)

    @pl.when(c == 0)
    def _():
        s_scr[...] = jnp.zeros_like(s_scr)

    for j, (q_ref, k_ref, la_ref, v_ref, r_ref) in enumerate(seq_refs):
        q_main, k_upd, att, b_last = _gla_chunk_terms(q_ref[...], k_ref[...], la_ref[...], lhs_ref[...],
                                                      lev_ref[...], n_lev, CHUNK)
        v = v_ref[...]
        outs = []
        for h in range(GLA_HEADS):
            hs = slice(h * GLA_DK, (h + 1) * GLA_DK)
            vh = v[:, h * GLA_DV:(h + 1) * GLA_DV]
            s0 = s_scr[j, h]
            o_h = _dot(q_main[:, hs], s0.astype(BF16)) + _dot(att(h).astype(BF16), vh)
            outs.append(o_h)
            decay = jnp.exp2(b_last[:, hs]).T
            k_t = k_upd[:, hs].T.astype(BF16)
            s_scr[j, h] = jnp.concatenate([decay, decay], axis=1) * s0 + _dot(k_t, vh)
        o_ref[j] = _gla_out(jnp.concatenate(outs, axis=1), r_ref[...], norm_ref[...]).astype(BF16)

    @pl.when(c == pl.num_programs(1) - 1)
    def _():
        sfin_ref[...] = s_scr[...]


def _gla_prompt(qg, kg, la, vg, rg, norm_row, batch, seq_len):
    nc = seq_len // CHUNK
    n_par = 4 if batch % 4 == 0 else (2 if batch % 2 == 0 else 1)
    lhs2, level, n_lev = _chunk_tables(CHUNK)
    tok = lambda j, w: pl.BlockSpec((CHUNK, w), lambda b, c: ((b * n_par + j) * nc + c, 0))
    seq_specs, seq_args = [], []
    for j in range(n_par):
        seq_specs += [tok(j, GLA_KDIM), tok(j, GLA_KDIM), tok(j, GLA_KDIM), tok(j, GLA_VDIM), tok(j, GLA_VDIM)]
        seq_args += [qg, kg, la, vg, rg]
    og, s_fin = pl.pallas_call(
        functools.partial(_gla_prompt_kernel, n_lev, n_par),
        grid=(batch // n_par, nc),
        in_specs=seq_specs + [_const_spec((1, GLA_VDIM)), _const_spec(lhs2.shape), _const_spec(level.shape)],
        out_specs=[pl.BlockSpec((n_par, CHUNK, GLA_VDIM), lambda b, c: (b, c, 0)),
                   pl.BlockSpec((n_par, GLA_HEADS, GLA_DK, GLA_DV), lambda b, c: (b, 0, 0, 0))],
        out_shape=[jax.ShapeDtypeStruct((batch, seq_len, GLA_VDIM), BF16),
                   jax.ShapeDtypeStruct((batch, GLA_HEADS, GLA_DK, GLA_DV), F32)],
        scratch_shapes=[pltpu.VMEM((n_par, GLA_HEADS, GLA_DK, GLA_DV), F32)],
        compiler_params=_params(("arbitrary", "arbitrary")),
        name="gla_prompt",
    )(*seq_args, norm_row, lhs2, level)
    return og.reshape(batch * seq_len, GLA_VDIM), s_fin


def _gla_sample_kernel(n_lev, dec_seq, q_ref, k_ref, la_ref, v_ref, r_ref, norm_ref, lhs_ref, lev_ref,
                       s0_ref, o_ref, snew_ref):
    q_main, k_upd, att, b_last = _gla_chunk_terms(q_ref[...], k_ref[...], la_ref[...], lhs_ref[...],
                                                  lev_ref[...], n_lev, dec_seq)
    v = v_ref[...]
    n_b = CHUNK // dec_seq
    row_b = lax.broadcasted_iota(jnp.int32, (CHUNK, GLA_DK), 0) // dec_seq
    col_b = lax.broadcasted_iota(jnp.int32, (GLA_DK, CHUNK), 1) // dec_seq
    outs = []
    for h in range(GLA_HEADS):
        hs = slice(h * GLA_DK, (h + 1) * GLA_DK)
        vh = v[:, h * GLA_DV:(h + 1) * GLA_DV]
        qm = q_main[:, hs]
        decay_t = jnp.exp2(b_last[:, hs]).T
        k_t = k_upd[:, hs].T.astype(BF16)
        o_h = _dot(att(h).astype(BF16), vh)
        for bi in range(n_b):
            s0 = s0_ref[bi, h]
            o_h = o_h + _dot(jnp.where(row_b == bi, qm, jnp.zeros_like(qm)), s0.astype(BF16))
            decay = jnp.broadcast_to(decay_t[:, bi * dec_seq:bi * dec_seq + 1], (GLA_DK, GLA_DV))
            k_b = jnp.where(col_b == bi, k_t, jnp.zeros_like(k_t))
            snew_ref[bi, h] = decay * s0 + _dot(k_b, vh)
        outs.append(o_h)
    o_ref[...] = _gla_out(jnp.concatenate(outs, axis=1), r_ref[...], norm_ref[...]).astype(o_ref.dtype)


def _gla_sample(qg, kg, la, vg, rg, norm_row, state, n_prompt_rows, dec_batch, dec_seq):
    n_b = CHUNK // dec_seq
    off = n_prompt_rows // CHUNK
    lhs3, level, n_lev = _chunk_tables(dec_seq)
    tok = lambda w: pl.BlockSpec((CHUNK, w), lambda g: (off + g, 0))
    st = pl.BlockSpec((n_b, GLA_HEADS, GLA_DK, GLA_DV), lambda g: (g, 0, 0, 0))
    return pl.pallas_call(
        functools.partial(_gla_sample_kernel, n_lev, dec_seq),
        grid=(dec_batch // n_b,),
        in_specs=[tok(GLA_KDIM), tok(GLA_KDIM), tok(GLA_KDIM), tok(GLA_VDIM), tok(GLA_VDIM),
                  _const_spec((1, GLA_VDIM)), _const_spec(lhs3.shape), _const_spec(level.shape), st],
        out_specs=[pl.BlockSpec((CHUNK, GLA_VDIM), lambda g: (g, 0)), st],
        out_shape=[jax.ShapeDtypeStruct((dec_batch * dec_seq, GLA_VDIM), BF16),
                   jax.ShapeDtypeStruct(state.shape, F32)],
        compiler_params=_params(("arbitrary",)),
        name="gla_sample",
    )(qg, kg, la, vg, rg, norm_row, lhs3, level, state)


def _merge_kernel(n_prompt_tiles, xp_ref, xs_ref, oap_ref, oas_ref, ogp_ref, ogs_ref, ga_ref, gb_ref,
                  wa_ref, wb_ref, wo_ref, x1_ref):
    i = pl.program_id(0)
    is_p = i < n_prompt_tiles
    x = jnp.where(is_p, xp_ref[...], xs_ref[...])
    oa = jnp.where(is_p, oap_ref[...], oas_ref[...])
    og = jnp.where(is_p, ogp_ref[...], ogs_ref[...])
    m = (jax.nn.sigmoid(ga_ref[...].astype(F32)) * _dot(oa, wa_ref[...])
         + jax.nn.sigmoid(gb_ref[...].astype(F32)) * _dot(og, wb_ref[...]))
    _store_token_tiles(x1_ref, x + _dot(m.astype(BF16), wo_ref[...]))


def _merge(xp2, xs2, oa_p, oa_s, og_p, og_s, ga, gb, wa, wb, wo):
    n_p, n_s = xp2.shape[0], xs2.shape[0]
    npt, nst = n_p // ROW_TILE, n_s // ROW_TILE
    p_map = lambda i: (jnp.minimum(i, npt - 1), 0)
    s_map = lambda i: (jnp.maximum(i - npt, 0), 0)
    row = lambda w: pl.BlockSpec((ROW_TILE, w), lambda i: (i, 0))
    return pl.pallas_call(
        functools.partial(_merge_kernel, npt),
        grid=(npt + nst,),
        in_specs=[
            pl.BlockSpec((ROW_TILE, D_MODEL), p_map), pl.BlockSpec((ROW_TILE, D_MODEL), s_map),
            pl.BlockSpec((ROW_TILE, ATT_WIDTH), p_map), pl.BlockSpec((ROW_TILE, ATT_WIDTH), s_map),
            pl.BlockSpec((ROW_TILE, GLA_VDIM), p_map), pl.BlockSpec((ROW_TILE, GLA_VDIM), s_map),
            row(D_MODEL), row(D_MODEL),
            _const_spec(wa.shape), _const_spec(wb.shape), _const_spec(wo.shape),
        ],
        out_specs=pl.BlockSpec((ROW_TILE * TILE_ROWS, LANES), lambda i: (i, 0)),
        out_shape=jax.ShapeDtypeStruct(((n_p + n_s) * TILE_ROWS, LANES), F32),
        compiler_params=_params(("arbitrary",)),
        name="merge",
    )(xp2, xs2, oa_p, oa_s, og_p, og_s, ga, gb, wa, wb, wo)


CHUNK_ROWS = 8
CHUNKS_PER_BLOCK = FFN_BLOCK // CHUNK_ROWS
TILE_PACK = TOP_K * ROW_TILE + N_EXPERTS * (CHUNK_ROWS - 1)
TILE_PACK += -TILE_PACK % CHUNK_ROWS


def _tile_rows(ref, row):
    return ref.at[pl.ds(pl.multiple_of(row * TILE_ROWS, TILE_ROWS), TILE_ROWS), :]


def _route_kernel(x1_ref, gffn_ref, rwh_ref, rwl_ref, rb_ref, upper_ref, lower_ref, eidx_ref, prel_ref, qloc_ref,
                  gate_ref, cnt_ref, xts_ref, cnt_scr, h_scr, q_vmem, q_smem, q_sem):
    i = pl.program_id(0)

    @pl.when(i == 0)
    def _():
        cnt_scr[...] = jnp.zeros_like(cnt_scr)

    xts_ref[...] = jnp.zeros(xts_ref.shape, xts_ref.dtype)
    pl.when(i < pl.num_programs(0) - 1)(functools.partial(
        _route_tile, x1_ref, gffn_ref, rwh_ref, rwl_ref, rb_ref, upper_ref, lower_ref, eidx_ref, prel_ref,
        qloc_ref, gate_ref, cnt_ref, xts_ref, cnt_scr, h_scr, q_vmem, q_smem, q_sem))


def _route_tile(x1_ref, gffn_ref, rwh_ref, rwl_ref, rb_ref, upper_ref, lower_ref, eidx_ref, prel_ref, qloc_ref,
                gate_ref, cnt_ref, xts_ref, cnt_scr, h_scr, q_vmem, q_smem, q_sem):
    x1 = _load_token_tiles(x1_ref, ROW_TILE)
    ms = jnp.mean(x1 * x1, axis=-1, keepdims=True)
    h2 = x1 * lax.rsqrt(ms + EPS) * gffn_ref[...]
    h_hi = h2.astype(BF16)
    h_lo = (h2 - h_hi.astype(F32)).astype(BF16)
    rwh, rwl = rwh_ref[...], rwl_ref[...]
    logits = _nt_dot(rwh, h_hi) + _nt_dot(rwl, h_hi) + _nt_dot(rwh, h_lo) + rb_ref[...]
    eid = lax.broadcasted_iota(jnp.int32, logits.shape, 0)
    upper = upper_ref[...]
    seen = cnt_scr[...]
    vals, rows_e, hots, befores, counts = [], [], [], [], []
    lg = logits
    for _ in range(TOP_K):
        mx = jnp.max(lg, axis=0, keepdims=True)
        sel = jnp.min(jnp.where(lg == mx, eid, N_EXPERTS), axis=0, keepdims=True)
        onehot = eid == sel
        oh = onehot.astype(F32)
        hots.append(oh)
        befores.append(_dot(onehot.astype(BF16), upper))
        counts.append(jnp.sum(oh, axis=1, keepdims=True))
        vals.append(mx)
        rows_e.append(sel)
        lg = jnp.where(onehot, -jnp.inf, lg)
    run = (counts[0] + counts[1]) + (counts[2] + counts[3])
    run_pad = jnp.floor((run + (CHUNK_ROWS - 1)) * (1.0 / CHUNK_ROWS)) * CHUNK_ROWS
    tile_off = _dot(lower_ref[...], jnp.broadcast_to(run_pad, (N_EXPERTS, LANES)).astype(BF16))[:, 0:1]
    rows_p, rows_q = [], []
    ahead = jnp.zeros_like(run)
    for k in range(TOP_K):
        local = ahead + befores[k]
        rows_p.append(jnp.sum(hots[k] * (seen + local), axis=0, keepdims=True))
        rows_q.append(jnp.sum(hots[k] * (tile_off + local), axis=0, keepdims=True))
        ahead = ahead + counts[k]
    cnt_scr[...] = seen + run_pad
    ex = [jnp.exp(v - vals[0]) for v in vals]
    inv = 1.0 / (ex[0] + ex[1] + ex[2] + ex[3])
    q_rows = jnp.concatenate(rows_q, axis=0).astype(jnp.int32)
    eidx_ref[...] = jnp.concatenate(rows_e, axis=0)
    prel_ref[...] = jnp.concatenate(rows_p, axis=0).astype(jnp.int32)
    qloc_ref[...] = q_rows
    gate_ref[...] = jnp.concatenate([e * inv for e in ex], axis=0)
    cnt_ref[...] = jnp.broadcast_to(seen + run_pad, cnt_ref.shape)

    q_vmem[...] = jnp.concatenate([q_rows, jnp.zeros((SUBLANES - TOP_K, ROW_TILE), jnp.int32)], axis=0)
    to_smem = pltpu.make_async_copy(q_vmem, q_smem, q_sem)
    to_smem.start()
    _store_token_tiles(h_scr, h2)
    to_smem.wait()

    def body(t, carry):
        row = _tile_rows(h_scr, t)[...]
        for k in range(TOP_K):
            _tile_rows(xts_ref, q_smem[k, t])[...] = row
        return carry

    lax.fori_loop(0, ROW_TILE, body, 0, unroll=4)


def _route(x1, g_ffn, rw_hi, rw_lo, rb_col):
    n = x1.shape[0] // TILE_ROWS
    n_tiles = n // ROW_TILE
    last = n_tiles - 1
    assert TILE_PACK >= 4 * FFN_BLOCK
    upper = jnp.asarray(np.triu(np.ones((ROW_TILE, ROW_TILE), np.float32), 1), BF16)
    lower = jnp.asarray(np.tril(np.ones((N_EXPERTS, N_EXPERTS), np.float32), -1), BF16)
    col = lambda dt: (pl.BlockSpec((TOP_K, ROW_TILE), lambda i: (0, jnp.minimum(i, last))),
                      jax.ShapeDtypeStruct((TOP_K, n), dt))
    outs = [col(jnp.int32), col(jnp.int32), col(jnp.int32), col(F32),
            (_resident_out_spec((N_EXPERTS, LANES)), jax.ShapeDtypeStruct((N_EXPERTS, LANES), F32)),
            (pl.BlockSpec((TILE_PACK * TILE_ROWS, LANES), lambda i: (i, 0)),
             jax.ShapeDtypeStruct(((n_tiles + 1) * TILE_PACK * TILE_ROWS, LANES), F32))]
    return pl.pallas_call(
        _route_kernel,
        grid=(n_tiles + 1,),
        in_specs=[pl.BlockSpec((ROW_TILE * TILE_ROWS, LANES), lambda i: (jnp.minimum(i, last), 0)),
                  _const_spec((1, D_MODEL)),
                  _const_spec(rw_hi.shape), _const_spec(rw_lo.shape), _const_spec((N_EXPERTS, 1)),
                  _const_spec(upper.shape), _const_spec(lower.shape)],
        out_specs=[o[0] for o in outs],
        out_shape=[o[1] for o in outs],
        scratch_shapes=[pltpu.VMEM((N_EXPERTS, 1), F32), pltpu.VMEM((ROW_TILE * TILE_ROWS, LANES), F32),
                        pltpu.VMEM((SUBLANES, ROW_TILE), jnp.int32), pltpu.SMEM((SUBLANES, ROW_TILE), jnp.int32),
                        pltpu.SemaphoreType.DMA(())],
        compiler_params=_params(("arbitrary",)),
        name="route",
    )(x1, g_ffn, rw_hi, rw_lo, rb_col, upper, lower)


_INV_FIELDS = 3


def _invert_kernel(n_blocks, eidx_ref, prel_ref, qloc_ref, rstart_ref, acc_ref):
    i = pl.program_id(0)

    @pl.when(i == 0)
    def _():
        acc_ref[...] = jnp.zeros(acc_ref.shape, acc_ref.dtype)

    eidx, prel, qloc = eidx_ref[...], prel_ref[...], qloc_ref[...]
    rstart = rstart_ref[...]
    rows = eidx.shape[1]
    eid = lax.broadcasted_iota(jnp.int32, (N_EXPERTS, rows), 0)
    blk_id = lax.broadcasted_iota(jnp.int32, (n_blocks, rows), 0)
    off_id = lax.broadcasted_iota(jnp.int32, (LANES, rows), 0)
    blk_shift = FFN_BLOCK.bit_length() - 1
    chunk_shift = CHUNK_ROWS.bit_length() - 1
    a_parts, b_parts = [], []
    for k in range(TOP_K):
        base = jnp.sum(jnp.where(eid == eidx[k:k + 1], rstart, 0), axis=0, keepdims=True)
        pos = base + prel[k:k + 1]
        leader = (prel[k:k + 1] & (CHUNK_ROWS - 1)) == 0
        in_blk = (blk_id == lax.shift_right_logical(pos, blk_shift)) & leader
        a_parts.append(jnp.where(in_blk, 1.0, 0.0).astype(BF16))
        hit = off_id == (lax.shift_right_logical(pos, chunk_shift) & (CHUNKS_PER_BLOCK - 1))
        cid = lax.shift_right_logical(i * TILE_PACK + qloc[k:k + 1], chunk_shift)
        fields = [(cid & 255).astype(F32), lax.shift_right_logical(cid, 8).astype(F32), jnp.ones((1, rows), F32)]
        b_parts.append([jnp.where(hit, f, 0.0).astype(BF16) for f in fields])
    a = jnp.concatenate(a_parts, axis=1)
    for j in range(_INV_FIELDS):
        b = jnp.concatenate([b_parts[k][j] for k in range(TOP_K)], axis=1)
        acc_ref[j] += _nt_dot(a, b)


def _invert(eidx, prel, qloc, region_start, n_blocks, n_tiles):
    n_all = eidx.shape[1]
    n_chunks = n_tiles * TILE_PACK // CHUNK_ROWS
    assert n_chunks < 256 * 256, "chunk id is carried as two byte-sized fields"
    col = pl.BlockSpec((TOP_K, ROW_TILE), lambda i: (0, i))
    shape = (_INV_FIELDS, n_blocks, LANES)
    f = pl.pallas_call(
        functools.partial(_invert_kernel, n_blocks),
        grid=(n_all // ROW_TILE,),
        in_specs=[col, col, col, _const_spec((N_EXPERTS, 1))],
        out_specs=_resident_out_spec(shape),
        out_shape=jax.ShapeDtypeStruct(shape, F32),
        compiler_params=_params(("arbitrary",)),
        name="invert",
    )(eidx, prel, qloc, region_start)
    cid = (f[0] + 256.0 * f[1]).astype(jnp.int32)
    valid = f[2] > 0.0
    blk = lax.broadcasted_iota(jnp.int32, cid.shape, 0)
    off = lax.broadcasted_iota(jnp.int32, cid.shape, 1)
    dump = n_chunks + (blk & 1) * CHUNKS_PER_BLOCK + jnp.minimum(off, CHUNKS_PER_BLOCK - 1)
    return jnp.where(valid, cid, n_chunks + 3 * CHUNKS_PER_BLOCK), jnp.where(valid, cid, dump)


def _ffn_kernel(be_ref, nused_ref, src_cur_ref, src_nxt_ref, dst_cur_ref, dst_prev_ref,
                wup_ref, bup_ref, wdn_ref, bdn_ref, xts_hbm, out_hbm, xbuf0, xbuf1, obuf0, obuf1, wup_bf, wdn_bf,
                gsem, ssem):
    i = pl.program_id(0)
    n_used = nused_ref[0]
    xbufs, obufs = (xbuf0, xbuf1), (obuf0, obuf1)
    chunk_len = CHUNK_ROWS * TILE_ROWS

    def hbm_chunk(ref, cid):
        return ref.at[pl.ds(pl.multiple_of(cid * chunk_len, chunk_len), chunk_len), :]

    def vmem_chunk(ref, c):
        return ref.at[pl.ds(c * chunk_len, chunk_len), :]

    def gather_copy(src_ref, c, s):
        return pltpu.make_async_copy(hbm_chunk(xts_hbm, src_ref[0, 0, c]), vmem_chunk(xbufs[s], c), gsem.at[s])

    def scatter_copy(dst_ref, c, s):
        return pltpu.make_async_copy(vmem_chunk(obufs[s], c), hbm_chunk(out_hbm, dst_ref[0, 0, c]), ssem.at[s])

    def start_rows(copy_fn):
        for c in range(CHUNKS_PER_BLOCK):
            copy_fn(c).start(priority=c % 2)

    def wait_rows(copy_fn):
        for c in range(CHUNKS_PER_BLOCK):
            copy_fn(c).wait()

    first = i == 0
    changed = first | (be_ref[i] != be_ref[jnp.maximum(i - 1, 0)])

    @pl.when(first)
    def _():
        obuf0[...] = jnp.zeros(obuf0.shape, obuf0.dtype)
        obuf1[...] = jnp.zeros(obuf1.shape, obuf1.dtype)
        spare0 = out_hbm.shape[0] // chunk_len - TILE_PACK // CHUNK_ROWS
        start_rows(lambda c: pltpu.make_async_copy(
            vmem_chunk(obuf0, c), out_hbm.at[pl.ds((spare0 + c) * chunk_len, chunk_len), :], ssem.at[0]))
        start_rows(lambda c: gather_copy(src_cur_ref, c, 0))

    @pl.when(changed & (i < n_used))
    def _():
        wup_bf[...] = wup_ref[0].astype(BF16)
        wdn_bf[...] = wdn_ref[0].astype(BF16)

    def step(s):
        wait_rows(lambda r: gather_copy(src_cur_ref, r, s))
        x = _load_token_tiles(xbufs[s], FFN_BLOCK).astype(BF16)
        start_rows(lambda r: gather_copy(src_nxt_ref, r, 1 - s))
        start_rows(lambda r: scatter_copy(dst_prev_ref, r, 1 - s))
        hu = _dot(x, wup_bf[...]) + bup_ref[0]
        glu = jnp.minimum(hu[:, :D_FF], SWIGLU_LIMIT)
        lin = jnp.clip(hu[:, D_FF:], -SWIGLU_LIMIT, SWIGLU_LIMIT)
        act = glu * jax.nn.sigmoid(SWIGLU_ALPHA * glu) * (lin + 1.0)
        out = _dot(act.astype(BF16), wdn_bf[...]) + bdn_ref[0]
        wait_rows(lambda r: scatter_copy(dst_cur_ref, r, s))
        _store_token_tiles(obufs[s], out)

    def drain(s):
        start_rows(lambda r: scatter_copy(dst_cur_ref, r, s))
        wait_rows(lambda r: scatter_copy(dst_cur_ref, r, 1 - s))
        wait_rows(lambda r: scatter_copy(dst_cur_ref, r, s))
        wait_rows(lambda r: gather_copy(src_cur_ref, r, 1 - s))

    for s in range(2):
        pl.when((i < n_used) & (i % 2 == s))(functools.partial(step, s))
    for s in range(2):
        pl.when((i == n_used - 1) & (i % 2 == s))(functools.partial(drain, s))


def _ffn(block_expert, n_used, src, dst, x_ts, w_up, b_up, w_down, b_down):
    n_blocks = block_expert.shape[0]
    smem_block = lambda fn: pl.BlockSpec((1, 1, LANES), fn, memory_space=pltpu.SMEM)
    cur = smem_block(lambda i, be, nu: (i, 0, 0))
    nxt = smem_block(lambda i, be, nu: (jnp.minimum(i + 1, n_blocks - 1), 0, 0))
    prev = smem_block(lambda i, be, nu: (jnp.where(i == 0, n_blocks, i - 1), 0, 0))
    ex3 = lambda i, be, nu: (be[i], 0, 0)
    return pl.pallas_call(
        _ffn_kernel,
        grid_spec=pltpu.PrefetchScalarGridSpec(
            num_scalar_prefetch=2,
            grid=(n_blocks,),
            in_specs=[
                cur, nxt, cur, prev,
                pl.BlockSpec((1, D_MODEL, 2 * D_FF), ex3),
                pl.BlockSpec((1, 1, 2 * D_FF), ex3),
                pl.BlockSpec((1, D_FF, D_MODEL), ex3),
                pl.BlockSpec((1, 1, D_MODEL), ex3),
                pl.BlockSpec(memory_space=pl.ANY),
            ],
            out_specs=pl.BlockSpec(memory_space=pl.ANY),
            scratch_shapes=[pltpu.VMEM((FFN_BLOCK * TILE_ROWS, LANES), F32)] * 4 + [
                pltpu.VMEM((D_MODEL, 2 * D_FF), BF16), pltpu.VMEM((D_FF, D_MODEL), BF16),
                pltpu.SemaphoreType.DMA((2,)), pltpu.SemaphoreType.DMA((2,))],
        ),
        out_shape=jax.ShapeDtypeStruct(x_ts.shape, F32),
        input_output_aliases={10: 0},
        compiler_params=_params(("arbitrary",)),
        name="ffn",
    )(block_expert, n_used, src, src, dst, dst, w_up, b_up, w_down, b_down, x_ts)


def _final_kernel(q_ref, gate_ref, x1_ref, yts_ref, g_ref, y_ref, sum_scr):
    def body(t, carry):
        acc = _tile_rows(x1_ref, t)[...]
        for k in range(TOP_K):
            slot = k * ROW_TILE + t
            acc = acc + gate_ref[0, 0, slot] * _tile_rows(yts_ref, q_ref[0, 0, slot])[...]
        _tile_rows(sum_scr, t)[...] = acc
        return carry

    lax.fori_loop(0, ROW_TILE, body, 0, unroll=4)
    x2 = _load_token_tiles(sum_scr, ROW_TILE)
    ms = jnp.mean(x2 * x2, axis=-1, keepdims=True)
    y_ref[...] = x2 * lax.rsqrt(ms + EPS) * g_ref[...]


def _final(qloc_tiles, gate_tiles, x1, y_ts, g_final, row0, n_rows):
    t0 = row0 // ROW_TILE
    smem_tile = pl.BlockSpec((1, 1, TOP_K * ROW_TILE), lambda i: (t0 + i, 0, 0), memory_space=pltpu.SMEM)
    return pl.pallas_call(
        _final_kernel,
        grid=(n_rows // ROW_TILE,),
        in_specs=[smem_tile, smem_tile,
                  pl.BlockSpec((ROW_TILE * TILE_ROWS, LANES), lambda i: (t0 + i, 0)),
                  pl.BlockSpec((TILE_PACK * TILE_ROWS, LANES), lambda i: (t0 + i, 0)),
                  _const_spec((1, D_MODEL))],
        out_specs=pl.BlockSpec((ROW_TILE, D_MODEL), lambda i: (i, 0)),
        out_shape=jax.ShapeDtypeStruct((n_rows, D_MODEL), F32),
        scratch_shapes=[pltpu.VMEM((ROW_TILE * TILE_ROWS, LANES), F32)],
        compiler_params=_params(("arbitrary",)),
        name="final",
    )(qloc_tiles, gate_tiles, x1, y_ts, g_final)


def _rope_tables(seq_len, dec_seq):
    inv = ROPE_THETA ** (-np.arange(0, ROT_DIM, 2, dtype=np.float64) / ROT_DIM)
    pos = np.concatenate([np.arange(seq_len), PAST_LEN + np.arange(ROW_TILE) % dec_seq]).astype(np.float64)
    ang = pos[:, None] * inv[None, :]
    cos, sin = np.cos(ang), np.sin(ang)
    ones = np.ones((pos.shape[0], HEAD_DIM - ROT_DIM))
    cos_h = np.concatenate([cos, cos, ones], axis=1)
    sin_h = np.concatenate([-sin, sin, 0.0 * ones], axis=1)
    return (jnp.asarray(np.concatenate([cos_h, cos_h], axis=1), F32),
            jnp.asarray(np.concatenate([sin_h, sin_h], axis=1), F32))


def _block_tables(counts, n_all):
    max_rows = TOP_K * n_all + (n_all // ROW_TILE) * N_EXPERTS * (CHUNK_ROWS - 1)
    n_blocks = -(-max_rows // FFN_BLOCK) + N_EXPERTS
    padded = (counts + FFN_BLOCK - 1) // FFN_BLOCK * FFN_BLOCK
    pad_end = jnp.cumsum(padded)
    pad_start = (pad_end - padded).astype(jnp.int32).reshape(N_EXPERTS, 1)
    block_start = jnp.arange(n_blocks, dtype=jnp.int32) * FFN_BLOCK
    block_expert = jnp.minimum(jnp.sum(pad_end[None, :] <= block_start[:, None], axis=1), N_EXPERTS - 1)
    n_used = (pad_end[-1] // FFN_BLOCK).astype(jnp.int32).reshape(1)
    return pad_start, block_expert.astype(jnp.int32), n_used, n_blocks


def kernel(x_prompt, x_sample, cache_swa_k, cache_swa_v, state_gla, g_mix, w_in, w_gk_up, b_gk, sinks,
           gla_norm, w_branch_a, w_branch_b, w_out, g_ffn, router_w, router_b, w_up, b_up, w_down, b_down,
           g_final):
    batch, seq_len, _ = x_prompt.shape
    dec_batch, dec_seq, _ = x_sample.shape
    n_p, n_s = batch * seq_len, dec_batch * dec_seq
    n_all = n_p + n_s
    assert w_in.shape[0] == 1, "one layer: the final norm is fused after the only MoE"
    assert seq_len % ROW_TILE == 0 and n_s % ROW_TILE == 0 and ROW_TILE % dec_seq == 0
    assert dec_seq % SUBLANES == 0 and dec_batch % SAMPLE_GROUP == 0
    assert SAMPLE_GROUP * dec_seq == CHUNK and (dec_seq & (dec_seq - 1)) == 0

    xp2 = x_prompt.reshape(n_p, D_MODEL)
    xs2 = x_sample.reshape(n_s, D_MODEL)
    cos_tab, sin_tab = _rope_tables(seq_len, dec_seq)
    w = w_in[0]
    gl0 = _C_RG
    w_all = jnp.concatenate(
        [w[:, :gl0], w[:, gl0 + GK_RANK:], w[:, gl0:gl0 + GK_RANK],
         jnp.zeros((D_MODEL, LANES - GK_RANK), w.dtype)], axis=1).astype(BF16)
    wup_pad = jnp.concatenate([w_gk_up[0], jnp.zeros((LANES - GK_RANK, GLA_KDIM), F32)], axis=0).astype(BF16)
    qa, ka, va, qg, kg, vg, la, rg, ga, gb = _proj(
        xp2, xs2, g_mix[0].reshape(1, D_MODEL), cos_tab, sin_tab, w_all, wup_pad,
        b_gk[0].reshape(1, GLA_KDIM), seq_len)

    oa_p = _swa_prompt(sinks[0], qa, ka, va, batch, seq_len)
    oa_s, nk_s, nv_s = _swa_sample(sinks[0], qa, ka, va, cache_swa_k[0], cache_swa_v[0], n_p,
                                   dec_batch, dec_seq)
    norm_row = jnp.tile(gla_norm[0], GLA_HEADS).reshape(1, GLA_VDIM)
    og_p, s_fin = _gla_prompt(qg, kg, la, vg, rg, norm_row, batch, seq_len)
    og_s, s_new = _gla_sample(qg, kg, la, vg, rg, norm_row, state_gla[0], n_p, dec_batch, dec_seq)

    x1 = _merge(xp2, xs2, oa_p, oa_s, og_p, og_s, ga, gb, w_branch_a[0].astype(BF16),
                w_branch_b[0].astype(BF16), w_out[0].astype(BF16))

    g_ffn_row = g_ffn[0].reshape(1, D_MODEL)
    rw_t = router_w[0].T
    rw_hi = rw_t.astype(BF16)
    rw_lo = (rw_t - rw_hi.astype(F32)).astype(BF16)
    eidx, prel, qloc, gate, cnt, x_ts = _route(x1, g_ffn_row, rw_hi, rw_lo, router_b[0].reshape(N_EXPERTS, 1))
    n_tiles = n_all // ROW_TILE
    region_start, block_expert, n_used, n_blocks = _block_tables(cnt[:, 0].astype(jnp.int32), n_all)
    per_tile = lambda a: a.reshape(TOP_K, n_tiles, ROW_TILE).transpose(1, 0, 2).reshape(n_tiles, 1, TOP_K * ROW_TILE)
    qloc_tiles, gate_tiles = per_tile(qloc), per_tile(gate)
    src, dst = _invert(eidx, prel, qloc, region_start, n_blocks, n_tiles)
    prime = (n_tiles * TILE_PACK // CHUNK_ROWS + 2 * CHUNKS_PER_BLOCK
             + jnp.minimum(jnp.arange(LANES, dtype=jnp.int32), CHUNKS_PER_BLOCK - 1)).reshape(1, LANES)
    dst = jnp.concatenate([dst, prime], axis=0)
    y_ts = _ffn(block_expert, n_used, src.reshape(n_blocks, 1, LANES), dst.reshape(n_blocks + 1, 1, LANES),
                x_ts, w_up[0], b_up[0].reshape(N_EXPERTS, 1, 2 * D_FF), w_down[0],
                b_down[0].reshape(N_EXPERTS, 1, D_MODEL))

    g_out = g_final.reshape(1, D_MODEL)
    y_p = _final(qloc_tiles, gate_tiles, x1, y_ts, g_out, 0, n_p)
    y_s = _final(qloc_tiles, gate_tiles, x1, y_ts, g_out, n_p, n_s)

    kv_shape = (1, -1, WINDOW, KV_HEADS, HEAD_DIM)
    new_k_p = ka[:n_p].reshape(batch, seq_len, KV_WIDTH)[:, -WINDOW:].reshape(kv_shape)
    new_v_p = va[:n_p].reshape(batch, seq_len, KV_WIDTH)[:, -WINDOW:].reshape(kv_shape)
    return (y_p.reshape(batch, seq_len, D_MODEL), y_s.reshape(dec_batch, dec_seq, D_MODEL),
            new_k_p, new_v_p, s_fin[None], nk_s.reshape(kv_shape), nv_s.reshape(kv_shape), s_new[None])
```

```python
import functools

import numpy as np
import jax
import jax.numpy as jnp
from jax import lax
from jax.experimental import pallas as pl
from jax.experimental.pallas import tpu as pltpu

D_MODEL = 1024
PAST_LEN = 8192
HEAD_DIM = 64
N_HEADS = 8
KV_HEADS = 2
GROUP = N_HEADS // KV_HEADS
WINDOW = 128
ROT_DIM = HEAD_DIM // 4
ROPE_THETA = 500000.0
ATT_WIDTH = N_HEADS * HEAD_DIM
KV_WIDTH = KV_HEADS * HEAD_DIM
GLA_HEADS = 4
GLA_KDIM = D_MODEL // 2
GLA_VDIM = D_MODEL
GLA_DK = GLA_KDIM // GLA_HEADS
GLA_DV = GLA_VDIM // GLA_HEADS
GK_RANK = 16
GK_NORMALIZER = 16.0
N_EXPERTS = 32
TOP_K = 4
D_FF = D_MODEL
SWIGLU_LIMIT = 7.0
SWIGLU_ALPHA = 1.702
EPS = 1e-5
NEG_INF = -1e30

LANES = 128
SUBLANES = 8
VMEM_LIMIT_BYTES = 56 * 1024 * 1024

ROW_TILE = 512
CHUNK = 128
FFN_BLOCK = 512

BF16 = jnp.bfloat16
F32 = jnp.float32

_C_QA, _C_KA, _C_VA, _C_QG, _C_KG, _C_VG, _C_RG, _C_GA, _C_GB, _C_GL, _C_END = (
    0, 512, 640, 768, 1280, 1792, 2816, 3840, 4864, 5888, 6016)


def _const_spec(shape):
    nd = len(shape)
    return pl.BlockSpec(shape, lambda *_: (0,) * nd, pipeline_mode=pl.Buffered(1))


def _resident_out_spec(shape):
    nd = len(shape)
    return pl.BlockSpec(shape, lambda *_: (0,) * nd)


def _params(sem, vmem=VMEM_LIMIT_BYTES):
    return pltpu.CompilerParams(dimension_semantics=sem, vmem_limit_bytes=vmem)


def _nt_dot(a, b):
    return lax.dot_general(a, b, (((1,), (1,)), ((), ())), preferred_element_type=F32)


def _dot(a, b):
    return jnp.dot(a, b, preferred_element_type=F32)


TILE_ROWS = D_MODEL // LANES
assert TILE_ROWS == SUBLANES


def _load_token_tiles(ref, n_tokens):
    return jnp.concatenate([ref[pl.ds(c, n_tokens, stride=TILE_ROWS), :] for c in range(TILE_ROWS)], axis=1)


def _store_token_tiles(ref, x):
    for c in range(TILE_ROWS):
        ref[pl.ds(c, x.shape[0], stride=TILE_ROWS), :] = x[:, c * LANES:(c + 1) * LANES]


def _rope(x, cos_t, sin_t, n_rep):
    width = x.shape[1]
    cos_f = jnp.concatenate([cos_t] * n_rep, axis=1) if n_rep > 1 else cos_t
    sin_f = jnp.concatenate([sin_t] * n_rep, axis=1) if n_rep > 1 else sin_t
    lane = lax.broadcasted_iota(jnp.int32, x.shape, 1) % HEAD_DIM
    up = pltpu.roll(x, width - ROT_DIM // 2, 1)
    down = pltpu.roll(x, ROT_DIM // 2, 1)
    partner = jnp.where(lane < ROT_DIM // 2, up, down)
    return x * cos_f + partner * sin_f


def _proj_kernel(n_prompt_tiles, xp_ref, xs_ref, g_ref, cos_ref, sin_ref, w_ref, wup_ref, bgk_ref,
                 qa_ref, ka_ref, va_ref, qg_ref, kg_ref, vg_ref, la_ref, rg_ref, ga_ref, gb_ref):
    i = pl.program_id(0)
    x = jnp.where(i < n_prompt_tiles, xp_ref[...], xs_ref[...])
    ms = jnp.mean(x * x, axis=-1, keepdims=True)
    h = (x * lax.rsqrt(ms + EPS) * g_ref[...]).astype(BF16)
    cos_t = cos_ref[...]
    sin_t = sin_ref[...]

    def seg(a, b):
        return _dot(h, w_ref[:, a:b])

    qa = _rope(seg(_C_QA, _C_KA), cos_t, sin_t, ATT_WIDTH // LANES)
    qa_ref[...] = (qa * (HEAD_DIM ** -0.5)).astype(BF16)
    ka_ref[...] = _rope(seg(_C_KA, _C_VA), cos_t, sin_t, 1)
    va_ref[...] = seg(_C_VA, _C_QG)
    qg_ref[...] = seg(_C_QG, _C_KG) * (GLA_DK ** -0.5)
    kg_ref[...] = seg(_C_KG, _C_VG)
    vg_ref[...] = seg(_C_VG, _C_RG).astype(BF16)
    rg_ref[...] = seg(_C_RG, _C_GA).astype(BF16)
    ga_ref[...] = seg(_C_GA, _C_GB).astype(BF16)
    gb_ref[...] = seg(_C_GB, _C_GL).astype(BF16)
    gk_low = seg(_C_GL, _C_END).astype(BF16)
    z = _dot(gk_low, wup_ref[...]) + bgk_ref[...]
    log_sig = jnp.minimum(z, 0.0) - jnp.log1p(jnp.exp(-jnp.abs(z)))
    la_ref[...] = log_sig / GK_NORMALIZER


def _proj(xp2, xs2, g_mix, cos_tab, sin_tab, w_all, wup_pad, b_gk, seq_len):
    n_p, n_s = xp2.shape[0], xs2.shape[0]
    n_all = n_p + n_s
    npt, nst = n_p // ROW_TILE, n_s // ROW_TILE
    tiles_per_seq = seq_len // ROW_TILE

    def tab_map(i):
        return (jnp.where(i < npt, i % tiles_per_seq, tiles_per_seq), 0)

    row = lambda w: pl.BlockSpec((ROW_TILE, w), lambda i: (i, 0))
    widths = [(ATT_WIDTH, BF16), (KV_WIDTH, F32), (KV_WIDTH, F32), (GLA_KDIM, F32), (GLA_KDIM, F32),
              (GLA_VDIM, BF16), (GLA_KDIM, F32), (GLA_VDIM, BF16), (D_MODEL, BF16), (D_MODEL, BF16)]
    return pl.pallas_call(
        functools.partial(_proj_kernel, npt),
        grid=(npt + nst,),
        in_specs=[
            pl.BlockSpec((ROW_TILE, D_MODEL), lambda i: (jnp.minimum(i, npt - 1), 0)),
            pl.BlockSpec((ROW_TILE, D_MODEL), lambda i: (jnp.maximum(i - npt, 0), 0)),
            _const_spec((1, D_MODEL)),
            pl.BlockSpec((ROW_TILE, LANES), tab_map),
            pl.BlockSpec((ROW_TILE, LANES), tab_map),
            _const_spec(w_all.shape),
            _const_spec(wup_pad.shape),
            _const_spec((1, GLA_KDIM)),
        ],
        out_specs=[row(w) for w, _ in widths],
        out_shape=[jax.ShapeDtypeStruct((n_all, w), dt) for w, dt in widths],
        compiler_params=_params(("arbitrary",)),
        name="proj",
    )(xp2, xs2, g_mix, cos_tab, sin_tab, w_all, wup_pad, b_gk)


def _pair_blocks(kk):
    lane = lax.broadcasted_iota(jnp.int32, kk.shape, 1)
    lo = lane < HEAD_DIM
    swapped = pltpu.roll(kk, HEAD_DIM, 1)
    zero = jnp.zeros_like(kk)
    blocks = []
    for kh in range(KV_HEADS):
        left = jnp.where(lo, kk if kh == 0 else swapped, zero)
        right = jnp.where(lo, zero, swapped if kh == 0 else kk)
        blocks.append(jnp.concatenate([left, right], axis=0).astype(BF16))
    return blocks


def _sink_softmax(s, valid, sink):
    s = jnp.where(valid, s, NEG_INF)
    m = jnp.maximum(jnp.max(s, axis=-1, keepdims=True), sink)
    p = jnp.exp(s - m)
    denom = jnp.sum(p, axis=-1, keepdims=True) + jnp.exp(sink - m)
    return (p * (1.0 / denom)).astype(BF16)


def _attend(q, kk, vv, valid, sink_ref, o_ref, row0=0):
    rows, keys = valid.shape
    kblocks = _pair_blocks(kk)
    vblocks = _pair_blocks(vv)
    for kh in range(KV_HEADS):
        base = kh * GROUP * HEAD_DIM
        qq = jnp.concatenate([q[:, base:base + LANES], q[:, base + LANES:base + 2 * LANES]], axis=0)
        s = _nt_dot(qq, kblocks[kh])
        for r in range(2):
            probs = []
            for c in range(2):
                head = kh * GROUP + 2 * r + c
                probs.append(_sink_softmax(s[r * rows:(r + 1) * rows, c * keys:(c + 1) * keys],
                                           valid, sink_ref[head]))
            p = jnp.concatenate(probs, axis=1)
            o_ref[pl.ds(row0, rows), base + r * LANES:base + (r + 1) * LANES] = (
                _dot(p, vblocks[kh]).astype(o_ref.dtype))


def _swa_prompt_kernel(sink_ref, q_ref, kp_ref, k0_ref, k1_ref, vp_ref, v0_ref, v1_ref, o_ref):
    j = pl.program_id(1)
    row = lax.broadcasted_iota(jnp.int32, (WINDOW, 2 * WINDOW), 0)
    col = lax.broadcasted_iota(jnp.int32, (WINDOW, 2 * WINDOW), 1)
    band = (col > row) & (col <= row + WINDOW)
    k_blocks = (kp_ref[...], k0_ref[...], k1_ref[...])
    v_blocks = (vp_ref[...], v0_ref[...], v1_ref[...])
    q = q_ref[...]
    for half in range(2):
        kk = jnp.concatenate(k_blocks[half:half + 2], axis=0)
        vv = jnp.concatenate(v_blocks[half:half + 2], axis=0)
        valid = band & ((j > 0) | (col >= WINDOW)) if half == 0 else band
        _attend(q[half * WINDOW:(half + 1) * WINDOW], kk, vv, valid, sink_ref, o_ref, half * WINDOW)


def _swa_prompt(sinks, qa, ka, va, batch, seq_len):
    nb = seq_len // WINDOW
    assert nb % 2 == 0
    kv = lambda off: pl.BlockSpec((WINDOW, KV_WIDTH), lambda b, j, s: (b * nb + jnp.maximum(2 * j + off, 0), 0))
    pair = pl.BlockSpec((2 * WINDOW, ATT_WIDTH), lambda b, j, s: (b * (nb // 2) + j, 0))
    return pl.pallas_call(
        _swa_prompt_kernel,
        grid_spec=pltpu.PrefetchScalarGridSpec(
            num_scalar_prefetch=1,
            grid=(batch, nb // 2),
            in_specs=[pair, kv(-1), kv(0), kv(1), kv(-1), kv(0), kv(1)],
            out_specs=pair,
        ),
        out_shape=jax.ShapeDtypeStruct((batch * seq_len, ATT_WIDTH), BF16),
        compiler_params=_params(("arbitrary", "arbitrary")),
        name="swa_prompt",
    )(sinks, qa, ka, ka, ka, va, va, va)


SAMPLE_GROUP = 16


def _swa_sample_kernel(dec_seq, sink_ref, q_ref, kn_ref, vn_ref, ck_ref, cv_ref, o_ref, nk_ref, nv_ref):
    rows = SAMPLE_GROUP * dec_seq
    ck = ck_ref[...]
    cv = cv_ref[...]
    kn = kn_ref[...]
    vn = vn_ref[...]
    nk_ref[:, :WINDOW - dec_seq, :] = ck[:, dec_seq:, :]
    nv_ref[:, :WINDOW - dec_seq, :] = cv[:, dec_seq:, :]
    nk_ref[:, WINDOW - dec_seq:, :] = kn.reshape(SAMPLE_GROUP, dec_seq, KV_WIDTH)
    nv_ref[:, WINDOW - dec_seq:, :] = vn.reshape(SAMPLE_GROUP, dec_seq, KV_WIDTH)
    n_cache = SAMPLE_GROUP * WINDOW
    kk = jnp.concatenate([ck.reshape(n_cache, KV_WIDTH), kn], axis=0)
    vv = jnp.concatenate([cv.reshape(n_cache, KV_WIDTH), vn], axis=0)
    keys = n_cache + rows
    row = lax.broadcasted_iota(jnp.int32, (rows, keys), 0)
    col = lax.broadcasted_iota(jnp.int32, (rows, keys), 1)
    q_b, q_s = row // dec_seq, row % dec_seq
    is_cache = col < n_cache
    new = col - n_cache
    valid_cache = (col // WINDOW == q_b) & (col % WINDOW > q_s)
    valid_new = (new // dec_seq == q_b) & (new % dec_seq <= q_s)
    valid = (is_cache & valid_cache) | (jnp.logical_not(is_cache) & valid_new)
    _attend(q_ref[...], kk, vv, valid, sink_ref, o_ref)


def _swa_sample(sinks, qa, ka, va, cache_k, cache_v, n_prompt_rows, dec_batch, dec_seq):
    rows = SAMPLE_GROUP * dec_seq
    off = n_prompt_rows // rows
    tok = lambda g, s: (off + g, 0)
    cache = lambda g, s: (g, 0, 0)
    cshape = (dec_batch, WINDOW, KV_WIDTH)
    return pl.pallas_call(
        functools.partial(_swa_sample_kernel, dec_seq),
        grid_spec=pltpu.PrefetchScalarGridSpec(
            num_scalar_prefetch=1,
            grid=(dec_batch // SAMPLE_GROUP,),
            in_specs=[
                pl.BlockSpec((rows, ATT_WIDTH), tok),
                pl.BlockSpec((rows, KV_WIDTH), tok),
                pl.BlockSpec((rows, KV_WIDTH), tok),
                pl.BlockSpec((SAMPLE_GROUP, WINDOW, KV_WIDTH), cache),
                pl.BlockSpec((SAMPLE_GROUP, WINDOW, KV_WIDTH), cache),
            ],
            out_specs=[
                pl.BlockSpec((rows, ATT_WIDTH), lambda g, s: (g, 0)),
                pl.BlockSpec((SAMPLE_GROUP, WINDOW, KV_WIDTH), cache),
                pl.BlockSpec((SAMPLE_GROUP, WINDOW, KV_WIDTH), cache),
            ],
        ),
        out_shape=[jax.ShapeDtypeStruct((dec_batch * dec_seq, ATT_WIDTH), BF16),
                   jax.ShapeDtypeStruct(cshape, F32), jax.ShapeDtypeStruct(cshape, F32)],
        compiler_params=_params(("arbitrary",)),
        name="swa_sample",
    )(sinks, qa, ka, va, cache_k.reshape(cshape), cache_v.reshape(cshape))


def _chunk_tables(seg):
    n_lev = int(np.log2(seg))
    t = np.arange(CHUNK)
    seg_start = (t // seg) * seg
    u = np.arange(CHUNK)[None, :]

    def prefix(end):
        return ((u >= seg_start[:, None]) & (u <= end[:, None])).astype(np.float32)

    blocks = [prefix(t)]
    for d in range(min(n_lev, _MATMUL_LEVELS)):
        m = 1 << d
        ref = (t >> (d + 1) << (d + 1)) + m - 1
        blocks.append(prefix(ref))
    lhs = np.concatenate(blocks, axis=0)
    lhs2 = np.concatenate([lhs, lhs], axis=1)
    tt, ss = t[:, None], t[None, :]
    x = tt ^ ss
    lev = np.where(x > 0, np.floor(np.log2(np.maximum(x, 1))).astype(np.int32), n_lev)
    lev = np.where((ss > tt) | (tt // seg != ss // seg), -1, lev)
    lev = np.where(tt == ss, n_lev, lev)
    return jnp.asarray(lhs2, BF16), jnp.asarray(lev, jnp.int32), n_lev


_MATMUL_LEVELS = 3


def _group_row(x, group, row):
    width = x.shape[1]
    parts = [jnp.broadcast_to(x[g * group + row:g * group + row + 1, :], (group, width))
             for g in range(x.shape[0] // group)]
    return parts[0] if len(parts) == 1 else jnp.concatenate(parts, axis=0)


_HALF_ROW_LEVELS = 4
LOG2_E = 1.4426950408889634


def _halves(x, m, which):
    parts = [x[(2 * g + which) * m:(2 * g + which + 1) * m] for g in range(x.shape[0] // (2 * m))]
    return parts[0] if len(parts) == 1 else jnp.concatenate(parts, axis=0)


def _unhalve(xh, m, which):
    zero = jnp.zeros((m, xh.shape[1]), xh.dtype)
    parts = []
    for g in range(xh.shape[0] // m):
        blk = xh[g * m:(g + 1) * m]
        parts += [zero, blk] if which else [blk, zero]
    return jnp.concatenate(parts, axis=0)


def _gla_chunk_terms(q, k, la, lhs2, level, n_lev, seg):
    la2 = la * LOG2_E
    hi = la2.astype(BF16)
    lo = (la2 - hi.astype(F32)).astype(BF16)
    sums = _dot(lhs2, jnp.concatenate([hi, lo], axis=0))
    b = sums[0:CHUNK]
    b_last = _group_row(b, seg, seg - 1)
    q_main = (q * jnp.exp2(b)).astype(BF16)
    k_upd = k * jnp.exp2(b_last - b)
    q_lev = [None] * n_lev
    k_lev = [None] * n_lev
    for d in range(n_lev):
        m = 1 << d
        if d < _MATMUL_LEVELS:
            ref = sums[(1 + d) * CHUNK:(2 + d) * CHUNK]
        elif d < _HALF_ROW_LEVELS:
            ref = _group_row(b, 2 * m, m - 1)
        if d < _HALF_ROW_LEVELS:
            q_lev[d] = (q * jnp.exp2(b - ref)).astype(BF16)
            k_lev[d] = (k * jnp.exp2(ref - b)).astype(BF16)
        else:
            ref_h = _group_row(_halves(b, m, 0), m, m - 1)
            q_lev[d] = _unhalve((_halves(q, m, 1) * jnp.exp2(_halves(b, m, 1) - ref_h)).astype(BF16), m, 1)
            k_lev[d] = _unhalve((_halves(k, m, 0) * jnp.exp2(ref_h - _halves(b, m, 0))).astype(BF16), m, 0)
    q_b, k_b = q.astype(BF16), k.astype(BF16)

    def att(h):
        hs = slice(h * GLA_DK, (h + 1) * GLA_DK)
        acc = jnp.where(level == n_lev, _nt_dot(q_b[:, hs], k_b[:, hs]), 0.0)
        for d in range(n_lev):
            acc = jnp.where(level == d, _nt_dot(q_lev[d][:, hs], k_lev[d][:, hs]), acc)
        return acc

    return q_main, k_upd, att, b_last


def _gla_out(o, r, norm):
    parts = []
    for h in range(GLA_HEADS):
        oh = o[:, h * GLA_DV:(h + 1) * GLA_DV]
        ms = jnp.mean(oh * oh, axis=-1, keepdims=True)
        parts.append(oh * lax.rsqrt(ms + EPS))
    y = jnp.concatenate(parts, axis=1) * norm
    rf = r.astype(F32)
    return y * (rf * jax.nn.sigmoid(rf))


def _gla_prompt_kernel(n_lev, n_par, *refs):
    seq_refs = [refs[5 * j:5 * j + 5] for j in range(n_par)]
    norm_ref, lhs_ref, lev_ref = refs[5 * n_par:5 * n_par + 3]
    o_ref, sfin_ref, s_scr = refs[5 * n_par + 3:]
    c = pl.program_id(1)

    @pl.when(c == 0)
    def _():
        s_scr[...] = jnp.zeros_like(s_scr)

    for j, (q_ref, k_ref, la_ref, v_ref, r_ref) in enumerate(seq_refs):
        q_main, k_upd, att, b_last = _gla_chunk_terms(q_ref[...], k_ref[...], la_ref[...], lhs_ref[...],
                                                      lev_ref[...], n_lev, CHUNK)
        v = v_ref[...]
        outs = []
        for h in range(GLA_HEADS):
            hs = slice(h * GLA_DK, (h + 1) * GLA_DK)
            vh = v[:, h * GLA_DV:(h + 1) * GLA_DV]
            s0 = s_scr[j, h]
            o_h = _dot(q_main[:, hs], s0.astype(BF16)) + _dot(att(h).astype(BF16), vh)
            outs.append(o_h)
            decay = jnp.exp2(b_last[:, hs]).T
            k_t = k_upd[:, hs].T.astype(BF16)
            s_scr[j, h] = jnp.concatenate([decay, decay], axis=1) * s0 + _dot(k_t, vh)
        o_ref[j] = _gla_out(jnp.concatenate(outs, axis=1), r_ref[...], norm_ref[...]).astype(BF16)

    @pl.when(c == pl.num_programs(1) - 1)
    def _():
        sfin_ref[...] = s_scr[...]


def _gla_prompt(qg, kg, la, vg, rg, norm_row, batch, seq_len):
    nc = seq_len // CHUNK
    n_par = 4 if batch % 4 == 0 else (2 if batch % 2 == 0 else 1)
    lhs2, level, n_lev = _chunk_tables(CHUNK)
    tok = lambda j, w: pl.BlockSpec((CHUNK, w), lambda b, c: ((b * n_par + j) * nc + c, 0))
    seq_specs, seq_args = [], []
    for j in range(n_par):
        seq_specs += [tok(j, GLA_KDIM), tok(j, GLA_KDIM), tok(j, GLA_KDIM), tok(j, GLA_VDIM), tok(j, GLA_VDIM)]
        seq_args += [qg, kg, la, vg, rg]
    og, s_fin = pl.pallas_call(
        functools.partial(_gla_prompt_kernel, n_lev, n_par),
        grid=(batch // n_par, nc),
        in_specs=seq_specs + [_const_spec((1, GLA_VDIM)), _const_spec(lhs2.shape), _const_spec(level.shape)],
        out_specs=[pl.BlockSpec((n_par, CHUNK, GLA_VDIM), lambda b, c: (b, c, 0)),
                   pl.BlockSpec((n_par, GLA_HEADS, GLA_DK, GLA_DV), lambda b, c: (b, 0, 0, 0))],
        out_shape=[jax.ShapeDtypeStruct((batch, seq_len, GLA_VDIM), BF16),
                   jax.ShapeDtypeStruct((batch, GLA_HEADS, GLA_DK, GLA_DV), F32)],
        scratch_shapes=[pltpu.VMEM((n_par, GLA_HEADS, GLA_DK, GLA_DV), F32)],
        compiler_params=_params(("arbitrary", "arbitrary")),
        name="gla_prompt",
    )(*seq_args, norm_row, lhs2, level)
    return og.reshape(batch * seq_len, GLA_VDIM), s_fin


def _gla_sample_kernel(n_lev, dec_seq, q_ref, k_ref, la_ref, v_ref, r_ref, norm_ref, lhs_ref, lev_ref,
                       s0_ref, o_ref, snew_ref):
    q_main, k_upd, att, b_last = _gla_chunk_terms(q_ref[...], k_ref[...], la_ref[...], lhs_ref[...],
                                                  lev_ref[...], n_lev, dec_seq)
    v = v_ref[...]
    n_b = CHUNK // dec_seq
    row_b = lax.broadcasted_iota(jnp.int32, (CHUNK, GLA_DK), 0) // dec_seq
    col_b = lax.broadcasted_iota(jnp.int32, (GLA_DK, CHUNK), 1) // dec_seq
    outs = []
    for h in range(GLA_HEADS):
        hs = slice(h * GLA_DK, (h + 1) * GLA_DK)
        vh = v[:, h * GLA_DV:(h + 1) * GLA_DV]
        qm = q_main[:, hs]
        decay_t = jnp.exp2(b_last[:, hs]).T
        k_t = k_upd[:, hs].T.astype(BF16)
        o_h = _dot(att(h).astype(BF16), vh)
        for bi in range(n_b):
            s0 = s0_ref[bi, h]
            o_h = o_h + _dot(jnp.where(row_b == bi, qm, jnp.zeros_like(qm)), s0.astype(BF16))
            decay = jnp.broadcast_to(decay_t[:, bi * dec_seq:bi * dec_seq + 1], (GLA_DK, GLA_DV))
            k_b = jnp.where(col_b == bi, k_t, jnp.zeros_like(k_t))
            snew_ref[bi, h] = decay * s0 + _dot(k_b, vh)
        outs.append(o_h)
    o_ref[...] = _gla_out(jnp.concatenate(outs, axis=1), r_ref[...], norm_ref[...]).astype(o_ref.dtype)


def _gla_sample(qg, kg, la, vg, rg, norm_row, state, n_prompt_rows, dec_batch, dec_seq):
    n_b = CHUNK // dec_seq
    off = n_prompt_rows // CHUNK
    lhs3, level, n_lev = _chunk_tables(dec_seq)
    tok = lambda w: pl.BlockSpec((CHUNK, w), lambda g: (off + g, 0))
    st = pl.BlockSpec((n_b, GLA_HEADS, GLA_DK, GLA_DV), lambda g: (g, 0, 0, 0))
    return pl.pallas_call(
        functools.partial(_gla_sample_kernel, n_lev, dec_seq),
        grid=(dec_batch // n_b,),
        in_specs=[tok(GLA_KDIM), tok(GLA_KDIM), tok(GLA_KDIM), tok(GLA_VDIM), tok(GLA_VDIM),
                  _const_spec((1, GLA_VDIM)), _const_spec(lhs3.shape), _const_spec(level.shape), st],
        out_specs=[pl.BlockSpec((CHUNK, GLA_VDIM), lambda g: (g, 0)), st],
        out_shape=[jax.ShapeDtypeStruct((dec_batch * dec_seq, GLA_VDIM), BF16),
                   jax.ShapeDtypeStruct(state.shape, F32)],
        compiler_params=_params(("arbitrary",)),
        name="gla_sample",
    )(qg, kg, la, vg, rg, norm_row, lhs3, level, state)


def _merge_kernel(n_prompt_tiles, xp_ref, xs_ref, oap_ref, oas_ref, ogp_ref, ogs_ref, ga_ref, gb_ref,
                  wa_ref, wb_ref, wo_ref, x1_ref):
    i = pl.program_id(0)
    is_p = i < n_prompt_tiles
    x = jnp.where(is_p, xp_ref[...], xs_ref[...])
    oa = jnp.where(is_p, oap_ref[...], oas_ref[...])
    og = jnp.where(is_p, ogp_ref[...], ogs_ref[...])
    m = (jax.nn.sigmoid(ga_ref[...].astype(F32)) * _dot(oa, wa_ref[...])
         + jax.nn.sigmoid(gb_ref[...].astype(F32)) * _dot(og, wb_ref[...]))
    _store_token_tiles(x1_ref, x + _dot(m.astype(BF16), wo_ref[...]))


def _merge(xp2, xs2, oa_p, oa_s, og_p, og_s, ga, gb, wa, wb, wo):
    n_p, n_s = xp2.shape[0], xs2.shape[0]
    npt, nst = n_p // ROW_TILE, n_s // ROW_TILE
    p_map = lambda i: (jnp.minimum(i, npt - 1), 0)
    s_map = lambda i: (jnp.maximum(i - npt, 0), 0)
    row = lambda w: pl.BlockSpec((ROW_TILE, w), lambda i: (i, 0))
    return pl.pallas_call(
        functools.partial(_merge_kernel, npt),
        grid=(npt + nst,),
        in_specs=[
            pl.BlockSpec((ROW_TILE, D_MODEL), p_map), pl.BlockSpec((ROW_TILE, D_MODEL), s_map),
            pl.BlockSpec((ROW_TILE, ATT_WIDTH), p_map), pl.BlockSpec((ROW_TILE, ATT_WIDTH), s_map),
            pl.BlockSpec((ROW_TILE, GLA_VDIM), p_map), pl.BlockSpec((ROW_TILE, GLA_VDIM), s_map),
            row(D_MODEL), row(D_MODEL),
            _const_spec(wa.shape), _const_spec(wb.shape), _const_spec(wo.shape),
        ],
        out_specs=pl.BlockSpec((ROW_TILE * TILE_ROWS, LANES), lambda i: (i, 0)),
        out_shape=jax.ShapeDtypeStruct(((n_p + n_s) * TILE_ROWS, LANES), F32),
        compiler_params=_params(("arbitrary",)),
        name="merge",
    )(xp2, xs2, oa_p, oa_s, og_p, og_s, ga, gb, wa, wb, wo)


CHUNK_ROWS = 8
CHUNKS_PER_BLOCK = FFN_BLOCK // CHUNK_ROWS
TILE_PACK = TOP_K * ROW_TILE + N_EXPERTS * (CHUNK_ROWS - 1)
TILE_PACK += -TILE_PACK % CHUNK_ROWS


def _route_kernel(x1_ref, gffn_ref, rwh_ref, rwl_ref, rb_ref, upper_ref, lower_ref, eidx_ref, prel_ref, qloc_ref,
                  gate_ref, cnt_ref, cnt_scr):
    i = pl.program_id(0)

    @pl.when(i == 0)
    def _():
        cnt_scr[...] = jnp.zeros_like(cnt_scr)

    x1 = _load_token_tiles(x1_ref, ROW_TILE)
    ms = jnp.mean(x1 * x1, axis=-1, keepdims=True)
    h2 = x1 * lax.rsqrt(ms + EPS) * gffn_ref[...]
    h_hi = h2.astype(BF16)
    h_lo = (h2 - h_hi.astype(F32)).astype(BF16)
    rwh, rwl = rwh_ref[...], rwl_ref[...]
    logits = _nt_dot(rwh, h_hi) + _nt_dot(rwl, h_hi) + _nt_dot(rwh, h_lo) + rb_ref[...]
    eid = lax.broadcasted_iota(jnp.int32, logits.shape, 0)
    upper = upper_ref[...]
    seen = cnt_scr[...]
    vals, rows_e, hots, befores, counts = [], [], [], [], []
    lg = logits
    for _ in range(TOP_K):
        mx = jnp.max(lg, axis=0, keepdims=True)
        sel = jnp.min(jnp.where(lg == mx, eid, N_EXPERTS), axis=0, keepdims=True)
        onehot = eid == sel
        oh = onehot.astype(F32)
        hots.append(oh)
        befores.append(_dot(onehot.astype(BF16), upper))
        counts.append(jnp.sum(oh, axis=1, keepdims=True))
        vals.append(mx)
        rows_e.append(sel)
        lg = jnp.where(onehot, -jnp.inf, lg)
    run = (counts[0] + counts[1]) + (counts[2] + counts[3])
    run_pad = jnp.floor((run + (CHUNK_ROWS - 1)) * (1.0 / CHUNK_ROWS)) * CHUNK_ROWS
    tile_off = _dot(lower_ref[...], jnp.broadcast_to(run_pad, (N_EXPERTS, LANES)).astype(BF16))[:, 0:1]
    rows_p, rows_q = [], []
    ahead = jnp.zeros_like(run)
    for k in range(TOP_K):
        local = ahead + befores[k]
        rows_p.append(jnp.sum(hots[k] * (seen + local), axis=0, keepdims=True))
        rows_q.append(jnp.sum(hots[k] * (tile_off + local), axis=0, keepdims=True))
        ahead = ahead + counts[k]
    cnt_scr[...] = seen + run_pad
    ex = [jnp.exp(v - vals[0]) for v in vals]
    inv = 1.0 / (ex[0] + ex[1] + ex[2] + ex[3])
    eidx_ref[...] = jnp.concatenate(rows_e, axis=0)
    prel_ref[...] = jnp.concatenate(rows_p, axis=0).astype(jnp.int32)
    qloc_ref[...] = jnp.concatenate(rows_q, axis=0).astype(jnp.int32)
    gate_ref[...] = jnp.concatenate([e * inv for e in ex], axis=0)
    cnt_ref[...] = jnp.broadcast_to(seen + run_pad, cnt_ref.shape)


def _route(x1, g_ffn, rw_hi, rw_lo, rb_col):
    n = x1.shape[0] // TILE_ROWS
    upper = jnp.asarray(np.triu(np.ones((ROW_TILE, ROW_TILE), np.float32), 1), BF16)
    lower = jnp.asarray(np.tril(np.ones((N_EXPERTS, N_EXPERTS), np.float32), -1), BF16)
    col = lambda dt: (pl.BlockSpec((TOP_K, ROW_TILE), lambda i: (0, i)), jax.ShapeDtypeStruct((TOP_K, n), dt))
    outs = [col(jnp.int32), col(jnp.int32), col(jnp.int32), col(F32),
            (_resident_out_spec((N_EXPERTS, LANES)), jax.ShapeDtypeStruct((N_EXPERTS, LANES), F32))]
    return pl.pallas_call(
        _route_kernel,
        grid=(n // ROW_TILE,),
        in_specs=[pl.BlockSpec((ROW_TILE * TILE_ROWS, LANES), lambda i: (i, 0)), _const_spec((1, D_MODEL)),
                  _const_spec(rw_hi.shape), _const_spec(rw_lo.shape), _const_spec((N_EXPERTS, 1)),
                  _const_spec(upper.shape), _const_spec(lower.shape)],
        out_specs=[o[0] for o in outs],
        out_shape=[o[1] for o in outs],
        scratch_shapes=[pltpu.VMEM((N_EXPERTS, 1), F32)],
        compiler_params=_params(("arbitrary",)),
        name="route",
    )(x1, g_ffn, rw_hi, rw_lo, rb_col, upper, lower)


def _tile_rows(ref, row):
    return ref.at[pl.ds(pl.multiple_of(row * TILE_ROWS, TILE_ROWS), TILE_ROWS), :]


def _pack_kernel(q_ref, x1_ref, gffn_ref, o_ref, h_scr):
    o_ref[...] = jnp.zeros(o_ref.shape, o_ref.dtype)

    @pl.when(pl.program_id(0) < pl.num_programs(0) - 1)
    def _():
        x1 = _load_token_tiles(x1_ref, ROW_TILE)
        ms = jnp.mean(x1 * x1, axis=-1, keepdims=True)
        _store_token_tiles(h_scr, x1 * lax.rsqrt(ms + EPS) * gffn_ref[...])

        def body(t, carry):
            row = _tile_rows(h_scr, t)[...]
            for k in range(TOP_K):
                _tile_rows(o_ref, q_ref[0, 0, k * ROW_TILE + t])[...] = row
            return carry

        lax.fori_loop(0, ROW_TILE, body, 0, unroll=4)


def _pack(qloc_tiles, x1, g_ffn):
    n_tiles = qloc_tiles.shape[0]
    assert TILE_PACK >= 3 * FFN_BLOCK
    last = n_tiles - 1
    return pl.pallas_call(
        _pack_kernel,
        grid=(n_tiles + 1,),
        in_specs=[pl.BlockSpec((1, 1, TOP_K * ROW_TILE), lambda i: (jnp.minimum(i, last), 0, 0),
                               memory_space=pltpu.SMEM),
                  pl.BlockSpec((ROW_TILE * TILE_ROWS, LANES), lambda i: (jnp.minimum(i, last), 0)),
                  _const_spec((1, D_MODEL))],
        out_specs=pl.BlockSpec((TILE_PACK * TILE_ROWS, LANES), lambda i: (i, 0)),
        out_shape=jax.ShapeDtypeStruct(((n_tiles + 1) * TILE_PACK * TILE_ROWS, LANES), F32),
        scratch_shapes=[pltpu.VMEM((ROW_TILE * TILE_ROWS, LANES), F32)],
        compiler_params=_params(("arbitrary",)),
        name="pack",
    )(qloc_tiles, x1, g_ffn)


_INV_FIELDS = 3


def _invert_kernel(n_blocks, eidx_ref, prel_ref, qloc_ref, rstart_ref, acc_ref):
    i = pl.program_id(0)

    @pl.when(i == 0)
    def _():
        acc_ref[...] = jnp.zeros(acc_ref.shape, acc_ref.dtype)

    eidx, prel, qloc = eidx_ref[...], prel_ref[...], qloc_ref[...]
    rstart = rstart_ref[...]
    rows = eidx.shape[1]
    eid = lax.broadcasted_iota(jnp.int32, (N_EXPERTS, rows), 0)
    blk_id = lax.broadcasted_iota(jnp.int32, (n_blocks, rows), 0)
    off_id = lax.broadcasted_iota(jnp.int32, (LANES, rows), 0)
    blk_shift = FFN_BLOCK.bit_length() - 1
    chunk_shift = CHUNK_ROWS.bit_length() - 1
    a_parts, b_parts = [], []
    for k in range(TOP_K):
        base = jnp.sum(jnp.where(eid == eidx[k:k + 1], rstart, 0), axis=0, keepdims=True)
        pos = base + prel[k:k + 1]
        leader = (prel[k:k + 1] & (CHUNK_ROWS - 1)) == 0
        in_blk = (blk_id == lax.shift_right_logical(pos, blk_shift)) & leader
        a_parts.append(jnp.where(in_blk, 1.0, 0.0).astype(BF16))
        hit = off_id == (lax.shift_right_logical(pos, chunk_shift) & (CHUNKS_PER_BLOCK - 1))
        cid = lax.shift_right_logical(i * TILE_PACK + qloc[k:k + 1], chunk_shift)
        fields = [(cid & 255).astype(F32), lax.shift_right_logical(cid, 8).astype(F32), jnp.ones((1, rows), F32)]
        b_parts.append([jnp.where(hit, f, 0.0).astype(BF16) for f in fields])
    a = jnp.concatenate(a_parts, axis=1)
    for j in range(_INV_FIELDS):
        b = jnp.concatenate([b_parts[k][j] for k in range(TOP_K)], axis=1)
        acc_ref[j] += _nt_dot(a, b)


def _invert(eidx, prel, qloc, region_start, n_blocks, n_tiles):
    n_all = eidx.shape[1]
    n_chunks = n_tiles * TILE_PACK // CHUNK_ROWS
    assert n_chunks < 256 * 256, "chunk id is carried as two byte-sized fields"
    col = pl.BlockSpec((TOP_K, ROW_TILE), lambda i: (0, i))
    shape = (_INV_FIELDS, n_blocks, LANES)
    f = pl.pallas_call(
        functools.partial(_invert_kernel, n_blocks),
        grid=(n_all // ROW_TILE,),
        in_specs=[col, col, col, _const_spec((N_EXPERTS, 1))],
        out_specs=_resident_out_spec(shape),
        out_shape=jax.ShapeDtypeStruct(shape, F32),
        compiler_params=_params(("arbitrary",)),
        name="invert",
    )(eidx, prel, qloc, region_start)
    cid = (f[0] + 256.0 * f[1]).astype(jnp.int32)
    valid = f[2] > 0.0
    blk = lax.broadcasted_iota(jnp.int32, cid.shape, 0)
    off = lax.broadcasted_iota(jnp.int32, cid.shape, 1)
    dump = n_chunks + (blk & 1) * CHUNKS_PER_BLOCK + jnp.minimum(off, CHUNKS_PER_BLOCK - 1)
    return jnp.where(valid, cid, n_chunks + 3 * CHUNKS_PER_BLOCK), jnp.where(valid, cid, dump)


def _ffn_kernel(be_ref, nused_ref, src_cur_ref, src_nxt_ref, dst_cur_ref, dst_prev_ref,
                wup_ref, bup_ref, wdn_ref, bdn_ref, xts_hbm, out_hbm, xbuf0, xbuf1, obuf0, obuf1, wup_bf, wdn_bf,
                gsem, ssem):
    i = pl.program_id(0)
    n_used = nused_ref[0]
    xbufs, obufs = (xbuf0, xbuf1), (obuf0, obuf1)
    chunk_len = CHUNK_ROWS * TILE_ROWS

    def hbm_chunk(ref, cid):
        return ref.at[pl.ds(pl.multiple_of(cid * chunk_len, chunk_len), chunk_len), :]

    def vmem_chunk(ref, c):
        return ref.at[pl.ds(c * chunk_len, chunk_len), :]

    def gather_copy(src_ref, c, s):
        return pltpu.make_async_copy(hbm_chunk(xts_hbm, src_ref[0, 0, c]), vmem_chunk(xbufs[s], c), gsem.at[s])

    def scatter_copy(dst_ref, c, s):
        return pltpu.make_async_copy(vmem_chunk(obufs[s], c), hbm_chunk(out_hbm, dst_ref[0, 0, c]), ssem.at[s])

    GATHER_PRIORITY, SCATTER_PRIORITY = 1, 0

    def start_rows(copy_fn, priority):
        for c in range(CHUNKS_PER_BLOCK):
            copy_fn(c).start(priority=priority)

    def wait_rows(copy_fn):
        for c in range(CHUNKS_PER_BLOCK):
            copy_fn(c).wait()

    first = i == 0
    changed = first | (be_ref[i] != be_ref[jnp.maximum(i - 1, 0)])

    @pl.when(first)
    def _():
        obuf0[...] = jnp.zeros(obuf0.shape, obuf0.dtype)
        obuf1[...] = jnp.zeros(obuf1.shape, obuf1.dtype)
        spare0 = out_hbm.shape[0] // chunk_len - TILE_PACK // CHUNK_ROWS
        start_rows(lambda c: pltpu.make_async_copy(
            vmem_chunk(obuf0, c), out_hbm.at[pl.ds((spare0 + c) * chunk_len, chunk_len), :], ssem.at[0]),
            SCATTER_PRIORITY)
        start_rows(lambda c: gather_copy(src_cur_ref, c, 0), GATHER_PRIORITY)

    @pl.when(changed & (i < n_used))
    def _():
        wup_bf[...] = wup_ref[0].astype(BF16)
        wdn_bf[...] = wdn_ref[0].astype(BF16)

    def step(s):
        wait_rows(lambda r: gather_copy(src_cur_ref, r, s))
        x = _load_token_tiles(xbufs[s], FFN_BLOCK).astype(BF16)
        start_rows(lambda r: gather_copy(src_nxt_ref, r, 1 - s), GATHER_PRIORITY)
        start_rows(lambda r: scatter_copy(dst_prev_ref, r, 1 - s), SCATTER_PRIORITY)
        hu = _dot(x, wup_bf[...]) + bup_ref[0]
        glu = jnp.minimum(hu[:, :D_FF], SWIGLU_LIMIT)
        lin = jnp.clip(hu[:, D_FF:], -SWIGLU_LIMIT, SWIGLU_LIMIT)
        act = glu * jax.nn.sigmoid(SWIGLU_ALPHA * glu) * (lin + 1.0)
        out = _dot(act.astype(BF16), wdn_bf[...]) + bdn_ref[0]
        wait_rows(lambda r: scatter_copy(dst_cur_ref, r, s))
        _store_token_tiles(obufs[s], out)

    def drain(s):
        start_rows(lambda r: scatter_copy(dst_cur_ref, r, s), SCATTER_PRIORITY)
        wait_rows(lambda r: scatter_copy(dst_cur_ref, r, 1 - s))
        wait_rows(lambda r: scatter_copy(dst_cur_ref, r, s))
        wait_rows(lambda r: gather_copy(src_cur_ref, r, 1 - s))

    for s in range(2):
        pl.when((i < n_used) & (i % 2 == s))(functools.partial(step, s))
    for s in range(2):
        pl.when((i == n_used - 1) & (i % 2 == s))(functools.partial(drain, s))


def _ffn(block_expert, n_used, src, dst, x_ts, w_up, b_up, w_down, b_down):
    n_blocks = block_expert.shape[0]
    smem_block = lambda fn: pl.BlockSpec((1, 1, LANES), fn, memory_space=pltpu.SMEM)
    cur = smem_block(lambda i, be, nu: (i, 0, 0))
    nxt = smem_block(lambda i, be, nu: (jnp.minimum(i + 1, n_blocks - 1), 0, 0))
    prev = smem_block(lambda i, be, nu: (jnp.where(i == 0, n_blocks, i - 1), 0, 0))
    ex3 = lambda i, be, nu: (be[i], 0, 0)
    return pl.pallas_call(
        _ffn_kernel,
        grid_spec=pltpu.PrefetchScalarGridSpec(
            num_scalar_prefetch=2,
            grid=(n_blocks,),
            in_specs=[
                cur, nxt, cur, prev,
                pl.BlockSpec((1, D_MODEL, 2 * D_FF), ex3),
                pl.BlockSpec((1, 1, 2 * D_FF), ex3),
                pl.BlockSpec((1, D_FF, D_MODEL), ex3),
                pl.BlockSpec((1, 1, D_MODEL), ex3),
                pl.BlockSpec(memory_space=pl.ANY),
            ],
            out_specs=pl.BlockSpec(memory_space=pl.ANY),
            scratch_shapes=[pltpu.VMEM((FFN_BLOCK * TILE_ROWS, LANES), F32)] * 4 + [
                pltpu.VMEM((D_MODEL, 2 * D_FF), BF16), pltpu.VMEM((D_FF, D_MODEL), BF16),
                pltpu.SemaphoreType.DMA((2,)), pltpu.SemaphoreType.DMA((2,))],
        ),
        out_shape=jax.ShapeDtypeStruct(x_ts.shape, F32),
        input_output_aliases={10: 0},
        compiler_params=_params(("arbitrary",)),
        name="ffn",
    )(block_expert, n_used, src, src, dst, dst, w_up, b_up, w_down, b_down, x_ts)


def _final_kernel(q_ref, gate_ref, x1_ref, yts_ref, g_ref, y_ref, sum_scr):
    def body(t, carry):
        acc = _tile_rows(x1_ref, t)[...]
        for k in range(TOP_K):
            slot = k * ROW_TILE + t
            acc = acc + gate_ref[0, 0, slot] * _tile_rows(yts_ref, q_ref[0, 0, slot])[...]
        _tile_rows(sum_scr, t)[...] = acc
        return carry

    lax.fori_loop(0, ROW_TILE, body, 0, unroll=4)
    x2 = _load_token_tiles(sum_scr, ROW_TILE)
    ms = jnp.mean(x2 * x2, axis=-1, keepdims=True)
    y_ref[...] = x2 * lax.rsqrt(ms + EPS) * g_ref[...]


def _final(qloc_tiles, gate_tiles, x1, y_ts, g_final, row0, n_rows):
    t0 = row0 // ROW_TILE
    smem_tile = pl.BlockSpec((1, 1, TOP_K * ROW_TILE), lambda i: (t0 + i, 0, 0), memory_space=pltpu.SMEM)
    return pl.pallas_call(
        _final_kernel,
        grid=(n_rows // ROW_TILE,),
        in_specs=[smem_tile, smem_tile,
                  pl.BlockSpec((ROW_TILE * TILE_ROWS, LANES), lambda i: (t0 + i, 0)),
                  pl.BlockSpec((TILE_PACK * TILE_ROWS, LANES), lambda i: (t0 + i, 0)),
                  _const_spec((1, D_MODEL))],
        out_specs=pl.BlockSpec((ROW_TILE, D_MODEL), lambda i: (i, 0)),
        out_shape=jax.ShapeDtypeStruct((n_rows, D_MODEL), F32),
        scratch_shapes=[pltpu.VMEM((ROW_TILE * TILE_ROWS, LANES), F32)],
        compiler_params=_params(("arbitrary",)),
        name="final",
    )(qloc_tiles, gate_tiles, x1, y_ts, g_final)


def _rope_tables(seq_len, dec_seq):
    inv = ROPE_THETA ** (-np.arange(0, ROT_DIM, 2, dtype=np.float64) / ROT_DIM)
    pos = np.concatenate([np.arange(seq_len), PAST_LEN + np.arange(ROW_TILE) % dec_seq]).astype(np.float64)
    ang = pos[:, None] * inv[None, :]
    cos, sin = np.cos(ang), np.sin(ang)
    ones = np.ones((pos.shape[0], HEAD_DIM - ROT_DIM))
    cos_h = np.concatenate([cos, cos, ones], axis=1)
    sin_h = np.concatenate([-sin, sin, 0.0 * ones], axis=1)
    return (jnp.asarray(np.concatenate([cos_h, cos_h], axis=1), F32),
            jnp.asarray(np.concatenate([sin_h, sin_h], axis=1), F32))


def _block_tables(counts, n_all):
    max_rows = TOP_K * n_all + (n_all // ROW_TILE) * N_EXPERTS * (CHUNK_ROWS - 1)
    n_blocks = -(-max_rows // FFN_BLOCK) + N_EXPERTS
    padded = (counts + FFN_BLOCK - 1) // FFN_BLOCK * FFN_BLOCK
    pad_end = jnp.cumsum(padded)
    pad_start = (pad_end - padded).astype(jnp.int32).reshape(N_EXPERTS, 1)
    block_start = jnp.arange(n_blocks, dtype=jnp.int32) * FFN_BLOCK
    block_expert = jnp.minimum(jnp.sum(pad_end[None, :] <= block_start[:, None], axis=1), N_EXPERTS - 1)
    n_used = (pad_end[-1] // FFN_BLOCK).astype(jnp.int32).reshape(1)
    return pad_start, block_expert.astype(jnp.int32), n_used, n_blocks


def kernel(x_prompt, x_sample, cache_swa_k, cache_swa_v, state_gla, g_mix, w_in, w_gk_up, b_gk, sinks,
           gla_norm, w_branch_a, w_branch_b, w_out, g_ffn, router_w, router_b, w_up, b_up, w_down, b_down,
           g_final):
    batch, seq_len, _ = x_prompt.shape
    dec_batch, dec_seq, _ = x_sample.shape
    n_p, n_s = batch * seq_len, dec_batch * dec_seq
    n_all = n_p + n_s
    assert w_in.shape[0] == 1, "one layer: the final norm is fused after the only MoE"
    assert seq_len % ROW_TILE == 0 and n_s % ROW_TILE == 0 and ROW_TILE % dec_seq == 0
    assert dec_seq % SUBLANES == 0 and dec_batch % SAMPLE_GROUP == 0
    assert SAMPLE_GROUP * dec_seq == CHUNK and (dec_seq & (dec_seq - 1)) == 0

    xp2 = x_prompt.reshape(n_p, D_MODEL)
    xs2 = x_sample.reshape(n_s, D_MODEL)
    cos_tab, sin_tab = _rope_tables(seq_len, dec_seq)
    w = w_in[0]
    gl0 = _C_RG
    w_all = jnp.concatenate(
        [w[:, :gl0], w[:, gl0 + GK_RANK:], w[:, gl0:gl0 + GK_RANK],
         jnp.zeros((D_MODEL, LANES - GK_RANK), w.dtype)], axis=1).astype(BF16)
    wup_pad = jnp.concatenate([w_gk_up[0], jnp.zeros((LANES - GK_RANK, GLA_KDIM), F32)], axis=0).astype(BF16)
    qa, ka, va, qg, kg, vg, la, rg, ga, gb = _proj(
        xp2, xs2, g_mix[0].reshape(1, D_MODEL), cos_tab, sin_tab, w_all, wup_pad,
        b_gk[0].reshape(1, GLA_KDIM), seq_len)

    oa_p = _swa_prompt(sinks[0], qa, ka, va, batch, seq_len)
    oa_s, nk_s, nv_s = _swa_sample(sinks[0], qa, ka, va, cache_swa_k[0], cache_swa_v[0], n_p,
                                   dec_batch, dec_seq)
    norm_row = jnp.tile(gla_norm[0], GLA_HEADS).reshape(1, GLA_VDIM)
    og_p, s_fin = _gla_prompt(qg, kg, la, vg, rg, norm_row, batch, seq_len)
    og_s, s_new = _gla_sample(qg, kg, la, vg, rg, norm_row, state_gla[0], n_p, dec_batch, dec_seq)

    x1 = _merge(xp2, xs2, oa_p, oa_s, og_p, og_s, ga, gb, w_branch_a[0].astype(BF16),
                w_branch_b[0].astype(BF16), w_out[0].astype(BF16))

    g_ffn_row = g_ffn[0].reshape(1, D_MODEL)
    rw_t = router_w[0].T
    rw_hi = rw_t.astype(BF16)
    rw_lo = (rw_t - rw_hi.astype(F32)).astype(BF16)
    eidx, prel, qloc, gate, cnt = _route(x1, g_ffn_row, rw_hi, rw_lo, router_b[0].reshape(N_EXPERTS, 1))
    n_tiles = n_all // ROW_TILE
    region_start, block_expert, n_used, n_blocks = _block_tables(cnt[:, 0].astype(jnp.int32), n_all)
    per_tile = lambda a: a.reshape(TOP_K, n_tiles, ROW_TILE).transpose(1, 0, 2).reshape(n_tiles, 1, TOP_K * ROW_TILE)
    qloc_tiles, gate_tiles = per_tile(qloc), per_tile(gate)
    x_ts = _pack(qloc_tiles, x1, g_ffn_row)
    src, dst = _invert(eidx, prel, qloc, region_start, n_blocks, n_tiles)
    prime = (n_tiles * TILE_PACK // CHUNK_ROWS + 2 * CHUNKS_PER_BLOCK
             + jnp.minimum(jnp.arange(LANES, dtype=jnp.int32), CHUNKS_PER_BLOCK - 1)).reshape(1, LANES)
    dst = jnp.concatenate([dst, prime], axis=0)
    y_ts = _ffn(block_expert, n_used, src.reshape(n_blocks, 1, LANES), dst.reshape(n_blocks + 1, 1, LANES),
                x_ts, w_up[0], b_up[0].reshape(N_EXPERTS, 1, 2 * D_FF), w_down[0],
                b_down[0].reshape(N_EXPERTS, 1, D_MODEL))

    g_out = g_final.reshape(1, D_MODEL)
    y_p = _final(qloc_tiles, gate_tiles, x1, y_ts, g_out, 0, n_p)
    y_s = _final(qloc_tiles, gate_tiles, x1, y_ts, g_out, n_p, n_s)

    kv_shape = (1, -1, WINDOW, KV_HEADS, HEAD_DIM)
    new_k_p = ka[:n_p].reshape(batch, seq_len, KV_WIDTH)[:, -WINDOW:].reshape(kv_shape)
    new_v_p = va[:n_p].reshape(batch, seq_len, KV_WIDTH)[:, -WINDOW:].reshape(kv_shape)
    return (y_p.reshape(batch, seq_len, D_MODEL), y_s.reshape(dec_batch, dec_seq, D_MODEL),
            new_k_p, new_v_p, s_fin[None], nk_s.reshape(kv_shape), nv_s.reshape(kv_shape), s_new[None])
```

```python
import functools

import numpy as np
import jax
import jax.numpy as jnp
from jax import lax
from jax.experimental import pallas as pl
from jax.experimental.pallas import tpu as pltpu

D_MODEL = 1024
PAST_LEN = 8192
HEAD_DIM = 64
N_HEADS = 8
KV_HEADS = 2
GROUP = N_HEADS // KV_HEADS
WINDOW = 128
ROT_DIM = HEAD_DIM // 4
ROPE_THETA = 500000.0
ATT_WIDTH = N_HEADS * HEAD_DIM
KV_WIDTH = KV_HEADS * HEAD_DIM
GLA_HEADS = 4
GLA_KDIM = D_MODEL // 2
GLA_VDIM = D_MODEL
GLA_DK = GLA_KDIM // GLA_HEADS
GLA_DV = GLA_VDIM // GLA_HEADS
GK_RANK = 16
GK_NORMALIZER = 16.0
N_EXPERTS = 32
TOP_K = 4
D_FF = D_MODEL
SWIGLU_LIMIT = 7.0
SWIGLU_ALPHA = 1.702
EPS = 1e-5
NEG_INF = -1e30

LANES = 128
SUBLANES = 8
VMEM_LIMIT_BYTES = 56 * 1024 * 1024

ROW_TILE = 512
CHUNK = 128
FFN_BLOCK = 512

BF16 = jnp.bfloat16
F32 = jnp.float32

_C_QA, _C_KA, _C_VA, _C_QG, _C_KG, _C_VG, _C_GL = 0, 512, 640, 768, 1280, 1792, 2816


def _const_spec(shape):
    nd = len(shape)
    return pl.BlockSpec(shape, lambda *_: (0,) * nd, pipeline_mode=pl.Buffered(1))


def _resident_out_spec(shape):
    nd = len(shape)
    return pl.BlockSpec(shape, lambda *_: (0,) * nd)


def _params(sem, vmem=VMEM_LIMIT_BYTES):
    return pltpu.CompilerParams(dimension_semantics=sem, vmem_limit_bytes=vmem)


def _nt_dot(a, b):
    return lax.dot_general(a, b, (((1,), (1,)), ((), ())), preferred_element_type=F32)


def _dot(a, b):
    return jnp.dot(a, b, preferred_element_type=F32)


TILE_ROWS = D_MODEL // LANES
assert TILE_ROWS == SUBLANES


def _load_token_tiles(ref, n_tokens):
    return jnp.concatenate([ref[pl.ds(c, n_tokens, stride=TILE_ROWS), :] for c in range(TILE_ROWS)], axis=1)


def _store_token_tiles(ref, x):
    for c in range(TILE_ROWS):
        ref[pl.ds(c, x.shape[0], stride=TILE_ROWS), :] = x[:, c * LANES:(c + 1) * LANES]


def _rope(x, cos_t, sin_t, n_rep):
    width = x.shape[1]
    cos_f = jnp.concatenate([cos_t] * n_rep, axis=1) if n_rep > 1 else cos_t
    sin_f = jnp.concatenate([sin_t] * n_rep, axis=1) if n_rep > 1 else sin_t
    lane = lax.broadcasted_iota(jnp.int32, x.shape, 1) % HEAD_DIM
    up = pltpu.roll(x, width - ROT_DIM // 2, 1)
    down = pltpu.roll(x, ROT_DIM // 2, 1)
    partner = jnp.where(lane < ROT_DIM // 2, up, down)
    return x * cos_f + partner * sin_f


def _proj_kernel(n_prompt_tiles, xp_ref, xs_ref, g_ref, cos_ref, sin_ref, wa_ref, wb_ref, wgk_ref, wup_ref, bgk_ref,
                 qa_ref, ka_ref, va_ref, qg_ref, kg_ref, vg_ref, la_ref, rg_ref, ga_ref, gb_ref):
    i = pl.program_id(0)
    x = jnp.where(i < n_prompt_tiles, xp_ref[...], xs_ref[...])
    ms = jnp.mean(x * x, axis=-1, keepdims=True)
    h = (x * lax.rsqrt(ms + EPS) * g_ref[...]).astype(BF16)
    cos_t = cos_ref[...]
    sin_t = sin_ref[...]

    def seg(w_ref, a, b):
        return _dot(h, w_ref[:, a:b])

    qa = _rope(seg(wa_ref, _C_QA, _C_KA), cos_t, sin_t, ATT_WIDTH // LANES)
    qa_ref[...] = (qa * (HEAD_DIM ** -0.5)).astype(BF16)
    ka_ref[...] = _rope(seg(wa_ref, _C_KA, _C_VA), cos_t, sin_t, 1)
    va_ref[...] = seg(wa_ref, _C_VA, _C_QG)
    qg_ref[...] = seg(wa_ref, _C_QG, _C_KG) * (GLA_DK ** -0.5)
    kg_ref[...] = seg(wa_ref, _C_KG, _C_VG)
    vg_ref[...] = seg(wa_ref, _C_VG, _C_GL).astype(BF16)
    rg_ref[...] = seg(wb_ref, 0, GLA_VDIM).astype(BF16)
    ga_ref[...] = seg(wb_ref, GLA_VDIM, GLA_VDIM + D_MODEL).astype(BF16)
    gb_ref[...] = seg(wb_ref, GLA_VDIM + D_MODEL, GLA_VDIM + 2 * D_MODEL).astype(BF16)
    gk_low = _dot(h, wgk_ref[...]).astype(BF16)
    z = _dot(gk_low, wup_ref[...]) + bgk_ref[...]
    log_sig = jnp.minimum(z, 0.0) - jnp.log1p(jnp.exp(-jnp.abs(z)))
    la_ref[...] = log_sig / GK_NORMALIZER


def _proj(xp2, xs2, g_mix, cos_tab, sin_tab, w_a, w_b, w_gk, wup_pad, b_gk, seq_len):
    n_p, n_s = xp2.shape[0], xs2.shape[0]
    n_all = n_p + n_s
    npt, nst = n_p // ROW_TILE, n_s // ROW_TILE
    tiles_per_seq = seq_len // ROW_TILE

    def tab_map(i):
        return (jnp.where(i < npt, i % tiles_per_seq, tiles_per_seq), 0)

    row = lambda w: pl.BlockSpec((ROW_TILE, w), lambda i: (i, 0))
    widths = [(ATT_WIDTH, BF16), (KV_WIDTH, F32), (KV_WIDTH, F32), (GLA_KDIM, F32), (GLA_KDIM, F32),
              (GLA_VDIM, BF16), (GLA_KDIM, F32), (GLA_VDIM, BF16), (D_MODEL, BF16), (D_MODEL, BF16)]
    return pl.pallas_call(
        functools.partial(_proj_kernel, npt),
        grid=(npt + nst,),
        in_specs=[
            pl.BlockSpec((ROW_TILE, D_MODEL), lambda i: (jnp.minimum(i, npt - 1), 0)),
            pl.BlockSpec((ROW_TILE, D_MODEL), lambda i: (jnp.maximum(i - npt, 0), 0)),
            _const_spec((1, D_MODEL)),
            pl.BlockSpec((ROW_TILE, LANES), tab_map),
            pl.BlockSpec((ROW_TILE, LANES), tab_map),
            _const_spec(w_a.shape), _const_spec(w_b.shape), _const_spec(w_gk.shape),
            _const_spec(wup_pad.shape),
            _const_spec((1, GLA_KDIM)),
        ],
        out_specs=[row(w) for w, _ in widths],
        out_shape=[jax.ShapeDtypeStruct((n_all, w), dt) for w, dt in widths],
        compiler_params=_params(("arbitrary",)),
        name="proj",
    )(xp2, xs2, g_mix, cos_tab, sin_tab, w_a, w_b, w_gk, wup_pad, b_gk)


def _pair_blocks(kk):
    lane = lax.broadcasted_iota(jnp.int32, kk.shape, 1)
    lo = lane < HEAD_DIM
    swapped = pltpu.roll(kk, HEAD_DIM, 1)
    zero = jnp.zeros_like(kk)
    blocks = []
    for kh in range(KV_HEADS):
        left = jnp.where(lo, kk if kh == 0 else swapped, zero)
        right = jnp.where(lo, zero, swapped if kh == 0 else kk)
        blocks.append(jnp.concatenate([left, right], axis=0).astype(BF16))
    return blocks


def _sink_softmax(s, valid, sink):
    s = jnp.where(valid, s, NEG_INF)
    m = jnp.maximum(jnp.max(s, axis=-1, keepdims=True), sink)
    p = jnp.exp(s - m)
    denom = jnp.sum(p, axis=-1, keepdims=True) + jnp.exp(sink - m)
    return p.astype(BF16), 1.0 / denom


def _attend(q, kk, vv, valid, sink_ref, o_ref, row0=0):
    rows, keys = valid.shape
    kblocks = _pair_blocks(kk)
    vblocks = _pair_blocks(vv)
    for kh in range(KV_HEADS):
        base = kh * GROUP * HEAD_DIM
        qq = jnp.concatenate([q[:, base:base + LANES], q[:, base + LANES:base + 2 * LANES]], axis=0)
        s = _nt_dot(qq, kblocks[kh])
        first_head = lax.broadcasted_iota(jnp.int32, (rows, LANES), 1) < HEAD_DIM
        for r in range(2):
            probs, scales = [], []
            for c in range(2):
                head = kh * GROUP + 2 * r + c
                p_c, inv_c = _sink_softmax(s[r * rows:(r + 1) * rows, c * keys:(c + 1) * keys],
                                           valid, sink_ref[head])
                probs.append(p_c)
                scales.append(inv_c)
            p = jnp.concatenate(probs, axis=1)
            scale = jnp.where(first_head, scales[0], scales[1])
            o_ref[pl.ds(row0, rows), base + r * LANES:base + (r + 1) * LANES] = (
                _dot(p, vblocks[kh]) * scale).astype(o_ref.dtype)


def _swa_prompt_kernel(sink_ref, q_ref, kp_ref, k0_ref, k1_ref, vp_ref, v0_ref, v1_ref, o_ref):
    j = pl.program_id(1)
    row = lax.broadcasted_iota(jnp.int32, (WINDOW, 2 * WINDOW), 0)
    col = lax.broadcasted_iota(jnp.int32, (WINDOW, 2 * WINDOW), 1)
    band = (col > row) & (col <= row + WINDOW)
    k_blocks = (kp_ref[...], k0_ref[...], k1_ref[...])
    v_blocks = (vp_ref[...], v0_ref[...], v1_ref[...])
    q = q_ref[...]
    for half in range(2):
        kk = jnp.concatenate(k_blocks[half:half + 2], axis=0)
        vv = jnp.concatenate(v_blocks[half:half + 2], axis=0)
        valid = band & ((j > 0) | (col >= WINDOW)) if half == 0 else band
        _attend(q[half * WINDOW:(half + 1) * WINDOW], kk, vv, valid, sink_ref, o_ref, half * WINDOW)


def _swa_prompt(sinks, qa, ka, va, batch, seq_len):
    nb = seq_len // WINDOW
    assert nb % 2 == 0
    kv = lambda off: pl.BlockSpec((WINDOW, KV_WIDTH), lambda b, j, s: (b * nb + jnp.maximum(2 * j + off, 0), 0))
    pair = pl.BlockSpec((2 * WINDOW, ATT_WIDTH), lambda b, j, s: (b * (nb // 2) + j, 0))
    return pl.pallas_call(
        _swa_prompt_kernel,
        grid_spec=pltpu.PrefetchScalarGridSpec(
            num_scalar_prefetch=1,
            grid=(batch, nb // 2),
            in_specs=[pair, kv(-1), kv(0), kv(1), kv(-1), kv(0), kv(1)],
            out_specs=pair,
        ),
        out_shape=jax.ShapeDtypeStruct((batch * seq_len, ATT_WIDTH), BF16),
        compiler_params=_params(("arbitrary", "arbitrary")),
        name="swa_prompt",
    )(sinks, qa, ka, ka, ka, va, va, va)


SAMPLE_GROUP = 16


def _swa_sample_kernel(dec_seq, sink_ref, q_ref, kn_ref, vn_ref, ck_ref, cv_ref, o_ref, nk_ref, nv_ref):
    rows = SAMPLE_GROUP * dec_seq
    ck = ck_ref[...]
    cv = cv_ref[...]
    kn = kn_ref[...]
    vn = vn_ref[...]
    nk_ref[:, :WINDOW - dec_seq, :] = ck[:, dec_seq:, :]
    nv_ref[:, :WINDOW - dec_seq, :] = cv[:, dec_seq:, :]
    nk_ref[:, WINDOW - dec_seq:, :] = kn.reshape(SAMPLE_GROUP, dec_seq, KV_WIDTH)
    nv_ref[:, WINDOW - dec_seq:, :] = vn.reshape(SAMPLE_GROUP, dec_seq, KV_WIDTH)
    n_cache = SAMPLE_GROUP * WINDOW
    kk = jnp.concatenate([ck.reshape(n_cache, KV_WIDTH), kn], axis=0)
    vv = jnp.concatenate([cv.reshape(n_cache, KV_WIDTH), vn], axis=0)
    keys = n_cache + rows
    row = lax.broadcasted_iota(jnp.int32, (rows, keys), 0)
    col = lax.broadcasted_iota(jnp.int32, (rows, keys), 1)
    q_b, q_s = row // dec_seq, row % dec_seq
    is_cache = col < n_cache
    new = col - n_cache
    valid_cache = (col // WINDOW == q_b) & (col % WINDOW > q_s)
    valid_new = (new // dec_seq == q_b) & (new % dec_seq <= q_s)
    valid = (is_cache & valid_cache) | (jnp.logical_not(is_cache) & valid_new)
    _attend(q_ref[...], kk, vv, valid, sink_ref, o_ref)


def _swa_sample(sinks, qa, ka, va, cache_k, cache_v, n_prompt_rows, dec_batch, dec_seq):
    rows = SAMPLE_GROUP * dec_seq
    off = n_prompt_rows // rows
    tok = lambda g, s: (off + g, 0)
    cache = lambda g, s: (g, 0, 0)
    cshape = (dec_batch, WINDOW, KV_WIDTH)
    return pl.pallas_call(
        functools.partial(_swa_sample_kernel, dec_seq),
        grid_spec=pltpu.PrefetchScalarGridSpec(
            num_scalar_prefetch=1,
            grid=(dec_batch // SAMPLE_GROUP,),
            in_specs=[
                pl.BlockSpec((rows, ATT_WIDTH), tok),
                pl.BlockSpec((rows, KV_WIDTH), tok),
                pl.BlockSpec((rows, KV_WIDTH), tok),
                pl.BlockSpec((SAMPLE_GROUP, WINDOW, KV_WIDTH), cache),
                pl.BlockSpec((SAMPLE_GROUP, WINDOW, KV_WIDTH), cache),
            ],
            out_specs=[
                pl.BlockSpec((rows, ATT_WIDTH), lambda g, s: (g, 0)),
                pl.BlockSpec((SAMPLE_GROUP, WINDOW, KV_WIDTH), cache),
                pl.BlockSpec((SAMPLE_GROUP, WINDOW, KV_WIDTH), cache),
            ],
        ),
        out_shape=[jax.ShapeDtypeStruct((dec_batch * dec_seq, ATT_WIDTH), BF16),
                   jax.ShapeDtypeStruct(cshape, F32), jax.ShapeDtypeStruct(cshape, F32)],
        compiler_params=_params(("arbitrary",)),
        name="swa_sample",
    )(sinks, qa, ka, va, cache_k.reshape(cshape), cache_v.reshape(cshape))


def _chunk_tables(seg):
    n_lev = int(np.log2(seg))
    t = np.arange(CHUNK)
    seg_start = (t // seg) * seg
    u = np.arange(CHUNK)[None, :]

    def prefix(end):
        return ((u >= seg_start[:, None]) & (u <= end[:, None])).astype(np.float32)

    blocks = [prefix(t)]
    for d in range(min(n_lev, _MATMUL_LEVELS)):
        m = 1 << d
        ref = (t >> (d + 1) << (d + 1)) + m - 1
        blocks.append(prefix(ref))
    lhs = np.concatenate(blocks, axis=0)
    lhs2 = np.concatenate([lhs, lhs], axis=1)
    tt, ss = t[:, None], t[None, :]
    x = tt ^ ss
    lev = np.where(x > 0, np.floor(np.log2(np.maximum(x, 1))).astype(np.int32), n_lev)
    lev = np.where((ss > tt) | (tt // seg != ss // seg), -1, lev)
    lev = np.where(tt == ss, n_lev, lev)
    return jnp.asarray(lhs2, BF16), jnp.asarray(lev, jnp.int32), n_lev


_MATMUL_LEVELS = 3


def _group_row(x, group, row):
    width = x.shape[1]
    parts = [jnp.broadcast_to(x[g * group + row:g * group + row + 1, :], (group, width))
             for g in range(x.shape[0] // group)]
    return parts[0] if len(parts) == 1 else jnp.concatenate(parts, axis=0)


_HALF_ROW_LEVELS = 4
LOG2_E = 1.4426950408889634


def _halves(x, m, which):
    parts = [x[(2 * g + which) * m:(2 * g + which + 1) * m] for g in range(x.shape[0] // (2 * m))]
    return parts[0] if len(parts) == 1 else jnp.concatenate(parts, axis=0)


def _unhalve(xh, m, which):
    zero = jnp.zeros((m, xh.shape[1]), xh.dtype)
    parts = []
    for g in range(xh.shape[0] // m):
        blk = xh[g * m:(g + 1) * m]
        parts += [zero, blk] if which else [blk, zero]
    return jnp.concatenate(parts, axis=0)


def _gla_chunk_terms(q, k, la, lhs2, level, n_lev, seg):
    la2 = la * LOG2_E
    hi = la2.astype(BF16)
    lo = (la2 - hi.astype(F32)).astype(BF16)
    sums = _dot(lhs2, jnp.concatenate([hi, lo], axis=0))
    b = sums[0:CHUNK]
    b_last = _group_row(b, seg, seg - 1)
    q_main = (q * jnp.exp2(b)).astype(BF16)
    k_upd = k * jnp.exp2(b_last - b)
    q_lev = [None] * n_lev
    k_lev = [None] * n_lev
    for d in range(n_lev):
        m = 1 << d
        if d < _MATMUL_LEVELS:
            ref = sums[(1 + d) * CHUNK:(2 + d) * CHUNK]
        elif d < _HALF_ROW_LEVELS:
            ref = _group_row(b, 2 * m, m - 1)
        if d < _HALF_ROW_LEVELS:
            q_lev[d] = (q * jnp.exp2(b - ref)).astype(BF16)
            k_lev[d] = (k * jnp.exp2(ref - b)).astype(BF16)
        else:
            ref_h = _group_row(_halves(b, m, 0), m, m - 1)
            q_lev[d] = _unhalve((_halves(q, m, 1) * jnp.exp2(_halves(b, m, 1) - ref_h)).astype(BF16), m, 1)
            k_lev[d] = _unhalve((_halves(k, m, 0) * jnp.exp2(ref_h - _halves(b, m, 0))).astype(BF16), m, 0)
    q_b, k_b = q.astype(BF16), k.astype(BF16)

    def att(h):
        hs = slice(h * GLA_DK, (h + 1) * GLA_DK)
        acc = jnp.where(level == n_lev, _nt_dot(q_b[:, hs], k_b[:, hs]), 0.0)
        for d in range(n_lev):
            acc = jnp.where(level == d, _nt_dot(q_lev[d][:, hs], k_lev[d][:, hs]), acc)
        return acc

    return q_main, k_upd, att, b_last


def _gla_out(o, r, norm):
    parts = []
    for h in range(GLA_HEADS):
        oh = o[:, h * GLA_DV:(h + 1) * GLA_DV]
        ms = jnp.mean(oh * oh, axis=-1, keepdims=True)
        parts.append(oh * lax.rsqrt(ms + EPS))
    y = jnp.concatenate(parts, axis=1) * norm
    rf = r.astype(F32)
    return y * (rf * jax.nn.sigmoid(rf))


def _gla_prompt_kernel(n_lev, n_par, *refs):
    seq_refs = [refs[5 * j:5 * j + 5] for j in range(n_par)]
    norm_ref, lhs_ref, lev_ref = refs[5 * n_par:5 * n_par + 3]
    o_ref, sfin_ref, s_scr = refs[5 * n_par + 3:]
    c = pl.program_id(1)

    @pl.when(c == 0)
    def _():
        s_scr[...] = jnp.zeros_like(s_scr)

    for j, (q_ref, k_ref, la_ref, v_ref, r_ref) in enumerate(seq_refs):
        q_main, k_upd, att, b_last = _gla_chunk_terms(q_ref[...], k_ref[...], la_ref[...], lhs_ref[...],
                                                      lev_ref[...], n_lev, CHUNK)
        v = v_ref[...]
        outs = []
        for h in range(GLA_HEADS):
            hs = slice(h * GLA_DK, (h + 1) * GLA_DK)
            vh = v[:, h * GLA_DV:(h + 1) * GLA_DV]
            s0 = s_scr[j, h]
            o_h = _dot(q_main[:, hs], s0.astype(BF16)) + _dot(att(h).astype(BF16), vh)
            outs.append(o_h)
            decay = jnp.exp2(b_last[:, hs]).T
            k_t = k_upd[:, hs].T.astype(BF16)
            s_scr[j, h] = jnp.concatenate([decay, decay], axis=1) * s0 + _dot(k_t, vh)
        o_ref[j] = _gla_out(jnp.concatenate(outs, axis=1), r_ref[...], norm_ref[...]).astype(BF16)

    @pl.when(c == pl.num_programs(1) - 1)
    def _():
        sfin_ref[...] = s_scr[...]


def _gla_prompt(qg, kg, la, vg, rg, norm_row, batch, seq_len):
    nc = seq_len // CHUNK
    n_par = 4 if batch % 4 == 0 else (2 if batch % 2 == 0 else 1)
    lhs2, level, n_lev = _chunk_tables(CHUNK)
    tok = lambda j, w: pl.BlockSpec((CHUNK, w), lambda b, c: ((b * n_par + j) * nc + c, 0))
    seq_specs, seq_args = [], []
    for j in range(n_par):
        seq_specs += [tok(j, GLA_KDIM), tok(j, GLA_KDIM), tok(j, GLA_KDIM), tok(j, GLA_VDIM), tok(j, GLA_VDIM)]
        seq_args += [qg, kg, la, vg, rg]
    og, s_fin = pl.pallas_call(
        functools.partial(_gla_prompt_kernel, n_lev, n_par),
        grid=(batch // n_par, nc),
        in_specs=seq_specs + [_const_spec((1, GLA_VDIM)), _const_spec(lhs2.shape), _const_spec(level.shape)],
        out_specs=[pl.BlockSpec((n_par, CHUNK, GLA_VDIM), lambda b, c: (b, c, 0)),
                   pl.BlockSpec((n_par, GLA_HEADS, GLA_DK, GLA_DV), lambda b, c: (b, 0, 0, 0))],
        out_shape=[jax.ShapeDtypeStruct((batch, seq_len, GLA_VDIM), BF16),
                   jax.ShapeDtypeStruct((batch, GLA_HEADS, GLA_DK, GLA_DV), F32)],
        scratch_shapes=[pltpu.VMEM((n_par, GLA_HEADS, GLA_DK, GLA_DV), F32)],
        compiler_params=_params(("arbitrary", "arbitrary")),
        name="gla_prompt",
    )(*seq_args, norm_row, lhs2, level)
    return og.reshape(batch * seq_len, GLA_VDIM), s_fin


def _gla_sample_kernel(n_lev, dec_seq, q_ref, k_ref, la_ref, v_ref, r_ref, norm_ref, lhs_ref, lev_ref,
                       s0_ref, o_ref, snew_ref):
    q_main, k_upd, att, b_last = _gla_chunk_terms(q_ref[...], k_ref[...], la_ref[...], lhs_ref[...],
                                                  lev_ref[...], n_lev, dec_seq)
    v = v_ref[...]
    n_b = CHUNK // dec_seq
    row_b = lax.broadcasted_iota(jnp.int32, (CHUNK, GLA_DK), 0) // dec_seq
    col_b = lax.broadcasted_iota(jnp.int32, (GLA_DK, CHUNK), 1) // dec_seq
    outs = []
    for h in range(GLA_HEADS):
        hs = slice(h * GLA_DK, (h + 1) * GLA_DK)
        vh = v[:, h * GLA_DV:(h + 1) * GLA_DV]
        qm = q_main[:, hs]
        decay_t = jnp.exp2(b_last[:, hs]).T
        k_t = k_upd[:, hs].T.astype(BF16)
        o_h = _dot(att(h).astype(BF16), vh)
        for bi in range(n_b):
            s0 = s0_ref[bi, h]
            o_h = o_h + _dot(jnp.where(row_b == bi, qm, jnp.zeros_like(qm)), s0.astype(BF16))
            decay = jnp.broadcast_to(decay_t[:, bi * dec_seq:bi * dec_seq + 1], (GLA_DK, GLA_DV))
            k_b = jnp.where(col_b == bi, k_t, jnp.zeros_like(k_t))
            snew_ref[bi, h] = decay * s0 + _dot(k_b, vh)
        outs.append(o_h)
    o_ref[...] = _gla_out(jnp.concatenate(outs, axis=1), r_ref[...], norm_ref[...]).astype(o_ref.dtype)


def _gla_sample(qg, kg, la, vg, rg, norm_row, state, n_prompt_rows, dec_batch, dec_seq):
    n_b = CHUNK // dec_seq
    off = n_prompt_rows // CHUNK
    lhs3, level, n_lev = _chunk_tables(dec_seq)
    tok = lambda w: pl.BlockSpec((CHUNK, w), lambda g: (off + g, 0))
    st = pl.BlockSpec((n_b, GLA_HEADS, GLA_DK, GLA_DV), lambda g: (g, 0, 0, 0))
    return pl.pallas_call(
        functools.partial(_gla_sample_kernel, n_lev, dec_seq),
        grid=(dec_batch // n_b,),
        in_specs=[tok(GLA_KDIM), tok(GLA_KDIM), tok(GLA_KDIM), tok(GLA_VDIM), tok(GLA_VDIM),
                  _const_spec((1, GLA_VDIM)), _const_spec(lhs3.shape), _const_spec(level.shape), st],
        out_specs=[pl.BlockSpec((CHUNK, GLA_VDIM), lambda g: (g, 0)), st],
        out_shape=[jax.ShapeDtypeStruct((dec_batch * dec_seq, GLA_VDIM), BF16),
                   jax.ShapeDtypeStruct(state.shape, F32)],
        compiler_params=_params(("arbitrary",)),
        name="gla_sample",
    )(qg, kg, la, vg, rg, norm_row, lhs3, level, state)


def _merge_kernel(n_prompt_tiles, xp_ref, xs_ref, oap_ref, oas_ref, ogp_ref, ogs_ref, ga_ref, gb_ref,
                  wa_ref, wb_ref, wo_ref, x1_ref):
    i = pl.program_id(0)
    is_p = i < n_prompt_tiles
    x = jnp.where(is_p, xp_ref[...], xs_ref[...])
    oa = jnp.where(is_p, oap_ref[...], oas_ref[...])
    og = jnp.where(is_p, ogp_ref[...], ogs_ref[...])
    m = (jax.nn.sigmoid(ga_ref[...].astype(F32)) * _dot(oa, wa_ref[...])
         + jax.nn.sigmoid(gb_ref[...].astype(F32)) * _dot(og, wb_ref[...]))
    _store_token_tiles(x1_ref, x + _dot(m.astype(BF16), wo_ref[...]))


def _merge(xp2, xs2, oa_p, oa_s, og_p, og_s, ga, gb, wa, wb, wo):
    n_p, n_s = xp2.shape[0], xs2.shape[0]
    npt, nst = n_p // ROW_TILE, n_s // ROW_TILE
    p_map = lambda i: (jnp.minimum(i, npt - 1), 0)
    s_map = lambda i: (jnp.maximum(i - npt, 0), 0)
    row = lambda w: pl.BlockSpec((ROW_TILE, w), lambda i: (i, 0))
    return pl.pallas_call(
        functools.partial(_merge_kernel, npt),
        grid=(npt + nst,),
        in_specs=[
            pl.BlockSpec((ROW_TILE, D_MODEL), p_map), pl.BlockSpec((ROW_TILE, D_MODEL), s_map),
            pl.BlockSpec((ROW_TILE, ATT_WIDTH), p_map), pl.BlockSpec((ROW_TILE, ATT_WIDTH), s_map),
            pl.BlockSpec((ROW_TILE, GLA_VDIM), p_map), pl.BlockSpec((ROW_TILE, GLA_VDIM), s_map),
            row(D_MODEL), row(D_MODEL),
            _const_spec(wa.shape), _const_spec(wb.shape), _const_spec(wo.shape),
        ],
        out_specs=pl.BlockSpec((ROW_TILE * TILE_ROWS, LANES), lambda i: (i, 0)),
        out_shape=jax.ShapeDtypeStruct(((n_p + n_s) * TILE_ROWS, LANES), F32),
        compiler_params=_params(("arbitrary",)),
        name="merge",
    )(xp2, xs2, oa_p, oa_s, og_p, og_s, ga, gb, wa, wb, wo)


CHUNK_ROWS = 8
CHUNKS_PER_BLOCK = FFN_BLOCK // CHUNK_ROWS
TILE_PACK = TOP_K * ROW_TILE + N_EXPERTS * (CHUNK_ROWS - 1)
TILE_PACK += -TILE_PACK % CHUNK_ROWS


def _route_kernel(x1_ref, gffn_ref, rwh_ref, rwl_ref, rb_ref, upper_ref, lower_ref, eidx_ref, prel_ref, qloc_ref,
                  gate_ref, cnt_ref, cnt_scr):
    i = pl.program_id(0)

    @pl.when(i == 0)
    def _():
        cnt_scr[...] = jnp.zeros_like(cnt_scr)

    x1 = _load_token_tiles(x1_ref, ROW_TILE)
    ms = jnp.mean(x1 * x1, axis=-1, keepdims=True)
    h2 = x1 * lax.rsqrt(ms + EPS) * gffn_ref[...]
    h_hi = h2.astype(BF16)
    h_lo = (h2 - h_hi.astype(F32)).astype(BF16)
    rwh, rwl = rwh_ref[...], rwl_ref[...]
    logits = _nt_dot(rwh, h_hi) + _nt_dot(rwl, h_hi) + _nt_dot(rwh, h_lo) + rb_ref[...]
    eid = lax.broadcasted_iota(jnp.int32, logits.shape, 0)
    upper = upper_ref[...]
    seen = cnt_scr[...]
    vals, rows_e, hots, befores, counts = [], [], [], [], []
    lg = logits
    for _ in range(TOP_K):
        mx = jnp.max(lg, axis=0, keepdims=True)
        sel = jnp.min(jnp.where(lg == mx, eid, N_EXPERTS), axis=0, keepdims=True)
        onehot = eid == sel
        oh = onehot.astype(F32)
        hots.append(oh)
        befores.append(_dot(onehot.astype(BF16), upper))
        counts.append(jnp.sum(oh, axis=1, keepdims=True))
        vals.append(mx)
        rows_e.append(sel)
        lg = jnp.where(onehot, -jnp.inf, lg)
    run = (counts[0] + counts[1]) + (counts[2] + counts[3])
    run_pad = jnp.floor((run + (CHUNK_ROWS - 1)) * (1.0 / CHUNK_ROWS)) * CHUNK_ROWS
    tile_off = _dot(lower_ref[...], jnp.broadcast_to(run_pad, (N_EXPERTS, LANES)).astype(BF16))[:, 0:1]
    rows_p, rows_q = [], []
    ahead = jnp.zeros_like(run)
    for k in range(TOP_K):
        local = ahead + befores[k]
        rows_p.append(jnp.sum(hots[k] * (seen + local), axis=0, keepdims=True))
        rows_q.append(jnp.sum(hots[k] * (tile_off + local), axis=0, keepdims=True))
        ahead = ahead + counts[k]
    cnt_scr[...] = seen + run_pad
    ex = [jnp.exp(v - vals[0]) for v in vals]
    inv = 1.0 / (ex[0] + ex[1] + ex[2] + ex[3])
    eidx_ref[...] = jnp.concatenate(rows_e, axis=0)
    prel_ref[...] = jnp.concatenate(rows_p, axis=0).astype(jnp.int32)
    qloc_ref[...] = jnp.concatenate(rows_q, axis=0).astype(jnp.int32)
    gate_ref[...] = jnp.concatenate([e * inv for e in ex], axis=0)
    cnt_ref[...] = jnp.broadcast_to(seen + run_pad, cnt_ref.shape)


def _route(x1, g_ffn, rw_hi, rw_lo, rb_col):
    n = x1.shape[0] // TILE_ROWS
    upper = jnp.asarray(np.triu(np.ones((ROW_TILE, ROW_TILE), np.float32), 1), BF16)
    lower = jnp.asarray(np.tril(np.ones((N_EXPERTS, N_EXPERTS), np.float32), -1), BF16)
    col = lambda dt: (pl.BlockSpec((TOP_K, ROW_TILE), lambda i: (0, i)), jax.ShapeDtypeStruct((TOP_K, n), dt))
    outs = [col(jnp.int32), col(jnp.int32), col(jnp.int32), col(F32),
            (_resident_out_spec((N_EXPERTS, LANES)), jax.ShapeDtypeStruct((N_EXPERTS, LANES), F32))]
    return pl.pallas_call(
        _route_kernel,
        grid=(n // ROW_TILE,),
        in_specs=[pl.BlockSpec((ROW_TILE * TILE_ROWS, LANES), lambda i: (i, 0)), _const_spec((1, D_MODEL)),
                  _const_spec(rw_hi.shape), _const_spec(rw_lo.shape), _const_spec((N_EXPERTS, 1)),
                  _const_spec(upper.shape), _const_spec(lower.shape)],
        out_specs=[o[0] for o in outs],
        out_shape=[o[1] for o in outs],
        scratch_shapes=[pltpu.VMEM((N_EXPERTS, 1), F32)],
        compiler_params=_params(("arbitrary",)),
        name="route",
    )(x1, g_ffn, rw_hi, rw_lo, rb_col, upper, lower)


def _tile_rows(ref, row):
    return ref.at[pl.ds(pl.multiple_of(row * TILE_ROWS, TILE_ROWS), TILE_ROWS), :]


def _pack_kernel(q_ref, x1_ref, gffn_ref, o_ref, h_scr):
    o_ref[...] = jnp.zeros(o_ref.shape, o_ref.dtype)

    @pl.when(pl.program_id(0) < pl.num_programs(0) - 1)
    def _():
        x1 = _load_token_tiles(x1_ref, ROW_TILE)
        ms = jnp.mean(x1 * x1, axis=-1, keepdims=True)
        _store_token_tiles(h_scr, x1 * lax.rsqrt(ms + EPS) * gffn_ref[...])

        def body(t, carry):
            row = _tile_rows(h_scr, t)[...]
            for k in range(TOP_K):
                _tile_rows(o_ref, q_ref[0, 0, k * ROW_TILE + t])[...] = row
            return carry

        lax.fori_loop(0, ROW_TILE, body, 0, unroll=4)


def _pack(qloc_tiles, x1, g_ffn):
    n_tiles = qloc_tiles.shape[0]
    assert TILE_PACK >= 3 * FFN_BLOCK
    last = n_tiles - 1
    return pl.pallas_call(
        _pack_kernel,
        grid=(n_tiles + 1,),
        in_specs=[pl.BlockSpec((1, 1, TOP_K * ROW_TILE), lambda i: (jnp.minimum(i, last), 0, 0),
                               memory_space=pltpu.SMEM),
                  pl.BlockSpec((ROW_TILE * TILE_ROWS, LANES), lambda i: (jnp.minimum(i, last), 0)),
                  _const_spec((1, D_MODEL))],
        out_specs=pl.BlockSpec((TILE_PACK * TILE_ROWS, LANES), lambda i: (i, 0)),
        out_shape=jax.ShapeDtypeStruct(((n_tiles + 1) * TILE_PACK * TILE_ROWS, LANES), F32),
        scratch_shapes=[pltpu.VMEM((ROW_TILE * TILE_ROWS, LANES), F32)],
        compiler_params=_params(("arbitrary",)),
        name="pack",
    )(qloc_tiles, x1, g_ffn)


_INV_FIELDS = 3


def _invert_kernel(n_blocks, eidx_ref, prel_ref, qloc_ref, rstart_ref, acc_ref):
    i = pl.program_id(0)

    @pl.when(i == 0)
    def _():
        acc_ref[...] = jnp.zeros(acc_ref.shape, acc_ref.dtype)

    eidx, prel, qloc = eidx_ref[...], prel_ref[...], qloc_ref[...]
    rstart = rstart_ref[...]
    rows = eidx.shape[1]
    eid = lax.broadcasted_iota(jnp.int32, (N_EXPERTS, rows), 0)
    blk_id = lax.broadcasted_iota(jnp.int32, (n_blocks, rows), 0)
    off_id = lax.broadcasted_iota(jnp.int32, (LANES, rows), 0)
    blk_shift = FFN_BLOCK.bit_length() - 1
    chunk_shift = CHUNK_ROWS.bit_length() - 1
    a_parts, b_parts = [], []
    for k in range(TOP_K):
        base = jnp.sum(jnp.where(eid == eidx[k:k + 1], rstart, 0), axis=0, keepdims=True)
        pos = base + prel[k:k + 1]
        leader = (prel[k:k + 1] & (CHUNK_ROWS - 1)) == 0
        in_blk = (blk_id == lax.shift_right_logical(pos, blk_shift)) & leader
        a_parts.append(jnp.where(in_blk, 1.0, 0.0).astype(BF16))
        hit = off_id == (lax.shift_right_logical(pos, chunk_shift) & (CHUNKS_PER_BLOCK - 1))
        cid = lax.shift_right_logical(i * TILE_PACK + qloc[k:k + 1], chunk_shift)
        fields = [(cid & 255).astype(F32), lax.shift_right_logical(cid, 8).astype(F32), jnp.ones((1, rows), F32)]
        b_parts.append([jnp.where(hit, f, 0.0).astype(BF16) for f in fields])
    a = jnp.concatenate(a_parts, axis=1)
    for j in range(_INV_FIELDS):
        b = jnp.concatenate([b_parts[k][j] for k in range(TOP_K)], axis=1)
        acc_ref[j] += _nt_dot(a, b)


def _invert(eidx, prel, qloc, region_start, n_blocks, n_tiles):
    n_all = eidx.shape[1]
    n_chunks = n_tiles * TILE_PACK // CHUNK_ROWS
    assert n_chunks < 256 * 256, "chunk id is carried as two byte-sized fields"
    col = pl.BlockSpec((TOP_K, ROW_TILE), lambda i: (0, i))
    shape = (_INV_FIELDS, n_blocks, LANES)
    f = pl.pallas_call(
        functools.partial(_invert_kernel, n_blocks),
        grid=(n_all // ROW_TILE,),
        in_specs=[col, col, col, _const_spec((N_EXPERTS, 1))],
        out_specs=_resident_out_spec(shape),
        out_shape=jax.ShapeDtypeStruct(shape, F32),
        compiler_params=_params(("arbitrary",)),
        name="invert",
    )(eidx, prel, qloc, region_start)
    cid = (f[0] + 256.0 * f[1]).astype(jnp.int32)
    valid = f[2] > 0.0
    blk = lax.broadcasted_iota(jnp.int32, cid.shape, 0)
    off = lax.broadcasted_iota(jnp.int32, cid.shape, 1)
    dump = n_chunks + (blk & 1) * CHUNKS_PER_BLOCK + jnp.minimum(off, CHUNKS_PER_BLOCK - 1)
    return jnp.where(valid, cid, n_chunks + 3 * CHUNKS_PER_BLOCK), jnp.where(valid, cid, dump)


def _ffn_kernel(be_ref, nused_ref, src_cur_ref, src_nxt_ref, dst_cur_ref, dst_prev_ref,
                wup_ref, bup_ref, wdn_ref, bdn_ref, xts_hbm, out_hbm, xbuf0, xbuf1, obuf0, obuf1, wup_bf, wdn_bf,
                gsem, ssem):
    i = pl.program_id(0)
    n_used = nused_ref[0]
    xbufs, obufs = (xbuf0, xbuf1), (obuf0, obuf1)
    chunk_len = CHUNK_ROWS * TILE_ROWS

    def hbm_chunk(ref, cid):
        return ref.at[pl.ds(pl.multiple_of(cid * chunk_len, chunk_len), chunk_len), :]

    def vmem_chunk(ref, c):
        return ref.at[pl.ds(c * chunk_len, chunk_len), :]

    def gather_copy(src_ref, c, s):
        return pltpu.make_async_copy(hbm_chunk(xts_hbm, src_ref[0, 0, c]), vmem_chunk(xbufs[s], c), gsem.at[s])

    def scatter_copy(dst_ref, c, s):
        return pltpu.make_async_copy(vmem_chunk(obufs[s], c), hbm_chunk(out_hbm, dst_ref[0, 0, c]), ssem.at[s])

    GATHER_PRIORITY, SCATTER_PRIORITY = 1, 0

    def start_rows(copy_fn, priority):
        for c in range(CHUNKS_PER_BLOCK):
            copy_fn(c).start(priority=priority)

    def wait_rows(copy_fn):
        for c in range(CHUNKS_PER_BLOCK):
            copy_fn(c).wait()

    first = i == 0
    changed = first | (be_ref[i] != be_ref[jnp.maximum(i - 1, 0)])

    @pl.when(first)
    def _():
        obuf0[...] = jnp.zeros(obuf0.shape, obuf0.dtype)
        obuf1[...] = jnp.zeros(obuf1.shape, obuf1.dtype)
        spare0 = out_hbm.shape[0] // chunk_len - TILE_PACK // CHUNK_ROWS
        start_rows(lambda c: pltpu.make_async_copy(
            vmem_chunk(obuf0, c), out_hbm.at[pl.ds((spare0 + c) * chunk_len, chunk_len), :], ssem.at[0]),
            SCATTER_PRIORITY)
        start_rows(lambda c: gather_copy(src_cur_ref, c, 0), GATHER_PRIORITY)

    @pl.when(changed & (i < n_used))
    def _():
        wup_bf[...] = wup_ref[0].astype(BF16)
        wdn_bf[...] = wdn_ref[0].astype(BF16)

    def step(s):
        wait_rows(lambda r: gather_copy(src_cur_ref, r, s))
        x = _load_token_tiles(xbufs[s], FFN_BLOCK).astype(BF16)
        start_rows(lambda r: gather_copy(src_nxt_ref, r, 1 - s), GATHER_PRIORITY)
        start_rows(lambda r: scatter_copy(dst_prev_ref, r, 1 - s), SCATTER_PRIORITY)
        hu = _dot(x, wup_bf[...]) + bup_ref[0]
        glu = jnp.minimum(hu[:, :D_FF], SWIGLU_LIMIT)
        lin = jnp.clip(hu[:, D_FF:], -SWIGLU_LIMIT, SWIGLU_LIMIT)
        act = glu * jax.nn.sigmoid(SWIGLU_ALPHA * glu) * (lin + 1.0)
        out = _dot(act.astype(BF16), wdn_bf[...]) + bdn_ref[0]
        wait_rows(lambda r: scatter_copy(dst_cur_ref, r, s))
        _store_token_tiles(obufs[s], out)

    def drain(s):
        start_rows(lambda r: scatter_copy(dst_cur_ref, r, s), SCATTER_PRIORITY)
        wait_rows(lambda r: scatter_copy(dst_cur_ref, r, 1 - s))
        wait_rows(lambda r: scatter_copy(dst_cur_ref, r, s))
        wait_rows(lambda r: gather_copy(src_cur_ref, r, 1 - s))

    for s in range(2):
        pl.when((i < n_used) & (i % 2 == s))(functools.partial(step, s))
    for s in range(2):
        pl.when((i == n_used - 1) & (i % 2 == s))(functools.partial(drain, s))


def _ffn(block_expert, n_used, src, dst, x_ts, w_up, b_up, w_down, b_down):
    n_blocks = block_expert.shape[0]
    smem_block = lambda fn: pl.BlockSpec((1, 1, LANES), fn, memory_space=pltpu.SMEM)
    cur = smem_block(lambda i, be, nu: (i, 0, 0))
    nxt = smem_block(lambda i, be, nu: (jnp.minimum(i + 1, n_blocks - 1), 0, 0))
    prev = smem_block(lambda i, be, nu: (jnp.where(i == 0, n_blocks, i - 1), 0, 0))
    ex3 = lambda i, be, nu: (be[i], 0, 0)
    return pl.pallas_call(
        _ffn_kernel,
        grid_spec=pltpu.PrefetchScalarGridSpec(
            num_scalar_prefetch=2,
            grid=(n_blocks,),
            in_specs=[
                cur, nxt, cur, prev,
                pl.BlockSpec((1, D_MODEL, 2 * D_FF), ex3),
                pl.BlockSpec((1, 1, 2 * D_FF), ex3),
                pl.BlockSpec((1, D_FF, D_MODEL), ex3),
                pl.BlockSpec((1, 1, D_MODEL), ex3),
                pl.BlockSpec(memory_space=pl.ANY),
            ],
            out_specs=pl.BlockSpec(memory_space=pl.ANY),
            scratch_shapes=[pltpu.VMEM((FFN_BLOCK * TILE_ROWS, LANES), F32)] * 4 + [
                pltpu.VMEM((D_MODEL, 2 * D_FF), BF16), pltpu.VMEM((D_FF, D_MODEL), BF16),
                pltpu.SemaphoreType.DMA((2,)), pltpu.SemaphoreType.DMA((2,))],
        ),
        out_shape=jax.ShapeDtypeStruct(x_ts.shape, F32),
        input_output_aliases={10: 0},
        compiler_params=_params(("arbitrary",)),
        name="ffn",
    )(block_expert, n_used, src, src, dst, dst, w_up, b_up, w_down, b_down, x_ts)


def _final_kernel(q_ref, gate_ref, x1_ref, yts_ref, g_ref, y_ref, sum_scr):
    def body(t, carry):
        acc = _tile_rows(x1_ref, t)[...]
        for k in range(TOP_K):
            slot = k * ROW_TILE + t
            acc = acc + gate_ref[0, 0, slot] * _tile_rows(yts_ref, q_ref[0, 0, slot])[...]
        _tile_rows(sum_scr, t)[...] = acc
        return carry

    lax.fori_loop(0, ROW_TILE, body, 0, unroll=4)
    x2 = _load_token_tiles(sum_scr, ROW_TILE)
    ms = jnp.mean(x2 * x2, axis=-1, keepdims=True)
    y_ref[...] = x2 * lax.rsqrt(ms + EPS) * g_ref[...]


def _final(qloc_tiles, gate_tiles, x1, y_ts, g_final, row0, n_rows):
    t0 = row0 // ROW_TILE
    smem_tile = pl.BlockSpec((1, 1, TOP_K * ROW_TILE), lambda i: (t0 + i, 0, 0), memory_space=pltpu.SMEM)
    return pl.pallas_call(
        _final_kernel,
        grid=(n_rows // ROW_TILE,),
        in_specs=[smem_tile, smem_tile,
                  pl.BlockSpec((ROW_TILE * TILE_ROWS, LANES), lambda i: (t0 + i, 0)),
                  pl.BlockSpec((TILE_PACK * TILE_ROWS, LANES), lambda i: (t0 + i, 0)),
                  _const_spec((1, D_MODEL))],
        out_specs=pl.BlockSpec((ROW_TILE, D_MODEL), lambda i: (i, 0)),
        out_shape=jax.ShapeDtypeStruct((n_rows, D_MODEL), F32),
        scratch_shapes=[pltpu.VMEM((ROW_TILE * TILE_ROWS, LANES), F32)],
        compiler_params=_params(("arbitrary",)),
        name="final",
    )(qloc_tiles, gate_tiles, x1, y_ts, g_final)


def _rope_tables(seq_len, dec_seq):
    inv = ROPE_THETA ** (-np.arange(0, ROT_DIM, 2, dtype=np.float64) / ROT_DIM)
    pos = np.concatenate([np.arange(seq_len), PAST_LEN + np.arange(ROW_TILE) % dec_seq]).astype(np.float64)
    ang = pos[:, None] * inv[None, :]
    cos, sin = np.cos(ang), np.sin(ang)
    ones = np.ones((pos.shape[0], HEAD_DIM - ROT_DIM))
    cos_h = np.concatenate([cos, cos, ones], axis=1)
    sin_h = np.concatenate([-sin, sin, 0.0 * ones], axis=1)
    return (jnp.asarray(np.concatenate([cos_h, cos_h], axis=1), F32),
            jnp.asarray(np.concatenate([sin_h, sin_h], axis=1), F32))


def _block_tables(counts, n_all):
    max_rows = TOP_K * n_all + (n_all // ROW_TILE) * N_EXPERTS * (CHUNK_ROWS - 1)
    n_blocks = -(-max_rows // FFN_BLOCK) + N_EXPERTS
    padded = (counts + FFN_BLOCK - 1) // FFN_BLOCK * FFN_BLOCK
    pad_end = jnp.cumsum(padded)
    pad_start = (pad_end - padded).astype(jnp.int32).reshape(N_EXPERTS, 1)
    block_start = jnp.arange(n_blocks, dtype=jnp.int32) * FFN_BLOCK
    block_expert = jnp.minimum(jnp.sum(pad_end[None, :] <= block_start[:, None], axis=1), N_EXPERTS - 1)
    n_used = (pad_end[-1] // FFN_BLOCK).astype(jnp.int32).reshape(1)
    return pad_start, block_expert.astype(jnp.int32), n_used, n_blocks


def kernel(x_prompt, x_sample, cache_swa_k, cache_swa_v, state_gla, g_mix, w_in, w_gk_up, b_gk, sinks,
           gla_norm, w_branch_a, w_branch_b, w_out, g_ffn, router_w, router_b, w_up, b_up, w_down, b_down,
           g_final):
    batch, seq_len, _ = x_prompt.shape
    dec_batch, dec_seq, _ = x_sample.shape
    n_p, n_s = batch * seq_len, dec_batch * dec_seq
    n_all = n_p + n_s
    assert w_in.shape[0] == 1, "one layer: the final norm is fused after the only MoE"
    assert seq_len % ROW_TILE == 0 and n_s % ROW_TILE == 0 and ROW_TILE % dec_seq == 0
    assert dec_seq % SUBLANES == 0 and dec_batch % SAMPLE_GROUP == 0
    assert SAMPLE_GROUP * dec_seq == CHUNK and (dec_seq & (dec_seq - 1)) == 0

    xp2 = x_prompt.reshape(n_p, D_MODEL)
    xs2 = x_sample.reshape(n_s, D_MODEL)
    cos_tab, sin_tab = _rope_tables(seq_len, dec_seq)
    w = w_in[0]
    w_a = w[:, :_C_GL].astype(BF16)
    w_b = w[:, _C_GL + GK_RANK:].astype(BF16)
    w_gk = jnp.pad(w[:, _C_GL:_C_GL + GK_RANK], ((0, 0), (0, LANES - GK_RANK))).astype(BF16)
    wup_pad = jnp.pad(w_gk_up[0], ((0, LANES - GK_RANK), (0, 0))).astype(BF16)
    qa, ka, va, qg, kg, vg, la, rg, ga, gb = _proj(
        xp2, xs2, g_mix[0].reshape(1, D_MODEL), cos_tab, sin_tab, w_a, w_b, w_gk, wup_pad,
        b_gk[0].reshape(1, GLA_KDIM), seq_len)

    oa_p = _swa_prompt(sinks[0], qa, ka, va, batch, seq_len)
    oa_s, nk_s, nv_s = _swa_sample(sinks[0], qa, ka, va, cache_swa_k[0], cache_swa_v[0], n_p,
                                   dec_batch, dec_seq)
    norm_row = jnp.tile(gla_norm[0], GLA_HEADS).reshape(1, GLA_VDIM)
    og_p, s_fin = _gla_prompt(qg, kg, la, vg, rg, norm_row, batch, seq_len)
    og_s, s_new = _gla_sample(qg, kg, la, vg, rg, norm_row, state_gla[0], n_p, dec_batch, dec_seq)

    x1 = _merge(xp2, xs2, oa_p, oa_s, og_p, og_s, ga, gb, w_branch_a[0].astype(BF16),
                w_branch_b[0].astype(BF16), w_out[0].astype(BF16))

    g_ffn_row = g_ffn[0].reshape(1, D_MODEL)
    rw_t = router_w[0].T
    rw_hi = rw_t.astype(BF16)
    rw_lo = (rw_t - rw_hi.astype(F32)).astype(BF16)
    eidx, prel, qloc, gate, cnt = _route(x1, g_ffn_row, rw_hi, rw_lo, router_b[0].reshape(N_EXPERTS, 1))
    n_tiles = n_all // ROW_TILE
    region_start, block_expert, n_used, n_blocks = _block_tables(cnt[:, 0].astype(jnp.int32), n_all)
    per_tile = lambda a: a.reshape(TOP_K, n_tiles, ROW_TILE).transpose(1, 0, 2).reshape(n_tiles, 1, TOP_K * ROW_TILE)
    qloc_tiles, gate_tiles = per_tile(qloc), per_tile(gate)
    x_ts = _pack(qloc_tiles, x1, g_ffn_row)
    src, dst = _invert(eidx, prel, qloc, region_start, n_blocks, n_tiles)
    prime = (n_tiles * TILE_PACK // CHUNK_ROWS + 2 * CHUNKS_PER_BLOCK
             + jnp.minimum(jnp.arange(LANES, dtype=jnp.int32), CHUNKS_PER_BLOCK - 1)).reshape(1, LANES)
    dst = jnp.concatenate([dst, prime], axis=0)
    y_ts = _ffn(block_expert, n_used, src.reshape(n_blocks, 1, LANES), dst.reshape(n_blocks + 1, 1, LANES),
                x_ts, w_up[0], b_up[0].reshape(N_EXPERTS, 1, 2 * D_FF), w_down[0],
                b_down[0].reshape(N_EXPERTS, 1, D_MODEL))

    g_out = g_final.reshape(1, D_MODEL)
    y_p = _final(qloc_tiles, gate_tiles, x1, y_ts, g_out, 0, n_p)
    y_s = _final(qloc_tiles, gate_tiles, x1, y_ts, g_out, n_p, n_s)

    kv_shape = (1, -1, WINDOW, KV_HEADS, HEAD_DIM)
    new_k_p = ka[:n_p].reshape(batch, seq_len, KV_WIDTH)[:, -WINDOW:].reshape(kv_shape)
    new_v_p = va[:n_p].reshape(batch, seq_len, KV_WIDTH)[:, -WINDOW:].reshape(kv_shape)
    return (y_p.reshape(batch, seq_len, D_MODEL), y_s.reshape(dec_batch, dec_seq, D_MODEL),
            new_k_p, new_v_p, s_fin[None], nk_s.reshape(kv_shape), nv_s.reshape(kv_shape), s_new[None])
```

```python
import functools

import numpy as np
import jax
import jax.numpy as jnp
from jax import lax
from jax.experimental import pallas as pl
from jax.experimental.pallas import tpu as pltpu

D_MODEL = 1024
PAST_LEN = 8192
HEAD_DIM = 64
N_HEADS = 8
KV_HEADS = 2
GROUP = N_HEADS // KV_HEADS
WINDOW = 128
ROT_DIM = HEAD_DIM // 4
ROPE_THETA = 500000.0
ATT_WIDTH = N_HEADS * HEAD_DIM
KV_WIDTH = KV_HEADS * HEAD_DIM
GLA_HEADS = 4
GLA_KDIM = D_MODEL // 2
GLA_VDIM = D_MODEL
GLA_DK = GLA_KDIM // GLA_HEADS
GLA_DV = GLA_VDIM // GLA_HEADS
GK_RANK = 16
GK_NORMALIZER = 16.0
N_EXPERTS = 32
TOP_K = 4
D_FF = D_MODEL
SWIGLU_LIMIT = 7.0
SWIGLU_ALPHA = 1.702
EPS = 1e-5
NEG_INF = -1e30

LANES = 128
SUBLANES = 8
VMEM_LIMIT_BYTES = 56 * 1024 * 1024

ROW_TILE = 512
CHUNK = 128
FFN_BLOCK = 512

BF16 = jnp.bfloat16
F32 = jnp.float32

_C_QA, _C_KA, _C_VA, _C_QG, _C_KG, _C_VG, _C_GL = 0, 512, 640, 768, 1280, 1792, 2816


def _const_spec(shape):
    nd = len(shape)
    return pl.BlockSpec(shape, lambda *_: (0,) * nd, pipeline_mode=pl.Buffered(1))


def _resident_out_spec(shape):
    nd = len(shape)
    return pl.BlockSpec(shape, lambda *_: (0,) * nd)


def _params(sem, vmem=VMEM_LIMIT_BYTES):
    return pltpu.CompilerParams(dimension_semantics=sem, vmem_limit_bytes=vmem)


def _nt_dot(a, b):
    return lax.dot_general(a, b, (((1,), (1,)), ((), ())), preferred_element_type=F32)


def _dot(a, b):
    return jnp.dot(a, b, preferred_element_type=F32)


TILE_ROWS = D_MODEL // LANES
assert TILE_ROWS == SUBLANES


def _load_token_tiles(ref, n_tokens):
    return jnp.concatenate([ref[pl.ds(c, n_tokens, stride=TILE_ROWS), :] for c in range(TILE_ROWS)], axis=1)


def _store_token_tiles(ref, x):
    for c in range(TILE_ROWS):
        ref[pl.ds(c, x.shape[0], stride=TILE_ROWS), :] = x[:, c * LANES:(c + 1) * LANES]


def _rope(x, cos_t, sin_t, n_rep):
    width = x.shape[1]
    cos_f = jnp.concatenate([cos_t] * n_rep, axis=1) if n_rep > 1 else cos_t
    sin_f = jnp.concatenate([sin_t] * n_rep, axis=1) if n_rep > 1 else sin_t
    lane = lax.broadcasted_iota(jnp.int32, x.shape, 1) % HEAD_DIM
    up = pltpu.roll(x, width - ROT_DIM // 2, 1)
    down = pltpu.roll(x, ROT_DIM // 2, 1)
    partner = jnp.where(lane < ROT_DIM // 2, up, down)
    return x * cos_f + partner * sin_f


def _proj_kernel(n_prompt_tiles, xp_ref, xs_ref, g_ref, cos_ref, sin_ref, wa_ref, wb_ref, wgk_ref, wup_ref, bgk_ref,
                 qa_ref, ka_ref, va_ref, qg_ref, kg_ref, vg_ref, la_ref, rg_ref, ga_ref, gb_ref):
    i = pl.program_id(0)
    x = jnp.where(i < n_prompt_tiles, xp_ref[...], xs_ref[...])
    ms = jnp.mean(x * x, axis=-1, keepdims=True)
    h = (x * lax.rsqrt(ms + EPS) * g_ref[...]).astype(BF16)
    cos_t = cos_ref[...]
    sin_t = sin_ref[...]

    def seg(w_ref, a, b):
        return _dot(h, w_ref[:, a:b])

    qa = _rope(seg(wa_ref, _C_QA, _C_KA), cos_t, sin_t, ATT_WIDTH // LANES)
    qa_ref[...] = (qa * (HEAD_DIM ** -0.5)).astype(BF16)
    ka_ref[...] = _rope(seg(wa_ref, _C_KA, _C_VA), cos_t, sin_t, 1)
    va_ref[...] = seg(wa_ref, _C_VA, _C_QG)
    qg_ref[...] = seg(wa_ref, _C_QG, _C_KG) * (GLA_DK ** -0.5)
    kg_ref[...] = seg(wa_ref, _C_KG, _C_VG)
    vg_ref[...] = seg(wa_ref, _C_VG, _C_GL).astype(BF16)
    rg_ref[...] = seg(wb_ref, 0, GLA_VDIM).astype(BF16)
    ga_ref[...] = seg(wb_ref, GLA_VDIM, GLA_VDIM + D_MODEL).astype(BF16)
    gb_ref[...] = seg(wb_ref, GLA_VDIM + D_MODEL, GLA_VDIM + 2 * D_MODEL).astype(BF16)
    gk_low = _dot(h, wgk_ref[...]).astype(BF16)
    z = _dot(gk_low, wup_ref[...]) + bgk_ref[...]
    log_sig = jnp.minimum(z, 0.0) - jnp.log1p(jnp.exp(-jnp.abs(z)))
    la_ref[...] = log_sig / GK_NORMALIZER


def _proj(xp2, xs2, g_mix, cos_tab, sin_tab, w_a, w_b, w_gk, wup_pad, b_gk, seq_len):
    n_p, n_s = xp2.shape[0], xs2.shape[0]
    n_all = n_p + n_s
    npt, nst = n_p // ROW_TILE, n_s // ROW_TILE
    tiles_per_seq = seq_len // ROW_TILE

    def tab_map(i):
        return (jnp.where(i < npt, i % tiles_per_seq, tiles_per_seq), 0)

    row = lambda w: pl.BlockSpec((ROW_TILE, w), lambda i: (i, 0))
    widths = [(ATT_WIDTH, BF16), (KV_WIDTH, F32), (KV_WIDTH, F32), (GLA_KDIM, F32), (GLA_KDIM, F32),
              (GLA_VDIM, BF16), (GLA_KDIM, F32), (GLA_VDIM, BF16), (D_MODEL, BF16), (D_MODEL, BF16)]
    return pl.pallas_call(
        functools.partial(_proj_kernel, npt),
        grid=(npt + nst,),
        in_specs=[
            pl.BlockSpec((ROW_TILE, D_MODEL), lambda i: (jnp.minimum(i, npt - 1), 0)),
            pl.BlockSpec((ROW_TILE, D_MODEL), lambda i: (jnp.maximum(i - npt, 0), 0)),
            _const_spec((1, D_MODEL)),
            pl.BlockSpec((ROW_TILE, LANES), tab_map),
            pl.BlockSpec((ROW_TILE, LANES), tab_map),
            _const_spec(w_a.shape), _const_spec(w_b.shape), _const_spec(w_gk.shape),
            _const_spec(wup_pad.shape),
            _const_spec((1, GLA_KDIM)),
        ],
        out_specs=[row(w) for w, _ in widths],
        out_shape=[jax.ShapeDtypeStruct((n_all, w), dt) for w, dt in widths],
        compiler_params=_params(("arbitrary",)),
        name="proj",
    )(xp2, xs2, g_mix, cos_tab, sin_tab, w_a, w_b, w_gk, wup_pad, b_gk)


def _pair_blocks(kk):
    lane = lax.broadcasted_iota(jnp.int32, kk.shape, 1)
    lo = lane < HEAD_DIM
    swapped = pltpu.roll(kk, HEAD_DIM, 1)
    zero = jnp.zeros_like(kk)
    blocks = []
    for kh in range(KV_HEADS):
        left = jnp.where(lo, kk if kh == 0 else swapped, zero)
        right = jnp.where(lo, zero, swapped if kh == 0 else kk)
        blocks.append(jnp.concatenate([left, right], axis=0).astype(BF16))
    return blocks


def _sink_softmax(s, valid, sink):
    s = jnp.where(valid, s, NEG_INF)
    m = jnp.maximum(jnp.max(s, axis=-1, keepdims=True), sink)
    p = jnp.exp(s - m)
    denom = jnp.sum(p, axis=-1, keepdims=True) + jnp.exp(sink - m)
    return p.astype(BF16), 1.0 / denom


def _attend(q, kk, vv, valid, sink_ref, o_ref, row0=0):
    rows, keys = valid.shape
    kblocks = _pair_blocks(kk)
    vblocks = _pair_blocks(vv)
    for kh in range(KV_HEADS):
        base = kh * GROUP * HEAD_DIM
        qq = jnp.concatenate([q[:, base:base + LANES], q[:, base + LANES:base + 2 * LANES]], axis=0)
        s = _nt_dot(qq, kblocks[kh])
        first_head = lax.broadcasted_iota(jnp.int32, (rows, LANES), 1) < HEAD_DIM
        for r in range(2):
            probs, scales = [], []
            for c in range(2):
                head = kh * GROUP + 2 * r + c
                p_c, inv_c = _sink_softmax(s[r * rows:(r + 1) * rows, c * keys:(c + 1) * keys],
                                           valid, sink_ref[head])
                probs.append(p_c)
                scales.append(inv_c)
            p = jnp.concatenate(probs, axis=1)
            scale = jnp.where(first_head, scales[0], scales[1])
            o_ref[pl.ds(row0, rows), base + r * LANES:base + (r + 1) * LANES] = (
                _dot(p, vblocks[kh]) * scale).astype(o_ref.dtype)


def _swa_prompt_kernel(sink_ref, q_ref, kp_ref, k0_ref, k1_ref, vp_ref, v0_ref, v1_ref, o_ref):
    j = pl.program_id(1)
    row = lax.broadcasted_iota(jnp.int32, (WINDOW, 2 * WINDOW), 0)
    col = lax.broadcasted_iota(jnp.int32, (WINDOW, 2 * WINDOW), 1)
    band = (col > row) & (col <= row + WINDOW)
    k_blocks = (kp_ref[...], k0_ref[...], k1_ref[...])
    v_blocks = (vp_ref[...], v0_ref[...], v1_ref[...])
    q = q_ref[...]
    for half in range(2):
        kk = jnp.concatenate(k_blocks[half:half + 2], axis=0)
        vv = jnp.concatenate(v_blocks[half:half + 2], axis=0)
        valid = band & ((j > 0) | (col >= WINDOW)) if half == 0 else band
        _attend(q[half * WINDOW:(half + 1) * WINDOW], kk, vv, valid, sink_ref, o_ref, half * WINDOW)


def _swa_prompt(sinks, qa, ka, va, batch, seq_len):
    nb = seq_len // WINDOW
    assert nb % 2 == 0
    kv = lambda off: pl.BlockSpec((WINDOW, KV_WIDTH), lambda b, j, s: (b * nb + jnp.maximum(2 * j + off, 0), 0))
    pair = pl.BlockSpec((2 * WINDOW, ATT_WIDTH), lambda b, j, s: (b * (nb // 2) + j, 0))
    return pl.pallas_call(
        _swa_prompt_kernel,
        grid_spec=pltpu.PrefetchScalarGridSpec(
            num_scalar_prefetch=1,
            grid=(batch, nb // 2),
            in_specs=[pair, kv(-1), kv(0), kv(1), kv(-1), kv(0), kv(1)],
            out_specs=pair,
        ),
        out_shape=jax.ShapeDtypeStruct((batch * seq_len, ATT_WIDTH), BF16),
        compiler_params=_params(("arbitrary", "arbitrary")),
        name="swa_prompt",
    )(sinks, qa, ka, ka, ka, va, va, va)


SAMPLE_GROUP = 16


def _swa_sample_kernel(dec_seq, sink_ref, q_ref, kn_ref, vn_ref, ck_ref, cv_ref, o_ref, nk_ref, nv_ref):
    rows = SAMPLE_GROUP * dec_seq
    ck = ck_ref[...]
    cv = cv_ref[...]
    kn = kn_ref[...]
    vn = vn_ref[...]
    nk_ref[:, :WINDOW - dec_seq, :] = ck[:, dec_seq:, :]
    nv_ref[:, :WINDOW - dec_seq, :] = cv[:, dec_seq:, :]
    nk_ref[:, WINDOW - dec_seq:, :] = kn.reshape(SAMPLE_GROUP, dec_seq, KV_WIDTH)
    nv_ref[:, WINDOW - dec_seq:, :] = vn.reshape(SAMPLE_GROUP, dec_seq, KV_WIDTH)
    n_cache = SAMPLE_GROUP * WINDOW
    kk = jnp.concatenate([ck.reshape(n_cache, KV_WIDTH), kn], axis=0)
    vv = jnp.concatenate([cv.reshape(n_cache, KV_WIDTH), vn], axis=0)
    keys = n_cache + rows
    row = lax.broadcasted_iota(jnp.int32, (rows, keys), 0)
    col = lax.broadcasted_iota(jnp.int32, (rows, keys), 1)
    q_b, q_s = row // dec_seq, row % dec_seq
    is_cache = col < n_cache
    new = col - n_cache
    valid_cache = (col // WINDOW == q_b) & (col % WINDOW > q_s)
    valid_new = (new // dec_seq == q_b) & (new % dec_seq <= q_s)
    valid = (is_cache & valid_cache) | (jnp.logical_not(is_cache) & valid_new)
    _attend(q_ref[...], kk, vv, valid, sink_ref, o_ref)


def _swa_sample(sinks, qa, ka, va, cache_k, cache_v, n_prompt_rows, dec_batch, dec_seq):
    rows = SAMPLE_GROUP * dec_seq
    off = n_prompt_rows // rows
    tok = lambda g, s: (off + g, 0)
    cache = lambda g, s: (g, 0, 0)
    cshape = (dec_batch, WINDOW, KV_WIDTH)
    return pl.pallas_call(
        functools.partial(_swa_sample_kernel, dec_seq),
        grid_spec=pltpu.PrefetchScalarGridSpec(
            num_scalar_prefetch=1,
            grid=(dec_batch // SAMPLE_GROUP,),
            in_specs=[
                pl.BlockSpec((rows, ATT_WIDTH), tok),
                pl.BlockSpec((rows, KV_WIDTH), tok),
                pl.BlockSpec((rows, KV_WIDTH), tok),
                pl.BlockSpec((SAMPLE_GROUP, WINDOW, KV_WIDTH), cache),
                pl.BlockSpec((SAMPLE_GROUP, WINDOW, KV_WIDTH), cache),
            ],
            out_specs=[
                pl.BlockSpec((rows, ATT_WIDTH), lambda g, s: (g, 0)),
                pl.BlockSpec((SAMPLE_GROUP, WINDOW, KV_WIDTH), cache),
                pl.BlockSpec((SAMPLE_GROUP, WINDOW, KV_WIDTH), cache),
            ],
        ),
        out_shape=[jax.ShapeDtypeStruct((dec_batch * dec_seq, ATT_WIDTH), BF16),
                   jax.ShapeDtypeStruct(cshape, F32), jax.ShapeDtypeStruct(cshape, F32)],
        compiler_params=_params(("arbitrary",)),
        name="swa_sample",
    )(sinks, qa, ka, va, cache_k.reshape(cshape), cache_v.reshape(cshape))


def _chunk_tables(seg):
    n_lev = int(np.log2(seg))
    t = np.arange(CHUNK)
    seg_start = (t // seg) * seg
    u = np.arange(CHUNK)[None, :]

    def prefix(end):
        return ((u >= seg_start[:, None]) & (u <= end[:, None])).astype(np.float32)

    blocks = [prefix(t)]
    for d in range(min(n_lev, _MATMUL_LEVELS)):
        m = 1 << d
        ref = (t >> (d + 1) << (d + 1)) + m - 1
        blocks.append(prefix(ref))
    lhs = np.concatenate(blocks, axis=0)
    lhs2 = np.concatenate([lhs, lhs], axis=1)
    tt, ss = t[:, None], t[None, :]
    x = tt ^ ss
    lev = np.where(x > 0, np.floor(np.log2(np.maximum(x, 1))).astype(np.int32), n_lev)
    lev = np.where((ss > tt) | (tt // seg != ss // seg), -1, lev)
    lev = np.where(tt == ss, n_lev, lev)
    return jnp.asarray(lhs2, BF16), jnp.asarray(lev, jnp.int32), n_lev


_MATMUL_LEVELS = 3


def _group_row(x, group, row):
    width = x.shape[1]
    parts = [jnp.broadcast_to(x[g * group + row:g * group + row + 1, :], (group, width))
             for g in range(x.shape[0] // group)]
    return parts[0] if len(parts) == 1 else jnp.concatenate(parts, axis=0)


_HALF_ROW_LEVELS = 4
LOG2_E = 1.4426950408889634


def _halves(x, m, which):
    parts = [x[(2 * g + which) * m:(2 * g + which + 1) * m] for g in range(x.shape[0] // (2 * m))]
    return parts[0] if len(parts) == 1 else jnp.concatenate(parts, axis=0)


def _unhalve(xh, m, which):
    zero = jnp.zeros((m, xh.shape[1]), xh.dtype)
    parts = []
    for g in range(xh.shape[0] // m):
        blk = xh[g * m:(g + 1) * m]
        parts += [zero, blk] if which else [blk, zero]
    return jnp.concatenate(parts, axis=0)


def _gla_chunk_terms(q, k, la, lhs2, level, n_lev, seg):
    la2 = la * LOG2_E
    hi = la2.astype(BF16)
    lo = (la2 - hi.astype(F32)).astype(BF16)
    sums = _dot(lhs2, jnp.concatenate([hi, lo], axis=0))
    b = sums[0:CHUNK]
    b_last = _group_row(b, seg, seg - 1)
    q_main = (q * jnp.exp2(b)).astype(BF16)
    k_upd = k * jnp.exp2(b_last - b)
    q_lev = [None] * n_lev
    k_lev = [None] * n_lev
    for d in range(n_lev):
        m = 1 << d
        if d < _MATMUL_LEVELS:
            ref = sums[(1 + d) * CHUNK:(2 + d) * CHUNK]
        elif d < _HALF_ROW_LEVELS:
            ref = _group_row(b, 2 * m, m - 1)
        if d < _HALF_ROW_LEVELS:
            q_lev[d] = (q * jnp.exp2(b - ref)).astype(BF16)
            k_lev[d] = (k * jnp.exp2(ref - b)).astype(BF16)
        else:
            ref_h = _group_row(_halves(b, m, 0), m, m - 1)
            q_lev[d] = _unhalve((_halves(q, m, 1) * jnp.exp2(_halves(b, m, 1) - ref_h)).astype(BF16), m, 1)
            k_lev[d] = _unhalve((_halves(k, m, 0) * jnp.exp2(ref_h - _halves(b, m, 0))).astype(BF16), m, 0)
    q_b, k_b = q.astype(BF16), k.astype(BF16)

    def att(h):
        hs = slice(h * GLA_DK, (h + 1) * GLA_DK)
        acc = jnp.where(level == n_lev, _nt_dot(q_b[:, hs], k_b[:, hs]), 0.0)
        for d in range(n_lev):
            acc = jnp.where(level == d, _nt_dot(q_lev[d][:, hs], k_lev[d][:, hs]), acc)
        return acc

    return q_main, k_upd, att, b_last


def _gla_out(o, r, norm):
    parts = []
    for h in range(GLA_HEADS):
        oh = o[:, h * GLA_DV:(h + 1) * GLA_DV]
        ms = jnp.mean(oh * oh, axis=-1, keepdims=True)
        parts.append(oh * lax.rsqrt(ms + EPS))
    y = jnp.concatenate(parts, axis=1) * norm
    rf = r.astype(F32)
    return y * (rf * jax.nn.sigmoid(rf))


def _gla_prompt_kernel(n_lev, n_par, *refs):
    seq_refs = [refs[5 * j:5 * j + 5] for j in range(n_par)]
    norm_ref, lhs_ref, lev_ref = refs[5 * n_par:5 * n_par + 3]
    o_ref, sfin_ref, s_scr = refs[5 * n_par + 3:]
    c = pl.program_id(1)

    @pl.when(c == 0)
    def _():
        s_scr[...] = jnp.zeros_like(s_scr)

    for j, (q_ref, k_ref, la_ref, v_ref, r_ref) in enumerate(seq_refs):
        q_main, k_upd, att, b_last = _gla_chunk_terms(q_ref[...], k_ref[...], la_ref[...], lhs_ref[...],
                                                      lev_ref[...], n_lev, CHUNK)
        v = v_ref[...]
        outs = []
        for h in range(GLA_HEADS):
            hs = slice(h * GLA_DK, (h + 1) * GLA_DK)
            vh = v[:, h * GLA_DV:(h + 1) * GLA_DV]
            s0 = s_scr[j, h]
            o_h = _dot(q_main[:, hs], s0.astype(BF16)) + _dot(att(h).astype(BF16), vh)
            outs.append(o_h)
            decay = jnp.exp2(b_last[:, hs]).T
            k_t = k_upd[:, hs].T.astype(BF16)
            s_scr[j, h] = jnp.concatenate([decay, decay], axis=1) * s0 + _dot(k_t, vh)
        o_ref[j] = _gla_out(jnp.concatenate(outs, axis=1), r_ref[...], norm_ref[...]).astype(BF16)

    @pl.when(c == pl.num_programs(1) - 1)
    def _():
        sfin_ref[...] = s_scr[...]


def _gla_prompt(qg, kg, la, vg, rg, norm_row, batch, seq_len):
    nc = seq_len // CHUNK
    n_par = 4 if batch % 4 == 0 else (2 if batch % 2 == 0 else 1)
    lhs2, level, n_lev = _chunk_tables(CHUNK)
    tok = lambda j, w: pl.BlockSpec((CHUNK, w), lambda b, c: ((b * n_par + j) * nc + c, 0))
    seq_specs, seq_args = [], []
    for j in range(n_par):
        seq_specs += [tok(j, GLA_KDIM), tok(j, GLA_KDIM), tok(j, GLA_KDIM), tok(j, GLA_VDIM), tok(j, GLA_VDIM)]
        seq_args += [qg, kg, la, vg, rg]
    og, s_fin = pl.pallas_call(
        functools.partial(_gla_prompt_kernel, n_lev, n_par),
        grid=(batch // n_par, nc),
        in_specs=seq_specs + [_const_spec((1, GLA_VDIM)), _const_spec(lhs2.shape), _const_spec(level.shape)],
        out_specs=[pl.BlockSpec((n_par, CHUNK, GLA_VDIM), lambda b, c: (b, c, 0)),
                   pl.BlockSpec((n_par, GLA_HEADS, GLA_DK, GLA_DV), lambda b, c: (b, 0, 0, 0))],
        out_shape=[jax.ShapeDtypeStruct((batch, seq_len, GLA_VDIM), BF16),
                   jax.ShapeDtypeStruct((batch, GLA_HEADS, GLA_DK, GLA_DV), F32)],
        scratch_shapes=[pltpu.VMEM((n_par, GLA_HEADS, GLA_DK, GLA_DV), F32)],
        compiler_params=_params(("arbitrary", "arbitrary")),
        name="gla_prompt",
    )(*seq_args, norm_row, lhs2, level)
    return og.reshape(batch * seq_len, GLA_VDIM), s_fin


def _gla_sample_kernel(n_lev, dec_seq, q_ref, k_ref, la_ref, v_ref, r_ref, norm_ref, lhs_ref, lev_ref,
                       s0_ref, o_ref, snew_ref):
    q_main, k_upd, att, b_last = _gla_chunk_terms(q_ref[...], k_ref[...], la_ref[...], lhs_ref[...],
                                                  lev_ref[...], n_lev, dec_seq)
    v = v_ref[...]
    n_b = CHUNK // dec_seq
    row_b = lax.broadcasted_iota(jnp.int32, (CHUNK, GLA_DK), 0) // dec_seq
    col_b = lax.broadcasted_iota(jnp.int32, (GLA_DK, CHUNK), 1) // dec_seq
    outs = []
    for h in range(GLA_HEADS):
        hs = slice(h * GLA_DK, (h + 1) * GLA_DK)
        vh = v[:, h * GLA_DV:(h + 1) * GLA_DV]
        qm = q_main[:, hs]
        decay_t = jnp.exp2(b_last[:, hs]).T
        k_t = k_upd[:, hs].T.astype(BF16)
        o_h = _dot(att(h).astype(BF16), vh)
        for bi in range(n_b):
            s0 = s0_ref[bi, h]
            o_h = o_h + _dot(jnp.where(row_b == bi, qm, jnp.zeros_like(qm)), s0.astype(BF16))
            decay = jnp.broadcast_to(decay_t[:, bi * dec_seq:bi * dec_seq + 1], (GLA_DK, GLA_DV))
            k_b = jnp.where(col_b == bi, k_t, jnp.zeros_like(k_t))
            snew_ref[bi, h] = decay * s0 + _dot(k_b, vh)
        outs.append(o_h)
    o_ref[...] = _gla_out(jnp.concatenate(outs, axis=1), r_ref[...], norm_ref[...]).astype(o_ref.dtype)


def _gla_sample(qg, kg, la, vg, rg, norm_row, state, n_prompt_rows, dec_batch, dec_seq):
    n_b = CHUNK // dec_seq
    off = n_prompt_rows // CHUNK
    lhs3, level, n_lev = _chunk_tables(dec_seq)
    tok = lambda w: pl.BlockSpec((CHUNK, w), lambda g: (off + g, 0))
    st = pl.BlockSpec((n_b, GLA_HEADS, GLA_DK, GLA_DV), lambda g: (g, 0, 0, 0))
    return pl.pallas_call(
        functools.partial(_gla_sample_kernel, n_lev, dec_seq),
        grid=(dec_batch // n_b,),
        in_specs=[tok(GLA_KDIM), tok(GLA_KDIM), tok(GLA_KDIM), tok(GLA_VDIM), tok(GLA_VDIM),
                  _const_spec((1, GLA_VDIM)), _const_spec(lhs3.shape), _const_spec(level.shape), st],
        out_specs=[pl.BlockSpec((CHUNK, GLA_VDIM), lambda g: (g, 0)), st],
        out_shape=[jax.ShapeDtypeStruct((dec_batch * dec_seq, GLA_VDIM), BF16),
                   jax.ShapeDtypeStruct(state.shape, F32)],
        compiler_params=_params(("arbitrary",)),
        name="gla_sample",
    )(qg, kg, la, vg, rg, norm_row, lhs3, level, state)


def _merge_kernel(n_prompt_tiles, xp_ref, xs_ref, oap_ref, oas_ref, ogp_ref, ogs_ref, ga_ref, gb_ref,
                  wa_ref, wb_ref, wo_ref, x1_ref):
    i = pl.program_id(0)
    is_p = i < n_prompt_tiles
    x = jnp.where(is_p, xp_ref[...], xs_ref[...])
    oa = jnp.where(is_p, oap_ref[...], oas_ref[...])
    og = jnp.where(is_p, ogp_ref[...], ogs_ref[...])
    m = (jax.nn.sigmoid(ga_ref[...].astype(F32)) * _dot(oa, wa_ref[...])
         + jax.nn.sigmoid(gb_ref[...].astype(F32)) * _dot(og, wb_ref[...]))
    _store_token_tiles(x1_ref, x + _dot(m.astype(BF16), wo_ref[...]))


def _merge(xp2, xs2, oa_p, oa_s, og_p, og_s, ga, gb, wa, wb, wo):
    n_p, n_s = xp2.shape[0], xs2.shape[0]
    npt, nst = n_p // ROW_TILE, n_s // ROW_TILE
    p_map = lambda i: (jnp.minimum(i, npt - 1), 0)
    s_map = lambda i: (jnp.maximum(i - npt, 0), 0)
    row = lambda w: pl.BlockSpec((ROW_TILE, w), lambda i: (i, 0))
    return pl.pallas_call(
        functools.partial(_merge_kernel, npt),
        grid=(npt + nst,),
        in_specs=[
            pl.BlockSpec((ROW_TILE, D_MODEL), p_map), pl.BlockSpec((ROW_TILE, D_MODEL), s_map),
            pl.BlockSpec((ROW_TILE, ATT_WIDTH), p_map), pl.BlockSpec((ROW_TILE, ATT_WIDTH), s_map),
            pl.BlockSpec((ROW_TILE, GLA_VDIM), p_map), pl.BlockSpec((ROW_TILE, GLA_VDIM), s_map),
            row(D_MODEL), row(D_MODEL),
            _const_spec(wa.shape), _const_spec(wb.shape), _const_spec(wo.shape),
        ],
        out_specs=pl.BlockSpec((ROW_TILE * TILE_ROWS, LANES), lambda i: (i, 0)),
        out_shape=jax.ShapeDtypeStruct(((n_p + n_s) * TILE_ROWS, LANES), F32),
        compiler_params=_params(("arbitrary",)),
        name="merge",
    )(xp2, xs2, oa_p, oa_s, og_p, og_s, ga, gb, wa, wb, wo)


CHUNK_ROWS = 8
CHUNKS_PER_BLOCK = FFN_BLOCK // CHUNK_ROWS
TILE_PACK = TOP_K * ROW_TILE + N_EXPERTS * (CHUNK_ROWS - 1)
TILE_PACK += -TILE_PACK % CHUNK_ROWS


def _route_kernel(x1_ref, gffn_ref, rwh_ref, rwl_ref, rb_ref, upper_ref, lower_ref, eidx_ref, prel_ref, qloc_ref,
                  gate_ref, cnt_ref, cnt_scr):
    i = pl.program_id(0)

    @pl.when(i == 0)
    def _():
        cnt_scr[...] = jnp.zeros_like(cnt_scr)

    x1 = _load_token_tiles(x1_ref, ROW_TILE)
    ms = jnp.mean(x1 * x1, axis=-1, keepdims=True)
    h2 = x1 * lax.rsqrt(ms + EPS) * gffn_ref[...]
    h_hi = h2.astype(BF16)
    h_lo = (h2 - h_hi.astype(F32)).astype(BF16)
    rwh, rwl = rwh_ref[...], rwl_ref[...]
    logits = _nt_dot(rwh, h_hi) + _nt_dot(rwl, h_hi) + _nt_dot(rwh, h_lo) + rb_ref[...]
    eid = lax.broadcasted_iota(jnp.int32, logits.shape, 0)
    upper = upper_ref[...]
    seen = cnt_scr[...]
    vals, rows_e, hots, befores, counts = [], [], [], [], []
    lg = logits
    for _ in range(TOP_K):
        mx = jnp.max(lg, axis=0, keepdims=True)
        sel = jnp.min(jnp.where(lg == mx, eid, N_EXPERTS), axis=0, keepdims=True)
        onehot = eid == sel
        oh = onehot.astype(F32)
        hots.append(oh)
        befores.append(_dot(onehot.astype(BF16), upper))
        counts.append(jnp.sum(oh, axis=1, keepdims=True))
        vals.append(mx)
        rows_e.append(sel)
        lg = jnp.where(onehot, -jnp.inf, lg)
    run = (counts[0] + counts[1]) + (counts[2] + counts[3])
    run_pad = jnp.floor((run + (CHUNK_ROWS - 1)) * (1.0 / CHUNK_ROWS)) * CHUNK_ROWS
    tile_off = _dot(lower_ref[...], jnp.broadcast_to(run_pad, (N_EXPERTS, LANES)).astype(BF16))[:, 0:1]
    rows_p, rows_q = [], []
    ahead = jnp.zeros_like(run)
    for k in range(TOP_K):
        local = ahead + befores[k]
        rows_p.append(jnp.sum(hots[k] * (seen + local), axis=0, keepdims=True))
        rows_q.append(jnp.sum(hots[k] * (tile_off + local), axis=0, keepdims=True))
        ahead = ahead + counts[k]
    cnt_scr[...] = seen + run_pad
    ex = [jnp.exp(v - vals[0]) for v in vals]
    inv = 1.0 / (ex[0] + ex[1] + ex[2] + ex[3])
    eidx_ref[...] = jnp.concatenate(rows_e, axis=0)
    prel_ref[...] = jnp.concatenate(rows_p, axis=0).astype(jnp.int32)
    qloc_ref[...] = jnp.concatenate(rows_q, axis=0).astype(jnp.int32)
    gate_ref[...] = jnp.concatenate([e * inv for e in ex], axis=0)
    cnt_ref[...] = jnp.broadcast_to(seen + run_pad, cnt_ref.shape)


def _route(x1, g_ffn, rw_hi, rw_lo, rb_col):
    n = x1.shape[0] // TILE_ROWS
    upper = jnp.asarray(np.triu(np.ones((ROW_TILE, ROW_TILE), np.float32), 1), BF16)
    lower = jnp.asarray(np.tril(np.ones((N_EXPERTS, N_EXPERTS), np.float32), -1), BF16)
    col = lambda dt: (pl.BlockSpec((TOP_K, ROW_TILE), lambda i: (0, i)), jax.ShapeDtypeStruct((TOP_K, n), dt))
    outs = [col(jnp.int32), col(jnp.int32), col(jnp.int32), col(F32),
            (_resident_out_spec((N_EXPERTS, LANES)), jax.ShapeDtypeStruct((N_EXPERTS, LANES), F32))]
    return pl.pallas_call(
        _route_kernel,
        grid=(n // ROW_TILE,),
        in_specs=[pl.BlockSpec((ROW_TILE * TILE_ROWS, LANES), lambda i: (i, 0)), _const_spec((1, D_MODEL)),
                  _const_spec(rw_hi.shape), _const_spec(rw_lo.shape), _const_spec((N_EXPERTS, 1)),
                  _const_spec(upper.shape), _const_spec(lower.shape)],
        out_specs=[o[0] for o in outs],
        out_shape=[o[1] for o in outs],
        scratch_shapes=[pltpu.VMEM((N_EXPERTS, 1), F32)],
        compiler_params=_params(("arbitrary",)),
        name="route",
    )(x1, g_ffn, rw_hi, rw_lo, rb_col, upper, lower)


def _tile_rows(ref, row):
    return _tile_at(ref, row * TILE_ROWS)


def _tile_at(ref, first_row):
    return ref.at[pl.ds(pl.multiple_of(first_row, TILE_ROWS), TILE_ROWS), :]


def _pack_kernel(q_ref, x1_ref, gffn_ref, o_ref, h_scr):
    o_ref[...] = jnp.zeros(o_ref.shape, o_ref.dtype)

    @pl.when(pl.program_id(0) < pl.num_programs(0) - 1)
    def _():
        x1 = _load_token_tiles(x1_ref, ROW_TILE)
        ms = jnp.mean(x1 * x1, axis=-1, keepdims=True)
        _store_token_tiles(h_scr, x1 * lax.rsqrt(ms + EPS) * gffn_ref[...])

        def body(t, carry):
            row = _tile_rows(h_scr, t)[...]
            for k in range(TOP_K):
                _tile_at(o_ref, q_ref[0, 0, k * ROW_TILE + t])[...] = row
            return carry

        lax.fori_loop(0, ROW_TILE, body, 0, unroll=8)


def _pack(qloc_tiles, x1, g_ffn):
    n_tiles = qloc_tiles.shape[0]
    assert TILE_PACK >= 3 * FFN_BLOCK
    last = n_tiles - 1
    return pl.pallas_call(
        _pack_kernel,
        grid=(n_tiles + 1,),
        in_specs=[pl.BlockSpec((1, 1, TOP_K * ROW_TILE), lambda i: (jnp.minimum(i, last), 0, 0),
                               memory_space=pltpu.SMEM),
                  pl.BlockSpec((ROW_TILE * TILE_ROWS, LANES), lambda i: (jnp.minimum(i, last), 0)),
                  _const_spec((1, D_MODEL))],
        out_specs=pl.BlockSpec((TILE_PACK * TILE_ROWS, LANES), lambda i: (i, 0)),
        out_shape=jax.ShapeDtypeStruct(((n_tiles + 1) * TILE_PACK * TILE_ROWS, LANES), F32),
        scratch_shapes=[pltpu.VMEM((ROW_TILE * TILE_ROWS, LANES), F32)],
        compiler_params=_params(("arbitrary",)),
        name="pack",
    )(qloc_tiles, x1, g_ffn)


_INV_FIELDS = 3
_BYTE_BITS = 8
_BYTE = 1 << _BYTE_BITS


def _invert_kernel(n_blocks, eidx_ref, prel_ref, qloc_ref, rstart_ref, acc_ref):
    i = pl.program_id(0)

    @pl.when(i == 0)
    def _():
        acc_ref[...] = jnp.zeros(acc_ref.shape, acc_ref.dtype)

    eidx, prel, qloc = eidx_ref[...], prel_ref[...], qloc_ref[...]
    rstart = rstart_ref[...]
    rows = eidx.shape[1]
    eid = lax.broadcasted_iota(jnp.int32, (N_EXPERTS, rows), 0)
    blk_id = lax.broadcasted_iota(jnp.int32, (n_blocks, rows), 0)
    off_id = lax.broadcasted_iota(jnp.int32, (LANES, rows), 0)
    blk_shift = FFN_BLOCK.bit_length() - 1
    chunk_shift = CHUNK_ROWS.bit_length() - 1
    a_parts, b_parts = [], []
    for k in range(TOP_K):
        base = jnp.sum(jnp.where(eid == eidx[k:k + 1], rstart, 0), axis=0, keepdims=True)
        pos = base + prel[k:k + 1]
        leader = (prel[k:k + 1] & (CHUNK_ROWS - 1)) == 0
        in_blk = (blk_id == lax.shift_right_logical(pos, blk_shift)) & leader
        a_parts.append(jnp.where(in_blk, 1.0, 0.0).astype(BF16))
        hit = off_id == (lax.shift_right_logical(pos, chunk_shift) & (CHUNKS_PER_BLOCK - 1))
        cid = lax.shift_right_logical(i * TILE_PACK + qloc[k:k + 1], chunk_shift)
        fields = [(cid & (_BYTE - 1)).astype(F32), lax.shift_right_logical(cid, _BYTE_BITS).astype(F32),
                  jnp.ones((1, rows), F32)]
        b_parts.append([jnp.where(hit, f, 0.0).astype(BF16) for f in fields])
    a = jnp.concatenate(a_parts, axis=1)
    for j in range(_INV_FIELDS):
        b = jnp.concatenate([b_parts[k][j] for k in range(TOP_K)], axis=1)
        acc_ref[j] += _nt_dot(a, b)


def _invert(eidx, prel, qloc, region_start, n_blocks, n_tiles):
    n_all = eidx.shape[1]
    n_chunks = n_tiles * TILE_PACK // CHUNK_ROWS
    assert n_chunks < _BYTE * _BYTE, "chunk id is carried as two byte-sized fields"
    col = pl.BlockSpec((TOP_K, ROW_TILE), lambda i: (0, i))
    shape = (_INV_FIELDS, n_blocks, LANES)
    f = pl.pallas_call(
        functools.partial(_invert_kernel, n_blocks),
        grid=(n_all // ROW_TILE,),
        in_specs=[col, col, col, _const_spec((N_EXPERTS, 1))],
        out_specs=_resident_out_spec(shape),
        out_shape=jax.ShapeDtypeStruct(shape, F32),
        compiler_params=_params(("arbitrary",)),
        name="invert",
    )(eidx, prel, qloc, region_start)
    cid = (f[0] + float(_BYTE) * f[1]).astype(jnp.int32)
    valid = f[2] > 0.0
    blk = lax.broadcasted_iota(jnp.int32, cid.shape, 0)
    off = lax.broadcasted_iota(jnp.int32, cid.shape, 1)
    dump = n_chunks + (blk & 1) * CHUNKS_PER_BLOCK + jnp.minimum(off, CHUNKS_PER_BLOCK - 1)
    return jnp.where(valid, cid, n_chunks + 3 * CHUNKS_PER_BLOCK), jnp.where(valid, cid, dump)


def _ffn_kernel(be_ref, nused_ref, src_cur_ref, src_nxt_ref, dst_cur_ref, dst_prev_ref,
                wup_ref, bup_ref, wdn_ref, bdn_ref, xts_hbm, out_hbm, xbuf0, xbuf1, obuf0, obuf1, wup_bf, wdn_bf,
                gsem, ssem):
    i = pl.program_id(0)
    n_used = nused_ref[0]
    xbufs, obufs = (xbuf0, xbuf1), (obuf0, obuf1)
    chunk_len = CHUNK_ROWS * TILE_ROWS

    def hbm_chunk(ref, cid):
        return ref.at[pl.ds(pl.multiple_of(cid * chunk_len, chunk_len), chunk_len), :]

    def vmem_chunk(ref, c):
        return ref.at[pl.ds(c * chunk_len, chunk_len), :]

    def gather_copy(src_ref, c, s):
        return pltpu.make_async_copy(hbm_chunk(xts_hbm, src_ref[0, 0, c]), vmem_chunk(xbufs[s], c), gsem.at[s])

    def scatter_copy(dst_ref, c, s):
        return pltpu.make_async_copy(vmem_chunk(obufs[s], c), hbm_chunk(out_hbm, dst_ref[0, 0, c]), ssem.at[s])

    GATHER_PRIORITY, SCATTER_PRIORITY = 1, 0

    def start_rows(copy_fn, priority):
        for c in range(CHUNKS_PER_BLOCK):
            copy_fn(c).start(priority=priority)

    def wait_rows(copy_fn):
        for c in range(CHUNKS_PER_BLOCK):
            copy_fn(c).wait()

    first = i == 0
    changed = first | (be_ref[i] != be_ref[jnp.maximum(i - 1, 0)])

    @pl.when(first)
    def _():
        obuf0[...] = jnp.zeros(obuf0.shape, obuf0.dtype)
        obuf1[...] = jnp.zeros(obuf1.shape, obuf1.dtype)
        spare0 = out_hbm.shape[0] // chunk_len - TILE_PACK // CHUNK_ROWS
        start_rows(lambda c: pltpu.make_async_copy(
            vmem_chunk(obuf0, c), out_hbm.at[pl.ds((spare0 + c) * chunk_len, chunk_len), :], ssem.at[0]),
            SCATTER_PRIORITY)
        start_rows(lambda c: gather_copy(src_cur_ref, c, 0), GATHER_PRIORITY)

    @pl.when(changed & (i < n_used))
    def _():
        wup_bf[...] = wup_ref[0].astype(BF16)
        wdn_bf[...] = wdn_ref[0].astype(BF16)

    def step(s):
        wait_rows(lambda r: gather_copy(src_cur_ref, r, s))
        x = _load_token_tiles(xbufs[s], FFN_BLOCK).astype(BF16)
        start_rows(lambda r: gather_copy(src_nxt_ref, r, 1 - s), GATHER_PRIORITY)
        start_rows(lambda r: scatter_copy(dst_prev_ref, r, 1 - s), SCATTER_PRIORITY)
        hu = _dot(x, wup_bf[...]) + bup_ref[0]
        glu = jnp.minimum(hu[:, :D_FF], SWIGLU_LIMIT)
        lin = jnp.clip(hu[:, D_FF:], -SWIGLU_LIMIT, SWIGLU_LIMIT)
        act = glu * jax.nn.sigmoid(SWIGLU_ALPHA * glu) * (lin + 1.0)
        out = _dot(act.astype(BF16), wdn_bf[...]) + bdn_ref[0]
        wait_rows(lambda r: scatter_copy(dst_cur_ref, r, s))
        _store_token_tiles(obufs[s], out)

    def drain(s):
        start_rows(lambda r: scatter_copy(dst_cur_ref, r, s), SCATTER_PRIORITY)
        wait_rows(lambda r: scatter_copy(dst_cur_ref, r, 1 - s))
        wait_rows(lambda r: scatter_copy(dst_cur_ref, r, s))
        wait_rows(lambda r: gather_copy(src_cur_ref, r, 1 - s))

    for s in range(2):
        pl.when((i < n_used) & (i % 2 == s))(functools.partial(step, s))
    for s in range(2):
        pl.when((i == n_used - 1) & (i % 2 == s))(functools.partial(drain, s))


def _ffn(block_expert, n_used, src, dst, x_ts, w_up, b_up, w_down, b_down):
    n_blocks = block_expert.shape[0]
    smem_block = lambda fn: pl.BlockSpec((1, 1, LANES), fn, memory_space=pltpu.SMEM)
    cur = smem_block(lambda i, be, nu: (i, 0, 0))
    nxt = smem_block(lambda i, be, nu: (jnp.minimum(i + 1, n_blocks - 1), 0, 0))
    prev = smem_block(lambda i, be, nu: (jnp.where(i == 0, n_blocks, i - 1), 0, 0))
    ex3 = lambda i, be, nu: (be[i], 0, 0)
    return pl.pallas_call(
        _ffn_kernel,
        grid_spec=pltpu.PrefetchScalarGridSpec(
            num_scalar_prefetch=2,
            grid=(n_blocks,),
            in_specs=[
                cur, nxt, cur, prev,
                pl.BlockSpec((1, D_MODEL, 2 * D_FF), ex3),
                pl.BlockSpec((1, 1, 2 * D_FF), ex3),
                pl.BlockSpec((1, D_FF, D_MODEL), ex3),
                pl.BlockSpec((1, 1, D_MODEL), ex3),
                pl.BlockSpec(memory_space=pl.ANY),
            ],
            out_specs=pl.BlockSpec(memory_space=pl.ANY),
            scratch_shapes=[pltpu.VMEM((FFN_BLOCK * TILE_ROWS, LANES), F32)] * 4 + [
                pltpu.VMEM((D_MODEL, 2 * D_FF), BF16), pltpu.VMEM((D_FF, D_MODEL), BF16),
                pltpu.SemaphoreType.DMA((2,)), pltpu.SemaphoreType.DMA((2,))],
        ),
        out_shape=jax.ShapeDtypeStruct(x_ts.shape, F32),
        input_output_aliases={10: 0},
        compiler_params=_params(("arbitrary",)),
        name="ffn",
    )(block_expert, n_used, src, src, dst, dst, w_up, b_up, w_down, b_down, x_ts)


def _final_kernel(q_ref, gate_ref, x1_ref, yts_ref, g_ref, y_ref, sum_scr):
    def body(t, carry):
        acc = _tile_rows(x1_ref, t)[...]
        for k in range(TOP_K):
            slot = k * ROW_TILE + t
            acc = acc + gate_ref[0, 0, slot] * _tile_at(yts_ref, q_ref[0, 0, slot])[...]
        _tile_rows(sum_scr, t)[...] = acc
        return carry

    lax.fori_loop(0, ROW_TILE, body, 0, unroll=8)
    x2 = _load_token_tiles(sum_scr, ROW_TILE)
    ms = jnp.mean(x2 * x2, axis=-1, keepdims=True)
    y_ref[...] = x2 * lax.rsqrt(ms + EPS) * g_ref[...]


def _final(qloc_tiles, gate_tiles, x1, y_ts, g_final, row0, n_rows):
    t0 = row0 // ROW_TILE
    smem_tile = pl.BlockSpec((1, 1, TOP_K * ROW_TILE), lambda i: (t0 + i, 0, 0), memory_space=pltpu.SMEM)
    return pl.pallas_call(
        _final_kernel,
        grid=(n_rows // ROW_TILE,),
        in_specs=[smem_tile, smem_tile,
                  pl.BlockSpec((ROW_TILE * TILE_ROWS, LANES), lambda i: (t0 + i, 0)),
                  pl.BlockSpec((TILE_PACK * TILE_ROWS, LANES), lambda i: (t0 + i, 0)),
                  _const_spec((1, D_MODEL))],
        out_specs=pl.BlockSpec((ROW_TILE, D_MODEL), lambda i: (i, 0)),
        out_shape=jax.ShapeDtypeStruct((n_rows, D_MODEL), F32),
        scratch_shapes=[pltpu.VMEM((ROW_TILE * TILE_ROWS, LANES), F32)],
        compiler_params=_params(("arbitrary",)),
        name="final",
    )(qloc_tiles, gate_tiles, x1, y_ts, g_final)


def _rope_tables(seq_len, dec_seq):
    inv = ROPE_THETA ** (-np.arange(0, ROT_DIM, 2, dtype=np.float64) / ROT_DIM)
    pos = np.concatenate([np.arange(seq_len), PAST_LEN + np.arange(ROW_TILE) % dec_seq]).astype(np.float64)
    ang = pos[:, None] * inv[None, :]
    cos, sin = np.cos(ang), np.sin(ang)
    ones = np.ones((pos.shape[0], HEAD_DIM - ROT_DIM))
    cos_h = np.concatenate([cos, cos, ones], axis=1)
    sin_h = np.concatenate([-sin, sin, 0.0 * ones], axis=1)
    return (jnp.asarray(np.concatenate([cos_h, cos_h], axis=1), F32),
            jnp.asarray(np.concatenate([sin_h, sin_h], axis=1), F32))


def _block_tables(counts, n_all):
    max_rows = TOP_K * n_all + (n_all // ROW_TILE) * N_EXPERTS * (CHUNK_ROWS - 1)
    n_blocks = -(-max_rows // FFN_BLOCK) + N_EXPERTS
    padded = (counts + FFN_BLOCK - 1) // FFN_BLOCK * FFN_BLOCK
    pad_end = jnp.cumsum(padded)
    pad_start = (pad_end - padded).astype(jnp.int32).reshape(N_EXPERTS, 1)
    block_start = jnp.arange(n_blocks, dtype=jnp.int32) * FFN_BLOCK
    block_expert = jnp.minimum(jnp.sum(pad_end[None, :] <= block_start[:, None], axis=1), N_EXPERTS - 1)
    n_used = (pad_end[-1] // FFN_BLOCK).astype(jnp.int32).reshape(1)
    return pad_start, block_expert.astype(jnp.int32), n_used, n_blocks


def kernel(x_prompt, x_sample, cache_swa_k, cache_swa_v, state_gla, g_mix, w_in, w_gk_up, b_gk, sinks,
           gla_norm, w_branch_a, w_branch_b, w_out, g_ffn, router_w, router_b, w_up, b_up, w_down, b_down,
           g_final):
    batch, seq_len, _ = x_prompt.shape
    dec_batch, dec_seq, _ = x_sample.shape
    n_p, n_s = batch * seq_len, dec_batch * dec_seq
    n_all = n_p + n_s
    assert w_in.shape[0] == 1, "one layer: the final norm is fused after the only MoE"
    assert seq_len % ROW_TILE == 0 and n_s % ROW_TILE == 0 and ROW_TILE % dec_seq == 0
    assert dec_seq % SUBLANES == 0 and dec_batch % SAMPLE_GROUP == 0
    assert SAMPLE_GROUP * dec_seq == CHUNK and (dec_seq & (dec_seq - 1)) == 0

    xp2 = x_prompt.reshape(n_p, D_MODEL)
    xs2 = x_sample.reshape(n_s, D_MODEL)
    cos_tab, sin_tab = _rope_tables(seq_len, dec_seq)
    w = w_in[0]
    w_a = w[:, :_C_GL].astype(BF16)
    w_b = w[:, _C_GL + GK_RANK:].astype(BF16)
    w_gk = jnp.pad(w[:, _C_GL:_C_GL + GK_RANK], ((0, 0), (0, LANES - GK_RANK))).astype(BF16)
    wup_pad = jnp.pad(w_gk_up[0], ((0, LANES - GK_RANK), (0, 0))).astype(BF16)
    qa, ka, va, qg, kg, vg, la, rg, ga, gb = _proj(
        xp2, xs2, g_mix[0].reshape(1, D_MODEL), cos_tab, sin_tab, w_a, w_b, w_gk, wup_pad,
        b_gk[0].reshape(1, GLA_KDIM), seq_len)

    oa_p = _swa_prompt(sinks[0], qa, ka, va, batch, seq_len)
    oa_s, nk_s, nv_s = _swa_sample(sinks[0], qa, ka, va, cache_swa_k[0], cache_swa_v[0], n_p,
                                   dec_batch, dec_seq)
    norm_row = jnp.tile(gla_norm[0], GLA_HEADS).reshape(1, GLA_VDIM)
    og_p, s_fin = _gla_prompt(qg, kg, la, vg, rg, norm_row, batch, seq_len)
    og_s, s_new = _gla_sample(qg, kg, la, vg, rg, norm_row, state_gla[0], n_p, dec_batch, dec_seq)

    x1 = _merge(xp2, xs2, oa_p, oa_s, og_p, og_s, ga, gb, w_branch_a[0].astype(BF16),
                w_branch_b[0].astype(BF16), w_out[0].astype(BF16))

    g_ffn_row = g_ffn[0].reshape(1, D_MODEL)
    rw_t = router_w[0].T
    rw_hi = rw_t.astype(BF16)
    rw_lo = (rw_t - rw_hi.astype(F32)).astype(BF16)
    eidx, prel, qloc, gate, cnt = _route(x1, g_ffn_row, rw_hi, rw_lo, router_b[0].reshape(N_EXPERTS, 1))
    n_tiles = n_all // ROW_TILE
    region_start, block_expert, n_used, n_blocks = _block_tables(cnt[:, 0].astype(jnp.int32), n_all)
    per_tile = lambda a: a.reshape(TOP_K, n_tiles, ROW_TILE).transpose(1, 0, 2).reshape(n_tiles, 1, TOP_K * ROW_TILE)
    qloc_tiles, gate_tiles = per_tile(qloc * TILE_ROWS), per_tile(gate)
    x_ts = _pack(qloc_tiles, x1, g_ffn_row)
    src, dst = _invert(eidx, prel, qloc, region_start, n_blocks, n_tiles)
    prime = (n_tiles * TILE_PACK // CHUNK_ROWS + 2 * CHUNKS_PER_BLOCK
             + jnp.minimum(jnp.arange(LANES, dtype=jnp.int32), CHUNKS_PER_BLOCK - 1)).reshape(1, LANES)
    dst = jnp.concatenate([dst, prime], axis=0)
    y_ts = _ffn(block_expert, n_used, src.reshape(n_blocks, 1, LANES), dst.reshape(n_blocks + 1, 1, LANES),
                x_ts, w_up[0], b_up[0].reshape(N_EXPERTS, 1, 2 * D_FF), w_down[0],
                b_down[0].reshape(N_EXPERTS, 1, D_MODEL))

    g_out = g_final.reshape(1, D_MODEL)
    y_p = _final(qloc_tiles, gate_tiles, x1, y_ts, g_out, 0, n_p)
    y_s = _final(qloc_tiles, gate_tiles, x1, y_ts, g_out, n_p, n_s)

    kv_shape = (1, -1, WINDOW, KV_HEADS, HEAD_DIM)
    new_k_p = ka[:n_p].reshape(batch, seq_len, KV_WIDTH)[:, -WINDOW:].reshape(kv_shape)
    new_v_p = va[:n_p].reshape(batch, seq_len, KV_WIDTH)[:, -WINDOW:].reshape(kv_shape)
    return (y_p.reshape(batch, seq_len, D_MODEL), y_s.reshape(dec_batch, dec_seq, D_MODEL),
            new_k_p, new_v_p, s_fin[None], nk_s.reshape(kv_shape), nv_s.reshape(kv_shape), s_new[None])
```

```python
import functools

import numpy as np
import jax
import jax.numpy as jnp
from jax import lax
from jax.experimental import pallas as pl
from jax.experimental.pallas import tpu as pltpu

D_MODEL = 1024
PAST_LEN = 8192
HEAD_DIM = 64
N_HEADS = 8
KV_HEADS = 2
GROUP = N_HEADS // KV_HEADS
WINDOW = 128
ROT_DIM = HEAD_DIM // 4
ROPE_THETA = 500000.0
ATT_WIDTH = N_HEADS * HEAD_DIM
KV_WIDTH = KV_HEADS * HEAD_DIM
GLA_HEADS = 4
GLA_KDIM = D_MODEL // 2
GLA_VDIM = D_MODEL
GLA_DK = GLA_KDIM // GLA_HEADS
GLA_DV = GLA_VDIM // GLA_HEADS
GK_RANK = 16
GK_NORMALIZER = 16.0
N_EXPERTS = 32
TOP_K = 4
D_FF = D_MODEL
SWIGLU_LIMIT = 7.0
SWIGLU_ALPHA = 1.702
EPS = 1e-5
NEG_INF = -1e30

LANES = 128
SUBLANES = 8
VMEM_LIMIT_BYTES = 56 * 1024 * 1024

ROW_TILE = 512
CHUNK = 128
FFN_BLOCK = 512

BF16 = jnp.bfloat16
F32 = jnp.float32

_C_QA, _C_KA, _C_VA, _C_QG, _C_KG, _C_VG, _C_GL = 0, 512, 640, 768, 1280, 1792, 2816


def _const_spec(shape):
    nd = len(shape)
    return pl.BlockSpec(shape, lambda *_: (0,) * nd, pipeline_mode=pl.Buffered(1))


def _resident_out_spec(shape):
    nd = len(shape)
    return pl.BlockSpec(shape, lambda *_: (0,) * nd)


def _params(sem, vmem=VMEM_LIMIT_BYTES):
    return pltpu.CompilerParams(dimension_semantics=sem, vmem_limit_bytes=vmem)


def _nt_dot(a, b):
    return lax.dot_general(a, b, (((1,), (1,)), ((), ())), preferred_element_type=F32)


def _dot(a, b):
    return jnp.dot(a, b, preferred_element_type=F32)


TILE_ROWS = D_MODEL // LANES
assert TILE_ROWS == SUBLANES


def _load_token_tiles(ref, n_tokens):
    return jnp.concatenate([ref[pl.ds(c, n_tokens, stride=TILE_ROWS), :] for c in range(TILE_ROWS)], axis=1)


def _store_token_tiles(ref, x):
    for c in range(TILE_ROWS):
        ref[pl.ds(c, x.shape[0], stride=TILE_ROWS), :] = x[:, c * LANES:(c + 1) * LANES]


def _rope(x, cos_t, sin_t, n_rep):
    width = x.shape[1]
    cos_f = jnp.concatenate([cos_t] * n_rep, axis=1) if n_rep > 1 else cos_t
    sin_f = jnp.concatenate([sin_t] * n_rep, axis=1) if n_rep > 1 else sin_t
    lane = lax.broadcasted_iota(jnp.int32, x.shape, 1) % HEAD_DIM
    up = pltpu.roll(x, width - ROT_DIM // 2, 1)
    down = pltpu.roll(x, ROT_DIM // 2, 1)
    partner = jnp.where(lane < ROT_DIM // 2, up, down)
    return x * cos_f + partner * sin_f


def _proj_kernel(n_prompt_tiles, xp_ref, xs_ref, g_ref, cos_ref, sin_ref, wa_ref, wb_ref, wgk_ref, wup_ref, bgk_ref,
                 qa_ref, ka_ref, va_ref, qg_ref, kg_ref, vg_ref, la_ref, rg_ref, ga_ref, gb_ref):
    i = pl.program_id(0)
    x = jnp.where(i < n_prompt_tiles, xp_ref[...], xs_ref[...])
    ms = jnp.mean(x * x, axis=-1, keepdims=True)
    h = (x * lax.rsqrt(ms + EPS) * g_ref[...]).astype(BF16)
    cos_t = cos_ref[...]
    sin_t = sin_ref[...]

    def seg(w_ref, a, b):
        return _dot(h, w_ref[:, a:b])

    qa = _rope(seg(wa_ref, _C_QA, _C_KA), cos_t, sin_t, ATT_WIDTH // LANES)
    qa_ref[...] = (qa * (HEAD_DIM ** -0.5)).astype(BF16)
    ka_ref[...] = _rope(seg(wa_ref, _C_KA, _C_VA), cos_t, sin_t, 1)
    va_ref[...] = seg(wa_ref, _C_VA, _C_QG)
    qg_ref[...] = seg(wa_ref, _C_QG, _C_KG) * (GLA_DK ** -0.5)
    kg_ref[...] = seg(wa_ref, _C_KG, _C_VG)
    vg_ref[...] = seg(wa_ref, _C_VG, _C_GL).astype(BF16)
    rg_ref[...] = seg(wb_ref, 0, GLA_VDIM).astype(BF16)
    ga_ref[...] = seg(wb_ref, GLA_VDIM, GLA_VDIM + D_MODEL).astype(BF16)
    gb_ref[...] = seg(wb_ref, GLA_VDIM + D_MODEL, GLA_VDIM + 2 * D_MODEL).astype(BF16)
    gk_low = _dot(h, wgk_ref[...]).astype(BF16)
    z = _dot(gk_low, wup_ref[...]) + bgk_ref[...]
    log_sig = jnp.minimum(z, 0.0) - jnp.log1p(jnp.exp(-jnp.abs(z)))
    la_ref[...] = log_sig / GK_NORMALIZER


def _proj(xp2, xs2, g_mix, cos_tab, sin_tab, w_a, w_b, w_gk, wup_pad, b_gk, seq_len):
    n_p, n_s = xp2.shape[0], xs2.shape[0]
    n_all = n_p + n_s
    npt, nst = n_p // ROW_TILE, n_s // ROW_TILE
    tiles_per_seq = seq_len // ROW_TILE

    def tab_map(i):
        return (jnp.where(i < npt, i % tiles_per_seq, tiles_per_seq), 0)

    row = lambda w: pl.BlockSpec((ROW_TILE, w), lambda i: (i, 0))
    widths = [(ATT_WIDTH, BF16), (KV_WIDTH, F32), (KV_WIDTH, F32), (GLA_KDIM, F32), (GLA_KDIM, F32),
              (GLA_VDIM, BF16), (GLA_KDIM, F32), (GLA_VDIM, BF16), (D_MODEL, BF16), (D_MODEL, BF16)]
    return pl.pallas_call(
        functools.partial(_proj_kernel, npt),
        grid=(npt + nst,),
        in_specs=[
            pl.BlockSpec((ROW_TILE, D_MODEL), lambda i: (jnp.minimum(i, npt - 1), 0)),
            pl.BlockSpec((ROW_TILE, D_MODEL), lambda i: (jnp.maximum(i - npt, 0), 0)),
            _const_spec((1, D_MODEL)),
            pl.BlockSpec((ROW_TILE, LANES), tab_map),
            pl.BlockSpec((ROW_TILE, LANES), tab_map),
            _const_spec(w_a.shape), _const_spec(w_b.shape), _const_spec(w_gk.shape),
            _const_spec(wup_pad.shape),
            _const_spec((1, GLA_KDIM)),
        ],
        out_specs=[row(w) for w, _ in widths],
        out_shape=[jax.ShapeDtypeStruct((n_all, w), dt) for w, dt in widths],
        compiler_params=_params(("arbitrary",)),
        name="proj",
    )(xp2, xs2, g_mix, cos_tab, sin_tab, w_a, w_b, w_gk, wup_pad, b_gk)


def _pair_blocks(kk):
    lane = lax.broadcasted_iota(jnp.int32, kk.shape, 1)
    lo = lane < HEAD_DIM
    swapped = pltpu.roll(kk, HEAD_DIM, 1)
    zero = jnp.zeros_like(kk)
    blocks = []
    for kh in range(KV_HEADS):
        left = jnp.where(lo, kk if kh == 0 else swapped, zero)
        right = jnp.where(lo, zero, swapped if kh == 0 else kk)
        blocks.append(jnp.concatenate([left, right], axis=0).astype(BF16))
    return blocks


def _sink_softmax(s, valid, sink):
    s = jnp.where(valid, s, NEG_INF)
    m = jnp.maximum(jnp.max(s, axis=-1, keepdims=True), sink)
    p = jnp.exp(s - m)
    denom = jnp.sum(p, axis=-1, keepdims=True) + jnp.exp(sink - m)
    return p.astype(BF16), 1.0 / denom


def _attend(q, kk, vv, valid, sink_ref, o_ref, row0=0):
    rows, keys = valid.shape
    kblocks = _pair_blocks(kk)
    vblocks = _pair_blocks(vv)
    for kh in range(KV_HEADS):
        base = kh * GROUP * HEAD_DIM
        qq = jnp.concatenate([q[:, base:base + LANES], q[:, base + LANES:base + 2 * LANES]], axis=0)
        s = _nt_dot(qq, kblocks[kh])
        first_head = lax.broadcasted_iota(jnp.int32, (rows, LANES), 1) < HEAD_DIM
        for r in range(2):
            probs, scales = [], []
            for c in range(2):
                head = kh * GROUP + 2 * r + c
                p_c, inv_c = _sink_softmax(s[r * rows:(r + 1) * rows, c * keys:(c + 1) * keys],
                                           valid, sink_ref[head])
                probs.append(p_c)
                scales.append(inv_c)
            p = jnp.concatenate(probs, axis=1)
            scale = jnp.where(first_head, scales[0], scales[1])
            o_ref[pl.ds(row0, rows), base + r * LANES:base + (r + 1) * LANES] = (
                _dot(p, vblocks[kh]) * scale).astype(o_ref.dtype)


def _swa_prompt_kernel(sink_ref, q_ref, kp_ref, k0_ref, k1_ref, vp_ref, v0_ref, v1_ref, o_ref):
    j = pl.program_id(1)
    row = lax.broadcasted_iota(jnp.int32, (WINDOW, 2 * WINDOW), 0)
    col = lax.broadcasted_iota(jnp.int32, (WINDOW, 2 * WINDOW), 1)
    band = (col > row) & (col <= row + WINDOW)
    k_blocks = (kp_ref[...], k0_ref[...], k1_ref[...])
    v_blocks = (vp_ref[...], v0_ref[...], v1_ref[...])
    q = q_ref[...]
    for half in range(2):
        kk = jnp.concatenate(k_blocks[half:half + 2], axis=0)
        vv = jnp.concatenate(v_blocks[half:half + 2], axis=0)
        valid = band & ((j > 0) | (col >= WINDOW)) if half == 0 else band
        _attend(q[half * WINDOW:(half + 1) * WINDOW], kk, vv, valid, sink_ref, o_ref, half * WINDOW)


def _swa_prompt(sinks, qa, ka, va, batch, seq_len):
    nb = seq_len // WINDOW
    assert nb % 2 == 0
    kv = lambda off: pl.BlockSpec((WINDOW, KV_WIDTH), lambda b, j, s: (b * nb + jnp.maximum(2 * j + off, 0), 0))
    pair = pl.BlockSpec((2 * WINDOW, ATT_WIDTH), lambda b, j, s: (b * (nb // 2) + j, 0))
    return pl.pallas_call(
        _swa_prompt_kernel,
        grid_spec=pltpu.PrefetchScalarGridSpec(
            num_scalar_prefetch=1,
            grid=(batch, nb // 2),
            in_specs=[pair, kv(-1), kv(0), kv(1), kv(-1), kv(0), kv(1)],
            out_specs=pair,
        ),
        out_shape=jax.ShapeDtypeStruct((batch * seq_len, ATT_WIDTH), BF16),
        compiler_params=_params(("arbitrary", "arbitrary")),
        name="swa_prompt",
    )(sinks, qa, ka, ka, ka, va, va, va)


SAMPLE_GROUP = 16


def _swa_sample_kernel(dec_seq, sink_ref, q_ref, kn_ref, vn_ref, ck_ref, cv_ref, o_ref, nk_ref, nv_ref):
    rows = SAMPLE_GROUP * dec_seq
    ck = ck_ref[...]
    cv = cv_ref[...]
    kn = kn_ref[...]
    vn = vn_ref[...]
    nk_ref[:, :WINDOW - dec_seq, :] = ck[:, dec_seq:, :]
    nv_ref[:, :WINDOW - dec_seq, :] = cv[:, dec_seq:, :]
    nk_ref[:, WINDOW - dec_seq:, :] = kn.reshape(SAMPLE_GROUP, dec_seq, KV_WIDTH)
    nv_ref[:, WINDOW - dec_seq:, :] = vn.reshape(SAMPLE_GROUP, dec_seq, KV_WIDTH)
    n_cache = SAMPLE_GROUP * WINDOW
    kk = jnp.concatenate([ck.reshape(n_cache, KV_WIDTH), kn], axis=0)
    vv = jnp.concatenate([cv.reshape(n_cache, KV_WIDTH), vn], axis=0)
    keys = n_cache + rows
    row = lax.broadcasted_iota(jnp.int32, (rows, keys), 0)
    col = lax.broadcasted_iota(jnp.int32, (rows, keys), 1)
    q_b, q_s = row // dec_seq, row % dec_seq
    is_cache = col < n_cache
    new = col - n_cache
    valid_cache = (col // WINDOW == q_b) & (col % WINDOW > q_s)
    valid_new = (new // dec_seq == q_b) & (new % dec_seq <= q_s)
    valid = (is_cache & valid_cache) | (jnp.logical_not(is_cache) & valid_new)
    _attend(q_ref[...], kk, vv, valid, sink_ref, o_ref)


def _swa_sample(sinks, qa, ka, va, cache_k, cache_v, n_prompt_rows, dec_batch, dec_seq):
    rows = SAMPLE_GROUP * dec_seq
    off = n_prompt_rows // rows
    tok = lambda g, s: (off + g, 0)
    cache = lambda g, s: (g, 0, 0)
    cshape = (dec_batch, WINDOW, KV_WIDTH)
    return pl.pallas_call(
        functools.partial(_swa_sample_kernel, dec_seq),
        grid_spec=pltpu.PrefetchScalarGridSpec(
            num_scalar_prefetch=1,
            grid=(dec_batch // SAMPLE_GROUP,),
            in_specs=[
                pl.BlockSpec((rows, ATT_WIDTH), tok),
                pl.BlockSpec((rows, KV_WIDTH), tok),
                pl.BlockSpec((rows, KV_WIDTH), tok),
                pl.BlockSpec((SAMPLE_GROUP, WINDOW, KV_WIDTH), cache),
                pl.BlockSpec((SAMPLE_GROUP, WINDOW, KV_WIDTH), cache),
            ],
            out_specs=[
                pl.BlockSpec((rows, ATT_WIDTH), lambda g, s: (g, 0)),
                pl.BlockSpec((SAMPLE_GROUP, WINDOW, KV_WIDTH), cache),
                pl.BlockSpec((SAMPLE_GROUP, WINDOW, KV_WIDTH), cache),
            ],
        ),
        out_shape=[jax.ShapeDtypeStruct((dec_batch * dec_seq, ATT_WIDTH), BF16),
                   jax.ShapeDtypeStruct(cshape, F32), jax.ShapeDtypeStruct(cshape, F32)],
        compiler_params=_params(("arbitrary",)),
        name="swa_sample",
    )(sinks, qa, ka, va, cache_k.reshape(cshape), cache_v.reshape(cshape))


def _chunk_tables(seg):
    n_lev = int(np.log2(seg))
    t = np.arange(CHUNK)
    seg_start = (t // seg) * seg
    u = np.arange(CHUNK)[None, :]

    def prefix(end):
        return ((u >= seg_start[:, None]) & (u <= end[:, None])).astype(np.float32)

    blocks = [prefix(t)]
    for d in range(min(n_lev, _MATMUL_LEVELS)):
        m = 1 << d
        ref = (t >> (d + 1) << (d + 1)) + m - 1
        blocks.append(prefix(ref))
    lhs = np.concatenate(blocks, axis=0)
    lhs2 = np.concatenate([lhs, lhs], axis=1)
    tt, ss = t[:, None], t[None, :]
    x = tt ^ ss
    lev = np.where(x > 0, np.floor(np.log2(np.maximum(x, 1))).astype(np.int32), n_lev)
    lev = np.where((ss > tt) | (tt // seg != ss // seg), -1, lev)
    lev = np.where(tt == ss, n_lev, lev)
    return jnp.asarray(lhs2, BF16), jnp.asarray(lev, jnp.int32), n_lev


_MATMUL_LEVELS = 3


def _group_row(x, group, row):
    width = x.shape[1]
    parts = [jnp.broadcast_to(x[g * group + row:g * group + row + 1, :], (group, width))
             for g in range(x.shape[0] // group)]
    return parts[0] if len(parts) == 1 else jnp.concatenate(parts, axis=0)


_HALF_ROW_LEVELS = 4
LOG2_E = 1.4426950408889634


def _halves(x, m, which):
    parts = [x[(2 * g + which) * m:(2 * g + which + 1) * m] for g in range(x.shape[0] // (2 * m))]
    return parts[0] if len(parts) == 1 else jnp.concatenate(parts, axis=0)


def _unhalve(xh, m, which):
    zero = jnp.zeros((m, xh.shape[1]), xh.dtype)
    parts = []
    for g in range(xh.shape[0] // m):
        blk = xh[g * m:(g + 1) * m]
        parts += [zero, blk] if which else [blk, zero]
    return jnp.concatenate(parts, axis=0)


def _gla_chunk_terms(q, k, la, lhs2, level, n_lev, seg):
    la2 = la * LOG2_E
    hi = la2.astype(BF16)
    lo = (la2 - hi.astype(F32)).astype(BF16)
    sums = _dot(lhs2, jnp.concatenate([hi, lo], axis=0))
    b = sums[0:CHUNK]
    b_last = _group_row(b, seg, seg - 1)
    q_main = (q * jnp.exp2(b)).astype(BF16)
    k_upd = k * jnp.exp2(b_last - b)
    q_lev = [None] * n_lev
    k_lev = [None] * n_lev
    for d in range(n_lev):
        m = 1 << d
        if d < _MATMUL_LEVELS:
            ref = sums[(1 + d) * CHUNK:(2 + d) * CHUNK]
        elif d < _HALF_ROW_LEVELS:
            ref = _group_row(b, 2 * m, m - 1)
        if d < _HALF_ROW_LEVELS:
            q_lev[d] = (q * jnp.exp2(b - ref)).astype(BF16)
            k_lev[d] = (k * jnp.exp2(ref - b)).astype(BF16)
        else:
            ref_h = _group_row(_halves(b, m, 0), m, m - 1)
            q_lev[d] = _unhalve((_halves(q, m, 1) * jnp.exp2(_halves(b, m, 1) - ref_h)).astype(BF16), m, 1)
            k_lev[d] = _unhalve((_halves(k, m, 0) * jnp.exp2(ref_h - _halves(b, m, 0))).astype(BF16), m, 0)
    q_b, k_b = q.astype(BF16), k.astype(BF16)

    def att(h):
        hs = slice(h * GLA_DK, (h + 1) * GLA_DK)
        acc = jnp.where(level == n_lev, _nt_dot(q_b[:, hs], k_b[:, hs]), 0.0)
        for d in range(n_lev):
            acc = jnp.where(level == d, _nt_dot(q_lev[d][:, hs], k_lev[d][:, hs]), acc)
        return acc

    return q_main, k_upd, att, b_last


def _gla_out(o, r, norm):
    parts = []
    for h in range(GLA_HEADS):
        oh = o[:, h * GLA_DV:(h + 1) * GLA_DV]
        ms = jnp.mean(oh * oh, axis=-1, keepdims=True)
        parts.append(oh * lax.rsqrt(ms + EPS))
    y = jnp.concatenate(parts, axis=1) * norm
    rf = r.astype(F32)
    return y * (rf * jax.nn.sigmoid(rf))


def _gla_prompt_kernel(n_lev, n_par, *refs):
    seq_refs = [refs[5 * j:5 * j + 5] for j in range(n_par)]
    norm_ref, lhs_ref, lev_ref = refs[5 * n_par:5 * n_par + 3]
    o_ref, sfin_ref, s_scr = refs[5 * n_par + 3:]
    c = pl.program_id(1)

    @pl.when(c == 0)
    def _():
        s_scr[...] = jnp.zeros_like(s_scr)

    for j, (q_ref, k_ref, la_ref, v_ref, r_ref) in enumerate(seq_refs):
        q_main, k_upd, att, b_last = _gla_chunk_terms(q_ref[...], k_ref[...], la_ref[...], lhs_ref[...],
                                                      lev_ref[...], n_lev, CHUNK)
        v = v_ref[...]
        outs = []
        for h in range(GLA_HEADS):
            hs = slice(h * GLA_DK, (h + 1) * GLA_DK)
            vh = v[:, h * GLA_DV:(h + 1) * GLA_DV]
            s0 = s_scr[j, h]
            o_h = _dot(q_main[:, hs], s0.astype(BF16)) + _dot(att(h).astype(BF16), vh)
            outs.append(o_h)
            decay = jnp.exp2(b_last[:, hs]).T
            k_t = k_upd[:, hs].T.astype(BF16)
            s_scr[j, h] = jnp.concatenate([decay, decay], axis=1) * s0 + _dot(k_t, vh)
        o_ref[j] = _gla_out(jnp.concatenate(outs, axis=1), r_ref[...], norm_ref[...]).astype(BF16)

    @pl.when(c == pl.num_programs(1) - 1)
    def _():
        sfin_ref[...] = s_scr[...]


def _gla_prompt(qg, kg, la, vg, rg, norm_row, batch, seq_len):
    nc = seq_len // CHUNK
    n_par = 4 if batch % 4 == 0 else (2 if batch % 2 == 0 else 1)
    lhs2, level, n_lev = _chunk_tables(CHUNK)
    tok = lambda j, w: pl.BlockSpec((CHUNK, w), lambda b, c: ((b * n_par + j) * nc + c, 0))
    seq_specs, seq_args = [], []
    for j in range(n_par):
        seq_specs += [tok(j, GLA_KDIM), tok(j, GLA_KDIM), tok(j, GLA_KDIM), tok(j, GLA_VDIM), tok(j, GLA_VDIM)]
        seq_args += [qg, kg, la, vg, rg]
    og, s_fin = pl.pallas_call(
        functools.partial(_gla_prompt_kernel, n_lev, n_par),
        grid=(batch // n_par, nc),
        in_specs=seq_specs + [_const_spec((1, GLA_VDIM)), _const_spec(lhs2.shape), _const_spec(level.shape)],
        out_specs=[pl.BlockSpec((n_par, CHUNK, GLA_VDIM), lambda b, c: (b, c, 0)),
                   pl.BlockSpec((n_par, GLA_HEADS, GLA_DK, GLA_DV), lambda b, c: (b, 0, 0, 0))],
        out_shape=[jax.ShapeDtypeStruct((batch, seq_len, GLA_VDIM), BF16),
                   jax.ShapeDtypeStruct((batch, GLA_HEADS, GLA_DK, GLA_DV), F32)],
        scratch_shapes=[pltpu.VMEM((n_par, GLA_HEADS, GLA_DK, GLA_DV), F32)],
        compiler_params=_params(("arbitrary", "arbitrary")),
        name="gla_prompt",
    )(*seq_args, norm_row, lhs2, level)
    return og.reshape(batch * seq_len, GLA_VDIM), s_fin


def _gla_sample_kernel(n_lev, dec_seq, q_ref, k_ref, la_ref, v_ref, r_ref, norm_ref, lhs_ref, lev_ref,
                       s0_ref, o_ref, snew_ref):
    q_main, k_upd, att, b_last = _gla_chunk_terms(q_ref[...], k_ref[...], la_ref[...], lhs_ref[...],
                                                  lev_ref[...], n_lev, dec_seq)
    v = v_ref[...]
    n_b = CHUNK // dec_seq
    row_b = lax.broadcasted_iota(jnp.int32, (CHUNK, GLA_DK), 0) // dec_seq
    col_b = lax.broadcasted_iota(jnp.int32, (GLA_DK, CHUNK), 1) // dec_seq
    outs = []
    for h in range(GLA_HEADS):
        hs = slice(h * GLA_DK, (h + 1) * GLA_DK)
        vh = v[:, h * GLA_DV:(h + 1) * GLA_DV]
        qm = q_main[:, hs]
        decay_t = jnp.exp2(b_last[:, hs]).T
        k_t = k_upd[:, hs].T.astype(BF16)
        o_h = _dot(att(h).astype(BF16), vh)
        for bi in range(n_b):
            s0 = s0_ref[bi, h]
            o_h = o_h + _dot(jnp.where(row_b == bi, qm, jnp.zeros_like(qm)), s0.astype(BF16))
            decay = jnp.broadcast_to(decay_t[:, bi * dec_seq:bi * dec_seq + 1], (GLA_DK, GLA_DV))
            k_b = jnp.where(col_b == bi, k_t, jnp.zeros_like(k_t))
            snew_ref[bi, h] = decay * s0 + _dot(k_b, vh)
        outs.append(o_h)
    o_ref[...] = _gla_out(jnp.concatenate(outs, axis=1), r_ref[...], norm_ref[...]).astype(o_ref.dtype)


def _gla_sample(qg, kg, la, vg, rg, norm_row, state, n_prompt_rows, dec_batch, dec_seq):
    n_b = CHUNK // dec_seq
    off = n_prompt_rows // CHUNK
    lhs3, level, n_lev = _chunk_tables(dec_seq)
    tok = lambda w: pl.BlockSpec((CHUNK, w), lambda g: (off + g, 0))
    st = pl.BlockSpec((n_b, GLA_HEADS, GLA_DK, GLA_DV), lambda g: (g, 0, 0, 0))
    return pl.pallas_call(
        functools.partial(_gla_sample_kernel, n_lev, dec_seq),
        grid=(dec_batch // n_b,),
        in_specs=[tok(GLA_KDIM), tok(GLA_KDIM), tok(GLA_KDIM), tok(GLA_VDIM), tok(GLA_VDIM),
                  _const_spec((1, GLA_VDIM)), _const_spec(lhs3.shape), _const_spec(level.shape), st],
        out_specs=[pl.BlockSpec((CHUNK, GLA_VDIM), lambda g: (g, 0)), st],
        out_shape=[jax.ShapeDtypeStruct((dec_batch * dec_seq, GLA_VDIM), BF16),
                   jax.ShapeDtypeStruct(state.shape, F32)],
        compiler_params=_params(("arbitrary",)),
        name="gla_sample",
    )(qg, kg, la, vg, rg, norm_row, lhs3, level, state)


def _merge_kernel(n_prompt_tiles, xp_ref, xs_ref, oap_ref, oas_ref, ogp_ref, ogs_ref, ga_ref, gb_ref,
                  wa_ref, wb_ref, wo_ref, x1_ref):
    i = pl.program_id(0)
    is_p = i < n_prompt_tiles
    x = jnp.where(is_p, xp_ref[...], xs_ref[...])
    oa = jnp.where(is_p, oap_ref[...], oas_ref[...])
    og = jnp.where(is_p, ogp_ref[...], ogs_ref[...])
    m = (jax.nn.sigmoid(ga_ref[...].astype(F32)) * _dot(oa, wa_ref[...])
         + jax.nn.sigmoid(gb_ref[...].astype(F32)) * _dot(og, wb_ref[...]))
    _store_token_tiles(x1_ref, x + _dot(m.astype(BF16), wo_ref[...]))


def _merge(xp2, xs2, oa_p, oa_s, og_p, og_s, ga, gb, wa, wb, wo):
    n_p, n_s = xp2.shape[0], xs2.shape[0]
    npt, nst = n_p // ROW_TILE, n_s // ROW_TILE
    p_map = lambda i: (jnp.minimum(i, npt - 1), 0)
    s_map = lambda i: (jnp.maximum(i - npt, 0), 0)
    row = lambda w: pl.BlockSpec((ROW_TILE, w), lambda i: (i, 0))
    return pl.pallas_call(
        functools.partial(_merge_kernel, npt),
        grid=(npt + nst,),
        in_specs=[
            pl.BlockSpec((ROW_TILE, D_MODEL), p_map), pl.BlockSpec((ROW_TILE, D_MODEL), s_map),
            pl.BlockSpec((ROW_TILE, ATT_WIDTH), p_map), pl.BlockSpec((ROW_TILE, ATT_WIDTH), s_map),
            pl.BlockSpec((ROW_TILE, GLA_VDIM), p_map), pl.BlockSpec((ROW_TILE, GLA_VDIM), s_map),
            row(D_MODEL), row(D_MODEL),
            _const_spec(wa.shape), _const_spec(wb.shape), _const_spec(wo.shape),
        ],
        out_specs=pl.BlockSpec((ROW_TILE * TILE_ROWS, LANES), lambda i: (i, 0)),
        out_shape=jax.ShapeDtypeStruct(((n_p + n_s) * TILE_ROWS, LANES), F32),
        compiler_params=_params(("arbitrary",)),
        name="merge",
    )(xp2, xs2, oa_p, oa_s, og_p, og_s, ga, gb, wa, wb, wo)


CHUNK_ROWS = 8
CHUNKS_PER_BLOCK = FFN_BLOCK // CHUNK_ROWS
TILE_PACK = TOP_K * ROW_TILE + N_EXPERTS * (CHUNK_ROWS - 1)
TILE_PACK += -TILE_PACK % CHUNK_ROWS


def _route_kernel(x1_ref, gffn_ref, rwh_ref, rwl_ref, rb_ref, upper_ref, lower_ref, eidx_ref, prel_ref, qloc_ref,
                  gate_ref, cnt_ref, cnt_scr):
    i = pl.program_id(0)

    @pl.when(i == 0)
    def _():
        cnt_scr[...] = jnp.zeros_like(cnt_scr)

    x1 = _load_token_tiles(x1_ref, ROW_TILE)
    ms = jnp.mean(x1 * x1, axis=-1, keepdims=True)
    h2 = x1 * lax.rsqrt(ms + EPS) * gffn_ref[...]
    h_hi = h2.astype(BF16)
    h_lo = (h2 - h_hi.astype(F32)).astype(BF16)
    rwh, rwl = rwh_ref[...], rwl_ref[...]
    logits = _nt_dot(rwh, h_hi) + _nt_dot(rwl, h_hi) + _nt_dot(rwh, h_lo) + rb_ref[...]
    eid = lax.broadcasted_iota(jnp.int32, logits.shape, 0)
    upper = upper_ref[...]
    seen = cnt_scr[...]
    vals, rows_e, hots, befores, counts = [], [], [], [], []
    lg = logits
    for _ in range(TOP_K):
        mx = jnp.max(lg, axis=0, keepdims=True)
        sel = jnp.min(jnp.where(lg == mx, eid, N_EXPERTS), axis=0, keepdims=True)
        onehot = eid == sel
        oh = onehot.astype(F32)
        hots.append(oh)
        befores.append(_dot(onehot.astype(BF16), upper))
        counts.append(jnp.sum(oh, axis=1, keepdims=True))
        vals.append(mx)
        rows_e.append(sel)
        lg = jnp.where(onehot, -jnp.inf, lg)
    run = (counts[0] + counts[1]) + (counts[2] + counts[3])
    run_pad = jnp.floor((run + (CHUNK_ROWS - 1)) * (1.0 / CHUNK_ROWS)) * CHUNK_ROWS
    tile_off = _dot(lower_ref[...], jnp.broadcast_to(run_pad, (N_EXPERTS, LANES)).astype(BF16))[:, 0:1]
    rows_p, rows_q = [], []
    ahead = jnp.zeros_like(run)
    for k in range(TOP_K):
        local = ahead + befores[k]
        rows_p.append(jnp.sum(hots[k] * (seen + local), axis=0, keepdims=True))
        rows_q.append(jnp.sum(hots[k] * (tile_off + local), axis=0, keepdims=True))
        ahead = ahead + counts[k]
    cnt_scr[...] = seen + run_pad
    ex = [jnp.exp(v - vals[0]) for v in vals]
    inv = 1.0 / (ex[0] + ex[1] + ex[2] + ex[3])
    eidx_ref[...] = jnp.concatenate(rows_e, axis=0)
    prel_ref[...] = jnp.concatenate(rows_p, axis=0).astype(jnp.int32)
    qloc_ref[...] = jnp.concatenate(rows_q, axis=0).astype(jnp.int32)
    gate_ref[...] = jnp.concatenate([e * inv for e in ex], axis=0)
    cnt_ref[...] = jnp.broadcast_to(seen + run_pad, cnt_ref.shape)


def _route(x1, g_ffn, rw_hi, rw_lo, rb_col):
    n = x1.shape[0] // TILE_ROWS
    upper = jnp.asarray(np.triu(np.ones((ROW_TILE, ROW_TILE), np.float32), 1), BF16)
    lower = jnp.asarray(np.tril(np.ones((N_EXPERTS, N_EXPERTS), np.float32), -1), BF16)
    col = lambda dt: (pl.BlockSpec((TOP_K, ROW_TILE), lambda i: (0, i)), jax.ShapeDtypeStruct((TOP_K, n), dt))
    outs = [col(jnp.int32), col(jnp.int32), col(jnp.int32), col(F32),
            (_resident_out_spec((N_EXPERTS, LANES)), jax.ShapeDtypeStruct((N_EXPERTS, LANES), F32))]
    return pl.pallas_call(
        _route_kernel,
        grid=(n // ROW_TILE,),
        in_specs=[pl.BlockSpec((ROW_TILE * TILE_ROWS, LANES), lambda i: (i, 0)), _const_spec((1, D_MODEL)),
                  _const_spec(rw_hi.shape), _const_spec(rw_lo.shape), _const_spec((N_EXPERTS, 1)),
                  _const_spec(upper.shape), _const_spec(lower.shape)],
        out_specs=[o[0] for o in outs],
        out_shape=[o[1] for o in outs],
        scratch_shapes=[pltpu.VMEM((N_EXPERTS, 1), F32)],
        compiler_params=_params(("arbitrary",)),
        name="route",
    )(x1, g_ffn, rw_hi, rw_lo, rb_col, upper, lower)


def _tile_rows(ref, row):
    return _tile_at(ref, row * TILE_ROWS)


def _tile_at(ref, first_row):
    return ref.at[pl.ds(pl.multiple_of(first_row, TILE_ROWS), TILE_ROWS), :]


def _pack_kernel(q_ref, x1_ref, gffn_ref, o_ref, h_scr):
    o_ref[...] = jnp.zeros(o_ref.shape, o_ref.dtype)

    @pl.when(pl.program_id(0) < pl.num_programs(0) - 1)
    def _():
        x1 = _load_token_tiles(x1_ref, ROW_TILE)
        ms = jnp.mean(x1 * x1, axis=-1, keepdims=True)
        _store_token_tiles(h_scr, x1 * lax.rsqrt(ms + EPS) * gffn_ref[...])

        def body(t, carry):
            row = _tile_rows(h_scr, t)[...]
            for k in range(TOP_K):
                _tile_at(o_ref, q_ref[0, 0, k * ROW_TILE + t])[...] = row
            return carry

        lax.fori_loop(0, ROW_TILE, body, 0, unroll=16)


def _pack(qloc_tiles, x1, g_ffn):
    n_tiles = qloc_tiles.shape[0]
    assert TILE_PACK >= 3 * FFN_BLOCK
    last = n_tiles - 1
    return pl.pallas_call(
        _pack_kernel,
        grid=(n_tiles + 1,),
        in_specs=[pl.BlockSpec((1, 1, TOP_K * ROW_TILE), lambda i: (jnp.minimum(i, last), 0, 0),
                               memory_space=pltpu.SMEM),
                  pl.BlockSpec((ROW_TILE * TILE_ROWS, LANES), lambda i: (jnp.minimum(i, last), 0)),
                  _const_spec((1, D_MODEL))],
        out_specs=pl.BlockSpec((TILE_PACK * TILE_ROWS, LANES), lambda i: (i, 0)),
        out_shape=jax.ShapeDtypeStruct(((n_tiles + 1) * TILE_PACK * TILE_ROWS, LANES), F32),
        scratch_shapes=[pltpu.VMEM((ROW_TILE * TILE_ROWS, LANES), F32)],
        compiler_params=_params(("arbitrary",)),
        name="pack",
    )(qloc_tiles, x1, g_ffn)


_INV_FIELDS = 3
_BYTE_BITS = 8
_BYTE = 1 << _BYTE_BITS


def _invert_kernel(n_blocks, eidx_ref, prel_ref, qloc_ref, rstart_ref, acc_ref):
    i = pl.program_id(0)

    @pl.when(i == 0)
    def _():
        acc_ref[...] = jnp.zeros(acc_ref.shape, acc_ref.dtype)

    eidx, prel, qloc = eidx_ref[...], prel_ref[...], qloc_ref[...]
    rstart = rstart_ref[...]
    rows = eidx.shape[1]
    eid = lax.broadcasted_iota(jnp.int32, (N_EXPERTS, rows), 0)
    blk_id = lax.broadcasted_iota(jnp.int32, (n_blocks, rows), 0)
    off_id = lax.broadcasted_iota(jnp.int32, (LANES, rows), 0)
    blk_shift = FFN_BLOCK.bit_length() - 1
    chunk_shift = CHUNK_ROWS.bit_length() - 1
    a_parts, b_parts = [], []
    for k in range(TOP_K):
        base = jnp.sum(jnp.where(eid == eidx[k:k + 1], rstart, 0), axis=0, keepdims=True)
        pos = base + prel[k:k + 1]
        leader = (prel[k:k + 1] & (CHUNK_ROWS - 1)) == 0
        in_blk = (blk_id == lax.shift_right_logical(pos, blk_shift)) & leader
        a_parts.append(jnp.where(in_blk, 1.0, 0.0).astype(BF16))
        hit = off_id == (lax.shift_right_logical(pos, chunk_shift) & (CHUNKS_PER_BLOCK - 1))
        cid = lax.shift_right_logical(i * TILE_PACK + qloc[k:k + 1], chunk_shift)
        fields = [(cid & (_BYTE - 1)).astype(F32), lax.shift_right_logical(cid, _BYTE_BITS).astype(F32),
                  jnp.ones((1, rows), F32)]
        b_parts.append([jnp.where(hit, f, 0.0).astype(BF16) for f in fields])
    a = jnp.concatenate(a_parts, axis=1)
    for j in range(_INV_FIELDS):
        b = jnp.concatenate([b_parts[k][j] for k in range(TOP_K)], axis=1)
        acc_ref[j] += _nt_dot(a, b)


def _invert(eidx, prel, qloc, region_start, n_blocks, n_tiles):
    n_all = eidx.shape[1]
    n_chunks = n_tiles * TILE_PACK // CHUNK_ROWS
    assert n_chunks < _BYTE * _BYTE, "chunk id is carried as two byte-sized fields"
    col = pl.BlockSpec((TOP_K, ROW_TILE), lambda i: (0, i))
    shape = (_INV_FIELDS, n_blocks, LANES)
    f = pl.pallas_call(
        functools.partial(_invert_kernel, n_blocks),
        grid=(n_all // ROW_TILE,),
        in_specs=[col, col, col, _const_spec((N_EXPERTS, 1))],
        out_specs=_resident_out_spec(shape),
        out_shape=jax.ShapeDtypeStruct(shape, F32),
        compiler_params=_params(("arbitrary",)),
        name="invert",
    )(eidx, prel, qloc, region_start)
    cid = (f[0] + float(_BYTE) * f[1]).astype(jnp.int32)
    valid = f[2] > 0.0
    blk = lax.broadcasted_iota(jnp.int32, cid.shape, 0)
    off = lax.broadcasted_iota(jnp.int32, cid.shape, 1)
    dump = n_chunks + (blk & 1) * CHUNKS_PER_BLOCK + jnp.minimum(off, CHUNKS_PER_BLOCK - 1)
    return jnp.where(valid, cid, n_chunks + 3 * CHUNKS_PER_BLOCK), jnp.where(valid, cid, dump)


def _ffn_kernel(be_ref, nused_ref, src_cur_ref, src_nxt_ref, dst_cur_ref, dst_prev_ref,
                wup_ref, bup_ref, wdn_ref, bdn_ref, xts_hbm, out_hbm, xbuf0, xbuf1, obuf0, obuf1, wup_bf, wdn_bf,
                gsem, ssem):
    i = pl.program_id(0)
    n_used = nused_ref[0]
    xbufs, obufs = (xbuf0, xbuf1), (obuf0, obuf1)
    chunk_len = CHUNK_ROWS * TILE_ROWS

    def hbm_chunk(ref, cid):
        return ref.at[pl.ds(pl.multiple_of(cid * chunk_len, chunk_len), chunk_len), :]

    def vmem_chunk(ref, c):
        return ref.at[pl.ds(c * chunk_len, chunk_len), :]

    def gather_copy(src_ref, c, s):
        return pltpu.make_async_copy(hbm_chunk(xts_hbm, src_ref[0, 0, c]), vmem_chunk(xbufs[s], c), gsem.at[s])

    def scatter_copy(dst_ref, c, s):
        return pltpu.make_async_copy(vmem_chunk(obufs[s], c), hbm_chunk(out_hbm, dst_ref[0, 0, c]), ssem.at[s])

    GATHER_PRIORITY, SCATTER_PRIORITY = 1, 0

    def start_rows(copy_fn, priority):
        for c in range(CHUNKS_PER_BLOCK):
            copy_fn(c).start(priority=priority)

    def wait_rows(copy_fn):
        for c in range(CHUNKS_PER_BLOCK):
            copy_fn(c).wait()

    first = i == 0
    changed = first | (be_ref[i] != be_ref[jnp.maximum(i - 1, 0)])

    @pl.when(first)
    def _():
        obuf0[...] = jnp.zeros(obuf0.shape, obuf0.dtype)
        obuf1[...] = jnp.zeros(obuf1.shape, obuf1.dtype)
        spare0 = out_hbm.shape[0] // chunk_len - TILE_PACK // CHUNK_ROWS
        start_rows(lambda c: pltpu.make_async_copy(
            vmem_chunk(obuf0, c), out_hbm.at[pl.ds((spare0 + c) * chunk_len, chunk_len), :], ssem.at[0]),
            SCATTER_PRIORITY)
        start_rows(lambda c: gather_copy(src_cur_ref, c, 0), GATHER_PRIORITY)

    @pl.when(changed & (i < n_used))
    def _():
        wup_bf[...] = wup_ref[0].astype(BF16)
        wdn_bf[...] = wdn_ref[0].astype(BF16)

    def step(s):
        wait_rows(lambda r: gather_copy(src_cur_ref, r, s))
        x = _load_token_tiles(xbufs[s], FFN_BLOCK).astype(BF16)
        start_rows(lambda r: gather_copy(src_nxt_ref, r, 1 - s), GATHER_PRIORITY)
        start_rows(lambda r: scatter_copy(dst_prev_ref, r, 1 - s), SCATTER_PRIORITY)
        hu = _dot(x, wup_bf[...]) + bup_ref[0]
        glu = jnp.minimum(hu[:, :D_FF], SWIGLU_LIMIT)
        lin = jnp.clip(hu[:, D_FF:], -SWIGLU_LIMIT, SWIGLU_LIMIT)
        act = glu * jax.nn.sigmoid(SWIGLU_ALPHA * glu) * (lin + 1.0)
        out = _dot(act.astype(BF16), wdn_bf[...]) + bdn_ref[0]
        wait_rows(lambda r: scatter_copy(dst_cur_ref, r, s))
        _store_token_tiles(obufs[s], out)

    def drain(s):
        start_rows(lambda r: scatter_copy(dst_cur_ref, r, s), SCATTER_PRIORITY)
        wait_rows(lambda r: scatter_copy(dst_cur_ref, r, 1 - s))
        wait_rows(lambda r: scatter_copy(dst_cur_ref, r, s))
        wait_rows(lambda r: gather_copy(src_cur_ref, r, 1 - s))

    for s in range(2):
        pl.when((i < n_used) & (i % 2 == s))(functools.partial(step, s))
    for s in range(2):
        pl.when((i == n_used - 1) & (i % 2 == s))(functools.partial(drain, s))


def _ffn(block_expert, n_used, src, dst, x_ts, w_up, b_up, w_down, b_down):
    n_blocks = block_expert.shape[0]
    smem_block = lambda fn: pl.BlockSpec((1, 1, LANES), fn, memory_space=pltpu.SMEM)
    cur = smem_block(lambda i, be, nu: (i, 0, 0))
    nxt = smem_block(lambda i, be, nu: (jnp.minimum(i + 1, n_blocks - 1), 0, 0))
    prev = smem_block(lambda i, be, nu: (jnp.where(i == 0, n_blocks, i - 1), 0, 0))
    ex3 = lambda i, be, nu: (be[i], 0, 0)
    return pl.pallas_call(
        _ffn_kernel,
        grid_spec=pltpu.PrefetchScalarGridSpec(
            num_scalar_prefetch=2,
            grid=(n_blocks,),
            in_specs=[
                cur, nxt, cur, prev,
                pl.BlockSpec((1, D_MODEL, 2 * D_FF), ex3),
                pl.BlockSpec((1, 1, 2 * D_FF), ex3),
                pl.BlockSpec((1, D_FF, D_MODEL), ex3),
                pl.BlockSpec((1, 1, D_MODEL), ex3),
                pl.BlockSpec(memory_space=pl.ANY),
            ],
            out_specs=pl.BlockSpec(memory_space=pl.ANY),
            scratch_shapes=[pltpu.VMEM((FFN_BLOCK * TILE_ROWS, LANES), F32)] * 4 + [
                pltpu.VMEM((D_MODEL, 2 * D_FF), BF16), pltpu.VMEM((D_FF, D_MODEL), BF16),
                pltpu.SemaphoreType.DMA((2,)), pltpu.SemaphoreType.DMA((2,))],
        ),
        out_shape=jax.ShapeDtypeStruct(x_ts.shape, F32),
        input_output_aliases={10: 0},
        compiler_params=_params(("arbitrary",)),
        name="ffn",
    )(block_expert, n_used, src, src, dst, dst, w_up, b_up, w_down, b_down, x_ts)


def _final_kernel(q_ref, gate_ref, x1_ref, yts_ref, g_ref, y_ref, sum_scr):
    def body(t, carry):
        acc = _tile_rows(x1_ref, t)[...]
        for k in range(TOP_K):
            slot = k * ROW_TILE + t
            acc = acc + gate_ref[0, 0, slot] * _tile_at(yts_ref, q_ref[0, 0, slot])[...]
        _tile_rows(sum_scr, t)[...] = acc
        return carry

    lax.fori_loop(0, ROW_TILE, body, 0, unroll=16)
    x2 = _load_token_tiles(sum_scr, ROW_TILE)
    ms = jnp.mean(x2 * x2, axis=-1, keepdims=True)
    y_ref[...] = x2 * lax.rsqrt(ms + EPS) * g_ref[...]


def _final(qloc_tiles, gate_tiles, x1, y_ts, g_final, row0, n_rows):
    t0 = row0 // ROW_TILE
    smem_tile = pl.BlockSpec((1, 1, TOP_K * ROW_TILE), lambda i: (t0 + i, 0, 0), memory_space=pltpu.SMEM)
    return pl.pallas_call(
        _final_kernel,
        grid=(n_rows // ROW_TILE,),
        in_specs=[smem_tile, smem_tile,
                  pl.BlockSpec((ROW_TILE * TILE_ROWS, LANES), lambda i: (t0 + i, 0)),
                  pl.BlockSpec((TILE_PACK * TILE_ROWS, LANES), lambda i: (t0 + i, 0)),
                  _const_spec((1, D_MODEL))],
        out_specs=pl.BlockSpec((ROW_TILE, D_MODEL), lambda i: (i, 0)),
        out_shape=jax.ShapeDtypeStruct((n_rows, D_MODEL), F32),
        scratch_shapes=[pltpu.VMEM((ROW_TILE * TILE_ROWS, LANES), F32)],
        compiler_params=_params(("arbitrary",)),
        name="final",
    )(qloc_tiles, gate_tiles, x1, y_ts, g_final)


def _rope_tables(seq_len, dec_seq):
    inv = ROPE_THETA ** (-np.arange(0, ROT_DIM, 2, dtype=np.float64) / ROT_DIM)
    pos = np.concatenate([np.arange(seq_len), PAST_LEN + np.arange(ROW_TILE) % dec_seq]).astype(np.float64)
    ang = pos[:, None] * inv[None, :]
    cos, sin = np.cos(ang), np.sin(ang)
    ones = np.ones((pos.shape[0], HEAD_DIM - ROT_DIM))
    cos_h = np.concatenate([cos, cos, ones], axis=1)
    sin_h = np.concatenate([-sin, sin, 0.0 * ones], axis=1)
    return (jnp.asarray(np.concatenate([cos_h, cos_h], axis=1), F32),
            jnp.asarray(np.concatenate([sin_h, sin_h], axis=1), F32))


def _block_tables(counts, n_all):
    max_rows = TOP_K * n_all + (n_all // ROW_TILE) * N_EXPERTS * (CHUNK_ROWS - 1)
    n_blocks = -(-max_rows // FFN_BLOCK) + N_EXPERTS
    padded = (counts + FFN_BLOCK - 1) // FFN_BLOCK * FFN_BLOCK
    pad_end = jnp.cumsum(padded)
    pad_start = (pad_end - padded).astype(jnp.int32).reshape(N_EXPERTS, 1)
    block_start = jnp.arange(n_blocks, dtype=jnp.int32) * FFN_BLOCK
    block_expert = jnp.minimum(jnp.sum(pad_end[None, :] <= block_start[:, None], axis=1), N_EXPERTS - 1)
    n_used = (pad_end[-1] // FFN_BLOCK).astype(jnp.int32).reshape(1)
    return pad_start, block_expert.astype(jnp.int32), n_used, n_blocks


def kernel(x_prompt, x_sample, cache_swa_k, cache_swa_v, state_gla, g_mix, w_in, w_gk_up, b_gk, sinks,
           gla_norm, w_branch_a, w_branch_b, w_out, g_ffn, router_w, router_b, w_up, b_up, w_down, b_down,
           g_final):
    batch, seq_len, _ = x_prompt.shape
    dec_batch, dec_seq, _ = x_sample.shape
    n_p, n_s = batch * seq_len, dec_batch * dec_seq
    n_all = n_p + n_s
    assert w_in.shape[0] == 1, "one layer: the final norm is fused after the only MoE"
    assert seq_len % ROW_TILE == 0 and n_s % ROW_TILE == 0 and ROW_TILE % dec_seq == 0
    assert dec_seq % SUBLANES == 0 and dec_batch % SAMPLE_GROUP == 0
    assert SAMPLE_GROUP * dec_seq == CHUNK and (dec_seq & (dec_seq - 1)) == 0

    xp2 = x_prompt.reshape(n_p, D_MODEL)
    xs2 = x_sample.reshape(n_s, D_MODEL)
    cos_tab, sin_tab = _rope_tables(seq_len, dec_seq)
    w = w_in[0]
    w_a = w[:, :_C_GL].astype(BF16)
    w_b = w[:, _C_GL + GK_RANK:].astype(BF16)
    w_gk = jnp.pad(w[:, _C_GL:_C_GL + GK_RANK], ((0, 0), (0, LANES - GK_RANK))).astype(BF16)
    wup_pad = jnp.pad(w_gk_up[0], ((0, LANES - GK_RANK), (0, 0))).astype(BF16)
    qa, ka, va, qg, kg, vg, la, rg, ga, gb = _proj(
        xp2, xs2, g_mix[0].reshape(1, D_MODEL), cos_tab, sin_tab, w_a, w_b, w_gk, wup_pad,
        b_gk[0].reshape(1, GLA_KDIM), seq_len)

    oa_p = _swa_prompt(sinks[0], qa, ka, va, batch, seq_len)
    oa_s, nk_s, nv_s = _swa_sample(sinks[0], qa, ka, va, cache_swa_k[0], cache_swa_v[0], n_p,
                                   dec_batch, dec_seq)
    norm_row = jnp.tile(gla_norm[0], GLA_HEADS).reshape(1, GLA_VDIM)
    og_p, s_fin = _gla_prompt(qg, kg, la, vg, rg, norm_row, batch, seq_len)
    og_s, s_new = _gla_sample(qg, kg, la, vg, rg, norm_row, state_gla[0], n_p, dec_batch, dec_seq)

    x1 = _merge(xp2, xs2, oa_p, oa_s, og_p, og_s, ga, gb, w_branch_a[0].astype(BF16),
                w_branch_b[0].astype(BF16), w_out[0].astype(BF16))

    g_ffn_row = g_ffn[0].reshape(1, D_MODEL)
    rw_t = router_w[0].T
    rw_hi = rw_t.astype(BF16)
    rw_lo = (rw_t - rw_hi.astype(F32)).astype(BF16)
    eidx, prel, qloc, gate, cnt = _route(x1, g_ffn_row, rw_hi, rw_lo, router_b[0].reshape(N_EXPERTS, 1))
    n_tiles = n_all // ROW_TILE
    region_start, block_expert, n_used, n_blocks = _block_tables(cnt[:, 0].astype(jnp.int32), n_all)
    per_tile = lambda a: a.reshape(TOP_K, n_tiles, ROW_TILE).transpose(1, 0, 2).reshape(n_tiles, 1, TOP_K * ROW_TILE)
    qloc_tiles, gate_tiles = per_tile(qloc * TILE_ROWS), per_tile(gate)
    x_ts = _pack(qloc_tiles, x1, g_ffn_row)
    src, dst = _invert(eidx, prel, qloc, region_start, n_blocks, n_tiles)
    prime = (n_tiles * TILE_PACK // CHUNK_ROWS + 2 * CHUNKS_PER_BLOCK
             + jnp.minimum(jnp.arange(LANES, dtype=jnp.int32), CHUNKS_PER_BLOCK - 1)).reshape(1, LANES)
    dst = jnp.concatenate([dst, prime], axis=0)
    y_ts = _ffn(block_expert, n_used, src.reshape(n_blocks, 1, LANES), dst.reshape(n_blocks + 1, 1, LANES),
                x_ts, w_up[0], b_up[0].reshape(N_EXPERTS, 1, 2 * D_FF), w_down[0],
                b_down[0].reshape(N_EXPERTS, 1, D_MODEL))

    g_out = g_final.reshape(1, D_MODEL)
    y_p = _final(qloc_tiles, gate_tiles, x1, y_ts, g_out, 0, n_p)
    y_s = _final(qloc_tiles, gate_tiles, x1, y_ts, g_out, n_p, n_s)

    kv_shape = (1, -1, WINDOW, KV_HEADS, HEAD_DIM)
    new_k_p = ka[:n_p].reshape(batch, seq_len, KV_WIDTH)[:, -WINDOW:].reshape(kv_shape)
    new_v_p = va[:n_p].reshape(batch, seq_len, KV_WIDTH)[:, -WINDOW:].reshape(kv_shape)
    return (y_p.reshape(batch, seq_len, D_MODEL), y_s.reshape(dec_batch, dec_seq, D_MODEL),
            new_k_p, new_v_p, s_fin[None], nk_s.reshape(kv_shape), nv_s.reshape(kv_shape), s_new[None])
```
